```python
import math
import jax, jax.numpy as jnp
from jax import lax
import numpy as np

D_MODEL = 2048
BATCH = 32
SEQ = 256
DEPTH = 2
DEC_BATCH = 4
DEC_SEQ = 4096
PAST_LEN = 256

GRID_W = 64
CONV_W = 4
Q_BLOCK = 128
EPS = 1e-6
ROPE_THETA = 10000.0
N_MOD = 6
HEAD_DIM = 128
ATTN_WIDTH = D_MODEL // 4
ATTN_HEADS = ATTN_WIDTH // HEAD_DIM
ATTN_KV_HEADS = ATTN_HEADS // 2
ATTN_GROUP = ATTN_HEADS // ATTN_KV_HEADS
KV_WIDTH = ATTN_KV_HEADS * HEAD_DIM
ATTN_SCALE = HEAD_DIM ** -0.5
LRU_WIDTH = D_MODEL // 4
LRU_BLOCKS = 8
LRU_BLOCK_W = LRU_WIDTH // LRU_BLOCKS
LRU_C = 8.0
DN_DK = 128
DN_DV = 128
DN_WIDTH = D_MODEL // 4
DN_HEADS = DN_WIDTH // DN_DV
DN_CHUNK = 64
S5_WIDTH = D_MODEL // 4
S5_CH = 16
S5_GROUPS = S5_WIDTH // S5_CH
S5_STATE = 64
D_FF = 4 * D_MODEL
DEEPNORM_ALPHA = (2 * DEPTH) ** 0.25
DEEPNORM_BETA = (8 * DEPTH) ** -0.25
IN_WIDTHS = (ATTN_WIDTH, KV_WIDTH, KV_WIDTH, LRU_WIDTH, LRU_WIDTH,
             DN_HEADS * DN_DK, DN_HEADS * DN_DK, DN_WIDTH, DN_WIDTH,
             2 * DN_HEADS, 2 * DN_HEADS, S5_WIDTH)
N_IN = sum(IN_WIDTHS)
MIX_WIDTH = ATTN_WIDTH + LRU_WIDTH + DN_WIDTH + S5_WIDTH

kernel_name = 'hybrid_diffusion_parallel_heads_step'

F32 = jnp.float32


def split_columns(proj):
    points = []
    acc = 0
    for w in IN_WIDTHS[:-1]:
        acc += w
        points.append(acc)
    return jnp.split(proj, points, axis=-1)


def rms_norm(x, g):
    xf = x.astype(F32)
    y = xf * lax.rsqrt(jnp.mean(xf * xf, axis=-1, keepdims=True) + EPS)
    return (y * g.astype(F32)).astype(x.dtype)


def layer_norm(x, g=None, b=None):
    xf = x.astype(F32)
    mu = jnp.mean(xf, axis=-1, keepdims=True)
    var = jnp.mean(jnp.square(xf - mu), axis=-1, keepdims=True)
    y = (xf - mu) * lax.rsqrt(var + EPS)
    if g is not None:
        y = y * g.astype(F32) + b.astype(F32)
    return y.astype(x.dtype)


def l2_norm(x):
    return x * lax.rsqrt(jnp.sum(x * x, axis=-1, keepdims=True) + EPS)


def centred_conv(x, w, b=None):
    left = (CONV_W - 1) // 2
    y = lax.conv_general_dilated(x, w[:, None, :].astype(x.dtype), window_strides=(1,),
                                 padding=[(left, CONV_W - 1 - left)],
                                 dimension_numbers=('NWC', 'WIO', 'NWC'),
                                 feature_group_count=x.shape[-1])
    if b is not None:
        y = y + b.astype(y.dtype)
    return y


def axial_rope(x):
    t = x.shape[1]
    rows = t // GRID_W
    row = jnp.repeat(jnp.arange(rows, dtype=F32), GRID_W)
    col = jnp.tile(jnp.arange(GRID_W, dtype=F32), rows)
    half = HEAD_DIM // 2
    quarter = half // 2
    inv_freq = jnp.power(ROPE_THETA, -jnp.arange(quarter, dtype=F32) / quarter)
    xf = x.astype(F32)

    def rotate(xa, pos):
        ang = pos[:, None] * inv_freq[None, :]
        cos = jnp.cos(ang)[None, :, None, :]
        sin = jnp.sin(ang)[None, :, None, :]
        x1, x2 = xa[..., :quarter], xa[..., quarter:]
        return jnp.concatenate([x1 * cos - x2 * sin, x2 * cos + x1 * sin], axis=-1)

    out = jnp.concatenate([rotate(xf[..., :half], row), rotate(xf[..., half:], col)], axis=-1)
    return out.astype(x.dtype)


def blocked_attention(q, k, v):
    b, t, h, dh = q.shape
    nb = t // Q_BLOCK
    qb = q.reshape(b, nb, Q_BLOCK, ATTN_KV_HEADS, ATTN_GROUP, dh).transpose(1, 0, 2, 3, 4, 5)

    def attend(qblk):
        s = jnp.einsum('bqkgd,bskd->bkgqs', qblk, k, preferred_element_type=F32) * ATTN_SCALE
        p = jax.nn.softmax(s, axis=-1).astype(v.dtype)
        return jnp.einsum('bkgqs,bskd->bqkgd', p, v)

    o = lax.map(attend, qb)
    return o.transpose(1, 0, 2, 3, 4, 5).reshape(b, t, h * dh)


def linear_scan(a, bx, h0, reverse):
    idx = -1 if reverse else 0
    bx = bx.at[:, idx].add(a[:, idx] * h0)

    def comb(e1, e2):
        a1, b1 = e1
        a2, b2 = e2
        return a1 * a2, a2 * b1 + b2

    _, h = lax.associative_scan(comb, (a, bx), reverse=reverse, axis=1)
    return h


def complex_scan(a_re, a_im, b_re, b_im, h0_re, h0_im, reverse):
    idx = -1 if reverse else 0
    b_re = b_re.at[:, idx].add(a_re * h0_re - a_im * h0_im)
    b_im = b_im.at[:, idx].add(a_re * h0_im + a_im * h0_re)
    ar = jnp.broadcast_to(a_re, b_re.shape)
    ai = jnp.broadcast_to(a_im, b_im.shape)

    def comb(e1, e2):
        ar1, ai1, br1, bi1 = e1
        ar2, ai2, br2, bi2 = e2
        return (ar1 * ar2 - ai1 * ai2, ar1 * ai2 + ai1 * ar2,
                ar2 * br1 - ai2 * bi1 + br2, ar2 * bi1 + ai2 * br1 + bi2)

    _, _, h_re, h_im = lax.associative_scan(comb, (ar, ai, b_re, b_im), reverse=reverse, axis=1)
    return h_re, h_im


def rglru_direction(xc, wa, ba, wx, bx, lam, h0, reverse):
    b, t, _ = xc.shape
    xb = xc.reshape(b, t, LRU_BLOCKS, LRU_BLOCK_W)
    r = jax.nn.sigmoid(jnp.einsum('btnc,ncd->btnd', xb, wa.astype(F32)).reshape(b, t, LRU_WIDTH) + ba.astype(F32))
    i = jax.nn.sigmoid(jnp.einsum('btnc,ncd->btnd', xb, wx.astype(F32)).reshape(b, t, LRU_WIDTH) + bx.astype(F32))
    log_a = -LRU_C * r * jax.nn.softplus(-lam.astype(F32))
    a = jnp.exp(log_a)
    inp = jnp.sqrt(-jnp.expm1(2.0 * log_a)) * (i * xc)
    h = linear_scan(a, inp, h0, reverse)
    return h, (h[:, 0] if reverse else h[:, -1])


def rglru_group(lx, lg, p, h0):
    xc = centred_conv(lx, p['lru_conv_w'], p['lru_conv_b']).astype(F32)
    h_f, hT_f = rglru_direction(xc, p['lru_wa'][0], p['lru_ba'][0], p['lru_wx'][0], p['lru_bx'][0],
                                p['lru_lambda'][0], h0[:, 0], False)
    h_b, hT_b = rglru_direction(xc, p['lru_wa'][1], p['lru_ba'][1], p['lru_wx'][1], p['lru_bx'][1],
                                p['lru_lambda'][1], h0[:, 1], True)
    out = (h_f + h_b) * jax.nn.gelu(lg.astype(F32))
    return out.astype(lx.dtype), jnp.stack([hT_f, hT_b], axis=1)


def chunk_gated_delta(q, k, v, beta, g, s0):
    b, t, h, dk = q.shape
    dv = v.shape[-1]
    n = t // DN_CHUNK

    def to_chunks(a):
        a = jnp.moveaxis(a, 2, 1)
        return a.reshape(a.shape[:2] + (n, DN_CHUNK) + a.shape[3:])

    q = to_chunks(q * dk ** -0.5)
    k = to_chunks(k)
    v = to_chunks(v)
    beta = to_chunks(beta)
    gc = jnp.cumsum(to_chunks(g), axis=-1)
    incl = jnp.tril(jnp.ones((DN_CHUNK, DN_CHUNK), dtype=bool))
    strict = jnp.tril(jnp.ones((DN_CHUNK, DN_CHUNK), dtype=bool), -1)
    decay = jnp.exp(jnp.where(incl, gc[..., :, None] - gc[..., None, :], -jnp.inf))
    kb = k * beta[..., None]
    a_mat = jnp.where(strict, jnp.einsum('bhncd,bhnsd->bhncs', kb, k) * decay, 0.0)
    eye = jnp.eye(DN_CHUNK, dtype=F32)
    rhs = jnp.concatenate([v * beta[..., None], kb * jnp.exp(gc)[..., None]], axis=-1)
    sol = lax.linalg.triangular_solve(eye + a_mat, rhs, left_side=True, lower=True, unit_diagonal=True)
    u_val, w = sol[..., :dv], sol[..., dv:]
    qk = jnp.einsum('bhncd,bhnsd->bhncs', q, k) * decay

    def step(s, inp):
        qi, ki, ui, wi, qki, gi = inp
        v_new = ui - jnp.einsum('bhcd,bhde->bhce', wi, s)
        o = (jnp.einsum('bhcd,bhde->bhce', qi * jnp.exp(gi)[..., None], s)
             + jnp.einsum('bhcs,bhse->bhce', qki, v_new))
        g_last = gi[..., -1]
        s = (s * jnp.exp(g_last)[..., None, None]
             + jnp.einsum('bhcd,bhce->bhde', ki * jnp.exp(g_last[..., None] - gi)[..., None], v_new))
        return s, o

    xs = tuple(jnp.moveaxis(a, 2, 0) for a in (q, k, u_val, w, qk, gc))
    s_final, o = lax.scan(step, s0, xs)
    o = jnp.moveaxis(o, 0, 2).reshape(b, h, t, dv)
    return jnp.moveaxis(o, 1, 2), s_final


def deltanet_group(dq, dk, dv, dz, dbeta, dalpha, p, s0):
    b, t, _ = dq.shape
    qkv = jax.nn.silu(centred_conv(jnp.concatenate([dq, dk, dv], axis=-1), p['dn_conv_w'])).astype(F32)
    q, k, v = jnp.split(qkv, [DN_HEADS * DN_DK, 2 * DN_HEADS * DN_DK], axis=-1)
    q = l2_norm(q.reshape(b, t, DN_HEADS, DN_DK))
    k = l2_norm(k.reshape(b, t, DN_HEADS, DN_DK))
    v = v.reshape(b, t, DN_HEADS, DN_DV)
    beta = jax.nn.sigmoid(dbeta.astype(F32)).reshape(b, t, 2, DN_HEADS)
    g = (-jnp.exp(p['dn_a_log'].astype(F32))
         * jax.nn.softplus(dalpha.astype(F32).reshape(b, t, 2, DN_HEADS) + p['dn_dt_bias'].astype(F32)))
    o_f, s_f = chunk_gated_delta(q, k, v, beta[:, :, 0], g[:, :, 0], s0[:, 0])
    flip = lambda a: jnp.flip(a, axis=1)
    o_b, s_b = chunk_gated_delta(flip(q), flip(k), flip(v), flip(beta[:, :, 1]), flip(g[:, :, 1]), s0[:, 1])
    o = o_f + flip(o_b)
    o = rms_norm(o, p['dn_norm_g']) * jax.nn.silu(dz.astype(F32).reshape(b, t, DN_HEADS, DN_DV))
    return o.reshape(b, t, DN_WIDTH).astype(dq.dtype), jnp.stack([s_f, s_b], axis=1)


def s5_direction(u, lam_re, lam_im, log_dt, b_re, b_im, c_re, c_im, h0_re, h0_im, reverse):
    dt = jnp.exp(log_dt.astype(F32))[:, None]
    lr = lam_re.astype(F32)
    li = lam_im.astype(F32)
    mag = jnp.exp(lr * dt)
    abar_re = mag * jnp.cos(li * dt)
    abar_im = mag * jnp.sin(li * dt)
    den = lr * lr + li * li
    nr = abar_re - 1.0
    ni = abar_im
    f_re = (nr * lr + ni * li) / den
    f_im = (ni * lr - nr * li) / den
    br = b_re.astype(F32)
    bi = b_im.astype(F32)
    bb_re = f_re[..., None] * br - f_im[..., None] * bi
    bb_im = f_re[..., None] * bi + f_im[..., None] * br
    bu_re = jnp.einsum('btgh,gph->btgp', u, bb_re)
    bu_im = jnp.einsum('btgh,gph->btgp', u, bb_im)
    h_re, h_im = complex_scan(abar_re, abar_im, bu_re, bu_im, h0_re, h0_im, reverse)
    y = (jnp.einsum('btgp,ghp->btgh', h_re, c_re.astype(F32))
         - jnp.einsum('btgp,ghp->btgh', h_im, c_im.astype(F32)))
    j = 0 if reverse else -1
    return y, h_re[:, j], h_im[:, j]


def s5_group(su, p, h0_re, h0_im):
    b, t, _ = su.shape
    uf = su.astype(F32)
    u = uf.reshape(b, t, S5_GROUPS, S5_CH)
    ys, fin_re, fin_im = [], [], []
    for d, rev in ((0, False), (1, True)):
        y, hr, hi = s5_direction(u, p['s5_lambda_re'][d], p['s5_lambda_im'][d], p['s5_log_dt'][d],
                                 p['s5_b_re'][d], p['s5_b_im'][d], p['s5_c_re'][d], p['s5_c_im'][d],
                                 h0_re[:, d], h0_im[:, d], rev)
        ys.append(y)
        fin_re.append(hr)
        fin_im.append(hi)
    y = (ys[0] + ys[1]).reshape(b, t, S5_WIDTH) + p['s5_d'].astype(F32) * uf
    gy = jax.nn.gelu(y)
    out = gy * jax.nn.sigmoid(gy @ p['s5_glu_w'].astype(F32) + p['s5_glu_b'].astype(F32))
    return out.astype(su.dtype), jnp.stack(fin_re, axis=1), jnp.stack(fin_im, axis=1)


def mixer_groups(u, p, cache):
    b, t, _ = u.shape
    (aq, ak, av, lx, lg, dq, dk, dv, dz, dbeta, dalpha, su) = split_columns(
        jnp.einsum('btd,dn->btn', u, p['w_in']))
    is_ctx = cache is None
    if is_ctx:
        h0_lru = jnp.zeros((b, 2, LRU_WIDTH), F32)
        s0_dn = jnp.zeros((b, 2, DN_HEADS, DN_DK, DN_DV), F32)
        h0_s5r = jnp.zeros((b, 2, S5_GROUPS, S5_STATE), F32)
        h0_s5i = jnp.zeros((b, 2, S5_GROUPS, S5_STATE), F32)
    else:
        h0_lru = cache['lru'].astype(F32)
        s0_dn = cache['dn'].astype(F32)
        h0_s5r = cache['s5_re'].astype(F32)
        h0_s5i = cache['s5_im'].astype(F32)
    q = rms_norm(aq.reshape(b, t, ATTN_HEADS, HEAD_DIM), p['q_norm_g'])
    k = rms_norm(ak.reshape(b, t, ATTN_KV_HEADS, HEAD_DIM), p['k_norm_g'])
    v = av.reshape(b, t, ATTN_KV_HEADS, HEAD_DIM)
    if is_ctx:
        attn = blocked_attention(q, k, v)
    else:
        k_all = jnp.concatenate([cache['k'].astype(k.dtype), axial_rope(k)], axis=1)
        v_all = jnp.concatenate([cache['v'].astype(v.dtype), v], axis=1)
        attn = blocked_attention(axial_rope(q), k_all, v_all)
    lru_out, lru_fin = rglru_group(lx, lg, p, h0_lru)
    dn_out, dn_fin = deltanet_group(dq, dk, dv, dz, dbeta, dalpha, p, s0_dn)
    s5_out, s5r_fin, s5i_fin = s5_group(su, p, h0_s5r, h0_s5i)
    mix = jnp.concatenate([attn, lru_out, dn_out, s5_out], axis=-1)
    if not is_ctx:
        return mix, None
    dt = u.dtype
    state = {'k': k, 'v': v, 'lru': lru_fin.astype(dt), 'dn': dn_fin.astype(dt),
             's5_re': s5r_fin.astype(dt), 's5_im': s5i_fin.astype(dt)}
    return mix, state


def trunk_layer(x, mod, p, cache):
    shift1, scale1, gate1, shift2, scale2, gate2 = jnp.split(mod, N_MOD, axis=-1)
    u = layer_norm(x) * (1.0 + scale1) + shift1
    mix, state = mixer_groups(u, p, cache)
    x = layer_norm(DEEPNORM_ALPHA * x + gate1 * jnp.einsum('btm,md->btd', mix, p['w_out']),
                   p['ln1_g'], p['ln1_b'])
    u = layer_norm(x) * (1.0 + scale2) + shift2
    hid = jnp.square(jax.nn.relu(jnp.einsum('btd,df->btf', u, p['w_mlp1'])))
    x = layer_norm(DEEPNORM_ALPHA * x + gate2 * jnp.einsum('btf,fd->btd', hid, p['w_mlp2']),
                   p['ln2_g'], p['ln2_b'])
    return x, state


def setup_inputs(seed: int = 0) -> dict:
    key = jax.random.key(seed)
    keys = iter(jax.random.split(key, 64))

    def nrm(shape, scale=1.0):
        return jax.random.normal(next(keys), shape, F32) * scale

    def uni(shape, lo, hi):
        return jax.random.uniform(next(keys), shape, F32, lo, hi)

    L = DEPTH
    a0 = uni((L, 2, LRU_WIDTH), 0.9, 0.999)
    base = a0 ** (1.0 / LRU_C)
    dn_dt = uni((L, 2, DN_HEADS), 0.001, 0.1)
    return {
        'x_prompt': nrm((BATCH, SEQ, D_MODEL)),
        'x_sample': nrm((DEC_BATCH, DEC_SEQ, D_MODEL)),
        'cache_attn_k': nrm((DEC_BATCH, L, PAST_LEN, ATTN_KV_HEADS, HEAD_DIM)),
        'cache_attn_v': nrm((DEC_BATCH, L, PAST_LEN, ATTN_KV_HEADS, HEAD_DIM)),
        'state_rglru': nrm((DEC_BATCH, L, 2, LRU_WIDTH), 0.5),
        'state_delta': nrm((DEC_BATCH, L, 2, DN_HEADS, DN_DK, DN_DV), 0.1),
        'state_s5_re': nrm((DEC_BATCH, L, 2, S5_GROUPS, S5_STATE), 0.5),
        'state_s5_im': nrm((DEC_BATCH, L, 2, S5_GROUPS, S5_STATE), 0.5),
        'c': nrm((DEC_BATCH, D_MODEL)),
        'c_ctx': nrm((D_MODEL,)),
        'w_ada': nrm((L, D_MODEL, N_MOD * D_MODEL), D_MODEL ** -0.5),
        'b_ada': nrm((L, N_MOD * D_MODEL), 0.02),
        'w_in': nrm((L, D_MODEL, N_IN), D_MODEL ** -0.5),
        'w_out': nrm((L, MIX_WIDTH, D_MODEL), MIX_WIDTH ** -0.5 * DEEPNORM_BETA),
        'ln1_g': 1.0 + nrm((L, D_MODEL), 0.02),
        'ln1_b': nrm((L, D_MODEL), 0.02),
        'ln2_g': 1.0 + nrm((L, D_MODEL), 0.02),
        'ln2_b': nrm((L, D_MODEL), 0.02),
        'w_mlp1': nrm((L, D_MODEL, D_FF), D_MODEL ** -0.5),
        'w_mlp2': nrm((L, D_FF, D_MODEL), D_FF ** -0.5 * DEEPNORM_BETA),
        'q_norm_g': 1.0 + nrm((L, HEAD_DIM), 0.02),
        'k_norm_g': 1.0 + nrm((L, HEAD_DIM), 0.02),
        'lru_conv_w': nrm((L, CONV_W, LRU_WIDTH), CONV_W ** -0.5),
        'lru_conv_b': nrm((L, LRU_WIDTH), 0.02),
        'lru_wa': nrm((L, 2, LRU_BLOCKS, LRU_BLOCK_W, LRU_BLOCK_W), LRU_BLOCK_W ** -0.5),
        'lru_ba': nrm((L, 2, LRU_WIDTH), 0.02),
        'lru_wx': nrm((L, 2, LRU_BLOCKS, LRU_BLOCK_W, LRU_BLOCK_W), LRU_BLOCK_W ** -0.5),
        'lru_bx': nrm((L, 2, LRU_WIDTH), 0.02),
        'lru_lambda': jnp.log(base) - jnp.log1p(-base),
        'dn_conv_w': nrm((L, CONV_W, 2 * DN_HEADS * DN_DK + DN_WIDTH), CONV_W ** -0.5),
        'dn_a_log': jnp.log(uni((L, 2, DN_HEADS), 1.0, 16.0)),
        'dn_dt_bias': dn_dt + jnp.log(-jnp.expm1(-dn_dt)),
        'dn_norm_g': 1.0 + nrm((L, DN_DV), 0.02),
        's5_lambda_re': -0.5 + nrm((L, 2, S5_GROUPS, S5_STATE), 0.01),
        's5_lambda_im': math.pi * jnp.arange(S5_STATE, dtype=F32) + nrm((L, 2, S5_GROUPS, S5_STATE), 0.01),
        's5_log_dt': jnp.log(uni((L, 2, S5_GROUPS), 0.001, 0.1)),
        's5_b_re': nrm((L, 2, S5_GROUPS, S5_STATE, S5_CH), (2.0 * S5_CH) ** -0.5),
        's5_b_im': nrm((L, 2, S5_GROUPS, S5_STATE, S5_CH), (2.0 * S5_CH) ** -0.5),
        's5_c_re': nrm((L, 2, S5_GROUPS, S5_CH, S5_STATE), (2.0 * S5_STATE) ** -0.5),
        's5_c_im': nrm((L, 2, S5_GROUPS, S5_CH, S5_STATE), (2.0 * S5_STATE) ** -0.5),
        's5_d': nrm((L, S5_WIDTH)),
        's5_glu_w': nrm((L, S5_WIDTH, S5_WIDTH), S5_WIDTH ** -0.5),
        's5_glu_b': nrm((L, S5_WIDTH), 0.02),
    }


def reference(x_prompt, x_sample, cache_attn_k, cache_attn_v, state_rglru, state_delta, state_s5_re,
              state_s5_im, c, c_ctx, w_ada, b_ada, w_in, w_out, ln1_g, ln1_b, ln2_g, ln2_b, w_mlp1, w_mlp2,
              q_norm_g, k_norm_g, lru_conv_w, lru_conv_b, lru_wa, lru_ba, lru_wx, lru_bx, lru_lambda,
              dn_conv_w, dn_a_log, dn_dt_bias, dn_norm_g, s5_lambda_re, s5_lambda_im, s5_log_dt,
              s5_b_re, s5_b_im, s5_c_re, s5_c_im, s5_d, s5_glu_w, s5_glu_b):
    y_prompt = x_prompt
    y_sample = x_sample
    ks, vs, lrus, dns, s5rs, s5is = [], [], [], [], [], []
    for l in range(DEPTH):
        p = {'w_in': w_in[l], 'w_out': w_out[l], 'ln1_g': ln1_g[l], 'ln1_b': ln1_b[l],
             'ln2_g': ln2_g[l], 'ln2_b': ln2_b[l], 'w_mlp1': w_mlp1[l], 'w_mlp2': w_mlp2[l],
             'q_norm_g': q_norm_g[l], 'k_norm_g': k_norm_g[l],
             'lru_conv_w': lru_conv_w[l], 'lru_conv_b': lru_conv_b[l], 'lru_wa': lru_wa[l], 'lru_ba': lru_ba[l],
             'lru_wx': lru_wx[l], 'lru_bx': lru_bx[l], 'lru_lambda': lru_lambda[l],
             'dn_conv_w': dn_conv_w[l], 'dn_a_log': dn_a_log[l], 'dn_dt_bias': dn_dt_bias[l],
             'dn_norm_g': dn_norm_g[l],
             's5_lambda_re': s5_lambda_re[l], 's5_lambda_im': s5_lambda_im[l], 's5_log_dt': s5_log_dt[l],
             's5_b_re': s5_b_re[l], 's5_b_im': s5_b_im[l], 's5_c_re': s5_c_re[l], 's5_c_im': s5_c_im[l],
             's5_d': s5_d[l], 's5_glu_w': s5_glu_w[l], 's5_glu_b': s5_glu_b[l]}
        mod_ctx = (jax.nn.silu(c_ctx) @ w_ada[l] + b_ada[l])[None, None, :]
        mod_lat = (jax.nn.silu(c) @ w_ada[l] + b_ada[l])[:, None, :]
        y_prompt, st = trunk_layer(y_prompt, mod_ctx, p, None)
        ks.append(st['k'])
        vs.append(st['v'])
        lrus.append(st['lru'])
        dns.append(st['dn'])
        s5rs.append(st['s5_re'])
        s5is.append(st['s5_im'])
        cache_l = {'k': cache_attn_k[:, l], 'v': cache_attn_v[:, l], 'lru': state_rglru[:, l],
                   'dn': state_delta[:, l], 's5_re': state_s5_re[:, l], 's5_im': state_s5_im[:, l]}
        y_sample, _ = trunk_layer(y_sample, mod_lat, p, cache_l)
    new_attn_k = jnp.stack(ks, axis=1)
    new_attn_v = jnp.stack(vs, axis=1)
    new_rglru = jnp.stack(lrus, axis=1)
    new_delta = jnp.stack(dns, axis=1)
    new_s5_re = jnp.stack(s5rs, axis=1)
    new_s5_im = jnp.stack(s5is, axis=1)
    return (y_prompt, y_sample, new_attn_k, new_attn_v, new_rglru, new_delta, new_s5_re, new_s5_im)
```

```python
import functools
import math

import jax
import jax.numpy as jnp
from jax import lax
from jax.experimental import pallas as pl
from jax.experimental.pallas import tpu as pltpu

F32 = jnp.float32
BF16 = jnp.bfloat16

D_MODEL = 2048
DEPTH = 2
GRID_W = 64
CONV_W = 4
EPS = 1e-6
ROPE_THETA = 10000.0
N_MOD = 6
HEAD_DIM = 128
ATTN_WIDTH = D_MODEL // 4
ATTN_HEADS = ATTN_WIDTH // HEAD_DIM
ATTN_KV_HEADS = ATTN_HEADS // 2
ATTN_GROUP = ATTN_HEADS // ATTN_KV_HEADS
KV_WIDTH = ATTN_KV_HEADS * HEAD_DIM
ATTN_SCALE = HEAD_DIM ** -0.5
LRU_WIDTH = D_MODEL // 4
LRU_BLOCKS = 8
LRU_BLOCK_W = LRU_WIDTH // LRU_BLOCKS
LRU_C = 8.0
DN_DK = 128
DN_DV = 128
DN_WIDTH = D_MODEL // 4
DN_HEADS = DN_WIDTH // DN_DV
DN_CHUNK = 64
S5_WIDTH = D_MODEL // 4
S5_CH = 16
S5_GROUPS = S5_WIDTH // S5_CH
S5_STATE = 64
D_FF = 4 * D_MODEL
DEEPNORM_ALPHA = (2 * DEPTH) ** 0.25
IN_WIDTHS = (ATTN_WIDTH, KV_WIDTH, KV_WIDTH, LRU_WIDTH, LRU_WIDTH,
             DN_HEADS * DN_DK, DN_HEADS * DN_DK, DN_WIDTH, DN_WIDTH,
             2 * DN_HEADS, 2 * DN_HEADS, S5_WIDTH)
N_IN = sum(IN_WIDTHS)

COL_Q = 0
COL_KV = 512
COL_LX = 1024
COL_LG = 1536
COL_DQ = 2048
COL_DK = 2560
COL_DV = 3072
COL_DZ = 3584
COL_SU = 4096
COL_BA = 4608
N_PROJ = 4864

MOD_ROWS = 8
VMEM_LIMIT = 56 * 1024 * 1024


def _cparams(sem):
    return pltpu.CompilerParams(dimension_semantics=sem, vmem_limit_bytes=VMEM_LIMIT)


def _ln(x):
    mu = jnp.mean(x, axis=-1, keepdims=True)
    xc = x - mu
    var = jnp.mean(xc * xc, axis=-1, keepdims=True)
    return xc * lax.rsqrt(var + EPS)


def _ada_kernel(c_ref, w_ref, b_ref, o_ref):
    cs = c_ref[...]
    s = cs * jax.nn.sigmoid(cs)
    o_ref[0] = jnp.dot(s.astype(BF16), w_ref[0].astype(BF16), preferred_element_type=F32) + b_ref[0]


def ada_mod(cond, w_ada, b_ada, tn=1024):
    depth, d, n = w_ada.shape
    return pl.pallas_call(
        _ada_kernel,
        grid=(depth, n // tn),
        in_specs=[pl.BlockSpec((MOD_ROWS, d), lambda l, j: (0, 0)),
                  pl.BlockSpec((1, d, tn), lambda l, j: (l, 0, j)),
                  pl.BlockSpec((1, 1, tn), lambda l, j: (l, 0, j))],
        out_specs=pl.BlockSpec((1, MOD_ROWS, tn), lambda l, j: (l, 0, j)),
        out_shape=jax.ShapeDtypeStruct((depth, MOD_ROWS, n), F32),
        compiler_params=_cparams(("parallel", "parallel")),
        name="ada_mod",
    )(cond, w_ada, b_ada.reshape(depth, 1, n))


def _in_kernel(x_ref, sc_ref, sh_ref, w_ref, o_ref, u_ref):
    @pl.when(pl.program_id(1) == 0)
    def _():
        u = _ln(x_ref[...]) * (1.0 + sc_ref[0]) + sh_ref[0]
        u_ref[...] = u.astype(BF16)

    o_ref[...] = jnp.dot(u_ref[...], w_ref[...], preferred_element_type=F32)


def in_proj(x, mod, w, row_of_tile, tm=1024, tn=256):
    m, d = x.shape
    n = w.shape[1]

    def mod_spec(kind):
        return pl.BlockSpec((1, 1, d), lambda i, j: (kind * MOD_ROWS + row_of_tile(i, tm), 0, 0))

    return pl.pallas_call(
        _in_kernel,
        grid=(m // tm, n // tn),
        in_specs=[pl.BlockSpec((tm, d), lambda i, j: (i, 0)),
                  mod_spec(1), mod_spec(0),
                  pl.BlockSpec((d, tn), lambda i, j: (0, j))],
        out_specs=pl.BlockSpec((tm, tn), lambda i, j: (i, j)),
        out_shape=jax.ShapeDtypeStruct((m, n), F32),
        scratch_shapes=[pltpu.VMEM((tm, d), BF16)],
        compiler_params=_cparams(("parallel", "arbitrary")),
        name="in_proj",
    )(x, mod, mod, w)


def _rms_heads(x, g, heads):
    outs = []
    for h in range(heads):
        xh = x[:, h * HEAD_DIM:(h + 1) * HEAD_DIM]
        outs.append(xh * lax.rsqrt(jnp.mean(xh * xh, axis=-1, keepdims=True) + EPS) * g)
    return outs


def _softmax_av(q, k, v):
    s = lax.dot_general(q, k, (((1,), (1,)), ((), ())), preferred_element_type=F32) * ATTN_SCALE
    m = jnp.max(s, axis=-1, keepdims=True)
    p = jnp.exp(s - m)
    l = jnp.sum(p, axis=-1, keepdims=True)
    o = jnp.dot(p.astype(BF16), v, preferred_element_type=F32)
    return o / l


def _ctx_attn_kernel(q_ref, kv_ref, qg_ref, kg_ref, o_ref, kn_ref, v_ref):
    qs = _rms_heads(q_ref[...], qg_ref[...], ATTN_HEADS)
    kv = kv_ref[...]
    ks = _rms_heads(kv[:, :KV_WIDTH], kg_ref[...], ATTN_KV_HEADS)
    v = kv[:, KV_WIDTH:]
    v_ref[...] = v
    t = q_ref.shape[0]
    for kh in range(ATTN_KV_HEADS):
        kn_ref[:, kh * HEAD_DIM:(kh + 1) * HEAD_DIM] = ks[kh]
        q2 = jnp.concatenate([qs[kh * ATTN_GROUP + g] for g in range(ATTN_GROUP)], axis=0).astype(BF16)
        o = _softmax_av(q2, ks[kh].astype(BF16), v[:, kh * HEAD_DIM:(kh + 1) * HEAD_DIM].astype(BF16))
        for g in range(ATTN_GROUP):
            h = kh * ATTN_GROUP + g
            o_ref[:, h * HEAD_DIM:(h + 1) * HEAD_DIM] = o[g * t:(g + 1) * t].astype(o_ref.dtype)


def ctx_attention(proj, qg, kg, batch, seq):
    return pl.pallas_call(
        _ctx_attn_kernel,
        grid=(batch,),
        in_specs=[pl.BlockSpec((seq, ATTN_WIDTH), lambda b: (b, COL_Q // ATTN_WIDTH)),
                  pl.BlockSpec((seq, 2 * KV_WIDTH), lambda b: (b, COL_KV // (2 * KV_WIDTH))),
                  pl.BlockSpec((1, HEAD_DIM), lambda b: (0, 0)),
                  pl.BlockSpec((1, HEAD_DIM), lambda b: (0, 0))],
        out_specs=[pl.BlockSpec((seq, ATTN_WIDTH), lambda b: (b, 0)),
                   pl.BlockSpec((seq, KV_WIDTH), lambda b: (b, 0)),
                   pl.BlockSpec((seq, KV_WIDTH), lambda b: (b, 0))],
        out_shape=[jax.ShapeDtypeStruct((batch * seq, ATTN_WIDTH), BF16),
                   jax.ShapeDtypeStruct((batch * seq, KV_WIDTH), F32),
                   jax.ShapeDtypeStruct((batch * seq, KV_WIDTH), F32)],
        compiler_params=_cparams(("parallel",)),
        name="ctx_attention",
    )(proj, proj, qg, kg)


def _rope(x, cos, sin, heads):
    w = x.shape[-1]
    lane = lax.broadcasted_iota(jnp.int32, x.shape, 1)
    quarter = HEAD_DIM // 4
    partner = jnp.where((lane % (2 * quarter)) < quarter,
                        pltpu.roll(x, w - quarter, 1), pltpu.roll(x, quarter, 1))
    cos_t = jnp.concatenate([cos] * heads, axis=1)
    sin_t = jnp.concatenate([sin] * heads, axis=1)
    return x * cos_t + partner * sin_t


def _lat_prep_kernel(q_ref, kv_ref, qg_ref, kg_ref, cos_ref, sin_ref, qo_ref, ko_ref, vo_ref):
    qn = jnp.concatenate(_rms_heads(q_ref[...], qg_ref[...], ATTN_HEADS), axis=1)
    kv = kv_ref[...]
    kn = jnp.concatenate(_rms_heads(kv[:, :KV_WIDTH], kg_ref[...], ATTN_KV_HEADS), axis=1)
    cos = cos_ref[...]
    sin = sin_ref[...]
    qo_ref[...] = _rope(qn, cos, sin, ATTN_HEADS).astype(BF16)
    ko_ref[...] = _rope(kn, cos, sin, ATTN_KV_HEADS).astype(BF16)
    vo_ref[...] = kv[:, KV_WIDTH:].astype(BF16)


def lat_prep(proj, qg, kg, cos, sin, row0, m, seq, tm=512):
    off = row0 // tm
    per = seq // tm
    return pl.pallas_call(
        _lat_prep_kernel,
        grid=(m // tm,),
        in_specs=[pl.BlockSpec((tm, ATTN_WIDTH), lambda i: (i + off, COL_Q // ATTN_WIDTH)),
                  pl.BlockSpec((tm, 2 * KV_WIDTH), lambda i: (i + off, COL_KV // (2 * KV_WIDTH))),
                  pl.BlockSpec((1, HEAD_DIM), lambda i: (0, 0)),
                  pl.BlockSpec((1, HEAD_DIM), lambda i: (0, 0)),
                  pl.BlockSpec((tm, HEAD_DIM), lambda i: (i % per, 0)),
                  pl.BlockSpec((tm, HEAD_DIM), lambda i: (i % per, 0))],
        out_specs=[pl.BlockSpec((tm, ATTN_WIDTH), lambda i: (i, 0)),
                   pl.BlockSpec((tm, KV_WIDTH), lambda i: (i, 0)),
                   pl.BlockSpec((tm, KV_WIDTH), lambda i: (i, 0))],
        out_shape=[jax.ShapeDtypeStruct((m, ATTN_WIDTH), BF16),
                   jax.ShapeDtypeStruct((m, KV_WIDTH), BF16),
                   jax.ShapeDtypeStruct((m, KV_WIDTH), BF16)],
        compiler_params=_cparams(("parallel",)),
        name="lat_prep",
    )(proj, proj, qg, kg, cos, sin)


def _lat_attn_kernel(q_ref, k_ref, v_ref, o_ref):
    k = k_ref[0]
    v = v_ref[0]
    for g in range(ATTN_GROUP):
        sl = slice(g * HEAD_DIM, (g + 1) * HEAD_DIM)
        o_ref[:, sl] = _softmax_av(q_ref[:, sl], k, v).astype(o_ref.dtype)


def lat_attention(q, k_all, v_all, seq, tq=256):
    b, s, _ = k_all.shape
    nq = seq // tq
    gw = ATTN_GROUP * HEAD_DIM
    return pl.pallas_call(
        _lat_attn_kernel,
        grid=(b, ATTN_KV_HEADS, nq),
        in_specs=[pl.BlockSpec((tq, gw), lambda bi, kh, qi: (bi * nq + qi, kh)),
                  pl.BlockSpec((1, s, HEAD_DIM), lambda bi, kh, qi: (bi, 0, kh)),
                  pl.BlockSpec((1, s, HEAD_DIM), lambda bi, kh, qi: (bi, 0, kh))],
        out_specs=pl.BlockSpec((tq, gw), lambda bi, kh, qi: (bi * nq + qi, kh)),
        out_shape=jax.ShapeDtypeStruct((b * seq, ATTN_WIDTH), BF16),
        compiler_params=_cparams(("parallel", "parallel", "arbitrary")),
        name="lat_attention",
    )(q, k_all, v_all)


def rope_tables(seq):
    t = jnp.arange(seq)
    row = (t // GRID_W).astype(F32)
    col = (t % GRID_W).astype(F32)
    quarter = HEAD_DIM // 4
    inv_freq = jnp.power(ROPE_THETA, -jnp.arange(quarter, dtype=F32) / quarter)
    ar = row[:, None] * inv_freq[None, :]
    ac = col[:, None] * inv_freq[None, :]
    cos = jnp.concatenate([jnp.cos(ar), jnp.cos(ar), jnp.cos(ac), jnp.cos(ac)], axis=1)
    sin = jnp.concatenate([-jnp.sin(ar), jnp.sin(ar), -jnp.sin(ac), jnp.sin(ac)], axis=1)
    return cos, sin


def _out_kernel(x_ref, a_ref, b_ref, c_ref, d_ref, w_ref, gate_ref, g_ref, bb_ref, sc_ref, sh_ref,
                x1_ref, u2_ref):
    q = D_MODEL // 4
    y = jnp.dot(a_ref[...], w_ref[0:q, :], preferred_element_type=F32)
    y += jnp.dot(b_ref[...], w_ref[q:2 * q, :], preferred_element_type=F32)
    y += jnp.dot(c_ref[...], w_ref[2 * q:3 * q, :], preferred_element_type=F32)
    y += jnp.dot(d_ref[...], w_ref[3 * q:, :], preferred_element_type=F32)
    z = DEEPNORM_ALPHA * x_ref[...] + gate_ref[0] * y
    x1 = _ln(z) * g_ref[...] + bb_ref[...]
    x1_ref[...] = x1
    u2_ref[...] = (_ln(x1) * (1.0 + sc_ref[0]) + sh_ref[0]).astype(BF16)


def out_proj(x, parts, w, mod, ln_g, ln_b, row_of_tile, tm=512):
    m, d = x.shape
    pw = d // 4

    def mod_spec(kind):
        return pl.BlockSpec((1, 1, d), lambda i: (kind * MOD_ROWS + row_of_tile(i, tm), 0, 0))

    vec = pl.BlockSpec((1, d), lambda i: (0, 0))
    part = pl.BlockSpec((tm, pw), lambda i: (i, 0))
    return pl.pallas_call(
        _out_kernel,
        grid=(m // tm,),
        in_specs=[pl.BlockSpec((tm, d), lambda i: (i, 0)), part, part, part, part,
                  pl.BlockSpec((d, d), lambda i: (0, 0)),
                  mod_spec(2), vec, vec, mod_spec(4), mod_spec(3)],
        out_specs=[pl.BlockSpec((tm, d), lambda i: (i, 0)), pl.BlockSpec((tm, d), lambda i: (i, 0))],
        out_shape=[jax.ShapeDtypeStruct((m, d), F32), jax.ShapeDtypeStruct((m, d), BF16)],
        compiler_params=_cparams(("parallel",)),
        name="out_proj",
    )(x, *parts, w, mod, ln_g, ln_b, mod, mod)


def _mlp_kernel(u_ref, x_ref, w1_ref, w2_ref, gate_ref, g_ref, b_ref, o_ref, acc_ref):
    f = pl.program_id(1)
    h = jnp.dot(u_ref[...], w1_ref[...], preferred_element_type=F32)
    h = jnp.square(jnp.maximum(h, 0.0)).astype(BF16)
    part = jnp.dot(h, w2_ref[...], preferred_element_type=F32)

    @pl.when(f == 0)
    def _():
        acc_ref[...] = part

    @pl.when(f > 0)
    def _():
        acc_ref[...] += part

    @pl.when(f == pl.num_programs(1) - 1)
    def _():
        z = DEEPNORM_ALPHA * x_ref[...] + gate_ref[0] * acc_ref[...]
        o_ref[...] = _ln(z) * g_ref[...] + b_ref[...]


def mlp(u, x, w1, w2, mod, ln_g, ln_b, row_of_tile, tm=512, tf=1024):
    m, d = x.shape
    ff = w1.shape[1]
    vec = pl.BlockSpec((1, d), lambda i, f: (0, 0))
    return pl.pallas_call(
        _mlp_kernel,
        grid=(m // tm, ff // tf),
        in_specs=[pl.BlockSpec((tm, d), lambda i, f: (i, 0)),
                  pl.BlockSpec((tm, d), lambda i, f: (i, 0)),
                  pl.BlockSpec((d, tf), lambda i, f: (0, f)),
                  pl.BlockSpec((tf, d), lambda i, f: (f, 0)),
                  pl.BlockSpec((1, 1, d), lambda i, f: (5 * MOD_ROWS + row_of_tile(i, tm), 0, 0)),
                  vec, vec],
        out_specs=pl.BlockSpec((tm, d), lambda i, f: (i, 0)),
        out_shape=jax.ShapeDtypeStruct((m, d), F32),
        scratch_shapes=[pltpu.VMEM((tm, d), F32)],
        compiler_params=_cparams(("parallel", "arbitrary")),
        name="mlp",
    )(u, x, w1, w2, mod, ln_g, ln_b)


def _centred_conv(x, w, b=None):
    t = x.shape[1]
    xp = jnp.pad(x, ((0, 0), (1, 2), (0, 0)))
    y = sum(xp[:, j:j + t] * w[j] for j in range(CONV_W))
    if b is not None:
        y = y + b
    return y


def _linear_scan(a, bx, h0, reverse):
    idx = -1 if reverse else 0
    bx = bx.at[:, idx].add(a[:, idx] * h0)

    def comb(e1, e2):
        a1, b1 = e1
        a2, b2 = e2
        return a1 * a2, a2 * b1 + b2

    _, h = lax.associative_scan(comb, (a, bx), reverse=reverse, axis=1)
    return h


def _rglru_dir(xc, wa, ba, wx, bx, lam, h0, reverse):
    b, t, _ = xc.shape
    xb = xc.reshape(b, t, LRU_BLOCKS, LRU_BLOCK_W)
    r = jax.nn.sigmoid(jnp.einsum('btnc,ncd->btnd', xb, wa).reshape(b, t, LRU_WIDTH) + ba)
    i = jax.nn.sigmoid(jnp.einsum('btnc,ncd->btnd', xb, wx).reshape(b, t, LRU_WIDTH) + bx)
    log_a = -LRU_C * r * jax.nn.softplus(-lam)
    a = jnp.exp(log_a)
    inp = jnp.sqrt(-jnp.expm1(2.0 * log_a)) * (i * xc)
    h = _linear_scan(a, inp, h0, reverse)
    return h, (h[:, 0] if reverse else h[:, -1])


def rglru_group(lx, lg, p, h0):
    xc = _centred_conv(lx, p['lru_conv_w'], p['lru_conv_b'])
    h_f, hT_f = _rglru_dir(xc, p['lru_wa'][0], p['lru_ba'][0], p['lru_wx'][0], p['lru_bx'][0],
                           p['lru_lambda'][0], h0[:, 0], False)
    h_b, hT_b = _rglru_dir(xc, p['lru_wa'][1], p['lru_ba'][1], p['lru_wx'][1], p['lru_bx'][1],
                           p['lru_lambda'][1], h0[:, 1], True)
    out = (h_f + h_b) * jax.nn.gelu(lg)
    return out, jnp.stack([hT_f, hT_b], axis=1)


def _l2_norm(x):
    return x * lax.rsqrt(jnp.sum(x * x, axis=-1, keepdims=True) + EPS)


def _chunk_gated_delta(q, k, v, beta, g, s0):
    b, t, h, dk = q.shape
    dv = v.shape[-1]
    n = t // DN_CHUNK

    def to_chunks(a):
        a = jnp.moveaxis(a, 2, 1)
        return a.reshape(a.shape[:2] + (n, DN_CHUNK) + a.shape[3:])

    q = to_chunks(q * dk ** -0.5)
    k = to_chunks(k)
    v = to_chunks(v)
    beta = to_chunks(beta)
    gc = jnp.cumsum(to_chunks(g), axis=-1)
    incl = jnp.tril(jnp.ones((DN_CHUNK, DN_CHUNK), dtype=bool))
    strict = jnp.tril(jnp.ones((DN_CHUNK, DN_CHUNK), dtype=bool), -1)
    decay = jnp.exp(jnp.where(incl, gc[..., :, None] - gc[..., None, :], -jnp.inf))
    kb = k * beta[..., None]
    a_mat = jnp.where(strict, jnp.einsum('bhncd,bhnsd->bhncs', kb, k) * decay, 0.0)
    eye = jnp.eye(DN_CHUNK, dtype=F32)
    rhs = jnp.concatenate([v * beta[..., None], kb * jnp.exp(gc)[..., None]], axis=-1)
    sol = lax.linalg.triangular_solve(eye + a_mat, rhs, left_side=True, lower=True, unit_diagonal=True)
    u_val, w = sol[..., :dv], sol[..., dv:]
    qk = jnp.einsum('bhncd,bhnsd->bhncs', q, k) * decay

    def step(s, inp):
        qi, ki, ui, wi, qki, gi = inp
        v_new = ui - jnp.einsum('bhcd,bhde->bhce', wi, s)
        o = (jnp.einsum('bhcd,bhde->bhce', qi * jnp.exp(gi)[..., None], s)
             + jnp.einsum('bhcs,bhse->bhce', qki, v_new))
        g_last = gi[..., -1]
        s = (s * jnp.exp(g_last)[..., None, None]
             + jnp.einsum('bhcd,bhce->bhde', ki * jnp.exp(g_last[..., None] - gi)[..., None], v_new))
        return s, o

    xs = tuple(jnp.moveaxis(a, 2, 0) for a in (q, k, u_val, w, qk, gc))
    s_final, o = lax.scan(step, s0, xs)
    o = jnp.moveaxis(o, 0, 2).reshape(b, h, t, dv)
    return jnp.moveaxis(o, 1, 2), s_final


def deltanet_group(dq, dk, dv, dz, dbeta, dalpha, p, s0):
    b, t, _ = dq.shape
    qkv = jax.nn.silu(_centred_conv(jnp.concatenate([dq, dk, dv], axis=-1), p['dn_conv_w']))
    q, k, v = jnp.split(qkv, [DN_HEADS * DN_DK, 2 * DN_HEADS * DN_DK], axis=-1)
    q = _l2_norm(q.reshape(b, t, DN_HEADS, DN_DK))
    k = _l2_norm(k.reshape(b, t, DN_HEADS, DN_DK))
    v = v.reshape(b, t, DN_HEADS, DN_DV)
    beta = jax.nn.sigmoid(dbeta).reshape(b, t, 2, DN_HEADS)
    g = -jnp.exp(p['dn_a_log']) * jax.nn.softplus(dalpha.reshape(b, t, 2, DN_HEADS) + p['dn_dt_bias'])
    o_f, s_f = _chunk_gated_delta(q, k, v, beta[:, :, 0], g[:, :, 0], s0[:, 0])
    flip = lambda a: jnp.flip(a, axis=1)
    o_b, s_b = _chunk_gated_delta(flip(q), flip(k), flip(v), flip(beta[:, :, 1]), flip(g[:, :, 1]), s0[:, 1])
    o = o_f + flip(o_b)
    o = o * lax.rsqrt(jnp.mean(o * o, axis=-1, keepdims=True) + EPS) * p['dn_norm_g']
    o = o * jax.nn.silu(dz.reshape(b, t, DN_HEADS, DN_DV))
    return o.reshape(b, t, DN_WIDTH), jnp.stack([s_f, s_b], axis=1)


def _complex_scan(a_re, a_im, b_re, b_im, h0_re, h0_im, reverse):
    idx = -1 if reverse else 0
    b_re = b_re.at[:, idx].add(a_re * h0_re - a_im * h0_im)
    b_im = b_im.at[:, idx].add(a_re * h0_im + a_im * h0_re)
    ar = jnp.broadcast_to(a_re, b_re.shape)
    ai = jnp.broadcast_to(a_im, b_im.shape)

    def comb(e1, e2):
        ar1, ai1, br1, bi1 = e1
        ar2, ai2, br2, bi2 = e2
        return (ar1 * ar2 - ai1 * ai2, ar1 * ai2 + ai1 * ar2,
                ar2 * br1 - ai2 * bi1 + br2, ar2 * bi1 + ai2 * br1 + bi2)

    _, _, h_re, h_im = lax.associative_scan(comb, (ar, ai, b_re, b_im), reverse=reverse, axis=1)
    return h_re, h_im


def _s5_dir(u, lam_re, lam_im, log_dt, b_re, b_im, c_re, c_im, h0_re, h0_im, reverse):
    dt = jnp.exp(log_dt)[:, None]
    mag = jnp.exp(lam_re * dt)
    abar_re = mag * jnp.cos(lam_im * dt)
    abar_im = mag * jnp.sin(lam_im * dt)
    den = lam_re * lam_re + lam_im * lam_im
    nr = abar_re - 1.0
    ni = abar_im
    f_re = (nr * lam_re + ni * lam_im) / den
    f_im = (ni * lam_re - nr * lam_im) / den
    bb_re = f_re[..., None] * b_re - f_im[..., None] * b_im
    bb_im = f_re[..., None] * b_im + f_im[..., None] * b_re
    bu_re = jnp.einsum('btgh,gph->btgp', u, bb_re)
    bu_im = jnp.einsum('btgh,gph->btgp', u, bb_im)
    h_re, h_im = _complex_scan(abar_re, abar_im, bu_re, bu_im, h0_re, h0_im, reverse)
    y = jnp.einsum('btgp,ghp->btgh', h_re, c_re) - jnp.einsum('btgp,ghp->btgh', h_im, c_im)
    j = 0 if reverse else -1
    return y, h_re[:, j], h_im[:, j]


def s5_group(su, p, h0_re, h0_im):
    b, t, _ = su.shape
    u = su.reshape(b, t, S5_GROUPS, S5_CH)
    ys, fin_re, fin_im = [], [], []
    for d, rev in ((0, False), (1, True)):
        y, hr, hi = _s5_dir(u, p['s5_lambda_re'][d], p['s5_lambda_im'][d], p['s5_log_dt'][d],
                            p['s5_b_re'][d], p['s5_b_im'][d], p['s5_c_re'][d], p['s5_c_im'][d],
                            h0_re[:, d], h0_im[:, d], rev)
        ys.append(y)
        fin_re.append(hr)
        fin_im.append(hi)
    y = (ys[0] + ys[1]).reshape(b, t, S5_WIDTH) + p['s5_d'] * su
    gy = jax.nn.gelu(y)
    out = gy * jax.nn.sigmoid(gy @ p['s5_glu_w'] + p['s5_glu_b'])
    return out, jnp.stack(fin_re, axis=1), jnp.stack(fin_im, axis=1)


def _reorder_w_in(w):
    ba0 = COL_SU
    su0 = ba0 + 4 * DN_HEADS
    pad = jnp.zeros(w.shape[:2] + (N_PROJ - N_IN,), w.dtype)
    return jnp.concatenate([w[..., :ba0], w[..., su0:], w[..., ba0:su0], pad], axis=-1).astype(BF16)


def kernel(x_prompt, x_sample, cache_attn_k, cache_attn_v, state_rglru, state_delta, state_s5_re, state_s5_im, c, c_ctx, w_ada, b_ada, w_in, w_out, ln1_g, ln1_b, ln2_g, ln2_b, w_mlp1, w_mlp2, q_norm_g, k_norm_g, lru_conv_w, lru_conv_b, lru_wa, lru_ba, lru_wx, lru_bx, lru_lambda, dn_conv_w, dn_a_log, dn_dt_bias, dn_norm_g, s5_lambda_re, s5_lambda_im, s5_log_dt, s5_b_re, s5_b_im, s5_c_re, s5_c_im, s5_d, s5_glu_w, s5_glu_b):
    bp, tp, d = x_prompt.shape
    bs, ts, _ = x_sample.shape
    n_p = bp * tp
    n_s = bs * ts
    ctx_row = bs

    def row_of_tile(i, tm):
        tok = i * tm
        return jnp.where(tok < n_p, ctx_row, (tok - n_p) // ts)

    cond = jnp.concatenate([c, c_ctx[None, :], jnp.zeros((MOD_ROWS - bs - 1, d), F32)], axis=0)
    mods = ada_mod(cond, w_ada, b_ada)
    w_in_r = _reorder_w_in(w_in)
    w_out_b = w_out.astype(BF16)
    w1_b = w_mlp1.astype(BF16)
    w2_b = w_mlp2.astype(BF16)
    cos, sin = rope_tables(ts)

    x = jnp.concatenate([x_prompt.reshape(n_p, d), x_sample.reshape(n_s, d)], axis=0)
    ks, vs, lrus, dns, s5rs, s5is = [], [], [], [], [], []
    for l in range(DEPTH):
        mod = mods[l].reshape(MOD_ROWS, N_MOD, d).transpose(1, 0, 2).reshape(N_MOD * MOD_ROWS, 1, d)
        p = {'lru_conv_w': lru_conv_w[l], 'lru_conv_b': lru_conv_b[l], 'lru_wa': lru_wa[l], 'lru_ba': lru_ba[l],
             'lru_wx': lru_wx[l], 'lru_bx': lru_bx[l], 'lru_lambda': lru_lambda[l],
             'dn_conv_w': dn_conv_w[l], 'dn_a_log': dn_a_log[l], 'dn_dt_bias': dn_dt_bias[l],
             'dn_norm_g': dn_norm_g[l],
             's5_lambda_re': s5_lambda_re[l], 's5_lambda_im': s5_lambda_im[l], 's5_log_dt': s5_log_dt[l],
             's5_b_re': s5_b_re[l], 's5_b_im': s5_b_im[l], 's5_c_re': s5_c_re[l], 's5_c_im': s5_c_im[l],
             's5_d': s5_d[l], 's5_glu_w': s5_glu_w[l], 's5_glu_b': s5_glu_b[l]}
        proj = in_proj(x, mod, w_in_r[l], row_of_tile)

        qg = q_norm_g[l].reshape(1, HEAD_DIM)
        kg = k_norm_g[l].reshape(1, HEAD_DIM)
        attn_p, kn_p, v_p = ctx_attention(proj, qg, kg, bp, tp)
        ks.append(kn_p.reshape(bp, tp, ATTN_KV_HEADS, HEAD_DIM))
        vs.append(v_p.reshape(bp, tp, ATTN_KV_HEADS, HEAD_DIM))
        q_s, k_s, v_s = lat_prep(proj, qg, kg, cos, sin, n_p, n_s, ts)
        k_all = jnp.concatenate([cache_attn_k[:, l].reshape(bs, -1, KV_WIDTH).astype(BF16),
                                 k_s.reshape(bs, ts, KV_WIDTH)], axis=1)
        v_all = jnp.concatenate([cache_attn_v[:, l].reshape(bs, -1, KV_WIDTH).astype(BF16),
                                 v_s.reshape(bs, ts, KV_WIDTH)], axis=1)
        attn_s = lat_attention(q_s, k_all, v_all, ts)
        attn = jnp.concatenate([attn_p, attn_s], axis=0)

        parts = {'lru': [], 'dn': [], 's5': []}
        for (r0, b, t, is_ctx) in ((0, bp, tp, True), (n_p, bs, ts, False)):
            pr = proj[r0:r0 + b * t].reshape(b, t, N_PROJ)
            col = lambda c0, w: pr[:, :, c0:c0 + w]
            if is_ctx:
                h0_lru = jnp.zeros((b, 2, LRU_WIDTH), F32)
                s0_dn = jnp.zeros((b, 2, DN_HEADS, DN_DK, DN_DV), F32)
                h0_s5r = jnp.zeros((b, 2, S5_GROUPS, S5_STATE), F32)
                h0_s5i = jnp.zeros((b, 2, S5_GROUPS, S5_STATE), F32)
            else:
                h0_lru = state_rglru[:, l]
                s0_dn = state_delta[:, l]
                h0_s5r = state_s5_re[:, l]
                h0_s5i = state_s5_im[:, l]
            lru_out, lru_fin = rglru_group(col(COL_LX, 512), col(COL_LG, 512), p, h0_lru)
            dn_out, dn_fin = deltanet_group(col(COL_DQ, 512), col(COL_DK, 512), col(COL_DV, 512), col(COL_DZ, 512),
                                            col(COL_BA, 8), col(COL_BA + 8, 8), p, s0_dn)
            s5_out, s5r_fin, s5i_fin = s5_group(col(COL_SU, 512), p, h0_s5r, h0_s5i)
            parts['lru'].append(lru_out.reshape(b * t, -1).astype(BF16))
            parts['dn'].append(dn_out.reshape(b * t, -1).astype(BF16))
            parts['s5'].append(s5_out.reshape(b * t, -1).astype(BF16))
            if is_ctx:
                lrus.append(lru_fin)
                dns.append(dn_fin)
                s5rs.append(s5r_fin)
                s5is.append(s5i_fin)
        mix_parts = [attn] + [jnp.concatenate(parts[n], axis=0) for n in ('lru', 'dn', 's5')]

        x1, u2 = out_proj(x, mix_parts, w_out_b[l], mod, ln1_g[l].reshape(1, d), ln1_b[l].reshape(1, d),
                          row_of_tile)
        x = mlp(u2, x1, w1_b[l], w2_b[l], mod, ln2_g[l].reshape(1, d), ln2_b[l].reshape(1, d), row_of_tile)

    y_prompt = x[:n_p].reshape(bp, tp, d)
    y_sample = x[n_p:].reshape(bs, ts, d)
    return (y_prompt, y_sample, jnp.stack(ks, axis=1), jnp.stack(vs, axis=1), jnp.stack(lrus, axis=1),
            jnp.stack(dns, axis=1), jnp.stack(s5rs, axis=1), jnp.stack(s5is, axis=1))
```

```python
import functools

import jax
import jax.numpy as jnp
from jax import lax
from jax.experimental import pallas as pl
from jax.experimental.pallas import tpu as pltpu

F32 = jnp.float32
BF16 = jnp.bfloat16

D_MODEL = 2048
DEPTH = 2
GRID_W = 64
CONV_W = 4
EPS = 1e-6
ROPE_THETA = 10000.0
N_MOD = 6
HEAD_DIM = 128
ATTN_WIDTH = D_MODEL // 4
ATTN_HEADS = ATTN_WIDTH // HEAD_DIM
ATTN_KV_HEADS = ATTN_HEADS // 2
ATTN_GROUP = ATTN_HEADS // ATTN_KV_HEADS
KV_WIDTH = ATTN_KV_HEADS * HEAD_DIM
ATTN_SCALE = HEAD_DIM ** -0.5
LRU_WIDTH = D_MODEL // 4
LRU_BLOCKS = 8
LRU_C = 8.0
DN_DK = 128
DN_DV = 128
DN_WIDTH = D_MODEL // 4
DN_HEADS = DN_WIDTH // DN_DV
DN_CHUNK = 64
S5_WIDTH = D_MODEL // 4
S5_CH = 16
S5_GROUPS = S5_WIDTH // S5_CH
S5_STATE = 64
S5_NSTATE = S5_GROUPS * S5_STATE
DEEPNORM_ALPHA = (2 * DEPTH) ** 0.25
MIXW = D_MODEL // 4

COL_DQ = 0
COL_DZ = 1536
COL_Q = 2048
COL_KV = 2560
COL_LX = 3072
COL_LG = 3584
COL_SU = 4096
COL_BA = 4608
N_PROJ = 4864
LANE = 128
MXU = 256

MOD_ROWS = 8
PITCH_PAD = 4
VMEM_LIMIT = 56 * 1024 * 1024


def _cparams(sem):
    return pltpu.CompilerParams(dimension_semantics=sem, vmem_limit_bytes=VMEM_LIMIT)


def _ln(x):
    mu = jnp.mean(x, axis=-1, keepdims=True)
    xc = x - mu
    var = jnp.mean(xc * xc, axis=-1, keepdims=True)
    return xc * lax.rsqrt(var + EPS)


def _softplus(x):
    return jnp.maximum(x, 0.0) + jnp.log1p(jnp.exp(-jnp.abs(x)))


def _gelu(x):
    return 0.5 * x * (1.0 + jnp.tanh(0.7978845608028654 * (x + 0.044715 * (x * x * x))))


def _dot(a, b):
    return jnp.dot(a, b, preferred_element_type=F32)


def _dot_nt(a, b):
    return lax.dot_general(a, b, (((1,), (1,)), ((), ())), preferred_element_type=F32)


def _ada_kernel(c_ref, w_ref, b_ref, o_ref):
    cs = c_ref[...]
    s = cs * jax.nn.sigmoid(cs)
    o_ref[0] = _dot(s.astype(BF16), w_ref[0].astype(BF16)) + b_ref[0]


def ada_mod(cond, w_ada, b_ada, tn=1024):
    depth, d, n = w_ada.shape
    return pl.pallas_call(
        _ada_kernel,
        grid=(depth, n // tn),
        in_specs=[pl.BlockSpec((MOD_ROWS, d), lambda l, j: (0, 0)),
                  pl.BlockSpec((1, d, tn), lambda l, j: (l, 0, j)),
                  pl.BlockSpec((1, 1, tn), lambda l, j: (l, 0, j))],
        out_specs=pl.BlockSpec((1, MOD_ROWS, tn), lambda l, j: (l, 0, j)),
        out_shape=jax.ShapeDtypeStruct((depth, MOD_ROWS, n), F32),
        compiler_params=_cparams(("parallel", "parallel")),
        name="ada_mod",
    )(cond, w_ada, b_ada.reshape(depth, 1, n))


def _in_kernel(x_ref, sc_ref, sh_ref, w_ref, o_ref, u_ref):
    @pl.when(pl.program_id(1) == 0)
    def _():
        u = _ln(x_ref[...]) * (1.0 + sc_ref[0]) + sh_ref[0]
        u_ref[...] = u.astype(BF16)

    o_ref[...] = _dot(u_ref[...], w_ref[...])


def in_proj(x, mod, w, mod_row, tm=1024, tn=256):
    m, d = x.shape
    n = w.shape[1]

    def mod_spec(kind):
        return pl.BlockSpec((1, 1, d), lambda i, j: (kind * MOD_ROWS + mod_row(i * tm), 0, 0))

    return pl.pallas_call(
        _in_kernel,
        grid=(m // tm, n // tn),
        in_specs=[pl.BlockSpec((tm, d), lambda i, j: (i, 0)),
                  mod_spec(1), mod_spec(0),
                  pl.BlockSpec((d, tn), lambda i, j: (0, j))],
        out_specs=pl.BlockSpec((tm, tn), lambda i, j: (i, j)),
        out_shape=jax.ShapeDtypeStruct((m, n), F32),
        scratch_shapes=[pltpu.VMEM((tm, d), BF16)],
        compiler_params=_cparams(("parallel", "arbitrary")),
        name="in_proj",
    )(x, mod, mod, w)


def _rms_heads(x, g, heads):
    outs = []
    for h in range(heads):
        xh = x[:, h * HEAD_DIM:(h + 1) * HEAD_DIM]
        outs.append(xh * lax.rsqrt(jnp.mean(xh * xh, axis=-1, keepdims=True) + EPS) * g)
    return outs


def _softmax_av(q, k, v):
    s = _dot_nt(q, k) * ATTN_SCALE
    m = jnp.max(s, axis=-1, keepdims=True)
    p = jnp.exp(s - m)
    l = jnp.sum(p, axis=-1, keepdims=True)
    return _dot(p.astype(BF16), v) / l


def _ctx_attn_kernel(q_ref, kv_ref, qg_ref, kg_ref, o_ref, kn_ref, v_ref):
    qs = _rms_heads(q_ref[...], qg_ref[...], ATTN_HEADS)
    kv = kv_ref[...]
    ks = _rms_heads(kv[:, :KV_WIDTH], kg_ref[...], ATTN_KV_HEADS)
    v = kv[:, KV_WIDTH:]
    v_ref[...] = v
    t = q_ref.shape[0]
    for kh in range(ATTN_KV_HEADS):
        kn_ref[:, kh * HEAD_DIM:(kh + 1) * HEAD_DIM] = ks[kh]
        q2 = jnp.concatenate([qs[kh * ATTN_GROUP + g] for g in range(ATTN_GROUP)], axis=0).astype(BF16)
        o = _softmax_av(q2, ks[kh].astype(BF16), v[:, kh * HEAD_DIM:(kh + 1) * HEAD_DIM].astype(BF16))
        for g in range(ATTN_GROUP):
            h = kh * ATTN_GROUP + g
            o_ref[:, h * HEAD_DIM:(h + 1) * HEAD_DIM] = o[g * t:(g + 1) * t].astype(BF16)


def ctx_attention(proj, qg, kg, batch, seq):
    return pl.pallas_call(
        _ctx_attn_kernel,
        grid=(batch,),
        in_specs=[pl.BlockSpec((seq, ATTN_WIDTH), lambda b: (b, COL_Q // ATTN_WIDTH)),
                  pl.BlockSpec((seq, 2 * KV_WIDTH), lambda b: (b, COL_KV // (2 * KV_WIDTH))),
                  pl.BlockSpec((1, HEAD_DIM), lambda b: (0, 0)),
                  pl.BlockSpec((1, HEAD_DIM), lambda b: (0, 0))],
        out_specs=[pl.BlockSpec((seq, ATTN_WIDTH), lambda b: (b, 0)),
                   pl.BlockSpec((seq, KV_WIDTH), lambda b: (b, 0)),
                   pl.BlockSpec((seq, KV_WIDTH), lambda b: (b, 0))],
        out_shape=[jax.ShapeDtypeStruct((batch * seq, ATTN_WIDTH), BF16),
                   jax.ShapeDtypeStruct((batch * seq, KV_WIDTH), F32),
                   jax.ShapeDtypeStruct((batch * seq, KV_WIDTH), F32)],
        compiler_params=_cparams(("parallel",)),
        name="ctx_attention",
    )(proj, proj, qg, kg)


def _rope(x, cos, sin, heads):
    w = x.shape[-1]
    lane = lax.broadcasted_iota(jnp.int32, x.shape, 1)
    quarter = HEAD_DIM // 4
    partner = jnp.where((lane % (2 * quarter)) < quarter,
                        pltpu.roll(x, w - quarter, 1), pltpu.roll(x, quarter, 1))
    cos_t = jnp.concatenate([cos] * heads, axis=1)
    sin_t = jnp.concatenate([sin] * heads, axis=1)
    return x * cos_t + partner * sin_t


def _lat_prep_kernel(q_ref, kv_ref, qg_ref, kg_ref, cos_ref, sin_ref, qo_ref, ko_ref, vo_ref):
    qn = jnp.concatenate(_rms_heads(q_ref[...], qg_ref[...], ATTN_HEADS), axis=1)
    kv = kv_ref[...]
    kn = jnp.concatenate(_rms_heads(kv[:, :KV_WIDTH], kg_ref[...], ATTN_KV_HEADS), axis=1)
    cos = cos_ref[...]
    sin = sin_ref[...]
    qo_ref[...] = _rope(qn, cos, sin, ATTN_HEADS).astype(BF16)
    ko_ref[...] = _rope(kn, cos, sin, ATTN_KV_HEADS).astype(BF16)
    vo_ref[...] = kv[:, KV_WIDTH:].astype(BF16)


def lat_prep(proj, qg, kg, cos, sin, seq, tm=512):
    m = proj.shape[0]
    per = seq // tm
    return pl.pallas_call(
        _lat_prep_kernel,
        grid=(m // tm,),
        in_specs=[pl.BlockSpec((tm, ATTN_WIDTH), lambda i: (i, COL_Q // ATTN_WIDTH)),
                  pl.BlockSpec((tm, 2 * KV_WIDTH), lambda i: (i, COL_KV // (2 * KV_WIDTH))),
                  pl.BlockSpec((1, HEAD_DIM), lambda i: (0, 0)),
                  pl.BlockSpec((1, HEAD_DIM), lambda i: (0, 0)),
                  pl.BlockSpec((tm, HEAD_DIM), lambda i: (i % per, 0)),
                  pl.BlockSpec((tm, HEAD_DIM), lambda i: (i % per, 0))],
        out_specs=[pl.BlockSpec((tm, ATTN_WIDTH), lambda i: (i, 0)),
                   pl.BlockSpec((tm, KV_WIDTH), lambda i: (i, 0)),
                   pl.BlockSpec((tm, KV_WIDTH), lambda i: (i, 0))],
        out_shape=[jax.ShapeDtypeStruct((m, ATTN_WIDTH), BF16),
                   jax.ShapeDtypeStruct((m, KV_WIDTH), BF16),
                   jax.ShapeDtypeStruct((m, KV_WIDTH), BF16)],
        compiler_params=_cparams(("parallel",)),
        name="lat_prep",
    )(proj, proj, qg, kg, cos, sin)


def _lat_attn_kernel(q_ref, k_ref, v_ref, o_ref):
    k = k_ref[0]
    v = v_ref[0]
    for g in range(ATTN_GROUP):
        sl = slice(g * HEAD_DIM, (g + 1) * HEAD_DIM)
        o_ref[:, sl] = _softmax_av(q_ref[:, sl], k, v).astype(BF16)


def lat_attention(q, k_all, v_all, seq, tq=256):
    b, s, _ = k_all.shape
    nq = seq // tq
    gw = ATTN_GROUP * HEAD_DIM
    return pl.pallas_call(
        _lat_attn_kernel,
        grid=(b, ATTN_KV_HEADS, nq),
        in_specs=[pl.BlockSpec((tq, gw), lambda bi, kh, qi: (bi * nq + qi, kh)),
                  pl.BlockSpec((1, s, HEAD_DIM), lambda bi, kh, qi: (bi, 0, kh)),
                  pl.BlockSpec((1, s, HEAD_DIM), lambda bi, kh, qi: (bi, 0, kh))],
        out_specs=pl.BlockSpec((tq, gw), lambda bi, kh, qi: (bi * nq + qi, kh)),
        out_shape=jax.ShapeDtypeStruct((b * seq, ATTN_WIDTH), BF16),
        compiler_params=_cparams(("parallel", "parallel", "arbitrary")),
        name="lat_attention",
    )(q, k_all, v_all)


def rope_tables(seq):
    t = jnp.arange(seq)
    row = (t // GRID_W).astype(F32)
    col = (t % GRID_W).astype(F32)
    quarter = HEAD_DIM // 4
    inv_freq = jnp.power(ROPE_THETA, -jnp.arange(quarter, dtype=F32) / quarter)
    ar = row[:, None] * inv_freq[None, :]
    ac = col[:, None] * inv_freq[None, :]
    cos = jnp.concatenate([jnp.cos(ar), jnp.cos(ar), jnp.cos(ac), jnp.cos(ac)], axis=1)
    sin = jnp.concatenate([-jnp.sin(ar), jnp.sin(ar), -jnp.sin(ac), jnp.sin(ac)], axis=1)
    return cos, sin


def _chunk_index(d, i, n):
    return i + d * (n - 1 - 2 * i)


def _lru_kernel(x_ref, xp_ref, xn_ref, cw_ref, cb_ref, wg_ref, bg_ref, sp_ref, h0_ref,
                h_ref, fin_ref, xpad_ref, a_ref, b_ref, hc_ref, *, nb, tt):
    d = pl.program_id(0)
    i = pl.program_id(2)
    n = pl.num_programs(2)
    ci = _chunk_index(d, i, n)
    pitch = tt + PITCH_PAD
    rows = nb * pitch
    nsl = LRU_WIDTH // LANE

    @pl.when(i == 0)
    def _():
        xpad_ref[...] = jnp.zeros_like(xpad_ref)
        hc_ref[...] = h0_ref[0]

    for s in range(nb):
        base = 8 + s * pitch
        xpad_ref[base:base + tt, :] = x_ref[s]
        xpad_ref[base - 1:base, :] = jnp.where(ci == 0, 0.0, xp_ref[s, 7:8, :])
        xpad_ref[base + tt:base + tt + 2, :] = jnp.where(ci == n - 1, 0.0, xn_ref[s, 0:2, :])
    w = cw_ref[...]
    xc = cb_ref[...] + sum(xpad_ref[7 + j:7 + j + rows, :] * w[j:j + 1, :] for j in range(CONV_W))
    pre = _dot(xc.astype(BF16), wg_ref[0]) + bg_ref[0]
    r = jax.nn.sigmoid(pre[:, :LRU_WIDTH])
    ig = jax.nn.sigmoid(pre[:, LRU_WIDTH:])
    a = jnp.exp((-LRU_C) * r * sp_ref[0])
    inp = jnp.sqrt(1.0 - a * a) * (ig * xc)
    for c in range(nsl):
        a_ref[c] = a[:, c * LANE:(c + 1) * LANE]
        b_ref[c] = inp[:, c * LANE:(c + 1) * LANE]

    def body(t, carry):
        row = t + d * (tt - 1 - 2 * t)
        out = []
        for c in range(nsl):
            idx = (c, pl.ds(row, nb, stride=pitch), slice(None))
            h = a_ref[idx] * carry[c] + b_ref[idx]
            b_ref[idx] = h
            out.append(h)
        return tuple(out)

    carry = tuple(hc_ref[:, c * LANE:(c + 1) * LANE] for c in range(nsl))
    carry = lax.fori_loop(0, tt, body, carry, unroll=8)
    for c in range(nsl):
        hc_ref[:, c * LANE:(c + 1) * LANE] = carry[c]
        for s in range(nb):
            h_ref[0, s, :, c * LANE:(c + 1) * LANE] = b_ref[c, s * pitch:s * pitch + tt, :]

    @pl.when(i == n - 1)
    def _():
        fin_ref[0] = hc_ref[...]


def lru_scan(proj3, h0, cw, cb, wg, bg, sp, nb, tt=128):
    b, t, _ = proj3.shape
    w = LRU_WIDTH
    n = t // tt
    t8 = tt // 8
    rows = nb * (tt + PITCH_PAD)
    col = COL_LX // w

    def cidx(d, i):
        return _chunk_index(d, i, n)

    return pl.pallas_call(
        functools.partial(_lru_kernel, nb=nb, tt=tt),
        grid=(2, b // nb, n),
        in_specs=[pl.BlockSpec((nb, tt, w), lambda d, g, i: (g, cidx(d, i), col)),
                  pl.BlockSpec((nb, 8, w), lambda d, g, i: (g, jnp.maximum(cidx(d, i) * t8 - 1, 0), col)),
                  pl.BlockSpec((nb, 8, w), lambda d, g, i: (g, jnp.minimum((cidx(d, i) + 1) * t8, t // 8 - 1), col)),
                  pl.BlockSpec((CONV_W, w), lambda d, g, i: (0, 0)),
                  pl.BlockSpec((1, w), lambda d, g, i: (0, 0)),
                  pl.BlockSpec((1, w, 2 * w), lambda d, g, i: (d, 0, 0)),
                  pl.BlockSpec((1, 1, 2 * w), lambda d, g, i: (d, 0, 0)),
                  pl.BlockSpec((1, 1, w), lambda d, g, i: (d, 0, 0)),
                  pl.BlockSpec((1, nb, w), lambda d, g, i: (d, g, 0))],
        out_specs=[pl.BlockSpec((1, nb, tt, w), lambda d, g, i: (d, g, cidx(d, i), 0)),
                   pl.BlockSpec((1, nb, w), lambda d, g, i: (d, g, 0))],
        out_shape=[jax.ShapeDtypeStruct((2, b, t, w), F32), jax.ShapeDtypeStruct((2, b, w), F32)],
        scratch_shapes=[pltpu.VMEM((rows + 16, w), F32),
                        pltpu.VMEM((w // LANE, rows, LANE), F32),
                        pltpu.VMEM((w // LANE, rows, LANE), F32),
                        pltpu.VMEM((nb, w), F32)],
        compiler_params=_cparams(("parallel", "parallel", "arbitrary")),
        name="lru_scan",
    )(proj3, proj3, proj3, cw, cb, wg, bg, sp, h0)


def _s5_kernel(u_ref, bb_ref, cc_ref, a_ref, h0_ref, y_ref, fin_ref, up_ref, s_ref, hc_ref, *, nb, tt):
    d = pl.program_id(0)
    i = pl.program_id(2)
    n = pl.num_programs(2)
    pitch = tt + PITCH_PAD
    nsl = S5_NSTATE // LANE

    @pl.when(i == 0)
    def _():
        up_ref[...] = jnp.zeros_like(up_ref)
        hc_ref[...] = h0_ref[0]

    for s in range(nb):
        up_ref[s * pitch:s * pitch + tt, :] = u_ref[s]
    u2 = up_ref[...].astype(BF16)
    per_k = MXU // S5_CH * S5_STATE // MXU
    for nt in range(2 * S5_NSTATE // MXU):
        kt = (nt % (S5_NSTATE // MXU)) // per_k
        tile = _dot(u2[:, kt * MXU:(kt + 1) * MXU], bb_ref[0, kt * MXU:(kt + 1) * MXU, nt * MXU:(nt + 1) * MXU])
        s_ref[2 * nt] = tile[:, :LANE]
        s_ref[2 * nt + 1] = tile[:, LANE:]

    a_all = a_ref[0]
    group = 4
    for c0 in range(0, nsl, group):
        cs = list(range(c0, c0 + group))
        ar = [jnp.broadcast_to(a_all[:, c * LANE:(c + 1) * LANE], (nb, LANE)) for c in cs]
        ai = [jnp.broadcast_to(a_all[:, S5_NSTATE + c * LANE:S5_NSTATE + (c + 1) * LANE], (nb, LANE)) for c in cs]

        def body(t, carry, cs=cs, ar=ar, ai=ai):
            row = t + d * (tt - 1 - 2 * t)
            out = []
            for j, c in enumerate(cs):
                hr, hi = carry[2 * j], carry[2 * j + 1]
                ire = (c, pl.ds(row, nb, stride=pitch), slice(None))
                iim = (nsl + c, pl.ds(row, nb, stride=pitch), slice(None))
                nr = ar[j] * hr - ai[j] * hi + s_ref[ire]
                ni = ar[j] * hi + ai[j] * hr + s_ref[iim]
                s_ref[ire] = nr
                s_ref[iim] = ni
                out += [nr, ni]
            return tuple(out)

        carry = []
        for c in cs:
            carry += [hc_ref[:, c * LANE:(c + 1) * LANE], hc_ref[:, S5_NSTATE + c * LANE:S5_NSTATE + (c + 1) * LANE]]
        carry = lax.fori_loop(0, tt, body, tuple(carry), unroll=4)
        for j, c in enumerate(cs):
            hc_ref[:, c * LANE:(c + 1) * LANE] = carry[2 * j]
            hc_ref[:, S5_NSTATE + c * LANE:S5_NSTATE + (c + 1) * LANE] = carry[2 * j + 1]

    n_out = S5_WIDTH // MXU
    per_n = nsl // n_out
    for nt in range(n_out):
        slabs = [per_n * nt + k for k in range(per_n)] + [nsl + per_n * nt + k for k in range(per_n)]
        lhs = jnp.concatenate([s_ref[c] for c in slabs], axis=1).astype(BF16)
        y = _dot(lhs, cc_ref[0, nt])
        for s in range(nb):
            y_ref[0, s, :, nt * MXU:(nt + 1) * MXU] = y[s * pitch:s * pitch + tt]

    @pl.when(i == n - 1)
    def _():
        fin_ref[0] = hc_ref[...]


def s5_scan(proj3, h0, bb, cc, a, nb, tt=128):
    b, t, _ = proj3.shape
    w = S5_WIDTH
    n = t // tt
    rows = nb * (tt + PITCH_PAD)
    ns2 = 2 * S5_NSTATE

    def cidx(d, i):
        return _chunk_index(d, i, n)

    return pl.pallas_call(
        functools.partial(_s5_kernel, nb=nb, tt=tt),
        grid=(2, b // nb, n),
        in_specs=[pl.BlockSpec((nb, tt, w), lambda d, g, i: (g, cidx(d, i), COL_SU // w)),
                  pl.BlockSpec((1, w, ns2), lambda d, g, i: (d, 0, 0)),
                  pl.BlockSpec((1,) + cc.shape[1:], lambda d, g, i: (d, 0, 0, 0)),
                  pl.BlockSpec((1, 1, ns2), lambda d, g, i: (d, 0, 0)),
                  pl.BlockSpec((1, nb, ns2), lambda d, g, i: (d, g, 0))],
        out_specs=[pl.BlockSpec((1, nb, tt, w), lambda d, g, i: (d, g, cidx(d, i), 0)),
                   pl.BlockSpec((1, nb, ns2), lambda d, g, i: (d, g, 0))],
        out_shape=[jax.ShapeDtypeStruct((2, b, t, w), F32), jax.ShapeDtypeStruct((2, b, ns2), F32)],
        scratch_shapes=[pltpu.VMEM((rows, w), F32),
                        pltpu.VMEM((ns2 // LANE, rows, LANE), F32),
                        pltpu.VMEM((nb, ns2), F32)],
        compiler_params=_cparams(("parallel", "parallel", "arbitrary")),
        name="s5_scan",
    )(proj3, bb, cc, a, h0)


def _split3(x):
    x1 = x.astype(BF16)
    r1 = x - x1.astype(F32)
    x2 = r1.astype(BF16)
    x3 = (r1 - x2.astype(F32)).astype(BF16)
    return x1, x2, x3


def _unit_tri_inverse(a, rr, cc):
    def mm(x, y):
        return _dot(x.astype(BF16), y.astype(BF16))

    same16 = (rr // 16) == (cc // 16)
    same32 = (rr // 32) == (cc // 32)
    nd = jnp.where(same16, -a, 0.0)
    t = jnp.where(rr == cc, 1.0, nd)
    p = nd
    for _ in range(3):
        p = mm(p, p)
        t = t + mm(t, p)
    e = jnp.where(same32 & jnp.logical_not(same16), a, 0.0)
    t = t - mm(mm(t, e), t)
    e = jnp.where(same32, 0.0, a)
    t = t - mm(mm(t, e), t)
    return t


def _dn_kernel(x_ref, xp_ref, xn_ref, ba_ref, cw_ref, al_ref, dtb_ref, s0_ref, o_ref, fin_ref,
               xpad_ref, s_ref, *, tb, direction):
    i = pl.program_id(1)
    n = pl.num_programs(1)
    reverse = direction == 1
    ci = (n - 1 - i) if reverse else i
    ch = DN_CHUNK
    nc = tb // ch
    hw = DN_HEADS * DN_DK

    @pl.when(i == 0)
    def _():
        s_ref[...] = s0_ref[0]

    xpad_ref[8:8 + tb, :] = x_ref[0]
    xpad_ref[0:8, :] = jnp.where(ci == 0, 0.0, xp_ref[0])
    xpad_ref[8 + tb:16 + tb, :] = jnp.where(ci == n - 1, 0.0, xn_ref[0])
    w = cw_ref[...]
    xc = sum(xpad_ref[7 + j:7 + j + tb, :] * w[j:j + 1, :] for j in range(CONV_W))
    qkv = xc * jax.nn.sigmoid(xc)

    ba = ba_ref[0]
    beta_all = jax.nn.sigmoid(ba)
    g_all = -jnp.exp(al_ref[...]) * _softplus(ba + dtb_ref[...])
    rb = lax.broadcasted_iota(jnp.int32, (tb, tb), 0)
    cb = lax.broadcasted_iota(jnp.int32, (tb, tb), 1)
    tri = ((rb // ch) == (cb // ch)) & ((cb >= rb) if reverse else (cb <= rb))
    tri = jnp.where(tri, 1.0, 0.0).astype(BF16)
    gc_all = sum(_dot(tri, piece) for piece in _split3(g_all))

    rr = lax.broadcasted_iota(jnp.int32, (ch, ch), 0)
    cc = lax.broadcasted_iota(jnp.int32, (ch, ch), 1)
    incl = (cc >= rr) if reverse else (cc <= rr)
    strict = (cc > rr) if reverse else (cc < rr)

    for c in (range(nc - 1, -1, -1) if reverse else range(nc)):
        r0 = c * ch
        gct = gc_all[r0:r0 + ch, :].T
        for h in range(DN_HEADS):
            lane = direction * DN_HEADS + h
            beta = beta_all[r0:r0 + ch, lane:lane + 1]
            gcol = gc_all[r0:r0 + ch, 2 * DN_HEADS + lane:2 * DN_HEADS + lane + 1]
            grow = gct[2 * DN_HEADS + lane:2 * DN_HEADS + lane + 1, :]
            decay = jnp.where(incl, jnp.exp(gcol - grow), 0.0)
            q = qkv[r0:r0 + ch, h * DN_DK:(h + 1) * DN_DK]
            k = qkv[r0:r0 + ch, hw + h * DN_DK:hw + (h + 1) * DN_DK]
            v = qkv[r0:r0 + ch, 2 * hw + h * DN_DV:2 * hw + (h + 1) * DN_DV]
            q = q * lax.rsqrt(jnp.sum(q * q, axis=-1, keepdims=True) + EPS) * (DN_DK ** -0.5)
            k = k * lax.rsqrt(jnp.sum(k * k, axis=-1, keepdims=True) + EPS)
            kb = k * beta
            kbf = k.astype(BF16)
            a_mat = jnp.where(strict, _dot_nt(kb.astype(BF16), kbf) * decay, 0.0)
            qk = _dot_nt(q.astype(BF16), kbf) * decay
            t_inv = _unit_tri_inverse(a_mat, rr, cc)
            t_off = jnp.where(rr == cc, 0.0, t_inv)
            rhs = jnp.concatenate([v * beta, kb * jnp.exp(gcol)], axis=1)
            sol = rhs + _dot(t_off.astype(BF16), rhs.astype(BF16))
            u_val = sol[:, :DN_DV]
            wv = sol[:, DN_DV:]
            s_old = s_ref[h]
            sb = s_old.astype(BF16)
            v_new = u_val - _dot(wv.astype(BF16), sb)
            o = _dot((q * jnp.exp(gcol)).astype(BF16), sb) + _dot(qk.astype(BF16), v_new.astype(BF16))
            g_last = gcol[0:1, :] if reverse else gcol[ch - 1:ch, :]
            kdec = k * jnp.exp(g_last - gcol)
            s_ref[h] = s_old * jnp.exp(g_last) + _dot(kdec.T.astype(BF16), v_new.astype(BF16))
            o_ref[0, r0:r0 + ch, h * DN_DV:(h + 1) * DN_DV] = o

    @pl.when(i == n - 1)
    def _():
        fin_ref[0] = s_ref[...]


def dn_dir(proj3, s0, cw, al, dtb, direction, tb=128):
    b, t, _ = proj3.shape
    n = t // tb
    t8 = tb // 8
    w3 = 3 * DN_WIDTH

    def cidx(i):
        return (n - 1 - i) if direction == 1 else i

    return pl.pallas_call(
        functools.partial(_dn_kernel, tb=tb, direction=direction),
        grid=(b, n),
        in_specs=[pl.BlockSpec((1, tb, w3), lambda bi, i: (bi, cidx(i), COL_DQ // w3)),
                  pl.BlockSpec((1, 8, w3), lambda bi, i: (bi, jnp.maximum(cidx(i) * t8 - 1, 0), COL_DQ // w3)),
                  pl.BlockSpec((1, 8, w3), lambda bi, i: (bi, jnp.minimum((cidx(i) + 1) * t8, t // 8 - 1), COL_DQ // w3)),
                  pl.BlockSpec((1, tb, LANE), lambda bi, i: (bi, cidx(i), COL_BA // LANE)),
                  pl.BlockSpec((CONV_W, w3), lambda bi, i: (0, 0)),
                  pl.BlockSpec((1, LANE), lambda bi, i: (0, 0)),
                  pl.BlockSpec((1, LANE), lambda bi, i: (0, 0)),
                  pl.BlockSpec((1, DN_HEADS, DN_DK, DN_DV), lambda bi, i: (bi, 0, 0, 0))],
        out_specs=[pl.BlockSpec((1, tb, DN_WIDTH), lambda bi, i: (bi, cidx(i), 0)),
                   pl.BlockSpec((1, DN_HEADS, DN_DK, DN_DV), lambda bi, i: (bi, 0, 0, 0))],
        out_shape=[jax.ShapeDtypeStruct((b, t, DN_WIDTH), F32),
                   jax.ShapeDtypeStruct((b, DN_HEADS, DN_DK, DN_DV), F32)],
        scratch_shapes=[pltpu.VMEM((tb + 16, w3), F32),
                        pltpu.VMEM((DN_HEADS, DN_DK, DN_DV), F32)],
        compiler_params=_cparams(("parallel", "arbitrary")),
        name=f"dn_dir{direction}",
    )(proj3, proj3, proj3, proj3, cw, al, dtb, s0)


def _post_kernel(hf_ref, hb_ref, lg_ref, of_ref, ob_ref, dz_ref, ng_ref, yf_ref, yb_ref, su_ref, sd_ref,
                 gw_ref, gb_ref, lru_ref, dn_ref, s5_ref):
    lru_ref[...] = ((hf_ref[0] + hb_ref[0]) * _gelu(lg_ref[...])).astype(BF16)
    o = of_ref[...] + ob_ref[...]
    dz = dz_ref[...]
    for h in range(DN_HEADS):
        sl = slice(h * DN_DV, (h + 1) * DN_DV)
        oh = o[:, sl]
        oh = oh * lax.rsqrt(jnp.mean(oh * oh, axis=-1, keepdims=True) + EPS) * ng_ref[...]
        zh = dz[:, sl]
        dn_ref[:, sl] = (oh * (zh * jax.nn.sigmoid(zh))).astype(BF16)
    y = yf_ref[0] + yb_ref[0] + sd_ref[...] * su_ref[...]
    gy = _gelu(y)
    s5_ref[...] = (gy * jax.nn.sigmoid(_dot(gy.astype(BF16), gw_ref[...]) + gb_ref[...])).astype(BF16)


def mix_post(proj, h_lru, o_f, o_b, y_s5, ng, sd, gw, gb, tm=512):
    m = proj.shape[0]
    w = MIXW
    tok = pl.BlockSpec((tm, w), lambda i: (i, 0))

    def pcol(c0):
        return pl.BlockSpec((tm, w), lambda i: (i, c0 // w))

    def dirspec(d):
        return pl.BlockSpec((1, tm, w), lambda i: (d, i, 0))

    def vec(n):
        return pl.BlockSpec((1, n), lambda i: (0, 0))

    out = jax.ShapeDtypeStruct((m, w), BF16)
    return pl.pallas_call(
        _post_kernel,
        grid=(m // tm,),
        in_specs=[dirspec(0), dirspec(1), pcol(COL_LG), tok, tok, pcol(COL_DZ), vec(DN_DV),
                  dirspec(0), dirspec(1), pcol(COL_SU), vec(w), pl.BlockSpec((w, w), lambda i: (0, 0)), vec(w)],
        out_specs=[tok, tok, tok],
        out_shape=[out, out, out],
        compiler_params=_cparams(("parallel",)),
        name="mix_post",
    )(h_lru, h_lru, proj, o_f, o_b, proj, ng, y_s5, y_s5, proj, sd, gw, gb)


def _out_kernel(x_ref, a_ref, b_ref, c_ref, d_ref, w_ref, gate_ref, g_ref, bb_ref, sc_ref, sh_ref,
                x1_ref, u2_ref):
    q = MIXW
    y = _dot(a_ref[...], w_ref[0:q, :])
    y += _dot(b_ref[...], w_ref[q:2 * q, :])
    y += _dot(c_ref[...], w_ref[2 * q:3 * q, :])
    y += _dot(d_ref[...], w_ref[3 * q:, :])
    z = DEEPNORM_ALPHA * x_ref[...] + gate_ref[0] * y
    x1 = _ln(z) * g_ref[...] + bb_ref[...]
    x1_ref[...] = x1
    u2_ref[...] = (_ln(x1) * (1.0 + sc_ref[0]) + sh_ref[0]).astype(BF16)


def out_proj(x, parts, w, mod, ln_g, ln_b, mod_row, tm=512):
    m, d = x.shape

    def mod_spec(kind):
        return pl.BlockSpec((1, 1, d), lambda i: (kind * MOD_ROWS + mod_row(i * tm), 0, 0))

    vec = pl.BlockSpec((1, d), lambda i: (0, 0))
    part = pl.BlockSpec((tm, MIXW), lambda i: (i, 0))
    return pl.pallas_call(
        _out_kernel,
        grid=(m // tm,),
        in_specs=[pl.BlockSpec((tm, d), lambda i: (i, 0)), part, part, part, part,
                  pl.BlockSpec((d, d), lambda i: (0, 0)),
                  mod_spec(2), vec, vec, mod_spec(4), mod_spec(3)],
        out_specs=[pl.BlockSpec((tm, d), lambda i: (i, 0)), pl.BlockSpec((tm, d), lambda i: (i, 0))],
        out_shape=[jax.ShapeDtypeStruct((m, d), F32), jax.ShapeDtypeStruct((m, d), BF16)],
        compiler_params=_cparams(("parallel",)),
        name="out_proj",
    )(x, *parts, w, mod, ln_g, ln_b, mod, mod)


def _mlp_kernel(u_ref, x_ref, w1_ref, w2_ref, gate_ref, g_ref, b_ref, o_ref, acc_ref):
    f = pl.program_id(1)
    h = _dot(u_ref[...], w1_ref[...])
    h = jnp.square(jnp.maximum(h, 0.0)).astype(BF16)
    part = _dot(h, w2_ref[...])

    @pl.when(f == 0)
    def _():
        acc_ref[...] = part

    @pl.when(f > 0)
    def _():
        acc_ref[...] += part

    @pl.when(f == pl.num_programs(1) - 1)
    def _():
        z = DEEPNORM_ALPHA * x_ref[...] + gate_ref[0] * acc_ref[...]
        o_ref[...] = _ln(z) * g_ref[...] + b_ref[...]


def mlp(u, x, w1, w2, mod, ln_g, ln_b, mod_row, tm=512, tf=1024):
    m, d = x.shape
    ff = w1.shape[1]
    vec = pl.BlockSpec((1, d), lambda i, f: (0, 0))
    return pl.pallas_call(
        _mlp_kernel,
        grid=(m // tm, ff // tf),
        in_specs=[pl.BlockSpec((tm, d), lambda i, f: (i, 0)),
                  pl.BlockSpec((tm, d), lambda i, f: (i, 0)),
                  pl.BlockSpec((d, tf), lambda i, f: (0, f)),
                  pl.BlockSpec((tf, d), lambda i, f: (f, 0)),
                  pl.BlockSpec((1, 1, d), lambda i, f: (5 * MOD_ROWS + mod_row(i * tm), 0, 0)),
                  vec, vec],
        out_specs=pl.BlockSpec((tm, d), lambda i, f: (i, 0)),
        out_shape=jax.ShapeDtypeStruct((m, d), F32),
        scratch_shapes=[pltpu.VMEM((tm, d), F32)],
        compiler_params=_cparams(("parallel", "arbitrary")),
        name="mlp",
    )(u, x, w1, w2, mod, ln_g, ln_b)


def _reorder_w_in(w):
    def cols(c0, n):
        return w[..., c0:c0 + n]

    aq, akv, lx, lg = cols(0, 512), cols(512, 512), cols(1024, 512), cols(1536, 512)
    dqkv, dz, ba, su = cols(2048, 1536), cols(3584, 512), cols(4096, 16), cols(4112, 512)
    pad = jnp.zeros(w.shape[:2] + (N_PROJ - COL_BA - 16,), w.dtype)
    return jnp.concatenate([dqkv, dz, aq, akv, lx, lg, su, ba, pad], axis=-1).astype(BF16)


def _block_diag(blocks):
    n, r, c = blocks.shape[-3:]
    eye = jnp.eye(n, dtype=blocks.dtype)
    out = blocks[..., :, :, None, :] * eye[:, None, :, None]
    return out.reshape(blocks.shape[:-3] + (n * r, n * c))


def _lru_params(wa, ba, wx, bx, lam):
    wg = jnp.concatenate([_block_diag(wa), _block_diag(wx)], axis=-1).astype(BF16)
    bg = jnp.concatenate([ba, bx], axis=-1)[:, None, :]
    sp = jax.nn.softplus(-lam)[:, None, :]
    return wg, bg, sp


def _s5_params(lam_re, lam_im, log_dt, b_re, b_im, c_re, c_im):
    dt = jnp.exp(log_dt)[..., None]
    mag = jnp.exp(lam_re * dt)
    abar_re = mag * jnp.cos(lam_im * dt)
    abar_im = mag * jnp.sin(lam_im * dt)
    den = lam_re * lam_re + lam_im * lam_im
    nr = abar_re - 1.0
    ni = abar_im
    f_re = (nr * lam_re + ni * lam_im) / den
    f_im = (ni * lam_re - nr * lam_im) / den
    bb_re = f_re[..., None] * b_re - f_im[..., None] * b_im
    bb_im = f_re[..., None] * b_im + f_im[..., None] * b_re
    to_in = lambda m: _block_diag(jnp.swapaxes(m, -1, -2))
    bb = jnp.concatenate([to_in(bb_re), to_in(bb_im)], axis=-1).astype(BF16)
    to_out = lambda m: _block_diag(jnp.swapaxes(m, -1, -2))
    n_out = S5_WIDTH // MXU
    per = S5_NSTATE // n_out
    c_r, c_i = to_out(c_re), -to_out(c_im)
    cc = jnp.stack([jnp.concatenate([c_r[:, nt * per:(nt + 1) * per, nt * MXU:(nt + 1) * MXU],
                                     c_i[:, nt * per:(nt + 1) * per, nt * MXU:(nt + 1) * MXU]], axis=1)
                    for nt in range(n_out)], axis=1).astype(BF16)
    a = jnp.concatenate([abar_re.reshape(2, 1, -1), abar_im.reshape(2, 1, -1)], axis=-1)
    return bb, cc, a


def _lane_row(vals, offset):
    return jnp.zeros((1, LANE), F32).at[0, offset:offset + vals.size].set(vals.reshape(-1))


def kernel(x_prompt, x_sample, cache_attn_k, cache_attn_v, state_rglru, state_delta, state_s5_re, state_s5_im, c, c_ctx, w_ada, b_ada, w_in, w_out, ln1_g, ln1_b, ln2_g, ln2_b, w_mlp1, w_mlp2, q_norm_g, k_norm_g, lru_conv_w, lru_conv_b, lru_wa, lru_ba, lru_wx, lru_bx, lru_lambda, dn_conv_w, dn_a_log, dn_dt_bias, dn_norm_g, s5_lambda_re, s5_lambda_im, s5_log_dt, s5_b_re, s5_b_im, s5_c_re, s5_c_im, s5_d, s5_glu_w, s5_glu_b):
    bp, tp, d = x_prompt.shape
    bs, ts, _ = x_sample.shape
    ctx_row = bs

    cond = jnp.concatenate([c, c_ctx[None, :], jnp.zeros((MOD_ROWS - bs - 1, d), F32)], axis=0)
    mods = ada_mod(cond, w_ada, b_ada)
    w_in_r = _reorder_w_in(w_in)
    w_out_b = w_out.astype(BF16)
    w1_b = w_mlp1.astype(BF16)
    w2_b = w_mlp2.astype(BF16)
    cos, sin = rope_tables(ts)

    streams = {
        'ctx': dict(x=x_prompt.reshape(bp * tp, d), b=bp, t=tp, nb=8, mod_row=lambda tok: ctx_row),
        'lat': dict(x=x_sample.reshape(bs * ts, d), b=bs, t=ts, nb=bs, mod_row=lambda tok: tok // ts),
    }
    ks, vs, lrus, dns, s5rs, s5is = [], [], [], [], [], []
    for l in range(DEPTH):
        mod = mods[l].reshape(MOD_ROWS, N_MOD, d).transpose(1, 0, 2).reshape(N_MOD * MOD_ROWS, 1, d)
        qg = q_norm_g[l].reshape(1, HEAD_DIM)
        kg = k_norm_g[l].reshape(1, HEAD_DIM)
        wg, bg, sp = _lru_params(lru_wa[l], lru_ba[l], lru_wx[l], lru_bx[l], lru_lambda[l])
        bb, cc, a5 = _s5_params(s5_lambda_re[l], s5_lambda_im[l], s5_log_dt[l], s5_b_re[l], s5_b_im[l],
                                s5_c_re[l], s5_c_im[l])
        al = _lane_row(dn_a_log[l], 2 * DN_HEADS)
        dtb = _lane_row(dn_dt_bias[l], 2 * DN_HEADS)
        for name, st in streams.items():
            b, t, nb = st['b'], st['t'], st['nb']
            is_ctx = name == 'ctx'
            proj = in_proj(st['x'], mod, w_in_r[l], st['mod_row'])
            proj3 = proj.reshape(b, t, N_PROJ)

            if is_ctx:
                attn, kn, vv = ctx_attention(proj, qg, kg, b, t)
                ks.append(kn.reshape(b, t, ATTN_KV_HEADS, HEAD_DIM))
                vs.append(vv.reshape(b, t, ATTN_KV_HEADS, HEAD_DIM))
                h0_lru = jnp.zeros((2, b, LRU_WIDTH), F32)
                s0_dn = jnp.zeros((2, b, DN_HEADS, DN_DK, DN_DV), F32)
                h0_s5 = jnp.zeros((2, b, 2 * S5_NSTATE), F32)
            else:
                q_s, k_s, v_s = lat_prep(proj, qg, kg, cos, sin, t)
                k_all = jnp.concatenate([cache_attn_k[:, l].reshape(b, -1, KV_WIDTH).astype(BF16),
                                         k_s.reshape(b, t, KV_WIDTH)], axis=1)
                v_all = jnp.concatenate([cache_attn_v[:, l].reshape(b, -1, KV_WIDTH).astype(BF16),
                                         v_s.reshape(b, t, KV_WIDTH)], axis=1)
                attn = lat_attention(q_s, k_all, v_all, t)
                h0_lru = jnp.swapaxes(state_rglru[:, l], 0, 1)
                s0_dn = jnp.swapaxes(state_delta[:, l], 0, 1)
                h0_s5 = jnp.swapaxes(jnp.concatenate([state_s5_re[:, l].reshape(b, 2, S5_NSTATE),
                                                      state_s5_im[:, l].reshape(b, 2, S5_NSTATE)], axis=-1), 0, 1)

            h_lru, lru_fin = lru_scan(proj3, h0_lru, lru_conv_w[l], lru_conv_b[l].reshape(1, -1), wg, bg, sp, nb)
            o_f, dn_fin_f = dn_dir(proj3, s0_dn[0], dn_conv_w[l], al, dtb, 0)
            o_b, dn_fin_b = dn_dir(proj3, s0_dn[1], dn_conv_w[l], al, dtb, 1)
            y_s5, s5_fin = s5_scan(proj3, h0_s5, bb, cc, a5, nb)
            m = b * t
            lru_out, dn_out, s5_out = mix_post(
                proj, h_lru.reshape(2, m, MIXW), o_f.reshape(m, MIXW), o_b.reshape(m, MIXW),
                y_s5.reshape(2, m, MIXW), dn_norm_g[l].reshape(1, DN_DV), s5_d[l].reshape(1, MIXW),
                s5_glu_w[l].astype(BF16), s5_glu_b[l].reshape(1, MIXW))
            if is_ctx:
                lrus.append(jnp.swapaxes(lru_fin, 0, 1))
                dns.append(jnp.stack([dn_fin_f, dn_fin_b], axis=1))
                s5_fin = jnp.swapaxes(s5_fin, 0, 1)
                s5rs.append(s5_fin[..., :S5_NSTATE].reshape(b, 2, S5_GROUPS, S5_STATE))
                s5is.append(s5_fin[..., S5_NSTATE:].reshape(b, 2, S5_GROUPS, S5_STATE))

            x1, u2 = out_proj(st['x'], [attn, lru_out, dn_out, s5_out], w_out_b[l], mod,
                              ln1_g[l].reshape(1, d), ln1_b[l].reshape(1, d), st['mod_row'])
            st['x'] = mlp(u2, x1, w1_b[l], w2_b[l], mod, ln2_g[l].reshape(1, d), ln2_b[l].reshape(1, d),
                          st['mod_row'])

    y_prompt = streams['ctx']['x'].reshape(bp, tp, d)
    y_sample = streams['lat']['x'].reshape(bs, ts, d)
    return (y_prompt, y_sample, jnp.stack(ks, axis=1), jnp.stack(vs, axis=1), jnp.stack(lrus, axis=1),
            jnp.stack(dns, axis=1), jnp.stack(s5rs, axis=1), jnp.stack(s5is, axis=1))
```

```python
import functools

import jax
import jax.numpy as jnp
from jax import lax
from jax.experimental import pallas as pl
from jax.experimental.pallas import tpu as pltpu

F32 = jnp.float32
BF16 = jnp.bfloat16

D_MODEL = 2048
DEPTH = 2
GRID_W = 64
CONV_W = 4
EPS = 1e-6
ROPE_THETA = 10000.0
N_MOD = 6
HEAD_DIM = 128
ATTN_WIDTH = D_MODEL // 4
ATTN_HEADS = ATTN_WIDTH // HEAD_DIM
ATTN_KV_HEADS = ATTN_HEADS // 2
ATTN_GROUP = ATTN_HEADS // ATTN_KV_HEADS
KV_WIDTH = ATTN_KV_HEADS * HEAD_DIM
ATTN_SCALE = HEAD_DIM ** -0.5
LRU_WIDTH = D_MODEL // 4
LRU_BLOCKS = 8
LRU_C = 8.0
DN_DK = 128
DN_DV = 128
DN_WIDTH = D_MODEL // 4
DN_HEADS = DN_WIDTH // DN_DV
DN_CHUNK = 64
S5_WIDTH = D_MODEL // 4
S5_CH = 16
S5_GROUPS = S5_WIDTH // S5_CH
S5_STATE = 64
S5_NSTATE = S5_GROUPS * S5_STATE
DEEPNORM_ALPHA = (2 * DEPTH) ** 0.25
MIXW = D_MODEL // 4

COL_DQ = 0
COL_DZ = 1536
COL_Q = 2048
COL_KV = 2560
COL_LX = 3072
COL_LG = 3584
COL_SU = 4096
COL_BA = 4608
N_PROJ = 4864
LANE = 128
MXU = 256

MOD_ROWS = 8
PITCH_PAD = 4
VMEM_LIMIT = 56 * 1024 * 1024


def _cparams(sem):
    return pltpu.CompilerParams(dimension_semantics=sem, vmem_limit_bytes=VMEM_LIMIT)


def _ln(x):
    mu = jnp.mean(x, axis=-1, keepdims=True)
    xc = x - mu
    var = jnp.mean(xc * xc, axis=-1, keepdims=True)
    return xc * lax.rsqrt(var + EPS)


def _softplus(x):
    return jnp.maximum(x, 0.0) + jnp.log1p(jnp.exp(-jnp.abs(x)))


def _gelu(x):
    return 0.5 * x * (1.0 + jnp.tanh(0.7978845608028654 * (x + 0.044715 * (x * x * x))))


def _dot(a, b):
    return jnp.dot(a, b, preferred_element_type=F32)


def _dot_nt(a, b):
    return lax.dot_general(a, b, (((1,), (1,)), ((), ())), preferred_element_type=F32)


def _ada_kernel(c_ref, w_ref, b_ref, o_ref):
    cs = c_ref[...]
    s = cs * jax.nn.sigmoid(cs)
    o_ref[0] = _dot(s.astype(BF16), w_ref[0].astype(BF16)) + b_ref[0]


def ada_mod(cond, w_ada, b_ada, tn=1024):
    depth, d, n = w_ada.shape
    return pl.pallas_call(
        _ada_kernel,
        grid=(depth, n // tn),
        in_specs=[pl.BlockSpec((MOD_ROWS, d), lambda l, j: (0, 0)),
                  pl.BlockSpec((1, d, tn), lambda l, j: (l, 0, j)),
                  pl.BlockSpec((1, 1, tn), lambda l, j: (l, 0, j))],
        out_specs=pl.BlockSpec((1, MOD_ROWS, tn), lambda l, j: (l, 0, j)),
        out_shape=jax.ShapeDtypeStruct((depth, MOD_ROWS, n), F32),
        compiler_params=_cparams(("parallel", "parallel")),
        name="ada_mod",
    )(cond, w_ada, b_ada.reshape(depth, 1, n))


def _in_kernel(x_ref, sc_ref, sh_ref, w_ref, o_ref, u_ref):
    @pl.when(pl.program_id(1) == 0)
    def _():
        u = _ln(x_ref[...]) * (1.0 + sc_ref[0]) + sh_ref[0]
        u_ref[...] = u.astype(BF16)

    o_ref[...] = _dot(u_ref[...], w_ref[...])


def in_proj(x, mod, w, mod_row, tm=1024, tn=256):
    m, d = x.shape
    n = w.shape[1]

    def mod_spec(kind):
        return pl.BlockSpec((1, 1, d), lambda i, j: (kind * MOD_ROWS + mod_row(i * tm), 0, 0))

    return pl.pallas_call(
        _in_kernel,
        grid=(m // tm, n // tn),
        in_specs=[pl.BlockSpec((tm, d), lambda i, j: (i, 0)),
                  mod_spec(1), mod_spec(0),
                  pl.BlockSpec((d, tn), lambda i, j: (0, j))],
        out_specs=pl.BlockSpec((tm, tn), lambda i, j: (i, j)),
        out_shape=jax.ShapeDtypeStruct((m, n), F32),
        scratch_shapes=[pltpu.VMEM((tm, d), BF16)],
        compiler_params=_cparams(("parallel", "arbitrary")),
        name="in_proj",
    )(x, mod, mod, w)


def _rms_heads(x, g, heads):
    outs = []
    for h in range(heads):
        xh = x[:, h * HEAD_DIM:(h + 1) * HEAD_DIM]
        outs.append(xh * lax.rsqrt(jnp.mean(xh * xh, axis=-1, keepdims=True) + EPS) * g)
    return outs


def _softmax_av(q, k, v):
    s = _dot_nt(q, k) * ATTN_SCALE
    m = jnp.max(s, axis=-1, keepdims=True)
    p = jnp.exp(s - m)
    l = jnp.sum(p, axis=-1, keepdims=True)
    return _dot(p.astype(BF16), v) / l


def _ctx_attn_kernel(q_ref, kv_ref, qg_ref, kg_ref, o_ref, kn_ref, v_ref):
    qs = _rms_heads(q_ref[...], qg_ref[...], ATTN_HEADS)
    kv = kv_ref[...]
    ks = _rms_heads(kv[:, :KV_WIDTH], kg_ref[...], ATTN_KV_HEADS)
    v = kv[:, KV_WIDTH:]
    v_ref[...] = v
    t = q_ref.shape[0]
    for kh in range(ATTN_KV_HEADS):
        kn_ref[:, kh * HEAD_DIM:(kh + 1) * HEAD_DIM] = ks[kh]
        q2 = jnp.concatenate([qs[kh * ATTN_GROUP + g] for g in range(ATTN_GROUP)], axis=0).astype(BF16)
        o = _softmax_av(q2, ks[kh].astype(BF16), v[:, kh * HEAD_DIM:(kh + 1) * HEAD_DIM].astype(BF16))
        for g in range(ATTN_GROUP):
            h = kh * ATTN_GROUP + g
            o_ref[:, h * HEAD_DIM:(h + 1) * HEAD_DIM] = o[g * t:(g + 1) * t].astype(BF16)


def ctx_attention(proj, qg, kg, batch, seq):
    return pl.pallas_call(
        _ctx_attn_kernel,
        grid=(batch,),
        in_specs=[pl.BlockSpec((seq, ATTN_WIDTH), lambda b: (b, COL_Q // ATTN_WIDTH)),
                  pl.BlockSpec((seq, 2 * KV_WIDTH), lambda b: (b, COL_KV // (2 * KV_WIDTH))),
                  pl.BlockSpec((1, HEAD_DIM), lambda b: (0, 0)),
                  pl.BlockSpec((1, HEAD_DIM), lambda b: (0, 0))],
        out_specs=[pl.BlockSpec((seq, ATTN_WIDTH), lambda b: (b, 0)),
                   pl.BlockSpec((seq, KV_WIDTH), lambda b: (b, 0)),
                   pl.BlockSpec((seq, KV_WIDTH), lambda b: (b, 0))],
        out_shape=[jax.ShapeDtypeStruct((batch * seq, ATTN_WIDTH), BF16),
                   jax.ShapeDtypeStruct((batch * seq, KV_WIDTH), F32),
                   jax.ShapeDtypeStruct((batch * seq, KV_WIDTH), F32)],
        compiler_params=_cparams(("parallel",)),
        name="ctx_attention",
    )(proj, proj, qg, kg)


def _rope(x, cos, sin, heads):
    w = x.shape[-1]
    lane = lax.broadcasted_iota(jnp.int32, x.shape, 1)
    quarter = HEAD_DIM // 4
    partner = jnp.where((lane % (2 * quarter)) < quarter,
                        pltpu.roll(x, w - quarter, 1), pltpu.roll(x, quarter, 1))
    cos_t = jnp.concatenate([cos] * heads, axis=1)
    sin_t = jnp.concatenate([sin] * heads, axis=1)
    return x * cos_t + partner * sin_t


def _lat_prep_kernel(q_ref, kv_ref, qg_ref, kg_ref, cos_ref, sin_ref, qo_ref, ko_ref, vo_ref):
    qn = jnp.concatenate(_rms_heads(q_ref[...], qg_ref[...], ATTN_HEADS), axis=1)
    kv = kv_ref[...]
    kn = jnp.concatenate(_rms_heads(kv[:, :KV_WIDTH], kg_ref[...], ATTN_KV_HEADS), axis=1)
    cos = cos_ref[...]
    sin = sin_ref[...]
    qo_ref[...] = _rope(qn, cos, sin, ATTN_HEADS).astype(BF16)
    ko_ref[...] = _rope(kn, cos, sin, ATTN_KV_HEADS).astype(BF16)
    vo_ref[...] = kv[:, KV_WIDTH:].astype(BF16)


def lat_prep(proj, qg, kg, cos, sin, seq, tm=512):
    m = proj.shape[0]
    per = seq // tm
    return pl.pallas_call(
        _lat_prep_kernel,
        grid=(m // tm,),
        in_specs=[pl.BlockSpec((tm, ATTN_WIDTH), lambda i: (i, COL_Q // ATTN_WIDTH)),
                  pl.BlockSpec((tm, 2 * KV_WIDTH), lambda i: (i, COL_KV // (2 * KV_WIDTH))),
                  pl.BlockSpec((1, HEAD_DIM), lambda i: (0, 0)),
                  pl.BlockSpec((1, HEAD_DIM), lambda i: (0, 0)),
                  pl.BlockSpec((tm, HEAD_DIM), lambda i: (i % per, 0)),
                  pl.BlockSpec((tm, HEAD_DIM), lambda i: (i % per, 0))],
        out_specs=[pl.BlockSpec((tm, ATTN_WIDTH), lambda i: (i, 0)),
                   pl.BlockSpec((tm, KV_WIDTH), lambda i: (i, 0)),
                   pl.BlockSpec((tm, KV_WIDTH), lambda i: (i, 0))],
        out_shape=[jax.ShapeDtypeStruct((m, ATTN_WIDTH), BF16),
                   jax.ShapeDtypeStruct((m, KV_WIDTH), BF16),
                   jax.ShapeDtypeStruct((m, KV_WIDTH), BF16)],
        compiler_params=_cparams(("parallel",)),
        name="lat_prep",
    )(proj, proj, qg, kg, cos, sin)


def _lat_attn_kernel(q_ref, k_ref, v_ref, o_ref):
    k = k_ref[0]
    v = v_ref[0]
    for g in range(ATTN_GROUP):
        sl = slice(g * HEAD_DIM, (g + 1) * HEAD_DIM)
        o_ref[:, sl] = _softmax_av(q_ref[:, sl], k, v).astype(BF16)


def lat_attention(q, k_all, v_all, seq, tq=256):
    b, s, _ = k_all.shape
    nq = seq // tq
    gw = ATTN_GROUP * HEAD_DIM
    return pl.pallas_call(
        _lat_attn_kernel,
        grid=(b, ATTN_KV_HEADS, nq),
        in_specs=[pl.BlockSpec((tq, gw), lambda bi, kh, qi: (bi * nq + qi, kh)),
                  pl.BlockSpec((1, s, HEAD_DIM), lambda bi, kh, qi: (bi, 0, kh)),
                  pl.BlockSpec((1, s, HEAD_DIM), lambda bi, kh, qi: (bi, 0, kh))],
        out_specs=pl.BlockSpec((tq, gw), lambda bi, kh, qi: (bi * nq + qi, kh)),
        out_shape=jax.ShapeDtypeStruct((b * seq, ATTN_WIDTH), BF16),
        compiler_params=_cparams(("parallel", "parallel", "arbitrary")),
        name="lat_attention",
    )(q, k_all, v_all)


def rope_tables(seq):
    t = jnp.arange(seq)
    row = (t // GRID_W).astype(F32)
    col = (t % GRID_W).astype(F32)
    quarter = HEAD_DIM // 4
    inv_freq = jnp.power(ROPE_THETA, -jnp.arange(quarter, dtype=F32) / quarter)
    ar = row[:, None] * inv_freq[None, :]
    ac = col[:, None] * inv_freq[None, :]
    cos = jnp.concatenate([jnp.cos(ar), jnp.cos(ar), jnp.cos(ac), jnp.cos(ac)], axis=1)
    sin = jnp.concatenate([-jnp.sin(ar), jnp.sin(ar), -jnp.sin(ac), jnp.sin(ac)], axis=1)
    return cos, sin


def _chunk_index(d, i, n):
    return i + d * (n - 1 - 2 * i)


def _lru_kernel(x_ref, xp_ref, xn_ref, cw_ref, cb_ref, wg_ref, bg_ref, sp_ref, h0_ref,
                h_ref, fin_ref, xpad_ref, a_ref, b_ref, hc_ref, *, nb, tt):
    d = pl.program_id(0)
    i = pl.program_id(2)
    n = pl.num_programs(2)
    ci = _chunk_index(d, i, n)
    pitch = tt + PITCH_PAD
    rows = nb * pitch
    nsl = LRU_WIDTH // LANE

    @pl.when(i == 0)
    def _():
        xpad_ref[...] = jnp.zeros_like(xpad_ref)
        hc_ref[...] = h0_ref[0]

    for s in range(nb):
        base = 8 + s * pitch
        xpad_ref[base:base + tt, :] = x_ref[s]
        xpad_ref[base - 1:base, :] = jnp.where(ci == 0, 0.0, xp_ref[s, 7:8, :])
        xpad_ref[base + tt:base + tt + 2, :] = jnp.where(ci == n - 1, 0.0, xn_ref[s, 0:2, :])
    w = cw_ref[...]
    xc = cb_ref[...] + sum(xpad_ref[7 + j:7 + j + rows, :] * w[j:j + 1, :] for j in range(CONV_W))
    pre = _dot(xc.astype(BF16), wg_ref[0]) + bg_ref[0]
    r = jax.nn.sigmoid(pre[:, :LRU_WIDTH])
    ig = jax.nn.sigmoid(pre[:, LRU_WIDTH:])
    a = jnp.exp((-LRU_C) * r * sp_ref[0])
    inp = jnp.sqrt(1.0 - a * a) * (ig * xc)
    for c in range(nsl):
        a_ref[c] = a[:, c * LANE:(c + 1) * LANE]
        b_ref[c] = inp[:, c * LANE:(c + 1) * LANE]

    def body(t, carry):
        row = t + d * (tt - 1 - 2 * t)
        out = []
        for c in range(nsl):
            idx = (c, pl.ds(row, nb, stride=pitch), slice(None))
            h = a_ref[idx] * carry[c] + b_ref[idx]
            b_ref[idx] = h
            out.append(h)
        return tuple(out)

    carry = tuple(hc_ref[:, c * LANE:(c + 1) * LANE] for c in range(nsl))
    carry = lax.fori_loop(0, tt, body, carry, unroll=8)
    for c in range(nsl):
        hc_ref[:, c * LANE:(c + 1) * LANE] = carry[c]
        for s in range(nb):
            h_ref[0, s, :, c * LANE:(c + 1) * LANE] = b_ref[c, s * pitch:s * pitch + tt, :]

    @pl.when(i == n - 1)
    def _():
        fin_ref[0] = hc_ref[...]


def lru_scan(proj3, h0, cw, cb, wg, bg, sp, nb, tt=128):
    b, t, _ = proj3.shape
    w = LRU_WIDTH
    n = t // tt
    t8 = tt // 8
    rows = nb * (tt + PITCH_PAD)
    col = COL_LX // w

    def cidx(d, i):
        return _chunk_index(d, i, n)

    return pl.pallas_call(
        functools.partial(_lru_kernel, nb=nb, tt=tt),
        grid=(2, b // nb, n),
        in_specs=[pl.BlockSpec((nb, tt, w), lambda d, g, i: (g, cidx(d, i), col)),
                  pl.BlockSpec((nb, 8, w), lambda d, g, i: (g, jnp.maximum(cidx(d, i) * t8 - 1, 0), col)),
                  pl.BlockSpec((nb, 8, w), lambda d, g, i: (g, jnp.minimum((cidx(d, i) + 1) * t8, t // 8 - 1), col)),
                  pl.BlockSpec((CONV_W, w), lambda d, g, i: (0, 0)),
                  pl.BlockSpec((1, w), lambda d, g, i: (0, 0)),
                  pl.BlockSpec((1, w, 2 * w), lambda d, g, i: (d, 0, 0)),
                  pl.BlockSpec((1, 1, 2 * w), lambda d, g, i: (d, 0, 0)),
                  pl.BlockSpec((1, 1, w), lambda d, g, i: (d, 0, 0)),
                  pl.BlockSpec((1, nb, w), lambda d, g, i: (d, g, 0))],
        out_specs=[pl.BlockSpec((1, nb, tt, w), lambda d, g, i: (d, g, cidx(d, i), 0)),
                   pl.BlockSpec((1, nb, w), lambda d, g, i: (d, g, 0))],
        out_shape=[jax.ShapeDtypeStruct((2, b, t, w), F32), jax.ShapeDtypeStruct((2, b, w), F32)],
        scratch_shapes=[pltpu.VMEM((rows + 16, w), F32),
                        pltpu.VMEM((w // LANE, rows, LANE), F32),
                        pltpu.VMEM((w // LANE, rows, LANE), F32),
                        pltpu.VMEM((nb, w), F32)],
        compiler_params=_cparams(("parallel", "parallel", "arbitrary")),
        name="lru_scan",
    )(proj3, proj3, proj3, cw, cb, wg, bg, sp, h0)


def _s5_kernel(u_ref, bb_ref, cc_ref, a_ref, h0_ref, y_ref, fin_ref, up_ref, s_ref, hc_ref, *, nb, tt):
    d = pl.program_id(0)
    i = pl.program_id(2)
    n = pl.num_programs(2)
    pitch = tt + PITCH_PAD
    nsl = S5_NSTATE // LANE

    @pl.when(i == 0)
    def _():
        up_ref[...] = jnp.zeros_like(up_ref)
        hc_ref[...] = h0_ref[0]

    for s in range(nb):
        up_ref[s * pitch:s * pitch + tt, :] = u_ref[s]
    u2 = up_ref[...].astype(BF16)
    per_k = MXU // S5_CH * S5_STATE // MXU
    for nt in range(2 * S5_NSTATE // MXU):
        kt = (nt % (S5_NSTATE // MXU)) // per_k
        tile = _dot(u2[:, kt * MXU:(kt + 1) * MXU], bb_ref[0, kt * MXU:(kt + 1) * MXU, nt * MXU:(nt + 1) * MXU])
        s_ref[2 * nt] = tile[:, :LANE]
        s_ref[2 * nt + 1] = tile[:, LANE:]

    a_all = a_ref[0]
    group = 4
    for c0 in range(0, nsl, group):
        cs = list(range(c0, c0 + group))
        ar = [jnp.broadcast_to(a_all[:, c * LANE:(c + 1) * LANE], (nb, LANE)) for c in cs]
        ai = [jnp.broadcast_to(a_all[:, S5_NSTATE + c * LANE:S5_NSTATE + (c + 1) * LANE], (nb, LANE)) for c in cs]

        def body(t, carry, cs=cs, ar=ar, ai=ai):
            row = t + d * (tt - 1 - 2 * t)
            out = []
            for j, c in enumerate(cs):
                hr, hi = carry[2 * j], carry[2 * j + 1]
                ire = (c, pl.ds(row, nb, stride=pitch), slice(None))
                iim = (nsl + c, pl.ds(row, nb, stride=pitch), slice(None))
                nr = ar[j] * hr - ai[j] * hi + s_ref[ire]
                ni = ar[j] * hi + ai[j] * hr + s_ref[iim]
                s_ref[ire] = nr
                s_ref[iim] = ni
                out += [nr, ni]
            return tuple(out)

        carry = []
        for c in cs:
            carry += [hc_ref[:, c * LANE:(c + 1) * LANE], hc_ref[:, S5_NSTATE + c * LANE:S5_NSTATE + (c + 1) * LANE]]
        carry = lax.fori_loop(0, tt, body, tuple(carry), unroll=4)
        for j, c in enumerate(cs):
            hc_ref[:, c * LANE:(c + 1) * LANE] = carry[2 * j]
            hc_ref[:, S5_NSTATE + c * LANE:S5_NSTATE + (c + 1) * LANE] = carry[2 * j + 1]

    n_out = S5_WIDTH // MXU
    per_n = nsl // n_out
    for nt in range(n_out):
        slabs = [per_n * nt + k for k in range(per_n)] + [nsl + per_n * nt + k for k in range(per_n)]
        lhs = jnp.concatenate([s_ref[c] for c in slabs], axis=1).astype(BF16)
        y = _dot(lhs, cc_ref[0, nt])
        for s in range(nb):
            y_ref[0, s, :, nt * MXU:(nt + 1) * MXU] = y[s * pitch:s * pitch + tt]

    @pl.when(i == n - 1)
    def _():
        fin_ref[0] = hc_ref[...]


def s5_scan(proj3, h0, bb, cc, a, nb, tt=128):
    b, t, _ = proj3.shape
    w = S5_WIDTH
    n = t // tt
    rows = nb * (tt + PITCH_PAD)
    ns2 = 2 * S5_NSTATE

    def cidx(d, i):
        return _chunk_index(d, i, n)

    return pl.pallas_call(
        functools.partial(_s5_kernel, nb=nb, tt=tt),
        grid=(2, b // nb, n),
        in_specs=[pl.BlockSpec((nb, tt, w), lambda d, g, i: (g, cidx(d, i), COL_SU // w)),
                  pl.BlockSpec((1, w, ns2), lambda d, g, i: (d, 0, 0)),
                  pl.BlockSpec((1,) + cc.shape[1:], lambda d, g, i: (d, 0, 0, 0)),
                  pl.BlockSpec((1, 1, ns2), lambda d, g, i: (d, 0, 0)),
                  pl.BlockSpec((1, nb, ns2), lambda d, g, i: (d, g, 0))],
        out_specs=[pl.BlockSpec((1, nb, tt, w), lambda d, g, i: (d, g, cidx(d, i), 0)),
                   pl.BlockSpec((1, nb, ns2), lambda d, g, i: (d, g, 0))],
        out_shape=[jax.ShapeDtypeStruct((2, b, t, w), F32), jax.ShapeDtypeStruct((2, b, ns2), F32)],
        scratch_shapes=[pltpu.VMEM((rows, w), F32),
                        pltpu.VMEM((ns2 // LANE, rows, LANE), F32),
                        pltpu.VMEM((nb, ns2), F32)],
        compiler_params=_cparams(("parallel", "parallel", "arbitrary")),
        name="s5_scan",
    )(proj3, bb, cc, a, h0)


def _split3(x):
    x1 = x.astype(BF16)
    r1 = x - x1.astype(F32)
    x2 = r1.astype(BF16)
    x3 = (r1 - x2.astype(F32)).astype(BF16)
    return x1, x2, x3


def _mm(x, y):
    return _dot(x.astype(BF16), y.astype(BF16))


def _unit_tri_inverses(mats, rr, cc):
    same16 = (rr // 16) == (cc // 16)
    same32 = (rr // 32) == (cc // 32)
    p = [jnp.where(same16, -a, 0.0) for a in mats]
    t = [jnp.where(rr == cc, 1.0, x) for x in p]
    for _ in range(3):
        p = [_mm(x, x) for x in p]
        t = [x + _mm(x, y) for x, y in zip(t, p)]
    for mask in (same32 & jnp.logical_not(same16), jnp.logical_not(same32)):
        te = [_mm(x, jnp.where(mask, a, 0.0)) for x, a in zip(t, mats)]
        t = [x - _mm(y, x) for x, y in zip(t, te)]
    return t


def _dn_kernel(xf_ref, xpf_ref, xnf_ref, baf_ref, xb_ref, xpb_ref, xnb_ref, bab_ref, cw_ref, al_ref, dtb_ref,
               s0_ref, of_ref, ob_ref, fin_ref, xpad_ref, s_ref, *, nbat, tb, shared):
    i = pl.program_id(1)
    n = pl.num_programs(1)
    ch = DN_CHUNK
    nc = tb // ch
    hw = DN_HEADS * DN_DK

    @pl.when(i == 0)
    def _():
        s_ref[...] = s0_ref[...]

    w = cw_ref[...]
    rb = lax.broadcasted_iota(jnp.int32, (tb, tb), 0)
    cb = lax.broadcasted_iota(jnp.int32, (tb, tb), 1)
    rr = lax.broadcasted_iota(jnp.int32, (ch, ch), 0)
    cc = lax.broadcasted_iota(jnp.int32, (ch, ch), 1)
    al = al_ref[...]
    dtb = dtb_ref[...]

    def qkv_heads(slot, x_ref, xp_ref, xn_ref, bat, ci):
        xpad_ref[slot, 8:8 + tb, :] = x_ref[bat]
        xpad_ref[slot, 0:8, :] = jnp.where(ci == 0, 0.0, xp_ref[bat])
        xpad_ref[slot, 8 + tb:16 + tb, :] = jnp.where(ci == n - 1, 0.0, xn_ref[bat])
        xc = sum(xpad_ref[slot, 7 + j:7 + j + tb, :] * w[j:j + 1, :] for j in range(CONV_W))
        qkv = xc * jax.nn.sigmoid(xc)
        out = {}
        for c in range(nc):
            r0 = c * ch
            for h in range(DN_HEADS):
                q = qkv[r0:r0 + ch, h * DN_DK:(h + 1) * DN_DK]
                k = qkv[r0:r0 + ch, hw + h * DN_DK:hw + (h + 1) * DN_DK]
                v = qkv[r0:r0 + ch, 2 * hw + h * DN_DV:2 * hw + (h + 1) * DN_DV]
                q = q * lax.rsqrt(jnp.sum(q * q, axis=-1, keepdims=True) + EPS) * (DN_DK ** -0.5)
                k = k * lax.rsqrt(jnp.sum(k * k, axis=-1, keepdims=True) + EPS)
                out[(c, h)] = (q, k, v)
        return out

    units = []
    for d in range(2):
        reverse = d == 1
        ci = (n - 1 - i) if reverse else i
        x_ref, xp_ref, xn_ref, ba_ref = (xb_ref, xpb_ref, xnb_ref, bab_ref) if reverse else (xf_ref, xpf_ref, xnf_ref, baf_ref)
        tri = ((rb // ch) == (cb // ch)) & ((cb >= rb) if reverse else (cb <= rb))
        tri = jnp.where(tri, 1.0, 0.0).astype(BF16)
        incl = (cc >= rr) if reverse else (cc <= rr)
        strict = (cc > rr) if reverse else (cc < rr)
        if d == 0 or not shared:
            heads = [qkv_heads(d * nbat + bat, x_ref, xp_ref, xn_ref, bat, ci) for bat in range(nbat)]
        for bat in range(nbat):
            ba = ba_ref[bat]
            beta_all = jax.nn.sigmoid(ba)
            g_all = -jnp.exp(al) * _softplus(ba + dtb)
            gc_all = sum(_dot(tri, piece) for piece in _split3(g_all))
            for c in range(nc):
                r0 = c * ch
                gct = gc_all[r0:r0 + ch, :].T
                for h in range(DN_HEADS):
                    lane = d * DN_HEADS + h
                    beta = beta_all[r0:r0 + ch, lane:lane + 1]
                    gcol = gc_all[r0:r0 + ch, 2 * DN_HEADS + lane:2 * DN_HEADS + lane + 1]
                    grow = gct[2 * DN_HEADS + lane:2 * DN_HEADS + lane + 1, :]
                    q, k, v = heads[bat][(c, h)]
                    g_last = gcol[0:1, :] if reverse else gcol[ch - 1:ch, :]
                    units.append(dict(d=d, bat=bat, c=c, h=h, q=q, k=k, v=v, beta=beta, gcol=gcol, kb=k * beta,
                                      decay=jnp.where(incl, jnp.exp(gcol - grow), 0.0), strict=strict,
                                      g_last=g_last))

    for u in units:
        kbf = u['k'].astype(BF16)
        u['a'] = jnp.where(u['strict'], _dot_nt(u['kb'].astype(BF16), kbf) * u['decay'], 0.0)
        u['qk'] = (_dot_nt(u['q'].astype(BF16), kbf) * u['decay']).astype(BF16)
    t_inv = _unit_tri_inverses([u['a'] for u in units], rr, cc)
    for u, t in zip(units, t_inv):
        u['rhs'] = jnp.concatenate([u['v'] * u['beta'], u['kb'] * jnp.exp(u['gcol'])], axis=1)
        u['t_off'] = jnp.where(rr == cc, 0.0, t)
    for u in units:
        sol = u['rhs'] + _mm(u['t_off'], u['rhs'])
        u['u_val'] = sol[:, :DN_DV]
        u['w'] = sol[:, DN_DV:].astype(BF16)
        u['qe'] = (u['q'] * jnp.exp(u['gcol'])).astype(BF16)
        u['kdec_t'] = (u['k'] * jnp.exp(u['g_last'] - u['gcol'])).T.astype(BF16)
        u['eg'] = jnp.exp(u['g_last'])

    state = {(d, bat, h): s_ref[d, bat, h] for d in range(2) for bat in range(nbat) for h in range(DN_HEADS)}
    for j in range(nc):
        cur = [u for u in units if u['c'] == (nc - 1 - j if u['d'] == 1 else j)]
        keys = [(u['d'], u['bat'], u['h']) for u in cur]
        sb = [state[key].astype(BF16) for key in keys]
        ws = [_dot(u['w'], s) for u, s in zip(cur, sb)]
        qs = [_dot(u['qe'], s) for u, s in zip(cur, sb)]
        vb = [(u['u_val'] - x).astype(BF16) for u, x in zip(cur, ws)]
        os = [x + _dot(u['qk'], y) for u, x, y in zip(cur, qs, vb)]
        sn = [state[key] * u['eg'] + _dot(u['kdec_t'], y) for u, key, y in zip(cur, keys, vb)]
        for u, key, o, s in zip(cur, keys, os, sn):
            state[key] = s
            o_ref = ob_ref if u['d'] == 1 else of_ref
            o_ref[u['bat'], u['c'] * ch:(u['c'] + 1) * ch, u['h'] * DN_DV:(u['h'] + 1) * DN_DV] = o
    for (d, bat, h), s in state.items():
        s_ref[d, bat, h] = s

    @pl.when(i == n - 1)
    def _():
        fin_ref[...] = s_ref[...]


def dn_scan(proj3, s0, cw, al, dtb, nbat, tb):
    b, t, _ = proj3.shape
    n = t // tb
    t8 = tb // 8
    w3 = 3 * DN_WIDTH
    cq = COL_DQ // w3
    cba = COL_BA // LANE

    def data_specs(cidx):
        return [pl.BlockSpec((nbat, tb, w3), lambda g, i: (g, cidx(i), cq)),
                pl.BlockSpec((nbat, 8, w3), lambda g, i: (g, jnp.maximum(cidx(i) * t8 - 1, 0), cq)),
                pl.BlockSpec((nbat, 8, w3), lambda g, i: (g, jnp.minimum((cidx(i) + 1) * t8, t // 8 - 1), cq)),
                pl.BlockSpec((nbat, tb, LANE), lambda g, i: (g, cidx(i), cba))]

    fwd = lambda i: i
    bwd = lambda i: n - 1 - i
    st_spec = pl.BlockSpec((2, nbat, DN_HEADS, DN_DK, DN_DV), lambda g, i: (0, g, 0, 0, 0))
    return pl.pallas_call(
        functools.partial(_dn_kernel, nbat=nbat, tb=tb, shared=(n == 1)),
        grid=(b // nbat, n),
        in_specs=data_specs(fwd) + data_specs(bwd) + [
            pl.BlockSpec((CONV_W, w3), lambda g, i: (0, 0)),
            pl.BlockSpec((1, LANE), lambda g, i: (0, 0)),
            pl.BlockSpec((1, LANE), lambda g, i: (0, 0)),
            st_spec],
        out_specs=[pl.BlockSpec((nbat, tb, DN_WIDTH), lambda g, i: (g, i, 0)),
                   pl.BlockSpec((nbat, tb, DN_WIDTH), lambda g, i: (g, n - 1 - i, 0)),
                   st_spec],
        out_shape=[jax.ShapeDtypeStruct((b, t, DN_WIDTH), F32),
                   jax.ShapeDtypeStruct((b, t, DN_WIDTH), F32),
                   jax.ShapeDtypeStruct((2, b, DN_HEADS, DN_DK, DN_DV), F32)],
        scratch_shapes=[pltpu.VMEM((2 * nbat, tb + 16, w3), F32),
                        pltpu.VMEM((2, nbat, DN_HEADS, DN_DK, DN_DV), F32)],
        compiler_params=_cparams(("parallel", "arbitrary")),
        name="dn_scan",
    )(*([proj3] * 8), cw, al, dtb, s0)


def _post_kernel(hf_ref, hb_ref, lg_ref, of_ref, ob_ref, dz_ref, ng_ref, yf_ref, yb_ref, su_ref, sd_ref,
                 gw_ref, gb_ref, lru_ref, dn_ref, s5_ref):
    lru_ref[...] = ((hf_ref[0] + hb_ref[0]) * _gelu(lg_ref[...])).astype(BF16)
    o = of_ref[...] + ob_ref[...]
    dz = dz_ref[...]
    for h in range(DN_HEADS):
        sl = slice(h * DN_DV, (h + 1) * DN_DV)
        oh = o[:, sl]
        oh = oh * lax.rsqrt(jnp.mean(oh * oh, axis=-1, keepdims=True) + EPS) * ng_ref[...]
        zh = dz[:, sl]
        dn_ref[:, sl] = (oh * (zh * jax.nn.sigmoid(zh))).astype(BF16)
    y = yf_ref[0] + yb_ref[0] + sd_ref[...] * su_ref[...]
    gy = _gelu(y)
    s5_ref[...] = (gy * jax.nn.sigmoid(_dot(gy.astype(BF16), gw_ref[...]) + gb_ref[...])).astype(BF16)


def mix_post(proj, h_lru, o_f, o_b, y_s5, ng, sd, gw, gb, tm=512):
    m = proj.shape[0]
    w = MIXW
    tok = pl.BlockSpec((tm, w), lambda i: (i, 0))

    def pcol(c0):
        return pl.BlockSpec((tm, w), lambda i: (i, c0 // w))

    def dirspec(d):
        return pl.BlockSpec((1, tm, w), lambda i: (d, i, 0))

    def vec(n):
        return pl.BlockSpec((1, n), lambda i: (0, 0))

    out = jax.ShapeDtypeStruct((m, w), BF16)
    return pl.pallas_call(
        _post_kernel,
        grid=(m // tm,),
        in_specs=[dirspec(0), dirspec(1), pcol(COL_LG), tok, tok, pcol(COL_DZ), vec(DN_DV),
                  dirspec(0), dirspec(1), pcol(COL_SU), vec(w), pl.BlockSpec((w, w), lambda i: (0, 0)), vec(w)],
        out_specs=[tok, tok, tok],
        out_shape=[out, out, out],
        compiler_params=_cparams(("parallel",)),
        name="mix_post",
    )(h_lru, h_lru, proj, o_f, o_b, proj, ng, y_s5, y_s5, proj, sd, gw, gb)


def _out_kernel(x_ref, a_ref, b_ref, c_ref, d_ref, w_ref, gate_ref, g_ref, bb_ref, sc_ref, sh_ref,
                x1_ref, u2_ref):
    q = MIXW
    y = _dot(a_ref[...], w_ref[0:q, :])
    y += _dot(b_ref[...], w_ref[q:2 * q, :])
    y += _dot(c_ref[...], w_ref[2 * q:3 * q, :])
    y += _dot(d_ref[...], w_ref[3 * q:, :])
    z = DEEPNORM_ALPHA * x_ref[...] + gate_ref[0] * y
    x1 = _ln(z) * g_ref[...] + bb_ref[...]
    x1_ref[...] = x1
    u2_ref[...] = (_ln(x1) * (1.0 + sc_ref[0]) + sh_ref[0]).astype(BF16)


def out_proj(x, parts, w, mod, ln_g, ln_b, mod_row, tm=512):
    m, d = x.shape

    def mod_spec(kind):
        return pl.BlockSpec((1, 1, d), lambda i: (kind * MOD_ROWS + mod_row(i * tm), 0, 0))

    vec = pl.BlockSpec((1, d), lambda i: (0, 0))
    part = pl.BlockSpec((tm, MIXW), lambda i: (i, 0))
    return pl.pallas_call(
        _out_kernel,
        grid=(m // tm,),
        in_specs=[pl.BlockSpec((tm, d), lambda i: (i, 0)), part, part, part, part,
                  pl.BlockSpec((d, d), lambda i: (0, 0)),
                  mod_spec(2), vec, vec, mod_spec(4), mod_spec(3)],
        out_specs=[pl.BlockSpec((tm, d), lambda i: (i, 0)), pl.BlockSpec((tm, d), lambda i: (i, 0))],
        out_shape=[jax.ShapeDtypeStruct((m, d), F32), jax.ShapeDtypeStruct((m, d), BF16)],
        compiler_params=_cparams(("parallel",)),
        name="out_proj",
    )(x, *parts, w, mod, ln_g, ln_b, mod, mod)


def _mlp_kernel(u_ref, x_ref, w1_ref, w2_ref, gate_ref, g_ref, b_ref, o_ref, acc_ref):
    f = pl.program_id(1)
    h = _dot(u_ref[...], w1_ref[...])
    h = jnp.square(jnp.maximum(h, 0.0)).astype(BF16)
    part = _dot(h, w2_ref[...])

    @pl.when(f == 0)
    def _():
        acc_ref[...] = part

    @pl.when(f > 0)
    def _():
        acc_ref[...] += part

    @pl.when(f == pl.num_programs(1) - 1)
    def _():
        z = DEEPNORM_ALPHA * x_ref[...] + gate_ref[0] * acc_ref[...]
        o_ref[...] = _ln(z) * g_ref[...] + b_ref[...]


def mlp(u, x, w1, w2, mod, ln_g, ln_b, mod_row, tm=512, tf=1024):
    m, d = x.shape
    ff = w1.shape[1]
    vec = pl.BlockSpec((1, d), lambda i, f: (0, 0))
    return pl.pallas_call(
        _mlp_kernel,
        grid=(m // tm, ff // tf),
        in_specs=[pl.BlockSpec((tm, d), lambda i, f: (i, 0)),
                  pl.BlockSpec((tm, d), lambda i, f: (i, 0)),
                  pl.BlockSpec((d, tf), lambda i, f: (0, f)),
                  pl.BlockSpec((tf, d), lambda i, f: (f, 0)),
                  pl.BlockSpec((1, 1, d), lambda i, f: (5 * MOD_ROWS + mod_row(i * tm), 0, 0)),
                  vec, vec],
        out_specs=pl.BlockSpec((tm, d), lambda i, f: (i, 0)),
        out_shape=jax.ShapeDtypeStruct((m, d), F32),
        scratch_shapes=[pltpu.VMEM((tm, d), F32)],
        compiler_params=_cparams(("parallel", "arbitrary")),
        name="mlp",
    )(u, x, w1, w2, mod, ln_g, ln_b)


def _reorder_w_in(w):
    def cols(c0, n):
        return w[..., c0:c0 + n]

    aq, akv, lx, lg = cols(0, 512), cols(512, 512), cols(1024, 512), cols(1536, 512)
    dqkv, dz, ba, su = cols(2048, 1536), cols(3584, 512), cols(4096, 16), cols(4112, 512)
    pad = jnp.zeros(w.shape[:2] + (N_PROJ - COL_BA - 16,), w.dtype)
    return jnp.concatenate([dqkv, dz, aq, akv, lx, lg, su, ba, pad], axis=-1).astype(BF16)


def _block_diag(blocks):
    n, r, c = blocks.shape[-3:]
    eye = jnp.eye(n, dtype=blocks.dtype)
    out = blocks[..., :, :, None, :] * eye[:, None, :, None]
    return out.reshape(blocks.shape[:-3] + (n * r, n * c))


def _lru_params(wa, ba, wx, bx, lam):
    wg = jnp.concatenate([_block_diag(wa), _block_diag(wx)], axis=-1).astype(BF16)
    bg = jnp.concatenate([ba, bx], axis=-1)[:, None, :]
    sp = jax.nn.softplus(-lam)[:, None, :]
    return wg, bg, sp


def _s5_params(lam_re, lam_im, log_dt, b_re, b_im, c_re, c_im):
    dt = jnp.exp(log_dt)[..., None]
    mag = jnp.exp(lam_re * dt)
    abar_re = mag * jnp.cos(lam_im * dt)
    abar_im = mag * jnp.sin(lam_im * dt)
    den = lam_re * lam_re + lam_im * lam_im
    nr = abar_re - 1.0
    ni = abar_im
    f_re = (nr * lam_re + ni * lam_im) / den
    f_im = (ni * lam_re - nr * lam_im) / den
    bb_re = f_re[..., None] * b_re - f_im[..., None] * b_im
    bb_im = f_re[..., None] * b_im + f_im[..., None] * b_re
    to_in = lambda m: _block_diag(jnp.swapaxes(m, -1, -2))
    bb = jnp.concatenate([to_in(bb_re), to_in(bb_im)], axis=-1).astype(BF16)
    to_out = lambda m: _block_diag(jnp.swapaxes(m, -1, -2))
    n_out = S5_WIDTH // MXU
    per = S5_NSTATE // n_out
    c_r, c_i = to_out(c_re), -to_out(c_im)
    cc = jnp.stack([jnp.concatenate([c_r[:, nt * per:(nt + 1) * per, nt * MXU:(nt + 1) * MXU],
                                     c_i[:, nt * per:(nt + 1) * per, nt * MXU:(nt + 1) * MXU]], axis=1)
                    for nt in range(n_out)], axis=1).astype(BF16)
    a = jnp.concatenate([abar_re.reshape(2, 1, -1), abar_im.reshape(2, 1, -1)], axis=-1)
    return bb, cc, a


def _lane_row(vals, offset):
    return jnp.zeros((1, LANE), F32).at[0, offset:offset + vals.size].set(vals.reshape(-1))


def kernel(x_prompt, x_sample, cache_attn_k, cache_attn_v, state_rglru, state_delta, state_s5_re, state_s5_im, c, c_ctx, w_ada, b_ada, w_in, w_out, ln1_g, ln1_b, ln2_g, ln2_b, w_mlp1, w_mlp2, q_norm_g, k_norm_g, lru_conv_w, lru_conv_b, lru_wa, lru_ba, lru_wx, lru_bx, lru_lambda, dn_conv_w, dn_a_log, dn_dt_bias, dn_norm_g, s5_lambda_re, s5_lambda_im, s5_log_dt, s5_b_re, s5_b_im, s5_c_re, s5_c_im, s5_d, s5_glu_w, s5_glu_b):
    bp, tp, d = x_prompt.shape
    bs, ts, _ = x_sample.shape
    ctx_row = bs

    cond = jnp.concatenate([c, c_ctx[None, :], jnp.zeros((MOD_ROWS - bs - 1, d), F32)], axis=0)
    mods = ada_mod(cond, w_ada, b_ada)
    w_in_r = _reorder_w_in(w_in)
    w_out_b = w_out.astype(BF16)
    w1_b = w_mlp1.astype(BF16)
    w2_b = w_mlp2.astype(BF16)
    cos, sin = rope_tables(ts)

    streams = {
        'ctx': dict(x=x_prompt.reshape(bp * tp, d), b=bp, t=tp, nb=8, dn_nbat=2, dn_tb=tp,
                    mod_row=lambda tok: ctx_row),
        'lat': dict(x=x_sample.reshape(bs * ts, d), b=bs, t=ts, nb=bs, dn_nbat=bs, dn_tb=2 * DN_CHUNK,
                    mod_row=lambda tok: tok // ts),
    }
    ks, vs, lrus, dns, s5rs, s5is = [], [], [], [], [], []
    for l in range(DEPTH):
        mod = mods[l].reshape(MOD_ROWS, N_MOD, d).transpose(1, 0, 2).reshape(N_MOD * MOD_ROWS, 1, d)
        qg = q_norm_g[l].reshape(1, HEAD_DIM)
        kg = k_norm_g[l].reshape(1, HEAD_DIM)
        wg, bg, sp = _lru_params(lru_wa[l], lru_ba[l], lru_wx[l], lru_bx[l], lru_lambda[l])
        bb, cc, a5 = _s5_params(s5_lambda_re[l], s5_lambda_im[l], s5_log_dt[l], s5_b_re[l], s5_b_im[l],
                                s5_c_re[l], s5_c_im[l])
        al = _lane_row(dn_a_log[l], 2 * DN_HEADS)
        dtb = _lane_row(dn_dt_bias[l], 2 * DN_HEADS)
        for name, st in streams.items():
            b, t, nb = st['b'], st['t'], st['nb']
            is_ctx = name == 'ctx'
            proj = in_proj(st['x'], mod, w_in_r[l], st['mod_row'])
            proj3 = proj.reshape(b, t, N_PROJ)

            if is_ctx:
                attn, kn, vv = ctx_attention(proj, qg, kg, b, t)
                ks.append(kn.reshape(b, t, ATTN_KV_HEADS, HEAD_DIM))
                vs.append(vv.reshape(b, t, ATTN_KV_HEADS, HEAD_DIM))
                h0_lru = jnp.zeros((2, b, LRU_WIDTH), F32)
                s0_dn = jnp.zeros((2, b, DN_HEADS, DN_DK, DN_DV), F32)
                h0_s5 = jnp.zeros((2, b, 2 * S5_NSTATE), F32)
            else:
                q_s, k_s, v_s = lat_prep(proj, qg, kg, cos, sin, t)
                k_all = jnp.concatenate([cache_attn_k[:, l].reshape(b, -1, KV_WIDTH).astype(BF16),
                                         k_s.reshape(b, t, KV_WIDTH)], axis=1)
                v_all = jnp.concatenate([cache_attn_v[:, l].reshape(b, -1, KV_WIDTH).astype(BF16),
                                         v_s.reshape(b, t, KV_WIDTH)], axis=1)
                attn = lat_attention(q_s, k_all, v_all, t)
                h0_lru = jnp.swapaxes(state_rglru[:, l], 0, 1)
                s0_dn = jnp.swapaxes(state_delta[:, l], 0, 1)
                h0_s5 = jnp.swapaxes(jnp.concatenate([state_s5_re[:, l].reshape(b, 2, S5_NSTATE),
                                                      state_s5_im[:, l].reshape(b, 2, S5_NSTATE)], axis=-1), 0, 1)

            h_lru, lru_fin = lru_scan(proj3, h0_lru, lru_conv_w[l], lru_conv_b[l].reshape(1, -1), wg, bg, sp, nb)
            o_f, o_b, dn_fin = dn_scan(proj3, s0_dn, dn_conv_w[l], al, dtb, st['dn_nbat'], st['dn_tb'])
            y_s5, s5_fin = s5_scan(proj3, h0_s5, bb, cc, a5, nb)
            m = b * t
            lru_out, dn_out, s5_out = mix_post(
                proj, h_lru.reshape(2, m, MIXW), o_f.reshape(m, MIXW), o_b.reshape(m, MIXW),
                y_s5.reshape(2, m, MIXW), dn_norm_g[l].reshape(1, DN_DV), s5_d[l].reshape(1, MIXW),
                s5_glu_w[l].astype(BF16), s5_glu_b[l].reshape(1, MIXW))
            if is_ctx:
                lrus.append(jnp.swapaxes(lru_fin, 0, 1))
                dns.append(jnp.swapaxes(dn_fin, 0, 1))
                s5_fin = jnp.swapaxes(s5_fin, 0, 1)
                s5rs.append(s5_fin[..., :S5_NSTATE].reshape(b, 2, S5_GROUPS, S5_STATE))
                s5is.append(s5_fin[..., S5_NSTATE:].reshape(b, 2, S5_GROUPS, S5_STATE))

            x1, u2 = out_proj(st['x'], [attn, lru_out, dn_out, s5_out], w_out_b[l], mod,
                              ln1_g[l].reshape(1, d), ln1_b[l].reshape(1, d), st['mod_row'])
            st['x'] = mlp(u2, x1, w1_b[l], w2_b[l], mod, ln2_g[l].reshape(1, d), ln2_b[l].reshape(1, d),
                          st['mod_row'])

    y_prompt = streams['ctx']['x'].reshape(bp, tp, d)
    y_sample = streams['lat']['x'].reshape(bs, ts, d)
    return (y_prompt, y_sample, jnp.stack(ks, axis=1), jnp.stack(vs, axis=1), jnp.stack(lrus, axis=1),
            jnp.stack(dns, axis=1), jnp.stack(s5rs, axis=1), jnp.stack(s5is, axis=1))
```

```python
import functools

import jax
import jax.numpy as jnp
from jax import lax
from jax.experimental import pallas as pl
from jax.experimental.pallas import tpu as pltpu

F32 = jnp.float32
BF16 = jnp.bfloat16

D_MODEL = 2048
DEPTH = 2
GRID_W = 64
CONV_W = 4
EPS = 1e-6
ROPE_THETA = 10000.0
N_MOD = 6
HEAD_DIM = 128
ATTN_WIDTH = D_MODEL // 4
ATTN_HEADS = ATTN_WIDTH // HEAD_DIM
ATTN_KV_HEADS = ATTN_HEADS // 2
ATTN_GROUP = ATTN_HEADS // ATTN_KV_HEADS
KV_WIDTH = ATTN_KV_HEADS * HEAD_DIM
ATTN_SCALE = HEAD_DIM ** -0.5
LRU_WIDTH = D_MODEL // 4
LRU_BLOCKS = 8
LRU_C = 8.0
DN_DK = 128
DN_DV = 128
DN_WIDTH = D_MODEL // 4
DN_HEADS = DN_WIDTH // DN_DV
DN_CHUNK = 64
S5_WIDTH = D_MODEL // 4
S5_CH = 16
S5_GROUPS = S5_WIDTH // S5_CH
S5_STATE = 64
S5_NSTATE = S5_GROUPS * S5_STATE
DEEPNORM_ALPHA = (2 * DEPTH) ** 0.25
MIXW = D_MODEL // 4

COL_DQ = 0
COL_DZ = 1536
COL_Q = 2048
COL_KV = 2560
COL_LX = 3072
COL_LG = 3584
COL_SU = 4096
N_PROJ = 4608
N_BA = 2 * 2 * DN_HEADS
LANE = 128
MXU = 256

MOD_ROWS = 8
PITCH_PAD = 4
OUT_SUB = 256
VMEM_LIMIT = 56 * 1024 * 1024


def _cparams(sem):
    return pltpu.CompilerParams(dimension_semantics=sem, vmem_limit_bytes=VMEM_LIMIT)


def _ln(x):
    mu = jnp.mean(x, axis=-1, keepdims=True)
    xc = x - mu
    var = jnp.mean(xc * xc, axis=-1, keepdims=True)
    return xc * lax.rsqrt(var + EPS)


def _softplus(x):
    return jnp.maximum(x, 0.0) + jnp.log1p(jnp.exp(-jnp.abs(x)))


def _gelu(x):
    return 0.5 * x * (1.0 + jnp.tanh(0.7978845608028654 * (x + 0.044715 * (x * x * x))))


def _dot(a, b):
    return jnp.dot(a, b, preferred_element_type=F32)


def _dot_nt(a, b):
    return lax.dot_general(a, b, (((1,), (1,)), ((), ())), preferred_element_type=F32)


def _ada_kernel(c_ref, w_ref, b_ref, o_ref):
    cs = c_ref[...]
    s = cs * jax.nn.sigmoid(cs)
    o_ref[0] = _dot(s.astype(BF16), w_ref[0].astype(BF16)) + b_ref[0]


def ada_mod(cond, w_ada, b_ada, tn=1024):
    depth, d, n = w_ada.shape
    return pl.pallas_call(
        _ada_kernel,
        grid=(depth, n // tn),
        in_specs=[pl.BlockSpec((MOD_ROWS, d), lambda l, j: (0, 0)),
                  pl.BlockSpec((1, d, tn), lambda l, j: (l, 0, j)),
                  pl.BlockSpec((1, 1, tn), lambda l, j: (l, 0, j))],
        out_specs=pl.BlockSpec((1, MOD_ROWS, tn), lambda l, j: (l, 0, j)),
        out_shape=jax.ShapeDtypeStruct((depth, MOD_ROWS, n), F32),
        compiler_params=_cparams(("parallel", "parallel")),
        name="ada_mod",
    )(cond, w_ada, b_ada.reshape(depth, 1, n))


def _in_kernel(x_ref, sc_ref, sh_ref, w_ref, wba_ref, o_ref, ba_ref, u_ref):
    @pl.when(pl.program_id(1) == 0)
    def _():
        u = (_ln(x_ref[...]) * (1.0 + sc_ref[0]) + sh_ref[0]).astype(BF16)
        u_ref[...] = u
        ba_ref[...] = _dot(u, wba_ref[...])

    o_ref[...] = _dot(u_ref[...], w_ref[...])


def in_proj(x, mod, w, w_ba, mod_row, tm=512, tn=1536):
    m, d = x.shape
    n = w.shape[1]

    def mod_spec(kind):
        return pl.BlockSpec((1, 1, d), lambda i, j: (kind * MOD_ROWS + mod_row(i * tm), 0, 0))

    return pl.pallas_call(
        _in_kernel,
        grid=(m // tm, n // tn),
        in_specs=[pl.BlockSpec((tm, d), lambda i, j: (i, 0)),
                  mod_spec(1), mod_spec(0),
                  pl.BlockSpec((d, tn), lambda i, j: (0, j)),
                  pl.BlockSpec((d, LANE), lambda i, j: (0, 0))],
        out_specs=[pl.BlockSpec((tm, tn), lambda i, j: (i, j)),
                   pl.BlockSpec((tm, LANE), lambda i, j: (i, 0))],
        out_shape=[jax.ShapeDtypeStruct((m, n), F32), jax.ShapeDtypeStruct((m, LANE), F32)],
        scratch_shapes=[pltpu.VMEM((tm, d), BF16)],
        compiler_params=_cparams(("parallel", "arbitrary")),
        name="in_proj",
    )(x, mod, mod, w, w_ba)


def _rms_heads(x, g, heads):
    outs = []
    for h in range(heads):
        xh = x[:, h * HEAD_DIM:(h + 1) * HEAD_DIM]
        outs.append(xh * lax.rsqrt(jnp.mean(xh * xh, axis=-1, keepdims=True) + EPS) * g)
    return outs


def _softmax_av(q, k, v, scale):
    s = _dot_nt(q, k)
    if scale is not None:
        s = s * scale
    m = jnp.max(s, axis=-1, keepdims=True)
    p = jnp.exp(s - m)
    l = jnp.sum(p, axis=-1, keepdims=True)
    return _dot(p.astype(BF16), v) / l


def _ctx_attn_kernel(q_ref, kv_ref, qg_ref, kg_ref, o_ref, kn_ref, v_ref):
    qs = _rms_heads(q_ref[...], qg_ref[...], ATTN_HEADS)
    kv = kv_ref[...]
    ks = _rms_heads(kv[:, :KV_WIDTH], kg_ref[...], ATTN_KV_HEADS)
    v = kv[:, KV_WIDTH:]
    v_ref[...] = v
    t = q_ref.shape[0]
    for kh in range(ATTN_KV_HEADS):
        kn_ref[:, kh * HEAD_DIM:(kh + 1) * HEAD_DIM] = ks[kh]
        q2 = jnp.concatenate([qs[kh * ATTN_GROUP + g] for g in range(ATTN_GROUP)], axis=0).astype(BF16)
        o = _softmax_av(q2, ks[kh].astype(BF16), v[:, kh * HEAD_DIM:(kh + 1) * HEAD_DIM].astype(BF16), ATTN_SCALE)
        for g in range(ATTN_GROUP):
            h = kh * ATTN_GROUP + g
            o_ref[:, h * HEAD_DIM:(h + 1) * HEAD_DIM] = o[g * t:(g + 1) * t].astype(BF16)


def ctx_attention(proj, qg, kg, batch, seq):
    return pl.pallas_call(
        _ctx_attn_kernel,
        grid=(batch,),
        in_specs=[pl.BlockSpec((seq, ATTN_WIDTH), lambda b: (b, COL_Q // ATTN_WIDTH)),
                  pl.BlockSpec((seq, 2 * KV_WIDTH), lambda b: (b, COL_KV // (2 * KV_WIDTH))),
                  pl.BlockSpec((1, HEAD_DIM), lambda b: (0, 0)),
                  pl.BlockSpec((1, HEAD_DIM), lambda b: (0, 0))],
        out_specs=[pl.BlockSpec((seq, ATTN_WIDTH), lambda b: (b, 0)),
                   pl.BlockSpec((seq, KV_WIDTH), lambda b: (b, 0)),
                   pl.BlockSpec((seq, KV_WIDTH), lambda b: (b, 0))],
        out_shape=[jax.ShapeDtypeStruct((batch * seq, ATTN_WIDTH), BF16),
                   jax.ShapeDtypeStruct((batch * seq, KV_WIDTH), F32),
                   jax.ShapeDtypeStruct((batch * seq, KV_WIDTH), F32)],
        compiler_params=_cparams(("parallel",)),
        name="ctx_attention",
    )(proj, proj, qg, kg)


def _rope(x, cos, sin, heads):
    w = x.shape[-1]
    lane = lax.broadcasted_iota(jnp.int32, x.shape, 1)
    quarter = HEAD_DIM // 4
    partner = jnp.where((lane % (2 * quarter)) < quarter,
                        pltpu.roll(x, w - quarter, 1), pltpu.roll(x, quarter, 1))
    cos_t = jnp.concatenate([cos] * heads, axis=1)
    sin_t = jnp.concatenate([sin] * heads, axis=1)
    return x * cos_t + partner * sin_t


def _lat_prep_kernel(q_ref, kv_ref, qg_ref, kg_ref, cos_ref, sin_ref, qo_ref, ko_ref, vo_ref):
    qn = jnp.concatenate(_rms_heads(q_ref[...], qg_ref[...], ATTN_HEADS), axis=1)
    kv = kv_ref[...]
    kn = jnp.concatenate(_rms_heads(kv[:, :KV_WIDTH], kg_ref[...], ATTN_KV_HEADS), axis=1)
    cos = cos_ref[...]
    sin = sin_ref[...]
    qo_ref[...] = (_rope(qn, cos, sin, ATTN_HEADS) * ATTN_SCALE).astype(BF16)
    ko_ref[...] = _rope(kn, cos, sin, ATTN_KV_HEADS).astype(BF16)
    vo_ref[...] = kv[:, KV_WIDTH:].astype(BF16)


def lat_prep(proj, qg, kg, cos, sin, seq, tm=512):
    m = proj.shape[0]
    per = seq // tm
    return pl.pallas_call(
        _lat_prep_kernel,
        grid=(m // tm,),
        in_specs=[pl.BlockSpec((tm, ATTN_WIDTH), lambda i: (i, COL_Q // ATTN_WIDTH)),
                  pl.BlockSpec((tm, 2 * KV_WIDTH), lambda i: (i, COL_KV // (2 * KV_WIDTH))),
                  pl.BlockSpec((1, HEAD_DIM), lambda i: (0, 0)),
                  pl.BlockSpec((1, HEAD_DIM), lambda i: (0, 0)),
                  pl.BlockSpec((tm, HEAD_DIM), lambda i: (i % per, 0)),
                  pl.BlockSpec((tm, HEAD_DIM), lambda i: (i % per, 0))],
        out_specs=[pl.BlockSpec((tm, ATTN_WIDTH), lambda i: (i, 0)),
                   pl.BlockSpec((tm, KV_WIDTH), lambda i: (i, 0)),
                   pl.BlockSpec((tm, KV_WIDTH), lambda i: (i, 0))],
        out_shape=[jax.ShapeDtypeStruct((m, ATTN_WIDTH), BF16),
                   jax.ShapeDtypeStruct((m, KV_WIDTH), BF16),
                   jax.ShapeDtypeStruct((m, KV_WIDTH), BF16)],
        compiler_params=_cparams(("parallel",)),
        name="lat_prep",
    )(proj, proj, qg, kg, cos, sin)


def _lat_attn_kernel(q_ref, k_ref, v_ref, o_ref):
    k = k_ref[0]
    v = v_ref[0]
    for g in range(ATTN_GROUP):
        sl = slice(g * HEAD_DIM, (g + 1) * HEAD_DIM)
        o_ref[:, sl] = _softmax_av(q_ref[:, sl], k, v, None).astype(BF16)


def lat_attention(q, k_all, v_all, seq, tq=256):
    b, s, _ = k_all.shape
    nq = seq // tq
    gw = ATTN_GROUP * HEAD_DIM
    return pl.pallas_call(
        _lat_attn_kernel,
        grid=(b, ATTN_KV_HEADS, nq),
        in_specs=[pl.BlockSpec((tq, gw), lambda bi, kh, qi: (bi * nq + qi, kh)),
                  pl.BlockSpec((1, s, HEAD_DIM), lambda bi, kh, qi: (bi, 0, kh)),
                  pl.BlockSpec((1, s, HEAD_DIM), lambda bi, kh, qi: (bi, 0, kh))],
        out_specs=pl.BlockSpec((tq, gw), lambda bi, kh, qi: (bi * nq + qi, kh)),
        out_shape=jax.ShapeDtypeStruct((b * seq, ATTN_WIDTH), BF16),
        compiler_params=_cparams(("parallel", "parallel", "arbitrary")),
        name="lat_attention",
    )(q, k_all, v_all)


def rope_tables(seq):
    t = jnp.arange(seq)
    row = (t // GRID_W).astype(F32)
    col = (t % GRID_W).astype(F32)
    quarter = HEAD_DIM // 4
    inv_freq = jnp.power(ROPE_THETA, -jnp.arange(quarter, dtype=F32) / quarter)
    ar = row[:, None] * inv_freq[None, :]
    ac = col[:, None] * inv_freq[None, :]
    cos = jnp.concatenate([jnp.cos(ar), jnp.cos(ar), jnp.cos(ac), jnp.cos(ac)], axis=1)
    sin = jnp.concatenate([-jnp.sin(ar), jnp.sin(ar), -jnp.sin(ac), jnp.sin(ac)], axis=1)
    return cos, sin


def _chunk_index(d, i, n):
    return i + d * (n - 1 - 2 * i)


def _lru_kernel(x_ref, xp_ref, xn_ref, cw_ref, cb_ref, wg_ref, bg_ref, sp_ref, h0_ref,
                h_ref, fin_ref, xpad_ref, a_ref, b_ref, hc_ref, *, nb, tt):
    d = pl.program_id(0)
    i = pl.program_id(2)
    n = pl.num_programs(2)
    ci = _chunk_index(d, i, n)
    pitch = tt + PITCH_PAD
    rows = nb * pitch
    nsl = LRU_WIDTH // LANE

    @pl.when(i == 0)
    def _():
        xpad_ref[...] = jnp.zeros_like(xpad_ref)
        hc_ref[...] = h0_ref[0]

    for s in range(nb):
        base = 8 + s * pitch
        xpad_ref[base:base + tt, :] = x_ref[s]
        xpad_ref[base - 1:base, :] = jnp.where(ci == 0, 0.0, xp_ref[s, 7:8, :])
        xpad_ref[base + tt:base + tt + 2, :] = jnp.where(ci == n - 1, 0.0, xn_ref[s, 0:2, :])
    w = cw_ref[...]
    xc = cb_ref[...] + sum(xpad_ref[7 + j:7 + j + rows, :] * w[j:j + 1, :] for j in range(CONV_W))
    pre = _dot(xc.astype(BF16), wg_ref[0]) + bg_ref[0]
    r = jax.nn.sigmoid(pre[:, :LRU_WIDTH])
    ig = jax.nn.sigmoid(pre[:, LRU_WIDTH:])
    a = jnp.exp((-LRU_C) * r * sp_ref[0])
    inp = jnp.sqrt(1.0 - a * a) * (ig * xc)
    for c in range(nsl):
        a_ref[c] = a[:, c * LANE:(c + 1) * LANE]
        b_ref[c] = inp[:, c * LANE:(c + 1) * LANE]

    def body(t, carry):
        row = t + d * (tt - 1 - 2 * t)
        out = []
        for c in range(nsl):
            idx = (c, pl.ds(row, nb, stride=pitch), slice(None))
            h = a_ref[idx] * carry[c] + b_ref[idx]
            b_ref[idx] = h
            out.append(h)
        return tuple(out)

    carry = tuple(hc_ref[:, c * LANE:(c + 1) * LANE] for c in range(nsl))
    carry = lax.fori_loop(0, tt, body, carry, unroll=8)
    for c in range(nsl):
        hc_ref[:, c * LANE:(c + 1) * LANE] = carry[c]
        for s in range(nb):
            h_ref[0, s, :, c * LANE:(c + 1) * LANE] = b_ref[c, s * pitch:s * pitch + tt, :]

    @pl.when(i == n - 1)
    def _():
        fin_ref[0] = hc_ref[...]


def lru_scan(proj3, h0, cw, cb, wg, bg, sp, nb, tt=128):
    b, t, _ = proj3.shape
    w = LRU_WIDTH
    n = t // tt
    t8 = tt // 8
    rows = nb * (tt + PITCH_PAD)
    col = COL_LX // w

    def cidx(d, i):
        return _chunk_index(d, i, n)

    return pl.pallas_call(
        functools.partial(_lru_kernel, nb=nb, tt=tt),
        grid=(2, b // nb, n),
        in_specs=[pl.BlockSpec((nb, tt, w), lambda d, g, i: (g, cidx(d, i), col)),
                  pl.BlockSpec((nb, 8, w), lambda d, g, i: (g, jnp.maximum(cidx(d, i) * t8 - 1, 0), col)),
                  pl.BlockSpec((nb, 8, w), lambda d, g, i: (g, jnp.minimum((cidx(d, i) + 1) * t8, t // 8 - 1), col)),
                  pl.BlockSpec((CONV_W, w), lambda d, g, i: (0, 0)),
                  pl.BlockSpec((1, w), lambda d, g, i: (0, 0)),
                  pl.BlockSpec((1, w, 2 * w), lambda d, g, i: (d, 0, 0)),
                  pl.BlockSpec((1, 1, 2 * w), lambda d, g, i: (d, 0, 0)),
                  pl.BlockSpec((1, 1, w), lambda d, g, i: (d, 0, 0)),
                  pl.BlockSpec((1, nb, w), lambda d, g, i: (d, g, 0))],
        out_specs=[pl.BlockSpec((1, nb, tt, w), lambda d, g, i: (d, g, cidx(d, i), 0)),
                   pl.BlockSpec((1, nb, w), lambda d, g, i: (d, g, 0))],
        out_shape=[jax.ShapeDtypeStruct((2, b, t, w), F32), jax.ShapeDtypeStruct((2, b, w), F32)],
        scratch_shapes=[pltpu.VMEM((rows + 16, w), F32),
                        pltpu.VMEM((w // LANE, rows, LANE), F32),
                        pltpu.VMEM((w // LANE, rows, LANE), F32),
                        pltpu.VMEM((nb, w), F32)],
        compiler_params=_cparams(("parallel", "parallel", "arbitrary")),
        name="lru_scan",
    )(proj3, proj3, proj3, cw, cb, wg, bg, sp, h0)


def _s5_kernel(u_ref, bb_ref, cc_ref, a_ref, h0_ref, y_ref, fin_ref, up_ref, s_ref, hc_ref, *, nb, tt):
    d = pl.program_id(0)
    i = pl.program_id(2)
    n = pl.num_programs(2)
    pitch = tt + PITCH_PAD
    nsl = S5_NSTATE // LANE

    @pl.when(i == 0)
    def _():
        up_ref[...] = jnp.zeros_like(up_ref)
        hc_ref[...] = h0_ref[0]

    for s in range(nb):
        up_ref[s * pitch:s * pitch + tt, :] = u_ref[s]
    u2 = up_ref[...].astype(BF16)
    per_k = MXU // S5_CH * S5_STATE // MXU
    for nt in range(2 * S5_NSTATE // MXU):
        kt = (nt % (S5_NSTATE // MXU)) // per_k
        tile = _dot(u2[:, kt * MXU:(kt + 1) * MXU], bb_ref[0, kt * MXU:(kt + 1) * MXU, nt * MXU:(nt + 1) * MXU])
        s_ref[2 * nt] = tile[:, :LANE]
        s_ref[2 * nt + 1] = tile[:, LANE:]

    a_all = a_ref[0]
    group = 4
    for c0 in range(0, nsl, group):
        cs = list(range(c0, c0 + group))
        ar = [jnp.broadcast_to(a_all[:, c * LANE:(c + 1) * LANE], (nb, LANE)) for c in cs]
        ai = [jnp.broadcast_to(a_all[:, S5_NSTATE + c * LANE:S5_NSTATE + (c + 1) * LANE], (nb, LANE)) for c in cs]

        def body(t, carry, cs=cs, ar=ar, ai=ai):
            row = t + d * (tt - 1 - 2 * t)
            out = []
            for j, c in enumerate(cs):
                hr, hi = carry[2 * j], carry[2 * j + 1]
                ire = (c, pl.ds(row, nb, stride=pitch), slice(None))
                iim = (nsl + c, pl.ds(row, nb, stride=pitch), slice(None))
                nr = ar[j] * hr - ai[j] * hi + s_ref[ire]
                ni = ar[j] * hi + ai[j] * hr + s_ref[iim]
                s_ref[ire] = nr
                s_ref[iim] = ni
                out += [nr, ni]
            return tuple(out)

        carry = []
        for c in cs:
            carry += [hc_ref[:, c * LANE:(c + 1) * LANE], hc_ref[:, S5_NSTATE + c * LANE:S5_NSTATE + (c + 1) * LANE]]
        carry = lax.fori_loop(0, tt, body, tuple(carry), unroll=4)
        for j, c in enumerate(cs):
            hc_ref[:, c * LANE:(c + 1) * LANE] = carry[2 * j]
            hc_ref[:, S5_NSTATE + c * LANE:S5_NSTATE + (c + 1) * LANE] = carry[2 * j + 1]

    n_out = S5_WIDTH // MXU
    per_n = nsl // n_out
    for nt in range(n_out):
        slabs = [per_n * nt + k for k in range(per_n)] + [nsl + per_n * nt + k for k in range(per_n)]
        lhs = jnp.concatenate([s_ref[c] for c in slabs], axis=1).astype(BF16)
        y = _dot(lhs, cc_ref[0, nt])
        for s in range(nb):
            y_ref[0, s, :, nt * MXU:(nt + 1) * MXU] = y[s * pitch:s * pitch + tt]

    @pl.when(i == n - 1)
    def _():
        fin_ref[0] = hc_ref[...]


def s5_scan(proj3, h0, bb, cc, a, nb, tt=128):
    b, t, _ = proj3.shape
    w = S5_WIDTH
    n = t // tt
    rows = nb * (tt + PITCH_PAD)
    ns2 = 2 * S5_NSTATE

    def cidx(d, i):
        return _chunk_index(d, i, n)

    return pl.pallas_call(
        functools.partial(_s5_kernel, nb=nb, tt=tt),
        grid=(2, b // nb, n),
        in_specs=[pl.BlockSpec((nb, tt, w), lambda d, g, i: (g, cidx(d, i), COL_SU // w)),
                  pl.BlockSpec((1, w, ns2), lambda d, g, i: (d, 0, 0)),
                  pl.BlockSpec((1,) + cc.shape[1:], lambda d, g, i: (d, 0, 0, 0)),
                  pl.BlockSpec((1, 1, ns2), lambda d, g, i: (d, 0, 0)),
                  pl.BlockSpec((1, nb, ns2), lambda d, g, i: (d, g, 0))],
        out_specs=[pl.BlockSpec((1, nb, tt, w), lambda d, g, i: (d, g, cidx(d, i), 0)),
                   pl.BlockSpec((1, nb, ns2), lambda d, g, i: (d, g, 0))],
        out_shape=[jax.ShapeDtypeStruct((2, b, t, w), F32), jax.ShapeDtypeStruct((2, b, ns2), F32)],
        scratch_shapes=[pltpu.VMEM((rows, w), F32),
                        pltpu.VMEM((ns2 // LANE, rows, LANE), F32),
                        pltpu.VMEM((nb, ns2), F32)],
        compiler_params=_cparams(("parallel", "parallel", "arbitrary")),
        name="s5_scan",
    )(proj3, bb, cc, a, h0)


def _split3(x):
    x1 = x.astype(BF16)
    r1 = x - x1.astype(F32)
    x2 = r1.astype(BF16)
    x3 = (r1 - x2.astype(F32)).astype(BF16)
    return x1, x2, x3


def _mm(x, y):
    return _dot(x.astype(BF16), y.astype(BF16))


def _unit_tri_inverses(mats, rr, cc):
    same16 = (rr // 16) == (cc // 16)
    same32 = (rr // 32) == (cc // 32)
    p = [jnp.where(same16, -a, 0.0) for a in mats]
    t = [jnp.where(rr == cc, 1.0, x) for x in p]
    for _ in range(3):
        p = [_mm(x, x) for x in p]
        t = [x + _mm(x, y) for x, y in zip(t, p)]
    for mask in (same32 & jnp.logical_not(same16), jnp.logical_not(same32)):
        te = [_mm(x, jnp.where(mask, a, 0.0)) for x, a in zip(t, mats)]
        t = [x - _mm(y, x) for x, y in zip(t, te)]
    return t


def _dn_kernel(xf_ref, xpf_ref, xnf_ref, baf_ref, xb_ref, xpb_ref, xnb_ref, bab_ref, cw_ref, al_ref, dtb_ref,
               s0_ref, of_ref, ob_ref, fin_ref, xpad_ref, s_ref, *, nbat, tb, shared):
    i = pl.program_id(1)
    n = pl.num_programs(1)
    ch = DN_CHUNK
    nc = tb // ch
    hw = DN_HEADS * DN_DK

    @pl.when(i == 0)
    def _():
        s_ref[...] = s0_ref[...]

    w = cw_ref[...]
    rb = lax.broadcasted_iota(jnp.int32, (tb, tb), 0)
    cb = lax.broadcasted_iota(jnp.int32, (tb, tb), 1)
    rr = lax.broadcasted_iota(jnp.int32, (ch, ch), 0)
    cc = lax.broadcasted_iota(jnp.int32, (ch, ch), 1)
    al = al_ref[...]
    dtb = dtb_ref[...]

    def qkv_heads(slot, x_ref, xp_ref, xn_ref, bat, ci):
        xpad_ref[slot, 8:8 + tb, :] = x_ref[bat]
        xpad_ref[slot, 0:8, :] = jnp.where(ci == 0, 0.0, xp_ref[bat])
        xpad_ref[slot, 8 + tb:16 + tb, :] = jnp.where(ci == n - 1, 0.0, xn_ref[bat])
        xc = sum(xpad_ref[slot, 7 + j:7 + j + tb, :] * w[j:j + 1, :] for j in range(CONV_W))
        qkv = xc * jax.nn.sigmoid(xc)
        out = {}
        for c in range(nc):
            r0 = c * ch
            for h in range(DN_HEADS):
                q = qkv[r0:r0 + ch, h * DN_DK:(h + 1) * DN_DK]
                k = qkv[r0:r0 + ch, hw + h * DN_DK:hw + (h + 1) * DN_DK]
                v = qkv[r0:r0 + ch, 2 * hw + h * DN_DV:2 * hw + (h + 1) * DN_DV]
                q = q * lax.rsqrt(jnp.sum(q * q, axis=-1, keepdims=True) + EPS) * (DN_DK ** -0.5)
                k = k * lax.rsqrt(jnp.sum(k * k, axis=-1, keepdims=True) + EPS)
                out[(c, h)] = (q, k, v)
        return out

    units = []
    for d in range(2):
        reverse = d == 1
        ci = (n - 1 - i) if reverse else i
        x_ref, xp_ref, xn_ref, ba_ref = (xb_ref, xpb_ref, xnb_ref, bab_ref) if reverse else (xf_ref, xpf_ref, xnf_ref, baf_ref)
        tri = ((rb // ch) == (cb // ch)) & ((cb >= rb) if reverse else (cb <= rb))
        tri = jnp.where(tri, 1.0, 0.0).astype(BF16)
        incl = (cc >= rr) if reverse else (cc <= rr)
        strict = (cc > rr) if reverse else (cc < rr)
        if d == 0 or not shared:
            heads = [qkv_heads(d * nbat + bat, x_ref, xp_ref, xn_ref, bat, ci) for bat in range(nbat)]
        for bat in range(nbat):
            ba = ba_ref[bat]
            beta_all = jax.nn.sigmoid(ba)
            g_all = -jnp.exp(al) * _softplus(ba + dtb)
            gc_all = sum(_dot(tri, piece) for piece in _split3(g_all))
            for c in range(nc):
                r0 = c * ch
                gct = gc_all[r0:r0 + ch, :].T
                for h in range(DN_HEADS):
                    lane = d * DN_HEADS + h
                    beta = beta_all[r0:r0 + ch, lane:lane + 1]
                    gcol = gc_all[r0:r0 + ch, 2 * DN_HEADS + lane:2 * DN_HEADS + lane + 1]
                    grow = gct[2 * DN_HEADS + lane:2 * DN_HEADS + lane + 1, :]
                    q, k, v = heads[bat][(c, h)]
                    g_last = gcol[0:1, :] if reverse else gcol[ch - 1:ch, :]
                    units.append(dict(d=d, bat=bat, c=c, h=h, q=q, k=k, v=v, beta=beta, gcol=gcol, kb=k * beta,
                                      decay=jnp.where(incl, jnp.exp(gcol - grow), 0.0), strict=strict,
                                      g_last=g_last))

    for u in units:
        kbf = u['k'].astype(BF16)
        u['a'] = jnp.where(u['strict'], _dot_nt(u['kb'].astype(BF16), kbf) * u['decay'], 0.0)
        u['qk'] = (_dot_nt(u['q'].astype(BF16), kbf) * u['decay']).astype(BF16)
    t_inv = _unit_tri_inverses([u['a'] for u in units], rr, cc)
    for u, t in zip(units, t_inv):
        u['rhs'] = jnp.concatenate([u['v'] * u['beta'], u['kb'] * jnp.exp(u['gcol'])], axis=1)
        u['t_off'] = jnp.where(rr == cc, 0.0, t)
    for u in units:
        sol = u['rhs'] + _mm(u['t_off'], u['rhs'])
        u['u_val'] = sol[:, :DN_DV]
        u['w'] = sol[:, DN_DV:].astype(BF16)
        u['qe'] = (u['q'] * jnp.exp(u['gcol'])).astype(BF16)
        u['kdec_t'] = (u['k'] * jnp.exp(u['g_last'] - u['gcol'])).T.astype(BF16)
        u['eg'] = jnp.exp(u['g_last'])

    state = {(d, bat, h): s_ref[d, bat, h] for d in range(2) for bat in range(nbat) for h in range(DN_HEADS)}
    for j in range(nc):
        cur = [u for u in units if u['c'] == (nc - 1 - j if u['d'] == 1 else j)]
        keys = [(u['d'], u['bat'], u['h']) for u in cur]
        sb = [state[key].astype(BF16) for key in keys]
        ws = [_dot(u['w'], s) for u, s in zip(cur, sb)]
        qs = [_dot(u['qe'], s) for u, s in zip(cur, sb)]
        vb = [(u['u_val'] - x).astype(BF16) for u, x in zip(cur, ws)]
        os = [x + _dot(u['qk'], y) for u, x, y in zip(cur, qs, vb)]
        sn = [state[key] * u['eg'] + _dot(u['kdec_t'], y) for u, key, y in zip(cur, keys, vb)]
        for u, key, o, s in zip(cur, keys, os, sn):
            state[key] = s
            o_ref = ob_ref if u['d'] == 1 else of_ref
            o_ref[u['bat'], u['c'] * ch:(u['c'] + 1) * ch, u['h'] * DN_DV:(u['h'] + 1) * DN_DV] = o
    for (d, bat, h), s in state.items():
        s_ref[d, bat, h] = s

    @pl.when(i == n - 1)
    def _():
        fin_ref[...] = s_ref[...]


def dn_scan(proj3, ba3, s0, cw, al, dtb, nbat, tb):
    b, t, _ = proj3.shape
    n = t // tb
    t8 = tb // 8
    w3 = 3 * DN_WIDTH
    cq = COL_DQ // w3

    def data_specs(cidx):
        return [pl.BlockSpec((nbat, tb, w3), lambda g, i: (g, cidx(i), cq)),
                pl.BlockSpec((nbat, 8, w3), lambda g, i: (g, jnp.maximum(cidx(i) * t8 - 1, 0), cq)),
                pl.BlockSpec((nbat, 8, w3), lambda g, i: (g, jnp.minimum((cidx(i) + 1) * t8, t // 8 - 1), cq)),
                pl.BlockSpec((nbat, tb, LANE), lambda g, i: (g, cidx(i), 0))]

    fwd = lambda i: i
    bwd = lambda i: n - 1 - i
    st_spec = pl.BlockSpec((2, nbat, DN_HEADS, DN_DK, DN_DV), lambda g, i: (0, g, 0, 0, 0))
    return pl.pallas_call(
        functools.partial(_dn_kernel, nbat=nbat, tb=tb, shared=(n == 1)),
        grid=(b // nbat, n),
        in_specs=data_specs(fwd) + data_specs(bwd) + [
            pl.BlockSpec((CONV_W, w3), lambda g, i: (0, 0)),
            pl.BlockSpec((1, LANE), lambda g, i: (0, 0)),
            pl.BlockSpec((1, LANE), lambda g, i: (0, 0)),
            st_spec],
        out_specs=[pl.BlockSpec((nbat, tb, DN_WIDTH), lambda g, i: (g, i, 0)),
                   pl.BlockSpec((nbat, tb, DN_WIDTH), lambda g, i: (g, n - 1 - i, 0)),
                   st_spec],
        out_shape=[jax.ShapeDtypeStruct((b, t, DN_WIDTH), F32),
                   jax.ShapeDtypeStruct((b, t, DN_WIDTH), F32),
                   jax.ShapeDtypeStruct((2, b, DN_HEADS, DN_DK, DN_DV), F32)],
        scratch_shapes=[pltpu.VMEM((2 * nbat, tb + 16, w3), F32),
                        pltpu.VMEM((2, nbat, DN_HEADS, DN_DK, DN_DV), F32)],
        compiler_params=_cparams(("parallel", "arbitrary")),
        name="dn_scan",
    )(proj3, proj3, proj3, ba3, proj3, proj3, proj3, ba3, cw, al, dtb, s0)


def _post_kernel(hf_ref, hb_ref, lg_ref, of_ref, ob_ref, dz_ref, ng_ref, yf_ref, yb_ref, su_ref, sd_ref,
                 gw_ref, gb_ref, lru_ref, dn_ref, s5_ref):
    lru_ref[...] = ((hf_ref[0, 0] + hb_ref[0, 0]) * _gelu(lg_ref[...])).astype(BF16)
    o = of_ref[0] + ob_ref[0]
    dz = dz_ref[...]
    for h in range(DN_HEADS):
        sl = slice(h * DN_DV, (h + 1) * DN_DV)
        oh = o[:, sl]
        oh = oh * lax.rsqrt(jnp.mean(oh * oh, axis=-1, keepdims=True) + EPS) * ng_ref[...]
        zh = dz[:, sl]
        dn_ref[:, sl] = (oh * (zh * jax.nn.sigmoid(zh))).astype(BF16)
    y = yf_ref[0, 0] + yb_ref[0, 0] + sd_ref[...] * su_ref[...]
    gy = _gelu(y)
    s5_ref[...] = (gy * jax.nn.sigmoid(_dot(gy.astype(BF16), gw_ref[...]) + gb_ref[...])).astype(BF16)


def mix_post(proj, h_lru, o_f, o_b, y_s5, ng, sd, gw, gb):
    m = proj.shape[0]
    _, b, t, w = h_lru.shape
    tm = min(512, t)
    per = t // tm
    tok = pl.BlockSpec((tm, w), lambda i: (i, 0))

    def pcol(c0):
        return pl.BlockSpec((tm, w), lambda i: (i, c0 // w))

    def dirspec(d):
        return pl.BlockSpec((1, 1, tm, w), lambda i: (d, i // per, i % per, 0))

    seq = pl.BlockSpec((1, tm, w), lambda i: (i // per, i % per, 0))

    def vec(n):
        return pl.BlockSpec((1, n), lambda i: (0, 0))

    out = jax.ShapeDtypeStruct((m, w), BF16)
    return pl.pallas_call(
        _post_kernel,
        grid=(m // tm,),
        in_specs=[dirspec(0), dirspec(1), pcol(COL_LG), seq, seq, pcol(COL_DZ), vec(DN_DV),
                  dirspec(0), dirspec(1), pcol(COL_SU), vec(w), pl.BlockSpec((w, w), lambda i: (0, 0)), vec(w)],
        out_specs=[tok, tok, tok],
        out_shape=[out, out, out],
        compiler_params=_cparams(("parallel",)),
        name="mix_post",
    )(h_lru, h_lru, proj, o_f, o_b, proj, ng, y_s5, y_s5, proj, sd, gw, gb)


def _out_kernel(x_ref, a_ref, b_ref, c_ref, d_ref, w_ref, gate_ref, g_ref, bb_ref, sc_ref, sh_ref,
                x1_ref, u2_ref):
    q = MIXW
    for r0 in range(0, x_ref.shape[0], OUT_SUB):
        rows = slice(r0, r0 + OUT_SUB)
        y = _dot(a_ref[rows, :], w_ref[0:q, :])
        y += _dot(b_ref[rows, :], w_ref[q:2 * q, :])
        y += _dot(c_ref[rows, :], w_ref[2 * q:3 * q, :])
        y += _dot(d_ref[rows, :], w_ref[3 * q:, :])
        z = DEEPNORM_ALPHA * x_ref[rows, :] + gate_ref[0] * y
        x1 = _ln(z) * g_ref[...] + bb_ref[...]
        x1_ref[rows, :] = x1
        u2_ref[rows, :] = (_ln(x1) * (1.0 + sc_ref[0]) + sh_ref[0]).astype(BF16)


def out_proj(x, parts, w, mod, ln_g, ln_b, mod_row, tm=512):
    m, d = x.shape

    def mod_spec(kind):
        return pl.BlockSpec((1, 1, d), lambda i: (kind * MOD_ROWS + mod_row(i * tm), 0, 0))

    vec = pl.BlockSpec((1, d), lambda i: (0, 0))
    part = pl.BlockSpec((tm, MIXW), lambda i: (i, 0))
    return pl.pallas_call(
        _out_kernel,
        grid=(m // tm,),
        in_specs=[pl.BlockSpec((tm, d), lambda i: (i, 0)), part, part, part, part,
                  pl.BlockSpec((d, d), lambda i: (0, 0)),
                  mod_spec(2), vec, vec, mod_spec(4), mod_spec(3)],
        out_specs=[pl.BlockSpec((tm, d), lambda i: (i, 0)), pl.BlockSpec((tm, d), lambda i: (i, 0))],
        out_shape=[jax.ShapeDtypeStruct((m, d), F32), jax.ShapeDtypeStruct((m, d), BF16)],
        compiler_params=_cparams(("parallel",)),
        name="out_proj",
    )(x, *parts, w, mod, ln_g, ln_b, mod, mod)


def _mlp_kernel(u_ref, x_ref, w1_ref, w2_ref, gate_ref, g_ref, b_ref, o_ref, acc_ref):
    f = pl.program_id(1)
    h = _dot(u_ref[...], w1_ref[...])
    h = jnp.square(jnp.maximum(h, 0.0)).astype(BF16)
    part = _dot(h, w2_ref[...])

    @pl.when(f == 0)
    def _():
        acc_ref[...] = part

    @pl.when(f > 0)
    def _():
        acc_ref[...] += part

    @pl.when(f == pl.num_programs(1) - 1)
    def _():
        z = DEEPNORM_ALPHA * x_ref[...] + gate_ref[0] * acc_ref[...]
        o_ref[...] = _ln(z) * g_ref[...] + b_ref[...]


def mlp(u, x, w1, w2, mod, ln_g, ln_b, mod_row, tm=512, tf=1024):
    m, d = x.shape
    ff = w1.shape[1]
    vec = pl.BlockSpec((1, d), lambda i, f: (0, 0))
    return pl.pallas_call(
        _mlp_kernel,
        grid=(m // tm, ff // tf),
        in_specs=[pl.BlockSpec((tm, d), lambda i, f: (i, 0)),
                  pl.BlockSpec((tm, d), lambda i, f: (i, 0)),
                  pl.BlockSpec((d, tf), lambda i, f: (0, f)),
                  pl.BlockSpec((tf, d), lambda i, f: (f, 0)),
                  pl.BlockSpec((1, 1, d), lambda i, f: (5 * MOD_ROWS + mod_row(i * tm), 0, 0)),
                  vec, vec],
        out_specs=pl.BlockSpec((tm, d), lambda i, f: (i, 0)),
        out_shape=jax.ShapeDtypeStruct((m, d), F32),
        scratch_shapes=[pltpu.VMEM((tm, d), F32)],
        compiler_params=_cparams(("parallel", "arbitrary")),
        name="mlp",
    )(u, x, w1, w2, mod, ln_g, ln_b)


def _reorder_w_in(w):
    def cols(c0, n):
        return w[..., c0:c0 + n]

    aq, akv, lx, lg = cols(0, 512), cols(512, 512), cols(1024, 512), cols(1536, 512)
    dqkv, dz, ba, su = cols(2048, 1536), cols(3584, 512), cols(4096, N_BA), cols(4096 + N_BA, 512)
    pad = jnp.zeros(w.shape[:2] + (LANE - N_BA,), w.dtype)
    main = jnp.concatenate([dqkv, dz, aq, akv, lx, lg, su], axis=-1).astype(BF16)
    return main, jnp.concatenate([ba, pad], axis=-1).astype(BF16)


def _block_diag(blocks):
    n, r, c = blocks.shape[-3:]
    eye = jnp.eye(n, dtype=blocks.dtype)
    out = blocks[..., :, :, None, :] * eye[:, None, :, None]
    return out.reshape(blocks.shape[:-3] + (n * r, n * c))


def _lru_params(wa, ba, wx, bx, lam):
    wg = jnp.concatenate([_block_diag(wa), _block_diag(wx)], axis=-1).astype(BF16)
    bg = jnp.concatenate([ba, bx], axis=-1)[:, None, :]
    sp = jax.nn.softplus(-lam)[:, None, :]
    return wg, bg, sp


def _s5_params(lam_re, lam_im, log_dt, b_re, b_im, c_re, c_im):
    dt = jnp.exp(log_dt)[..., None]
    mag = jnp.exp(lam_re * dt)
    abar_re = mag * jnp.cos(lam_im * dt)
    abar_im = mag * jnp.sin(lam_im * dt)
    den = lam_re * lam_re + lam_im * lam_im
    nr = abar_re - 1.0
    ni = abar_im
    f_re = (nr * lam_re + ni * lam_im) / den
    f_im = (ni * lam_re - nr * lam_im) / den
    bb_re = f_re[..., None] * b_re - f_im[..., None] * b_im
    bb_im = f_re[..., None] * b_im + f_im[..., None] * b_re
    to_in = lambda m: _block_diag(jnp.swapaxes(m, -1, -2))
    bb = jnp.concatenate([to_in(bb_re), to_in(bb_im)], axis=-1).astype(BF16)
    to_out = lambda m: _block_diag(jnp.swapaxes(m, -1, -2))
    n_out = S5_WIDTH // MXU
    per = S5_NSTATE // n_out
    c_r, c_i = to_out(c_re), -to_out(c_im)
    cc = jnp.stack([jnp.concatenate([c_r[:, nt * per:(nt + 1) * per, nt * MXU:(nt + 1) * MXU],
                                     c_i[:, nt * per:(nt + 1) * per, nt * MXU:(nt + 1) * MXU]], axis=1)
                    for nt in range(n_out)], axis=1).astype(BF16)
    a = jnp.concatenate([abar_re.reshape(2, 1, -1), abar_im.reshape(2, 1, -1)], axis=-1)
    return bb, cc, a


def _lane_row(vals, offset):
    return jnp.zeros((1, LANE), F32).at[0, offset:offset + vals.size].set(vals.reshape(-1))


def kernel(x_prompt, x_sample, cache_attn_k, cache_attn_v, state_rglru, state_delta, state_s5_re, state_s5_im, c, c_ctx, w_ada, b_ada, w_in, w_out, ln1_g, ln1_b, ln2_g, ln2_b, w_mlp1, w_mlp2, q_norm_g, k_norm_g, lru_conv_w, lru_conv_b, lru_wa, lru_ba, lru_wx, lru_bx, lru_lambda, dn_conv_w, dn_a_log, dn_dt_bias, dn_norm_g, s5_lambda_re, s5_lambda_im, s5_log_dt, s5_b_re, s5_b_im, s5_c_re, s5_c_im, s5_d, s5_glu_w, s5_glu_b):
    bp, tp, d = x_prompt.shape
    bs, ts, _ = x_sample.shape
    ctx_row = bs

    cond = jnp.concatenate([c, c_ctx[None, :], jnp.zeros((MOD_ROWS - bs - 1, d), F32)], axis=0)
    mods = ada_mod(cond, w_ada, b_ada)
    w_in_r, w_ba = _reorder_w_in(w_in)
    w_out_b = w_out.astype(BF16)
    w1_b = w_mlp1.astype(BF16)
    w2_b = w_mlp2.astype(BF16)
    cos, sin = rope_tables(ts)

    streams = {
        'ctx': dict(x=x_prompt.reshape(bp * tp, d), b=bp, t=tp, nb=8, dn_nbat=2, dn_tb=tp,
                    mod_row=lambda tok: ctx_row),
        'lat': dict(x=x_sample.reshape(bs * ts, d), b=bs, t=ts, nb=bs, dn_nbat=bs, dn_tb=2 * DN_CHUNK,
                    mod_row=lambda tok: tok // ts),
    }
    ks, vs, lrus, dns, s5rs, s5is = [], [], [], [], [], []
    for l in range(DEPTH):
        mod = mods[l].reshape(MOD_ROWS, N_MOD, d).transpose(1, 0, 2).reshape(N_MOD * MOD_ROWS, 1, d)
        qg = q_norm_g[l].reshape(1, HEAD_DIM)
        kg = k_norm_g[l].reshape(1, HEAD_DIM)
        wg, bg, sp = _lru_params(lru_wa[l], lru_ba[l], lru_wx[l], lru_bx[l], lru_lambda[l])
        bb, cc, a5 = _s5_params(s5_lambda_re[l], s5_lambda_im[l], s5_log_dt[l], s5_b_re[l], s5_b_im[l],
                                s5_c_re[l], s5_c_im[l])
        al = _lane_row(dn_a_log[l], 2 * DN_HEADS)
        dtb = _lane_row(dn_dt_bias[l], 2 * DN_HEADS)
        for name, st in streams.items():
            b, t, nb = st['b'], st['t'], st['nb']
            is_ctx = name == 'ctx'
            proj, ba = in_proj(st['x'], mod, w_in_r[l], w_ba[l], st['mod_row'])
            proj3 = proj.reshape(b, t, N_PROJ)
            ba3 = ba.reshape(b, t, LANE)

            if is_ctx:
                attn, kn, vv = ctx_attention(proj, qg, kg, b, t)
                ks.append(kn.reshape(b, t, ATTN_KV_HEADS, HEAD_DIM))
                vs.append(vv.reshape(b, t, ATTN_KV_HEADS, HEAD_DIM))
                h0_lru = jnp.zeros((2, b, LRU_WIDTH), F32)
                s0_dn = jnp.zeros((2, b, DN_HEADS, DN_DK, DN_DV), F32)
                h0_s5 = jnp.zeros((2, b, 2 * S5_NSTATE), F32)
            else:
                q_s, k_s, v_s = lat_prep(proj, qg, kg, cos, sin, t)
                k_all = jnp.concatenate([cache_attn_k[:, l].reshape(b, -1, KV_WIDTH).astype(BF16),
                                         k_s.reshape(b, t, KV_WIDTH)], axis=1)
                v_all = jnp.concatenate([cache_attn_v[:, l].reshape(b, -1, KV_WIDTH).astype(BF16),
                                         v_s.reshape(b, t, KV_WIDTH)], axis=1)
                attn = lat_attention(q_s, k_all, v_all, t)
                h0_lru = jnp.swapaxes(state_rglru[:, l], 0, 1)
                s0_dn = jnp.swapaxes(state_delta[:, l], 0, 1)
                h0_s5 = jnp.swapaxes(jnp.concatenate([state_s5_re[:, l].reshape(b, 2, S5_NSTATE),
                                                      state_s5_im[:, l].reshape(b, 2, S5_NSTATE)], axis=-1), 0, 1)

            h_lru, lru_fin = lru_scan(proj3, h0_lru, lru_conv_w[l], lru_conv_b[l].reshape(1, -1), wg, bg, sp, nb)
            o_f, o_b, dn_fin = dn_scan(proj3, ba3, s0_dn, dn_conv_w[l], al, dtb, st['dn_nbat'], st['dn_tb'])
            y_s5, s5_fin = s5_scan(proj3, h0_s5, bb, cc, a5, nb)
            m = b * t
            lru_out, dn_out, s5_out = mix_post(
                proj, h_lru, o_f, o_b, y_s5, dn_norm_g[l].reshape(1, DN_DV), s5_d[l].reshape(1, MIXW),
                s5_glu_w[l].astype(BF16), s5_glu_b[l].reshape(1, MIXW))
            if is_ctx:
                lrus.append(jnp.swapaxes(lru_fin, 0, 1))
                dns.append(jnp.swapaxes(dn_fin, 0, 1))
                s5_fin = jnp.swapaxes(s5_fin, 0, 1)
                s5rs.append(s5_fin[..., :S5_NSTATE].reshape(b, 2, S5_GROUPS, S5_STATE))
                s5is.append(s5_fin[..., S5_NSTATE:].reshape(b, 2, S5_GROUPS, S5_STATE))

            x1, u2 = out_proj(st['x'], [attn, lru_out, dn_out, s5_out], w_out_b[l], mod,
                              ln1_g[l].reshape(1, d), ln1_b[l].reshape(1, d), st['mod_row'])
            st['x'] = mlp(u2, x1, w1_b[l], w2_b[l], mod, ln2_g[l].reshape(1, d), ln2_b[l].reshape(1, d),
                          st['mod_row'])

    y_prompt = streams['ctx']['x'].reshape(bp, tp, d)
    y_sample = streams['lat']['x'].reshape(bs, ts, d)
    return (y_prompt, y_sample, jnp.stack(ks, axis=1), jnp.stack(vs, axis=1), jnp.stack(lrus, axis=1),
            jnp.stack(dns, axis=1), jnp.stack(s5rs, axis=1), jnp.stack(s5is, axis=1))
```

```python
import functools

import jax
import jax.numpy as jnp
from jax import lax
from jax.experimental import pallas as pl
from jax.experimental.pallas import tpu as pltpu

F32 = jnp.float32
BF16 = jnp.bfloat16

D_MODEL = 2048
DEPTH = 2
GRID_W = 64
CONV_W = 4
EPS = 1e-6
ROPE_THETA = 10000.0
N_MOD = 6
HEAD_DIM = 128
ATTN_WIDTH = D_MODEL // 4
ATTN_HEADS = ATTN_WIDTH // HEAD_DIM
ATTN_KV_HEADS = ATTN_HEADS // 2
ATTN_GROUP = ATTN_HEADS // ATTN_KV_HEADS
KV_WIDTH = ATTN_KV_HEADS * HEAD_DIM
ATTN_SCALE = HEAD_DIM ** -0.5
LRU_WIDTH = D_MODEL // 4
LRU_BLOCKS = 8
LRU_C = 8.0
DN_DK = 128
DN_DV = 128
DN_WIDTH = D_MODEL // 4
DN_HEADS = DN_WIDTH // DN_DV
DN_CHUNK = 64
S5_WIDTH = D_MODEL // 4
S5_CH = 16
S5_GROUPS = S5_WIDTH // S5_CH
S5_STATE = 64
S5_NSTATE = S5_GROUPS * S5_STATE
DEEPNORM_ALPHA = (2 * DEPTH) ** 0.25
MIXW = D_MODEL // 4

COL_DQ = 0
COL_DZ = 1536
COL_Q = 2048
COL_KV = 2560
COL_LX = 3072
COL_LG = 3584
COL_SU = 4096
N_PROJ = 4608
N_BA = 2 * 2 * DN_HEADS
LANE = 128
MXU = 256

MOD_ROWS = 8
PITCH_PAD = 4
OUT_SUB = 256
VMEM_LIMIT = 56 * 1024 * 1024


def _cparams(sem):
    return pltpu.CompilerParams(dimension_semantics=sem, vmem_limit_bytes=VMEM_LIMIT)


def _ln(x):
    mu = jnp.mean(x, axis=-1, keepdims=True)
    xc = x - mu
    var = jnp.mean(xc * xc, axis=-1, keepdims=True)
    return xc * lax.rsqrt(var + EPS)


def _softplus(x):
    return jnp.maximum(x, 0.0) + jnp.log1p(jnp.exp(-jnp.abs(x)))


def _gelu(x):
    return 0.5 * x * (1.0 + jnp.tanh(0.7978845608028654 * (x + 0.044715 * (x * x * x))))


def _dot(a, b):
    return jnp.dot(a, b, preferred_element_type=F32)


def _dot_nt(a, b):
    return lax.dot_general(a, b, (((1,), (1,)), ((), ())), preferred_element_type=F32)


def _ada_kernel(c_ref, w_ref, b_ref, o_ref):
    cs = c_ref[...]
    s = cs * jax.nn.sigmoid(cs)
    o_ref[0] = _dot(s.astype(BF16), w_ref[0].astype(BF16)) + b_ref[0]


def ada_mod(cond, w_ada, b_ada, tn=1024):
    depth, d, n = w_ada.shape
    return pl.pallas_call(
        _ada_kernel,
        grid=(depth, n // tn),
        in_specs=[pl.BlockSpec((MOD_ROWS, d), lambda l, j: (0, 0)),
                  pl.BlockSpec((1, d, tn), lambda l, j: (l, 0, j)),
                  pl.BlockSpec((1, 1, tn), lambda l, j: (l, 0, j))],
        out_specs=pl.BlockSpec((1, MOD_ROWS, tn), lambda l, j: (l, 0, j)),
        out_shape=jax.ShapeDtypeStruct((depth, MOD_ROWS, n), F32),
        compiler_params=_cparams(("parallel", "parallel")),
        name="ada_mod",
    )(cond, w_ada, b_ada.reshape(depth, 1, n))


def _in_kernel(x_ref, sc_ref, sh_ref, w_ref, wba_ref, o_ref, ba_ref, u_ref):
    @pl.when(pl.program_id(1) == 0)
    def _():
        u = (_ln(x_ref[...]) * (1.0 + sc_ref[0]) + sh_ref[0]).astype(BF16)
        u_ref[...] = u
        ba_ref[...] = _dot(u, wba_ref[...])

    o_ref[...] = _dot(u_ref[...], w_ref[...])


def in_proj(x, mod, w, w_ba, layer, mod_row, tm=512, tn=1536):
    m, d = x.shape
    n = w.shape[2]

    def mod_spec(kind):
        return pl.BlockSpec((1, 1, d), lambda i, j: (kind * MOD_ROWS + mod_row(i * tm), 0, 0))

    return pl.pallas_call(
        _in_kernel,
        grid=(m // tm, n // tn),
        in_specs=[pl.BlockSpec((tm, d), lambda i, j: (i, 0)),
                  mod_spec(1), mod_spec(0),
                  pl.BlockSpec((None, d, tn), lambda i, j: (layer, 0, j)),
                  pl.BlockSpec((None, d, LANE), lambda i, j: (layer, 0, 0))],
        out_specs=[pl.BlockSpec((tm, tn), lambda i, j: (i, j)),
                   pl.BlockSpec((tm, LANE), lambda i, j: (i, 0))],
        out_shape=[jax.ShapeDtypeStruct((m, n), F32), jax.ShapeDtypeStruct((m, LANE), F32)],
        scratch_shapes=[pltpu.VMEM((tm, d), BF16)],
        compiler_params=_cparams(("parallel", "arbitrary")),
        name="in_proj",
    )(x, mod, mod, w, w_ba)


def _rms_heads(x, g, heads):
    outs = []
    for h in range(heads):
        xh = x[:, h * HEAD_DIM:(h + 1) * HEAD_DIM]
        outs.append(xh * lax.rsqrt(jnp.mean(xh * xh, axis=-1, keepdims=True) + EPS) * g)
    return outs


def _softmax_av(q, k, v, scale):
    s = _dot_nt(q, k)
    if scale is not None:
        s = s * scale
    m = jnp.max(s, axis=-1, keepdims=True)
    p = jnp.exp(s - m)
    l = jnp.sum(p, axis=-1, keepdims=True)
    return _dot(p.astype(BF16), v) / l


def _ctx_attn_kernel(q_ref, kv_ref, qg_ref, kg_ref, o_ref, kn_ref, v_ref):
    qs = _rms_heads(q_ref[...], qg_ref[...], ATTN_HEADS)
    kv = kv_ref[...]
    ks = _rms_heads(kv[:, :KV_WIDTH], kg_ref[...], ATTN_KV_HEADS)
    v = kv[:, KV_WIDTH:]
    v_ref[...] = v
    t = q_ref.shape[0]
    for kh in range(ATTN_KV_HEADS):
        kn_ref[:, kh * HEAD_DIM:(kh + 1) * HEAD_DIM] = ks[kh]
        q2 = jnp.concatenate([qs[kh * ATTN_GROUP + g] for g in range(ATTN_GROUP)], axis=0).astype(BF16)
        o = _softmax_av(q2, ks[kh].astype(BF16), v[:, kh * HEAD_DIM:(kh + 1) * HEAD_DIM].astype(BF16), ATTN_SCALE)
        for g in range(ATTN_GROUP):
            h = kh * ATTN_GROUP + g
            o_ref[:, h * HEAD_DIM:(h + 1) * HEAD_DIM] = o[g * t:(g + 1) * t].astype(BF16)


def ctx_attention(proj, qg, kg, batch, seq):
    return pl.pallas_call(
        _ctx_attn_kernel,
        grid=(batch,),
        in_specs=[pl.BlockSpec((seq, ATTN_WIDTH), lambda b: (b, COL_Q // ATTN_WIDTH)),
                  pl.BlockSpec((seq, 2 * KV_WIDTH), lambda b: (b, COL_KV // (2 * KV_WIDTH))),
                  pl.BlockSpec((1, HEAD_DIM), lambda b: (0, 0)),
                  pl.BlockSpec((1, HEAD_DIM), lambda b: (0, 0))],
        out_specs=[pl.BlockSpec((seq, ATTN_WIDTH), lambda b: (b, 0)),
                   pl.BlockSpec((seq, KV_WIDTH), lambda b: (b, 0)),
                   pl.BlockSpec((seq, KV_WIDTH), lambda b: (b, 0))],
        out_shape=[jax.ShapeDtypeStruct((batch * seq, ATTN_WIDTH), BF16),
                   jax.ShapeDtypeStruct((batch * seq, KV_WIDTH), F32),
                   jax.ShapeDtypeStruct((batch * seq, KV_WIDTH), F32)],
        compiler_params=_cparams(("parallel",)),
        name="ctx_attention",
    )(proj, proj, qg, kg)


def _rope(x, cos, sin, heads):
    w = x.shape[-1]
    lane = lax.broadcasted_iota(jnp.int32, x.shape, 1)
    quarter = HEAD_DIM // 4
    partner = jnp.where((lane % (2 * quarter)) < quarter,
                        pltpu.roll(x, w - quarter, 1), pltpu.roll(x, quarter, 1))
    cos_t = jnp.concatenate([cos] * heads, axis=1)
    sin_t = jnp.concatenate([sin] * heads, axis=1)
    return x * cos_t + partner * sin_t


def _lat_prep_kernel(q_ref, kv_ref, qg_ref, kg_ref, cos_ref, sin_ref, qo_ref, ko_ref, vo_ref):
    qn = jnp.concatenate(_rms_heads(q_ref[...], qg_ref[...], ATTN_HEADS), axis=1)
    kv = kv_ref[...]
    kn = jnp.concatenate(_rms_heads(kv[:, :KV_WIDTH], kg_ref[...], ATTN_KV_HEADS), axis=1)
    cos = cos_ref[...]
    sin = sin_ref[...]
    qo_ref[...] = (_rope(qn, cos, sin, ATTN_HEADS) * ATTN_SCALE).astype(BF16)
    ko_ref[...] = _rope(kn, cos, sin, ATTN_KV_HEADS).astype(BF16)
    vo_ref[...] = kv[:, KV_WIDTH:].astype(BF16)


def lat_prep(proj, qg, kg, cos, sin, seq, tm=512):
    m = proj.shape[0]
    per = seq // tm
    return pl.pallas_call(
        _lat_prep_kernel,
        grid=(m // tm,),
        in_specs=[pl.BlockSpec((tm, ATTN_WIDTH), lambda i: (i, COL_Q // ATTN_WIDTH)),
                  pl.BlockSpec((tm, 2 * KV_WIDTH), lambda i: (i, COL_KV // (2 * KV_WIDTH))),
                  pl.BlockSpec((1, HEAD_DIM), lambda i: (0, 0)),
                  pl.BlockSpec((1, HEAD_DIM), lambda i: (0, 0)),
                  pl.BlockSpec((tm, HEAD_DIM), lambda i: (i % per, 0)),
                  pl.BlockSpec((tm, HEAD_DIM), lambda i: (i % per, 0))],
        out_specs=[pl.BlockSpec((tm, ATTN_WIDTH), lambda i: (i, 0)),
                   pl.BlockSpec((tm, KV_WIDTH), lambda i: (i, 0)),
                   pl.BlockSpec((tm, KV_WIDTH), lambda i: (i, 0))],
        out_shape=[jax.ShapeDtypeStruct((m, ATTN_WIDTH), BF16),
                   jax.ShapeDtypeStruct((m, KV_WIDTH), BF16),
                   jax.ShapeDtypeStruct((m, KV_WIDTH), BF16)],
        compiler_params=_cparams(("parallel",)),
        name="lat_prep",
    )(proj, proj, qg, kg, cos, sin)


def _lat_attn_kernel(q_ref, k_ref, v_ref, o_ref):
    k = k_ref[0]
    v = v_ref[0]
    for g in range(ATTN_GROUP):
        sl = slice(g * HEAD_DIM, (g + 1) * HEAD_DIM)
        o_ref[:, sl] = _softmax_av(q_ref[:, sl], k, v, None).astype(BF16)


def lat_attention(q, k_all, v_all, seq, tq=256):
    b, s, _ = k_all.shape
    nq = seq // tq
    gw = ATTN_GROUP * HEAD_DIM
    return pl.pallas_call(
        _lat_attn_kernel,
        grid=(b, ATTN_KV_HEADS, nq),
        in_specs=[pl.BlockSpec((tq, gw), lambda bi, kh, qi: (bi * nq + qi, kh)),
                  pl.BlockSpec((1, s, HEAD_DIM), lambda bi, kh, qi: (bi, 0, kh)),
                  pl.BlockSpec((1, s, HEAD_DIM), lambda bi, kh, qi: (bi, 0, kh))],
        out_specs=pl.BlockSpec((tq, gw), lambda bi, kh, qi: (bi * nq + qi, kh)),
        out_shape=jax.ShapeDtypeStruct((b * seq, ATTN_WIDTH), BF16),
        compiler_params=_cparams(("parallel", "parallel", "arbitrary")),
        name="lat_attention",
    )(q, k_all, v_all)


def rope_tables(seq):
    t = jnp.arange(seq)
    row = (t // GRID_W).astype(F32)
    col = (t % GRID_W).astype(F32)
    quarter = HEAD_DIM // 4
    inv_freq = jnp.power(ROPE_THETA, -jnp.arange(quarter, dtype=F32) / quarter)
    ar = row[:, None] * inv_freq[None, :]
    ac = col[:, None] * inv_freq[None, :]
    cos = jnp.concatenate([jnp.cos(ar), jnp.cos(ar), jnp.cos(ac), jnp.cos(ac)], axis=1)
    sin = jnp.concatenate([-jnp.sin(ar), jnp.sin(ar), -jnp.sin(ac), jnp.sin(ac)], axis=1)
    return cos, sin


def _chunk_index(d, i, n):
    return i + d * (n - 1 - 2 * i)


def _lru_kernel(x_ref, xp_ref, xn_ref, cw_ref, cb_ref, wg_ref, bg_ref, sp_ref, h0_ref,
                h_ref, fin_ref, xpad_ref, a_ref, b_ref, hc_ref, *, nb, tt):
    d = pl.program_id(0)
    i = pl.program_id(2)
    n = pl.num_programs(2)
    ci = _chunk_index(d, i, n)
    pitch = tt + PITCH_PAD
    rows = nb * pitch
    nsl = LRU_WIDTH // LANE

    @pl.when(i == 0)
    def _():
        xpad_ref[...] = jnp.zeros_like(xpad_ref)
        hc_ref[...] = h0_ref[0]

    for s in range(nb):
        base = 8 + s * pitch
        xpad_ref[base:base + tt, :] = x_ref[s]
        xpad_ref[base - 1:base, :] = jnp.where(ci == 0, 0.0, xp_ref[s, 7:8, :])
        xpad_ref[base + tt:base + tt + 2, :] = jnp.where(ci == n - 1, 0.0, xn_ref[s, 0:2, :])
    w = cw_ref[...]
    xc = cb_ref[...] + sum(xpad_ref[7 + j:7 + j + rows, :] * w[j:j + 1, :] for j in range(CONV_W))
    pre = _dot(xc.astype(BF16), wg_ref[0]) + bg_ref[0]
    r = jax.nn.sigmoid(pre[:, :LRU_WIDTH])
    ig = jax.nn.sigmoid(pre[:, LRU_WIDTH:])
    a = jnp.exp((-LRU_C) * r * sp_ref[0])
    inp = jnp.sqrt(1.0 - a * a) * (ig * xc)
    for c in range(nsl):
        a_ref[c] = a[:, c * LANE:(c + 1) * LANE]
        b_ref[c] = inp[:, c * LANE:(c + 1) * LANE]

    def body(t, carry):
        row = t + d * (tt - 1 - 2 * t)
        out = []
        for c in range(nsl):
            idx = (c, pl.ds(row, nb, stride=pitch), slice(None))
            h = a_ref[idx] * carry[c] + b_ref[idx]
            b_ref[idx] = h
            out.append(h)
        return tuple(out)

    carry = tuple(hc_ref[:, c * LANE:(c + 1) * LANE] for c in range(nsl))
    carry = lax.fori_loop(0, tt, body, carry, unroll=8)
    for c in range(nsl):
        hc_ref[:, c * LANE:(c + 1) * LANE] = carry[c]
        for s in range(nb):
            h_ref[0, s, :, c * LANE:(c + 1) * LANE] = b_ref[c, s * pitch:s * pitch + tt, :]

    @pl.when(i == n - 1)
    def _():
        fin_ref[0] = hc_ref[...]


def lru_scan(proj3, h0, cw, cb, wg, bg, sp, nb, tt=128):
    b, t, _ = proj3.shape
    w = LRU_WIDTH
    n = t // tt
    t8 = tt // 8
    rows = nb * (tt + PITCH_PAD)
    col = COL_LX // w

    def cidx(d, i):
        return _chunk_index(d, i, n)

    return pl.pallas_call(
        functools.partial(_lru_kernel, nb=nb, tt=tt),
        grid=(2, b // nb, n),
        in_specs=[pl.BlockSpec((nb, tt, w), lambda d, g, i: (g, cidx(d, i), col)),
                  pl.BlockSpec((nb, 8, w), lambda d, g, i: (g, jnp.maximum(cidx(d, i) * t8 - 1, 0), col)),
                  pl.BlockSpec((nb, 8, w), lambda d, g, i: (g, jnp.minimum((cidx(d, i) + 1) * t8, t // 8 - 1), col)),
                  pl.BlockSpec((CONV_W, w), lambda d, g, i: (0, 0)),
                  pl.BlockSpec((1, w), lambda d, g, i: (0, 0)),
                  pl.BlockSpec((1, w, 2 * w), lambda d, g, i: (d, 0, 0)),
                  pl.BlockSpec((1, 1, 2 * w), lambda d, g, i: (d, 0, 0)),
                  pl.BlockSpec((1, 1, w), lambda d, g, i: (d, 0, 0)),
                  pl.BlockSpec((1, nb, w), lambda d, g, i: (d, g, 0))],
        out_specs=[pl.BlockSpec((1, nb, tt, w), lambda d, g, i: (d, g, cidx(d, i), 0)),
                   pl.BlockSpec((1, nb, w), lambda d, g, i: (d, g, 0))],
        out_shape=[jax.ShapeDtypeStruct((2, b, t, w), F32), jax.ShapeDtypeStruct((2, b, w), F32)],
        scratch_shapes=[pltpu.VMEM((rows + 16, w), F32),
                        pltpu.VMEM((w // LANE, rows, LANE), F32),
                        pltpu.VMEM((w // LANE, rows, LANE), F32),
                        pltpu.VMEM((nb, w), F32)],
        compiler_params=_cparams(("parallel", "parallel", "arbitrary")),
        name="lru_scan",
    )(proj3, proj3, proj3, cw, cb, wg, bg, sp, h0)


def _s5_kernel(u_ref, bb_ref, cc_ref, a_ref, h0_ref, y_ref, fin_ref, up_ref, s_ref, hc_ref, *, nb, tt):
    d = pl.program_id(0)
    i = pl.program_id(2)
    n = pl.num_programs(2)
    pitch = tt + PITCH_PAD
    nsl = S5_NSTATE // LANE

    @pl.when(i == 0)
    def _():
        up_ref[...] = jnp.zeros_like(up_ref)
        hc_ref[...] = h0_ref[0]

    for s in range(nb):
        up_ref[s * pitch:s * pitch + tt, :] = u_ref[s]
    u2 = up_ref[...].astype(BF16)
    per_k = MXU // S5_CH * S5_STATE // MXU
    for nt in range(2 * S5_NSTATE // MXU):
        kt = (nt % (S5_NSTATE // MXU)) // per_k
        tile = _dot(u2[:, kt * MXU:(kt + 1) * MXU], bb_ref[0, kt * MXU:(kt + 1) * MXU, nt * MXU:(nt + 1) * MXU])
        s_ref[2 * nt] = tile[:, :LANE]
        s_ref[2 * nt + 1] = tile[:, LANE:]

    a_all = a_ref[0]
    group = 8
    for c0 in range(0, nsl, group):
        cs = list(range(c0, c0 + group))
        ar = [jnp.broadcast_to(a_all[:, c * LANE:(c + 1) * LANE], (nb, LANE)) for c in cs]
        ai = [jnp.broadcast_to(a_all[:, S5_NSTATE + c * LANE:S5_NSTATE + (c + 1) * LANE], (nb, LANE)) for c in cs]

        def body(t, carry, cs=cs, ar=ar, ai=ai):
            row = t + d * (tt - 1 - 2 * t)
            out = []
            for j, c in enumerate(cs):
                hr, hi = carry[2 * j], carry[2 * j + 1]
                ire = (c, pl.ds(row, nb, stride=pitch), slice(None))
                iim = (nsl + c, pl.ds(row, nb, stride=pitch), slice(None))
                nr = ar[j] * hr - ai[j] * hi + s_ref[ire]
                ni = ar[j] * hi + ai[j] * hr + s_ref[iim]
                s_ref[ire] = nr
                s_ref[iim] = ni
                out += [nr, ni]
            return tuple(out)

        carry = []
        for c in cs:
            carry += [hc_ref[:, c * LANE:(c + 1) * LANE], hc_ref[:, S5_NSTATE + c * LANE:S5_NSTATE + (c + 1) * LANE]]
        carry = lax.fori_loop(0, tt, body, tuple(carry), unroll=4)
        for j, c in enumerate(cs):
            hc_ref[:, c * LANE:(c + 1) * LANE] = carry[2 * j]
            hc_ref[:, S5_NSTATE + c * LANE:S5_NSTATE + (c + 1) * LANE] = carry[2 * j + 1]

    n_out = S5_WIDTH // MXU
    per_n = nsl // n_out
    for nt in range(n_out):
        slabs = [per_n * nt + k for k in range(per_n)] + [nsl + per_n * nt + k for k in range(per_n)]
        lhs = jnp.concatenate([s_ref[c] for c in slabs], axis=1).astype(BF16)
        y = _dot(lhs, cc_ref[0, nt])
        for s in range(nb):
            y_ref[0, s, :, nt * MXU:(nt + 1) * MXU] = y[s * pitch:s * pitch + tt]

    @pl.when(i == n - 1)
    def _():
        fin_ref[0] = hc_ref[...]


def s5_scan(proj3, h0, bb, cc, a, nb, tt=128):
    b, t, _ = proj3.shape
    w = S5_WIDTH
    n = t // tt
    rows = nb * (tt + PITCH_PAD)
    ns2 = 2 * S5_NSTATE

    def cidx(d, i):
        return _chunk_index(d, i, n)

    return pl.pallas_call(
        functools.partial(_s5_kernel, nb=nb, tt=tt),
        grid=(2, b // nb, n),
        in_specs=[pl.BlockSpec((nb, tt, w), lambda d, g, i: (g, cidx(d, i), COL_SU // w)),
                  pl.BlockSpec((1, w, ns2), lambda d, g, i: (d, 0, 0)),
                  pl.BlockSpec((1,) + cc.shape[1:], lambda d, g, i: (d, 0, 0, 0)),
                  pl.BlockSpec((1, 1, ns2), lambda d, g, i: (d, 0, 0)),
                  pl.BlockSpec((1, nb, ns2), lambda d, g, i: (d, g, 0))],
        out_specs=[pl.BlockSpec((1, nb, tt, w), lambda d, g, i: (d, g, cidx(d, i), 0)),
                   pl.BlockSpec((1, nb, ns2), lambda d, g, i: (d, g, 0))],
        out_shape=[jax.ShapeDtypeStruct((2, b, t, w), F32), jax.ShapeDtypeStruct((2, b, ns2), F32)],
        scratch_shapes=[pltpu.VMEM((rows, w), F32),
                        pltpu.VMEM((ns2 // LANE, rows, LANE), F32),
                        pltpu.VMEM((nb, ns2), F32)],
        compiler_params=_cparams(("parallel", "parallel", "arbitrary")),
        name="s5_scan",
    )(proj3, bb, cc, a, h0)


def _split3(x):
    x1 = x.astype(BF16)
    r1 = x - x1.astype(F32)
    x2 = r1.astype(BF16)
    x3 = (r1 - x2.astype(F32)).astype(BF16)
    return x1, x2, x3


def _mm(x, y):
    return _dot(x.astype(BF16), y.astype(BF16))


def _unit_tri_inverses(mats, rr, cc):
    same16 = (rr // 16) == (cc // 16)
    same32 = (rr // 32) == (cc // 32)
    p = [jnp.where(same16, -a, 0.0) for a in mats]
    t = [jnp.where(rr == cc, 1.0, x) for x in p]
    for _ in range(3):
        p = [_mm(x, x) for x in p]
        t = [x + _mm(x, y) for x, y in zip(t, p)]
    for mask in (same32 & jnp.logical_not(same16), jnp.logical_not(same32)):
        te = [_mm(x, jnp.where(mask, a, 0.0)) for x, a in zip(t, mats)]
        t = [x - _mm(y, x) for x, y in zip(t, te)]
    return t


def _dn_prep_kernel(x_ref, xp_ref, xn_ref, cw_ref, o_ref, xpad_ref, *, tb):
    i = pl.program_id(1)
    n = pl.num_programs(1)
    hw = DN_HEADS * DN_DK
    xpad_ref[8:8 + tb, :] = x_ref[0]
    xpad_ref[0:8, :] = jnp.where(i == 0, 0.0, xp_ref[0])
    xpad_ref[8 + tb:16 + tb, :] = jnp.where(i == n - 1, 0.0, xn_ref[0])
    w = cw_ref[...]
    xc = sum(xpad_ref[7 + j:7 + j + tb, :] * w[j:j + 1, :] for j in range(CONV_W))
    qkv = xc * jax.nn.sigmoid(xc)
    for h in range(DN_HEADS):
        q = qkv[:, h * DN_DK:(h + 1) * DN_DK]
        k = qkv[:, hw + h * DN_DK:hw + (h + 1) * DN_DK]
        o_ref[0, :, h * DN_DK:(h + 1) * DN_DK] = (
            q * lax.rsqrt(jnp.sum(q * q, axis=-1, keepdims=True) + EPS) * (DN_DK ** -0.5))
        o_ref[0, :, hw + h * DN_DK:hw + (h + 1) * DN_DK] = (
            k * lax.rsqrt(jnp.sum(k * k, axis=-1, keepdims=True) + EPS))
    o_ref[0, :, 2 * hw:] = qkv[:, 2 * hw:]


def dn_prep(proj3, cw, tb=256):
    b, t, _ = proj3.shape
    n = t // tb
    t8 = tb // 8
    w3 = 3 * DN_WIDTH
    cq = COL_DQ // w3
    return pl.pallas_call(
        functools.partial(_dn_prep_kernel, tb=tb),
        grid=(b, n),
        in_specs=[pl.BlockSpec((1, tb, w3), lambda bi, i: (bi, i, cq)),
                  pl.BlockSpec((1, 8, w3), lambda bi, i: (bi, jnp.maximum(i * t8 - 1, 0), cq)),
                  pl.BlockSpec((1, 8, w3), lambda bi, i: (bi, jnp.minimum((i + 1) * t8, t // 8 - 1), cq)),
                  pl.BlockSpec((CONV_W, w3), lambda bi, i: (0, 0))],
        out_specs=pl.BlockSpec((1, tb, w3), lambda bi, i: (bi, i, 0)),
        out_shape=jax.ShapeDtypeStruct((b, t, w3), F32),
        scratch_shapes=[pltpu.VMEM((tb + 16, w3), F32)],
        compiler_params=_cparams(("parallel", "parallel")),
        name="dn_prep",
    )(proj3, proj3, proj3, cw)


def _dn_kernel(xf_ref, baf_ref, xb_ref, bab_ref, al_ref, dtb_ref, s0_ref, of_ref, ob_ref, fin_ref, s_ref,
               *, nbat, tb, shared):
    i = pl.program_id(1)
    n = pl.num_programs(1)
    ch = DN_CHUNK
    nc = tb // ch
    hw = DN_HEADS * DN_DK

    @pl.when(i == 0)
    def _():
        s_ref[...] = s0_ref[...]

    rb = lax.broadcasted_iota(jnp.int32, (tb, tb), 0)
    cb = lax.broadcasted_iota(jnp.int32, (tb, tb), 1)
    rr = lax.broadcasted_iota(jnp.int32, (ch, ch), 0)
    cc = lax.broadcasted_iota(jnp.int32, (ch, ch), 1)
    al = al_ref[...]
    dtb = dtb_ref[...]

    def qkv_heads(x_ref, bat):
        out = {}
        for c in range(nc):
            rows = slice(c * ch, (c + 1) * ch)
            for h in range(DN_HEADS):
                out[(c, h)] = (x_ref[bat, rows, h * DN_DK:(h + 1) * DN_DK],
                               x_ref[bat, rows, hw + h * DN_DK:hw + (h + 1) * DN_DK],
                               x_ref[bat, rows, 2 * hw + h * DN_DV:2 * hw + (h + 1) * DN_DV])
        return out

    units = []
    for d in range(2):
        reverse = d == 1
        x_ref, ba_ref = (xb_ref, bab_ref) if reverse else (xf_ref, baf_ref)
        tri = ((rb // ch) == (cb // ch)) & ((cb >= rb) if reverse else (cb <= rb))
        tri = jnp.where(tri, 1.0, 0.0).astype(BF16)
        incl = (cc >= rr) if reverse else (cc <= rr)
        strict = (cc > rr) if reverse else (cc < rr)
        if d == 0 or not shared:
            heads = [qkv_heads(x_ref, bat) for bat in range(nbat)]
        for bat in range(nbat):
            ba = ba_ref[bat]
            beta_all = jax.nn.sigmoid(ba)
            g_all = -jnp.exp(al) * _softplus(ba + dtb)
            gc_all = sum(_dot(tri, piece) for piece in _split3(g_all))
            for c in range(nc):
                r0 = c * ch
                gct = gc_all[r0:r0 + ch, :].T
                for h in range(DN_HEADS):
                    lane = d * DN_HEADS + h
                    beta = beta_all[r0:r0 + ch, lane:lane + 1]
                    gcol = gc_all[r0:r0 + ch, 2 * DN_HEADS + lane:2 * DN_HEADS + lane + 1]
                    grow = gct[2 * DN_HEADS + lane:2 * DN_HEADS + lane + 1, :]
                    q, k, v = heads[bat][(c, h)]
                    g_last = gcol[0:1, :] if reverse else gcol[ch - 1:ch, :]
                    units.append(dict(d=d, bat=bat, c=c, h=h, q=q, k=k, v=v, beta=beta, gcol=gcol, kb=k * beta,
                                      decay=jnp.where(incl, jnp.exp(gcol - grow), 0.0), strict=strict,
                                      g_last=g_last))

    for u in units:
        kbf = u['k'].astype(BF16)
        u['a'] = jnp.where(u['strict'], _dot_nt(u['kb'].astype(BF16), kbf) * u['decay'], 0.0)
        u['qk'] = (_dot_nt(u['q'].astype(BF16), kbf) * u['decay']).astype(BF16)
    t_inv = _unit_tri_inverses([u['a'] for u in units], rr, cc)
    for u, t in zip(units, t_inv):
        u['rhs'] = jnp.concatenate([u['v'] * u['beta'], u['kb'] * jnp.exp(u['gcol'])], axis=1)
        u['t_off'] = jnp.where(rr == cc, 0.0, t)
    for u in units:
        sol = u['rhs'] + _mm(u['t_off'], u['rhs'])
        u['u_val'] = sol[:, :DN_DV]
        u['w'] = sol[:, DN_DV:].astype(BF16)
        u['qe'] = (u['q'] * jnp.exp(u['gcol'])).astype(BF16)
        u['kdec_t'] = (u['k'] * jnp.exp(u['g_last'] - u['gcol'])).T.astype(BF16)
        u['eg'] = jnp.exp(u['g_last'])

    state = {(d, bat, h): s_ref[d, bat, h] for d in range(2) for bat in range(nbat) for h in range(DN_HEADS)}
    for j in range(nc):
        cur = [u for u in units if u['c'] == (nc - 1 - j if u['d'] == 1 else j)]
        keys = [(u['d'], u['bat'], u['h']) for u in cur]
        sb = [state[key].astype(BF16) for key in keys]
        ws = [_dot(u['w'], s) for u, s in zip(cur, sb)]
        qs = [_dot(u['qe'], s) for u, s in zip(cur, sb)]
        vb = [(u['u_val'] - x).astype(BF16) for u, x in zip(cur, ws)]
        os = [x + _dot(u['qk'], y) for u, x, y in zip(cur, qs, vb)]
        sn = [state[key] * u['eg'] + _dot(u['kdec_t'], y) for u, key, y in zip(cur, keys, vb)]
        for u, key, o, s in zip(cur, keys, os, sn):
            state[key] = s
            o_ref = ob_ref if u['d'] == 1 else of_ref
            o_ref[u['bat'], u['c'] * ch:(u['c'] + 1) * ch, u['h'] * DN_DV:(u['h'] + 1) * DN_DV] = o
    for (d, bat, h), s in state.items():
        s_ref[d, bat, h] = s

    @pl.when(i == n - 1)
    def _():
        fin_ref[...] = s_ref[...]


def dn_scan(qkv3, ba3, s0, al, dtb, nbat, tb):
    b, t, w3 = qkv3.shape
    n = t // tb

    def data_specs(cidx):
        return [pl.BlockSpec((nbat, tb, w3), lambda g, i: (g, cidx(i), 0)),
                pl.BlockSpec((nbat, tb, LANE), lambda g, i: (g, cidx(i), 0))]

    fwd = lambda i: i
    bwd = lambda i: n - 1 - i
    st_spec = pl.BlockSpec((2, nbat, DN_HEADS, DN_DK, DN_DV), lambda g, i: (0, g, 0, 0, 0))
    return pl.pallas_call(
        functools.partial(_dn_kernel, nbat=nbat, tb=tb, shared=(n == 1)),
        grid=(b // nbat, n),
        in_specs=data_specs(fwd) + data_specs(bwd) + [
            pl.BlockSpec((1, LANE), lambda g, i: (0, 0)),
            pl.BlockSpec((1, LANE), lambda g, i: (0, 0)),
            st_spec],
        out_specs=[pl.BlockSpec((nbat, tb, DN_WIDTH), lambda g, i: (g, i, 0)),
                   pl.BlockSpec((nbat, tb, DN_WIDTH), lambda g, i: (g, n - 1 - i, 0)),
                   st_spec],
        out_shape=[jax.ShapeDtypeStruct((b, t, DN_WIDTH), F32),
                   jax.ShapeDtypeStruct((b, t, DN_WIDTH), F32),
                   jax.ShapeDtypeStruct((2, b, DN_HEADS, DN_DK, DN_DV), F32)],
        scratch_shapes=[pltpu.VMEM((2, nbat, DN_HEADS, DN_DK, DN_DV), F32)],
        compiler_params=_cparams(("parallel", "arbitrary")),
        name="dn_scan",
    )(qkv3, ba3, qkv3, ba3, al, dtb, s0)


def _post_kernel(hf_ref, hb_ref, lg_ref, of_ref, ob_ref, dz_ref, ng_ref, yf_ref, yb_ref, su_ref, sd_ref,
                 gw_ref, gb_ref, lru_ref, dn_ref, s5_ref):
    lru_ref[...] = ((hf_ref[0, 0] + hb_ref[0, 0]) * _gelu(lg_ref[...])).astype(BF16)
    o = of_ref[0] + ob_ref[0]
    dz = dz_ref[...]
    for h in range(DN_HEADS):
        sl = slice(h * DN_DV, (h + 1) * DN_DV)
        oh = o[:, sl]
        oh = oh * lax.rsqrt(jnp.mean(oh * oh, axis=-1, keepdims=True) + EPS) * ng_ref[...]
        zh = dz[:, sl]
        dn_ref[:, sl] = (oh * (zh * jax.nn.sigmoid(zh))).astype(BF16)
    y = yf_ref[0, 0] + yb_ref[0, 0] + sd_ref[...] * su_ref[...]
    gy = _gelu(y)
    s5_ref[...] = (gy * jax.nn.sigmoid(_dot(gy.astype(BF16), gw_ref[...]) + gb_ref[...])).astype(BF16)


def mix_post(proj, h_lru, o_f, o_b, y_s5, ng, sd, gw, gb):
    m = proj.shape[0]
    _, b, t, w = h_lru.shape
    tm = min(512, t)
    per = t // tm
    tok = pl.BlockSpec((tm, w), lambda i: (i, 0))

    def pcol(c0):
        return pl.BlockSpec((tm, w), lambda i: (i, c0 // w))

    def dirspec(d):
        return pl.BlockSpec((1, 1, tm, w), lambda i: (d, i // per, i % per, 0))

    seq = pl.BlockSpec((1, tm, w), lambda i: (i // per, i % per, 0))

    def vec(n):
        return pl.BlockSpec((1, n), lambda i: (0, 0))

    out = jax.ShapeDtypeStruct((m, w), BF16)
    return pl.pallas_call(
        _post_kernel,
        grid=(m // tm,),
        in_specs=[dirspec(0), dirspec(1), pcol(COL_LG), seq, seq, pcol(COL_DZ), vec(DN_DV),
                  dirspec(0), dirspec(1), pcol(COL_SU), vec(w), pl.BlockSpec((w, w), lambda i: (0, 0)), vec(w)],
        out_specs=[tok, tok, tok],
        out_shape=[out, out, out],
        compiler_params=_cparams(("parallel",)),
        name="mix_post",
    )(h_lru, h_lru, proj, o_f, o_b, proj, ng, y_s5, y_s5, proj, sd, gw, gb)


def _out_kernel(x_ref, a_ref, b_ref, c_ref, d_ref, w_ref, gate_ref, g_ref, bb_ref, sc_ref, sh_ref,
                x1_ref, u2_ref):
    q = MIXW
    for r0 in range(0, x_ref.shape[0], OUT_SUB):
        rows = slice(r0, r0 + OUT_SUB)
        y = _dot(a_ref[rows, :], w_ref[0:q, :])
        y += _dot(b_ref[rows, :], w_ref[q:2 * q, :])
        y += _dot(c_ref[rows, :], w_ref[2 * q:3 * q, :])
        y += _dot(d_ref[rows, :], w_ref[3 * q:, :])
        z = DEEPNORM_ALPHA * x_ref[rows, :] + gate_ref[0] * y
        x1 = _ln(z) * g_ref[...] + bb_ref[...]
        x1_ref[rows, :] = x1
        u2_ref[rows, :] = (_ln(x1) * (1.0 + sc_ref[0]) + sh_ref[0]).astype(BF16)


def out_proj(x, parts, w, layer, mod, ln_g, ln_b, mod_row, tm=512):
    m, d = x.shape

    def mod_spec(kind):
        return pl.BlockSpec((1, 1, d), lambda i: (kind * MOD_ROWS + mod_row(i * tm), 0, 0))

    vec = pl.BlockSpec((1, d), lambda i: (0, 0))
    part = pl.BlockSpec((tm, MIXW), lambda i: (i, 0))
    return pl.pallas_call(
        _out_kernel,
        grid=(m // tm,),
        in_specs=[pl.BlockSpec((tm, d), lambda i: (i, 0)), part, part, part, part,
                  pl.BlockSpec((None, d, d), lambda i: (layer, 0, 0)),
                  mod_spec(2), vec, vec, mod_spec(4), mod_spec(3)],
        out_specs=[pl.BlockSpec((tm, d), lambda i: (i, 0)), pl.BlockSpec((tm, d), lambda i: (i, 0))],
        out_shape=[jax.ShapeDtypeStruct((m, d), F32), jax.ShapeDtypeStruct((m, d), BF16)],
        compiler_params=_cparams(("parallel",)),
        name="out_proj",
    )(x, *parts, w, mod, ln_g, ln_b, mod, mod)


def _mlp_kernel(u_ref, x_ref, w1_ref, w2_ref, gate_ref, g_ref, b_ref, o_ref, acc_ref):
    f = pl.program_id(1)
    h = _dot(u_ref[...], w1_ref[...])
    h = jnp.square(jnp.maximum(h, 0.0)).astype(BF16)
    part = _dot(h, w2_ref[...])

    @pl.when(f == 0)
    def _():
        acc_ref[...] = part

    @pl.when(f > 0)
    def _():
        acc_ref[...] += part

    @pl.when(f == pl.num_programs(1) - 1)
    def _():
        z = DEEPNORM_ALPHA * x_ref[...] + gate_ref[0] * acc_ref[...]
        o_ref[...] = _ln(z) * g_ref[...] + b_ref[...]


def mlp(u, x, w1, w2, layer, mod, ln_g, ln_b, mod_row, tm=512, tf=1024):
    m, d = x.shape
    ff = w1.shape[2]
    vec = pl.BlockSpec((1, d), lambda i, f: (0, 0))
    return pl.pallas_call(
        _mlp_kernel,
        grid=(m // tm, ff // tf),
        in_specs=[pl.BlockSpec((tm, d), lambda i, f: (i, 0)),
                  pl.BlockSpec((tm, d), lambda i, f: (i, 0)),
                  pl.BlockSpec((None, d, tf), lambda i, f: (layer, 0, f)),
                  pl.BlockSpec((None, tf, d), lambda i, f: (layer, f, 0)),
                  pl.BlockSpec((1, 1, d), lambda i, f: (5 * MOD_ROWS + mod_row(i * tm), 0, 0)),
                  vec, vec],
        out_specs=pl.BlockSpec((tm, d), lambda i, f: (i, 0)),
        out_shape=jax.ShapeDtypeStruct((m, d), F32),
        scratch_shapes=[pltpu.VMEM((tm, d), F32)],
        compiler_params=_cparams(("parallel", "arbitrary")),
        name="mlp",
    )(u, x, w1, w2, mod, ln_g, ln_b)


def _reorder_w_in(w):
    def cols(c0, n):
        return w[..., c0:c0 + n]

    aq, akv, lx, lg = cols(0, 512), cols(512, 512), cols(1024, 512), cols(1536, 512)
    dqkv, dz, ba, su = cols(2048, 1536), cols(3584, 512), cols(4096, N_BA), cols(4096 + N_BA, 512)
    pad = jnp.zeros(w.shape[:2] + (LANE - N_BA,), w.dtype)
    main = jnp.concatenate([dqkv, dz, aq, akv, lx, lg, su], axis=-1).astype(BF16)
    return main, jnp.concatenate([ba, pad], axis=-1).astype(BF16)


def _block_diag(blocks):
    n, r, c = blocks.shape[-3:]
    eye = jnp.eye(n, dtype=blocks.dtype)
    out = blocks[..., :, :, None, :] * eye[:, None, :, None]
    return out.reshape(blocks.shape[:-3] + (n * r, n * c))


def _lru_params(wa, ba, wx, bx, lam):
    wg = jnp.concatenate([_block_diag(wa), _block_diag(wx)], axis=-1).astype(BF16)
    bg = jnp.concatenate([ba, bx], axis=-1)[:, None, :]
    sp = jax.nn.softplus(-lam)[:, None, :]
    return wg, bg, sp


def _s5_params(lam_re, lam_im, log_dt, b_re, b_im, c_re, c_im):
    dt = jnp.exp(log_dt)[..., None]
    mag = jnp.exp(lam_re * dt)
    abar_re = mag * jnp.cos(lam_im * dt)
    abar_im = mag * jnp.sin(lam_im * dt)
    den = lam_re * lam_re + lam_im * lam_im
    nr = abar_re - 1.0
    ni = abar_im
    f_re = (nr * lam_re + ni * lam_im) / den
    f_im = (ni * lam_re - nr * lam_im) / den
    bb_re = f_re[..., None] * b_re - f_im[..., None] * b_im
    bb_im = f_re[..., None] * b_im + f_im[..., None] * b_re
    to_in = lambda m: _block_diag(jnp.swapaxes(m, -1, -2))
    bb = jnp.concatenate([to_in(bb_re), to_in(bb_im)], axis=-1).astype(BF16)
    to_out = lambda m: _block_diag(jnp.swapaxes(m, -1, -2))
    n_out = S5_WIDTH // MXU
    per = S5_NSTATE // n_out
    c_r, c_i = to_out(c_re), -to_out(c_im)
    cc = jnp.stack([jnp.concatenate([c_r[:, nt * per:(nt + 1) * per, nt * MXU:(nt + 1) * MXU],
                                     c_i[:, nt * per:(nt + 1) * per, nt * MXU:(nt + 1) * MXU]], axis=1)
                    for nt in range(n_out)], axis=1).astype(BF16)
    a = jnp.concatenate([abar_re.reshape(2, 1, -1), abar_im.reshape(2, 1, -1)], axis=-1)
    return bb, cc, a


def _lane_row(vals, offset):
    return jnp.zeros((1, LANE), F32).at[0, offset:offset + vals.size].set(vals.reshape(-1))


def kernel(x_prompt, x_sample, cache_attn_k, cache_attn_v, state_rglru, state_delta, state_s5_re, state_s5_im, c, c_ctx, w_ada, b_ada, w_in, w_out, ln1_g, ln1_b, ln2_g, ln2_b, w_mlp1, w_mlp2, q_norm_g, k_norm_g, lru_conv_w, lru_conv_b, lru_wa, lru_ba, lru_wx, lru_bx, lru_lambda, dn_conv_w, dn_a_log, dn_dt_bias, dn_norm_g, s5_lambda_re, s5_lambda_im, s5_log_dt, s5_b_re, s5_b_im, s5_c_re, s5_c_im, s5_d, s5_glu_w, s5_glu_b):
    bp, tp, d = x_prompt.shape
    bs, ts, _ = x_sample.shape
    ctx_row = bs

    cond = jnp.concatenate([c, c_ctx[None, :], jnp.zeros((MOD_ROWS - bs - 1, d), F32)], axis=0)
    mods = ada_mod(cond, w_ada, b_ada)
    w_in_r, w_ba = _reorder_w_in(w_in)
    w_out_b = w_out.astype(BF16)
    w1_b = w_mlp1.astype(BF16)
    w2_b = w_mlp2.astype(BF16)
    cos, sin = rope_tables(ts)

    streams = {
        'ctx': dict(x=x_prompt.reshape(bp * tp, d), b=bp, t=tp, nb=8, s5_tt=128, dn_nbat=2, dn_tb=tp,
                    mod_row=lambda tok: ctx_row),
        'lat': dict(x=x_sample.reshape(bs * ts, d), b=bs, t=ts, nb=bs, s5_tt=256, dn_nbat=bs, dn_tb=2 * DN_CHUNK,
                    mod_row=lambda tok: tok // ts),
    }
    ks, vs, lrus, dns, s5rs, s5is = [], [], [], [], [], []
    for l in range(DEPTH):
        mod = mods[l].reshape(MOD_ROWS, N_MOD, d).transpose(1, 0, 2).reshape(N_MOD * MOD_ROWS, 1, d)
        qg = q_norm_g[l].reshape(1, HEAD_DIM)
        kg = k_norm_g[l].reshape(1, HEAD_DIM)
        wg, bg, sp = _lru_params(lru_wa[l], lru_ba[l], lru_wx[l], lru_bx[l], lru_lambda[l])
        bb, cc, a5 = _s5_params(s5_lambda_re[l], s5_lambda_im[l], s5_log_dt[l], s5_b_re[l], s5_b_im[l],
                                s5_c_re[l], s5_c_im[l])
        al = _lane_row(dn_a_log[l], 2 * DN_HEADS)
        dtb = _lane_row(dn_dt_bias[l], 2 * DN_HEADS)
        for name, st in streams.items():
            b, t, nb = st['b'], st['t'], st['nb']
            is_ctx = name == 'ctx'
            proj, ba = in_proj(st['x'], mod, w_in_r, w_ba, l, st['mod_row'])
            proj3 = proj.reshape(b, t, N_PROJ)
            ba3 = ba.reshape(b, t, LANE)

            if is_ctx:
                attn, kn, vv = ctx_attention(proj, qg, kg, b, t)
                ks.append(kn.reshape(b, t, ATTN_KV_HEADS, HEAD_DIM))
                vs.append(vv.reshape(b, t, ATTN_KV_HEADS, HEAD_DIM))
                h0_lru = jnp.zeros((2, b, LRU_WIDTH), F32)
                s0_dn = jnp.zeros((2, b, DN_HEADS, DN_DK, DN_DV), F32)
                h0_s5 = jnp.zeros((2, b, 2 * S5_NSTATE), F32)
            else:
                q_s, k_s, v_s = lat_prep(proj, qg, kg, cos, sin, t)
                k_all = jnp.concatenate([cache_attn_k[:, l].reshape(b, -1, KV_WIDTH).astype(BF16),
                                         k_s.reshape(b, t, KV_WIDTH)], axis=1)
                v_all = jnp.concatenate([cache_attn_v[:, l].reshape(b, -1, KV_WIDTH).astype(BF16),
                                         v_s.reshape(b, t, KV_WIDTH)], axis=1)
                attn = lat_attention(q_s, k_all, v_all, t)
                h0_lru = jnp.swapaxes(state_rglru[:, l], 0, 1)
                s0_dn = jnp.swapaxes(state_delta[:, l], 0, 1)
                h0_s5 = jnp.swapaxes(jnp.concatenate([state_s5_re[:, l].reshape(b, 2, S5_NSTATE),
                                                      state_s5_im[:, l].reshape(b, 2, S5_NSTATE)], axis=-1), 0, 1)

            h_lru, lru_fin = lru_scan(proj3, h0_lru, lru_conv_w[l], lru_conv_b[l].reshape(1, -1), wg, bg, sp, nb)
            o_f, o_b, dn_fin = dn_scan(dn_prep(proj3, dn_conv_w[l]), ba3, s0_dn, al, dtb,
                                       st['dn_nbat'], st['dn_tb'])
            y_s5, s5_fin = s5_scan(proj3, h0_s5, bb, cc, a5, nb, st['s5_tt'])
            lru_out, dn_out, s5_out = mix_post(
                proj, h_lru, o_f, o_b, y_s5, dn_norm_g[l].reshape(1, DN_DV), s5_d[l].reshape(1, MIXW),
                s5_glu_w[l].astype(BF16), s5_glu_b[l].reshape(1, MIXW))
            if is_ctx:
                lrus.append(jnp.swapaxes(lru_fin, 0, 1))
                dns.append(jnp.swapaxes(dn_fin, 0, 1))
                s5_fin = jnp.swapaxes(s5_fin, 0, 1)
                s5rs.append(s5_fin[..., :S5_NSTATE].reshape(b, 2, S5_GROUPS, S5_STATE))
                s5is.append(s5_fin[..., S5_NSTATE:].reshape(b, 2, S5_GROUPS, S5_STATE))

            x1, u2 = out_proj(st['x'], [attn, lru_out, dn_out, s5_out], w_out_b, l, mod,
                              ln1_g[l].reshape(1, d), ln1_b[l].reshape(1, d), st['mod_row'])
            st['x'] = mlp(u2, x1, w1_b, w2_b, l, mod, ln2_g[l].reshape(1, d), ln2_b[l].reshape(1, d),
                          st['mod_row'])

    y_prompt = streams['ctx']['x'].reshape(bp, tp, d)
    y_sample = streams['lat']['x'].reshape(bs, ts, d)
    return (y_prompt, y_sample, jnp.stack(ks, axis=1), jnp.stack(vs, axis=1), jnp.stack(lrus, axis=1),
            jnp.stack(dns, axis=1), jnp.stack(s5rs, axis=1), jnp.stack(s5is, axis=1))
```

```python
import functools

import jax
import jax.numpy as jnp
from jax import lax
from jax.experimental import pallas as pl
from jax.experimental.pallas import tpu as pltpu

F32 = jnp.float32
BF16 = jnp.bfloat16

D_MODEL = 2048
DEPTH = 2
GRID_W = 64
CONV_W = 4
EPS = 1e-6
ROPE_THETA = 10000.0
N_MOD = 6
HEAD_DIM = 128
ATTN_WIDTH = D_MODEL // 4
ATTN_HEADS = ATTN_WIDTH // HEAD_DIM
ATTN_KV_HEADS = ATTN_HEADS // 2
ATTN_GROUP = ATTN_HEADS // ATTN_KV_HEADS
KV_WIDTH = ATTN_KV_HEADS * HEAD_DIM
ATTN_SCALE = HEAD_DIM ** -0.5
LRU_WIDTH = D_MODEL // 4
LRU_BLOCKS = 8
LRU_C = 8.0
DN_DK = 128
DN_DV = 128
DN_WIDTH = D_MODEL // 4
DN_HEADS = DN_WIDTH // DN_DV
DN_CHUNK = 64
S5_WIDTH = D_MODEL // 4
S5_CH = 16
S5_GROUPS = S5_WIDTH // S5_CH
S5_STATE = 64
S5_NSTATE = S5_GROUPS * S5_STATE
DEEPNORM_ALPHA = (2 * DEPTH) ** 0.25
MIXW = D_MODEL // 4

COL_DQ = 0
COL_DZ = 1536
COL_Q = 2048
COL_KV = 2560
COL_LX = 3072
COL_LG = 3584
COL_SU = 4096
N_PROJ = 4608
N_BA = 2 * 2 * DN_HEADS
LANE = 128
MXU = 256

MOD_ROWS = 8
PITCH_PAD = 4
OUT_SUB = 256
IN_SUB = 256
VMEM_LIMIT = 56 * 1024 * 1024


def _cparams(sem):
    return pltpu.CompilerParams(dimension_semantics=sem, vmem_limit_bytes=VMEM_LIMIT)


def _ln(x):
    mu = jnp.mean(x, axis=-1, keepdims=True)
    xc = x - mu
    var = jnp.mean(xc * xc, axis=-1, keepdims=True)
    return xc * lax.rsqrt(var + EPS)


def _softplus(x):
    return jnp.maximum(x, 0.0) + jnp.log1p(jnp.exp(-jnp.abs(x)))


def _gelu(x):
    return 0.5 * x * (1.0 + jnp.tanh(0.7978845608028654 * (x + 0.044715 * (x * x * x))))


def _dot(a, b):
    return jnp.dot(a, b, preferred_element_type=F32)


def _dot_nt(a, b):
    return lax.dot_general(a, b, (((1,), (1,)), ((), ())), preferred_element_type=F32)


def _ada_kernel(c_ref, w_ref, b_ref, o_ref):
    cs = c_ref[...]
    s = cs * jax.nn.sigmoid(cs)
    o_ref[0] = _dot(s.astype(BF16), w_ref[0].astype(BF16)) + b_ref[0]


def ada_mod(cond, w_ada, b_ada, tn=1024):
    depth, d, n = w_ada.shape
    return pl.pallas_call(
        _ada_kernel,
        grid=(depth, n // tn),
        in_specs=[pl.BlockSpec((MOD_ROWS, d), lambda l, j: (0, 0)),
                  pl.BlockSpec((1, d, tn), lambda l, j: (l, 0, j)),
                  pl.BlockSpec((1, 1, tn), lambda l, j: (l, 0, j))],
        out_specs=pl.BlockSpec((1, MOD_ROWS, tn), lambda l, j: (l, 0, j)),
        out_shape=jax.ShapeDtypeStruct((depth, MOD_ROWS, n), F32),
        compiler_params=_cparams(("parallel", "parallel")),
        name="ada_mod",
    )(cond, w_ada, b_ada.reshape(depth, 1, n))


def _in_kernel(x_ref, sc_ref, sh_ref, w_ref, wba_ref, o_ref, ba_ref, u_ref):
    @pl.when(pl.program_id(1) == 0)
    def _():
        for r0 in range(0, x_ref.shape[0], IN_SUB):
            rows = slice(r0, r0 + IN_SUB)
            u_ref[rows, :] = (_ln(x_ref[rows, :]) * (1.0 + sc_ref[0]) + sh_ref[0]).astype(BF16)
        ba_ref[...] = _dot(u_ref[...], wba_ref[...])

    o_ref[...] = _dot(u_ref[...], w_ref[...])


def in_proj(x, mod, w, w_ba, layer, mod_row, tm=1024, tn=768):
    m, d = x.shape
    n = w.shape[2]

    def mod_spec(kind):
        return pl.BlockSpec((1, 1, d), lambda i, j: (kind * MOD_ROWS + mod_row(i * tm), 0, 0))

    return pl.pallas_call(
        _in_kernel,
        grid=(m // tm, n // tn),
        in_specs=[pl.BlockSpec((tm, d), lambda i, j: (i, 0)),
                  mod_spec(1), mod_spec(0),
                  pl.BlockSpec((None, d, tn), lambda i, j: (layer, 0, j)),
                  pl.BlockSpec((None, d, LANE), lambda i, j: (layer, 0, 0))],
        out_specs=[pl.BlockSpec((tm, tn), lambda i, j: (i, j)),
                   pl.BlockSpec((tm, LANE), lambda i, j: (i, 0))],
        out_shape=[jax.ShapeDtypeStruct((m, n), F32), jax.ShapeDtypeStruct((m, LANE), F32)],
        scratch_shapes=[pltpu.VMEM((tm, d), BF16)],
        compiler_params=_cparams(("parallel", "arbitrary")),
        name="in_proj",
    )(x, mod, mod, w, w_ba)


def _rms_heads(x, g, heads):
    outs = []
    for h in range(heads):
        xh = x[:, h * HEAD_DIM:(h + 1) * HEAD_DIM]
        outs.append(xh * lax.rsqrt(jnp.mean(xh * xh, axis=-1, keepdims=True) + EPS) * g)
    return outs


def _softmax_av(q, k, v, scale):
    s = _dot_nt(q, k)
    if scale is not None:
        s = s * scale
    m = jnp.max(s, axis=-1, keepdims=True)
    p = jnp.exp(s - m)
    l = jnp.sum(p, axis=-1, keepdims=True)
    return _dot(p.astype(BF16), v) / l


def _ctx_attn_kernel(q_ref, kv_ref, qg_ref, kg_ref, o_ref, kn_ref, v_ref):
    qs = _rms_heads(q_ref[...], qg_ref[...], ATTN_HEADS)
    kv = kv_ref[...]
    ks = _rms_heads(kv[:, :KV_WIDTH], kg_ref[...], ATTN_KV_HEADS)
    v = kv[:, KV_WIDTH:]
    v_ref[...] = v
    t = q_ref.shape[0]
    for kh in range(ATTN_KV_HEADS):
        kn_ref[:, kh * HEAD_DIM:(kh + 1) * HEAD_DIM] = ks[kh]
        q2 = jnp.concatenate([qs[kh * ATTN_GROUP + g] for g in range(ATTN_GROUP)], axis=0).astype(BF16)
        o = _softmax_av(q2, ks[kh].astype(BF16), v[:, kh * HEAD_DIM:(kh + 1) * HEAD_DIM].astype(BF16), ATTN_SCALE)
        for g in range(ATTN_GROUP):
            h = kh * ATTN_GROUP + g
            o_ref[:, h * HEAD_DIM:(h + 1) * HEAD_DIM] = o[g * t:(g + 1) * t].astype(BF16)


def ctx_attention(proj, qg, kg, batch, seq):
    return pl.pallas_call(
        _ctx_attn_kernel,
        grid=(batch,),
        in_specs=[pl.BlockSpec((seq, ATTN_WIDTH), lambda b: (b, COL_Q // ATTN_WIDTH)),
                  pl.BlockSpec((seq, 2 * KV_WIDTH), lambda b: (b, COL_KV // (2 * KV_WIDTH))),
                  pl.BlockSpec((1, HEAD_DIM), lambda b: (0, 0)),
                  pl.BlockSpec((1, HEAD_DIM), lambda b: (0, 0))],
        out_specs=[pl.BlockSpec((seq, ATTN_WIDTH), lambda b: (b, 0)),
                   pl.BlockSpec((seq, KV_WIDTH), lambda b: (b, 0)),
                   pl.BlockSpec((seq, KV_WIDTH), lambda b: (b, 0))],
        out_shape=[jax.ShapeDtypeStruct((batch * seq, ATTN_WIDTH), BF16),
                   jax.ShapeDtypeStruct((batch * seq, KV_WIDTH), F32),
                   jax.ShapeDtypeStruct((batch * seq, KV_WIDTH), F32)],
        compiler_params=_cparams(("parallel",)),
        name="ctx_attention",
    )(proj, proj, qg, kg)


def _rope(x, cos, sin, heads):
    w = x.shape[-1]
    lane = lax.broadcasted_iota(jnp.int32, x.shape, 1)
    quarter = HEAD_DIM // 4
    partner = jnp.where((lane % (2 * quarter)) < quarter,
                        pltpu.roll(x, w - quarter, 1), pltpu.roll(x, quarter, 1))
    cos_t = jnp.concatenate([cos] * heads, axis=1)
    sin_t = jnp.concatenate([sin] * heads, axis=1)
    return x * cos_t + partner * sin_t


def _lat_prep_kernel(q_ref, kv_ref, qg_ref, kg_ref, cos_ref, sin_ref, qo_ref, ko_ref, vo_ref):
    qn = jnp.concatenate(_rms_heads(q_ref[...], qg_ref[...], ATTN_HEADS), axis=1)
    kv = kv_ref[...]
    kn = jnp.concatenate(_rms_heads(kv[:, :KV_WIDTH], kg_ref[...], ATTN_KV_HEADS), axis=1)
    cos = cos_ref[...]
    sin = sin_ref[...]
    qo_ref[...] = (_rope(qn, cos, sin, ATTN_HEADS) * ATTN_SCALE).astype(BF16)
    ko_ref[...] = _rope(kn, cos, sin, ATTN_KV_HEADS).astype(BF16)
    vo_ref[...] = kv[:, KV_WIDTH:].astype(BF16)


def lat_prep(proj, qg, kg, cos, sin, seq, tm=512):
    m = proj.shape[0]
    per = seq // tm
    return pl.pallas_call(
        _lat_prep_kernel,
        grid=(m // tm,),
        in_specs=[pl.BlockSpec((tm, ATTN_WIDTH), lambda i: (i, COL_Q // ATTN_WIDTH)),
                  pl.BlockSpec((tm, 2 * KV_WIDTH), lambda i: (i, COL_KV // (2 * KV_WIDTH))),
                  pl.BlockSpec((1, HEAD_DIM), lambda i: (0, 0)),
                  pl.BlockSpec((1, HEAD_DIM), lambda i: (0, 0)),
                  pl.BlockSpec((tm, HEAD_DIM), lambda i: (i % per, 0)),
                  pl.BlockSpec((tm, HEAD_DIM), lambda i: (i % per, 0))],
        out_specs=[pl.BlockSpec((tm, ATTN_WIDTH), lambda i: (i, 0)),
                   pl.BlockSpec((tm, KV_WIDTH), lambda i: (i, 0)),
                   pl.BlockSpec((tm, KV_WIDTH), lambda i: (i, 0))],
        out_shape=[jax.ShapeDtypeStruct((m, ATTN_WIDTH), BF16),
                   jax.ShapeDtypeStruct((m, KV_WIDTH), BF16),
                   jax.ShapeDtypeStruct((m, KV_WIDTH), BF16)],
        compiler_params=_cparams(("parallel",)),
        name="lat_prep",
    )(proj, proj, qg, kg, cos, sin)


def _lat_attn_kernel(q_ref, k_ref, v_ref, o_ref):
    k = k_ref[0]
    v = v_ref[0]
    for g in range(ATTN_GROUP):
        sl = slice(g * HEAD_DIM, (g + 1) * HEAD_DIM)
        o_ref[:, sl] = _softmax_av(q_ref[:, sl], k, v, None).astype(BF16)


def lat_attention(q, k_all, v_all, seq, tq=256):
    b, s, _ = k_all.shape
    nq = seq // tq
    gw = ATTN_GROUP * HEAD_DIM
    return pl.pallas_call(
        _lat_attn_kernel,
        grid=(b, ATTN_KV_HEADS, nq),
        in_specs=[pl.BlockSpec((tq, gw), lambda bi, kh, qi: (bi * nq + qi, kh)),
                  pl.BlockSpec((1, s, HEAD_DIM), lambda bi, kh, qi: (bi, 0, kh)),
                  pl.BlockSpec((1, s, HEAD_DIM), lambda bi, kh, qi: (bi, 0, kh))],
        out_specs=pl.BlockSpec((tq, gw), lambda bi, kh, qi: (bi * nq + qi, kh)),
        out_shape=jax.ShapeDtypeStruct((b * seq, ATTN_WIDTH), BF16),
        compiler_params=_cparams(("parallel", "parallel", "arbitrary")),
        name="lat_attention",
    )(q, k_all, v_all)


def rope_tables(seq):
    t = jnp.arange(seq)
    row = (t // GRID_W).astype(F32)
    col = (t % GRID_W).astype(F32)
    quarter = HEAD_DIM // 4
    inv_freq = jnp.power(ROPE_THETA, -jnp.arange(quarter, dtype=F32) / quarter)
    ar = row[:, None] * inv_freq[None, :]
    ac = col[:, None] * inv_freq[None, :]
    cos = jnp.concatenate([jnp.cos(ar), jnp.cos(ar), jnp.cos(ac), jnp.cos(ac)], axis=1)
    sin = jnp.concatenate([-jnp.sin(ar), jnp.sin(ar), -jnp.sin(ac), jnp.sin(ac)], axis=1)
    return cos, sin


def _chunk_index(d, i, n):
    return i + d * (n - 1 - 2 * i)


def _lru_kernel(x_ref, xp_ref, xn_ref, cw_ref, cb_ref, wg_ref, bg_ref, sp_ref, h0_ref,
                h_ref, fin_ref, xpad_ref, a_ref, b_ref, hc_ref, *, nb, tt):
    d = pl.program_id(0)
    i = pl.program_id(2)
    n = pl.num_programs(2)
    ci = _chunk_index(d, i, n)
    pitch = tt + PITCH_PAD
    rows = nb * pitch
    nsl = LRU_WIDTH // LANE

    @pl.when(i == 0)
    def _():
        xpad_ref[...] = jnp.zeros_like(xpad_ref)
        hc_ref[...] = h0_ref[0]

    for s in range(nb):
        base = 8 + s * pitch
        xpad_ref[base:base + tt, :] = x_ref[s]
        xpad_ref[base - 1:base, :] = jnp.where(ci == 0, 0.0, xp_ref[s, 7:8, :])
        xpad_ref[base + tt:base + tt + 2, :] = jnp.where(ci == n - 1, 0.0, xn_ref[s, 0:2, :])
    w = cw_ref[...]
    xc = cb_ref[...] + sum(xpad_ref[7 + j:7 + j + rows, :] * w[j:j + 1, :] for j in range(CONV_W))
    pre = _dot(xc.astype(BF16), wg_ref[0]) + bg_ref[0]
    r = jax.nn.sigmoid(pre[:, :LRU_WIDTH])
    ig = jax.nn.sigmoid(pre[:, LRU_WIDTH:])
    a = jnp.exp((-LRU_C) * r * sp_ref[0])
    inp = jnp.sqrt(1.0 - a * a) * (ig * xc)
    for c in range(nsl):
        a_ref[c] = a[:, c * LANE:(c + 1) * LANE]
        b_ref[c] = inp[:, c * LANE:(c + 1) * LANE]

    def body(t, carry):
        row = t + d * (tt - 1 - 2 * t)
        out = []
        for c in range(nsl):
            idx = (c, pl.ds(row, nb, stride=pitch), slice(None))
            h = a_ref[idx] * carry[c] + b_ref[idx]
            b_ref[idx] = h
            out.append(h)
        return tuple(out)

    carry = tuple(hc_ref[:, c * LANE:(c + 1) * LANE] for c in range(nsl))
    carry = lax.fori_loop(0, tt, body, carry, unroll=8)
    for c in range(nsl):
        hc_ref[:, c * LANE:(c + 1) * LANE] = carry[c]
        for s in range(nb):
            h_ref[0, s, :, c * LANE:(c + 1) * LANE] = b_ref[c, s * pitch:s * pitch + tt, :]

    @pl.when(i == n - 1)
    def _():
        fin_ref[0] = hc_ref[...]


def lru_scan(proj3, h0, cw, cb, wg, bg, sp, nb, tt=128):
    b, t, _ = proj3.shape
    w = LRU_WIDTH
    n = t // tt
    t8 = tt // 8
    rows = nb * (tt + PITCH_PAD)
    col = COL_LX // w

    def cidx(d, i):
        return _chunk_index(d, i, n)

    return pl.pallas_call(
        functools.partial(_lru_kernel, nb=nb, tt=tt),
        grid=(2, b // nb, n),
        in_specs=[pl.BlockSpec((nb, tt, w), lambda d, g, i: (g, cidx(d, i), col)),
                  pl.BlockSpec((nb, 8, w), lambda d, g, i: (g, jnp.maximum(cidx(d, i) * t8 - 1, 0), col)),
                  pl.BlockSpec((nb, 8, w), lambda d, g, i: (g, jnp.minimum((cidx(d, i) + 1) * t8, t // 8 - 1), col)),
                  pl.BlockSpec((CONV_W, w), lambda d, g, i: (0, 0)),
                  pl.BlockSpec((1, w), lambda d, g, i: (0, 0)),
                  pl.BlockSpec((1, w, 2 * w), lambda d, g, i: (d, 0, 0)),
                  pl.BlockSpec((1, 1, 2 * w), lambda d, g, i: (d, 0, 0)),
                  pl.BlockSpec((1, 1, w), lambda d, g, i: (d, 0, 0)),
                  pl.BlockSpec((1, nb, w), lambda d, g, i: (d, g, 0))],
        out_specs=[pl.BlockSpec((1, nb, tt, w), lambda d, g, i: (d, g, cidx(d, i), 0)),
                   pl.BlockSpec((1, nb, w), lambda d, g, i: (d, g, 0))],
        out_shape=[jax.ShapeDtypeStruct((2, b, t, w), F32), jax.ShapeDtypeStruct((2, b, w), F32)],
        scratch_shapes=[pltpu.VMEM((rows + 16, w), F32),
                        pltpu.VMEM((w // LANE, rows, LANE), F32),
                        pltpu.VMEM((w // LANE, rows, LANE), F32),
                        pltpu.VMEM((nb, w), F32)],
        compiler_params=_cparams(("parallel", "parallel", "arbitrary")),
        name="lru_scan",
    )(proj3, proj3, proj3, cw, cb, wg, bg, sp, h0)


def _s5_kernel(u_ref, bb_ref, cc_ref, a_ref, h0_ref, y_ref, fin_ref, up_ref, s_ref, hc_ref, *, nb, tt):
    d = pl.program_id(0)
    i = pl.program_id(2)
    n = pl.num_programs(2)
    pitch = tt + PITCH_PAD
    nsl = S5_NSTATE // LANE

    @pl.when(i == 0)
    def _():
        up_ref[...] = jnp.zeros_like(up_ref)
        hc_ref[...] = h0_ref[0]

    for s in range(nb):
        up_ref[s * pitch:s * pitch + tt, :] = u_ref[s]
    u2 = up_ref[...].astype(BF16)
    per_k = MXU // S5_CH * S5_STATE // MXU
    for nt in range(2 * S5_NSTATE // MXU):
        kt = (nt % (S5_NSTATE // MXU)) // per_k
        tile = _dot(u2[:, kt * MXU:(kt + 1) * MXU], bb_ref[0, kt * MXU:(kt + 1) * MXU, nt * MXU:(nt + 1) * MXU])
        s_ref[2 * nt] = tile[:, :LANE]
        s_ref[2 * nt + 1] = tile[:, LANE:]

    a_all = a_ref[0]
    group = 8
    for c0 in range(0, nsl, group):
        cs = list(range(c0, c0 + group))
        ar = [jnp.broadcast_to(a_all[:, c * LANE:(c + 1) * LANE], (nb, LANE)) for c in cs]
        ai = [jnp.broadcast_to(a_all[:, S5_NSTATE + c * LANE:S5_NSTATE + (c + 1) * LANE], (nb, LANE)) for c in cs]

        def body(t, carry, cs=cs, ar=ar, ai=ai):
            row = t + d * (tt - 1 - 2 * t)
            out = []
            for j, c in enumerate(cs):
                hr, hi = carry[2 * j], carry[2 * j + 1]
                ire = (c, pl.ds(row, nb, stride=pitch), slice(None))
                iim = (nsl + c, pl.ds(row, nb, stride=pitch), slice(None))
                nr = ar[j] * hr - ai[j] * hi + s_ref[ire]
                ni = ar[j] * hi + ai[j] * hr + s_ref[iim]
                s_ref[ire] = nr
                s_ref[iim] = ni
                out += [nr, ni]
            return tuple(out)

        carry = []
        for c in cs:
            carry += [hc_ref[:, c * LANE:(c + 1) * LANE], hc_ref[:, S5_NSTATE + c * LANE:S5_NSTATE + (c + 1) * LANE]]
        carry = lax.fori_loop(0, tt, body, tuple(carry), unroll=4)
        for j, c in enumerate(cs):
            hc_ref[:, c * LANE:(c + 1) * LANE] = carry[2 * j]
            hc_ref[:, S5_NSTATE + c * LANE:S5_NSTATE + (c + 1) * LANE] = carry[2 * j + 1]

    n_out = S5_WIDTH // MXU
    per_n = nsl // n_out
    for nt in range(n_out):
        slabs = [per_n * nt + k for k in range(per_n)] + [nsl + per_n * nt + k for k in range(per_n)]
        lhs = jnp.concatenate([s_ref[c] for c in slabs], axis=1).astype(BF16)
        y = _dot(lhs, cc_ref[0, nt])
        for s in range(nb):
            y_ref[0, s, :, nt * MXU:(nt + 1) * MXU] = y[s * pitch:s * pitch + tt]

    @pl.when(i == n - 1)
    def _():
        fin_ref[0] = hc_ref[...]


def s5_scan(proj3, h0, bb, cc, a, nb, tt=128):
    b, t, _ = proj3.shape
    w = S5_WIDTH
    n = t // tt
    rows = nb * (tt + PITCH_PAD)
    ns2 = 2 * S5_NSTATE

    def cidx(d, i):
        return _chunk_index(d, i, n)

    return pl.pallas_call(
        functools.partial(_s5_kernel, nb=nb, tt=tt),
        grid=(2, b // nb, n),
        in_specs=[pl.BlockSpec((nb, tt, w), lambda d, g, i: (g, cidx(d, i), COL_SU // w)),
                  pl.BlockSpec((1, w, ns2), lambda d, g, i: (d, 0, 0)),
                  pl.BlockSpec((1,) + cc.shape[1:], lambda d, g, i: (d, 0, 0, 0)),
                  pl.BlockSpec((1, 1, ns2), lambda d, g, i: (d, 0, 0)),
                  pl.BlockSpec((1, nb, ns2), lambda d, g, i: (d, g, 0))],
        out_specs=[pl.BlockSpec((1, nb, tt, w), lambda d, g, i: (d, g, cidx(d, i), 0)),
                   pl.BlockSpec((1, nb, ns2), lambda d, g, i: (d, g, 0))],
        out_shape=[jax.ShapeDtypeStruct((2, b, t, w), F32), jax.ShapeDtypeStruct((2, b, ns2), F32)],
        scratch_shapes=[pltpu.VMEM((rows, w), F32),
                        pltpu.VMEM((ns2 // LANE, rows, LANE), F32),
                        pltpu.VMEM((nb, ns2), F32)],
        compiler_params=_cparams(("parallel", "parallel", "arbitrary")),
        name="s5_scan",
    )(proj3, bb, cc, a, h0)


def _split3(x):
    x1 = x.astype(BF16)
    r1 = x - x1.astype(F32)
    x2 = r1.astype(BF16)
    x3 = (r1 - x2.astype(F32)).astype(BF16)
    return x1, x2, x3


def _mm(x, y):
    return _dot(x.astype(BF16), y.astype(BF16))


def _unit_tri_inverses(mats, rr, cc):
    same16 = (rr // 16) == (cc // 16)
    same32 = (rr // 32) == (cc // 32)
    p = [jnp.where(same16, -a, 0.0) for a in mats]
    t = [jnp.where(rr == cc, 1.0, x) for x in p]
    for _ in range(3):
        p = [_mm(x, x) for x in p]
        t = [x + _mm(x, y) for x, y in zip(t, p)]
    for mask in (same32 & jnp.logical_not(same16), jnp.logical_not(same32)):
        te = [_mm(x, jnp.where(mask, a, 0.0)) for x, a in zip(t, mats)]
        t = [x - _mm(y, x) for x, y in zip(t, te)]
    return t


def _dn_prep_kernel(x_ref, xp_ref, xn_ref, cw_ref, o_ref, xpad_ref, *, tb):
    i = pl.program_id(1)
    n = pl.num_programs(1)
    hw = DN_HEADS * DN_DK
    xpad_ref[8:8 + tb, :] = x_ref[0]
    xpad_ref[0:8, :] = jnp.where(i == 0, 0.0, xp_ref[0])
    xpad_ref[8 + tb:16 + tb, :] = jnp.where(i == n - 1, 0.0, xn_ref[0])
    w = cw_ref[...]
    xc = sum(xpad_ref[7 + j:7 + j + tb, :] * w[j:j + 1, :] for j in range(CONV_W))
    qkv = xc * jax.nn.sigmoid(xc)
    for h in range(DN_HEADS):
        q = qkv[:, h * DN_DK:(h + 1) * DN_DK]
        k = qkv[:, hw + h * DN_DK:hw + (h + 1) * DN_DK]
        o_ref[0, :, h * DN_DK:(h + 1) * DN_DK] = (
            q * lax.rsqrt(jnp.sum(q * q, axis=-1, keepdims=True) + EPS) * (DN_DK ** -0.5))
        o_ref[0, :, hw + h * DN_DK:hw + (h + 1) * DN_DK] = (
            k * lax.rsqrt(jnp.sum(k * k, axis=-1, keepdims=True) + EPS))
    o_ref[0, :, 2 * hw:] = qkv[:, 2 * hw:]


def dn_prep(proj3, cw, tb=256):
    b, t, _ = proj3.shape
    n = t // tb
    t8 = tb // 8
    w3 = 3 * DN_WIDTH
    cq = COL_DQ // w3
    return pl.pallas_call(
        functools.partial(_dn_prep_kernel, tb=tb),
        grid=(b, n),
        in_specs=[pl.BlockSpec((1, tb, w3), lambda bi, i: (bi, i, cq)),
                  pl.BlockSpec((1, 8, w3), lambda bi, i: (bi, jnp.maximum(i * t8 - 1, 0), cq)),
                  pl.BlockSpec((1, 8, w3), lambda bi, i: (bi, jnp.minimum((i + 1) * t8, t // 8 - 1), cq)),
                  pl.BlockSpec((CONV_W, w3), lambda bi, i: (0, 0))],
        out_specs=pl.BlockSpec((1, tb, w3), lambda bi, i: (bi, i, 0)),
        out_shape=jax.ShapeDtypeStruct((b, t, w3), F32),
        scratch_shapes=[pltpu.VMEM((tb + 16, w3), F32)],
        compiler_params=_cparams(("parallel", "parallel")),
        name="dn_prep",
    )(proj3, proj3, proj3, cw)


def _dn_kernel(xf_ref, baf_ref, xb_ref, bab_ref, al_ref, dtb_ref, s0_ref, of_ref, ob_ref, fin_ref, s_ref,
               *, nbat, tb, shared):
    i = pl.program_id(1)
    n = pl.num_programs(1)
    ch = DN_CHUNK
    nc = tb // ch
    hw = DN_HEADS * DN_DK

    @pl.when(i == 0)
    def _():
        s_ref[...] = s0_ref[...]

    rb = lax.broadcasted_iota(jnp.int32, (tb, tb), 0)
    cb = lax.broadcasted_iota(jnp.int32, (tb, tb), 1)
    rr = lax.broadcasted_iota(jnp.int32, (ch, ch), 0)
    cc = lax.broadcasted_iota(jnp.int32, (ch, ch), 1)
    al = al_ref[...]
    dtb = dtb_ref[...]

    def qkv_heads(x_ref, bat):
        out = {}
        for c in range(nc):
            rows = slice(c * ch, (c + 1) * ch)
            for h in range(DN_HEADS):
                out[(c, h)] = (x_ref[bat, rows, h * DN_DK:(h + 1) * DN_DK],
                               x_ref[bat, rows, hw + h * DN_DK:hw + (h + 1) * DN_DK],
                               x_ref[bat, rows, 2 * hw + h * DN_DV:2 * hw + (h + 1) * DN_DV])
        return out

    units = []
    for d in range(2):
        reverse = d == 1
        x_ref, ba_ref = (xb_ref, bab_ref) if reverse else (xf_ref, baf_ref)
        tri = ((rb // ch) == (cb // ch)) & ((cb >= rb) if reverse else (cb <= rb))
        tri = jnp.where(tri, 1.0, 0.0).astype(BF16)
        incl = (cc >= rr) if reverse else (cc <= rr)
        strict = (cc > rr) if reverse else (cc < rr)
        if d == 0 or not shared:
            heads = [qkv_heads(x_ref, bat) for bat in range(nbat)]
        for bat in range(nbat):
            ba = ba_ref[bat]
            beta_all = jax.nn.sigmoid(ba)
            g_all = -jnp.exp(al) * _softplus(ba + dtb)
            gc_all = sum(_dot(tri, piece) for piece in _split3(g_all))
            for c in range(nc):
                r0 = c * ch
                gct = gc_all[r0:r0 + ch, :].T
                for h in range(DN_HEADS):
                    lane = d * DN_HEADS + h
                    beta = beta_all[r0:r0 + ch, lane:lane + 1]
                    gcol = gc_all[r0:r0 + ch, 2 * DN_HEADS + lane:2 * DN_HEADS + lane + 1]
                    grow = gct[2 * DN_HEADS + lane:2 * DN_HEADS + lane + 1, :]
                    q, k, v = heads[bat][(c, h)]
                    g_last = gcol[0:1, :] if reverse else gcol[ch - 1:ch, :]
                    units.append(dict(d=d, bat=bat, c=c, h=h, q=q, k=k, v=v, beta=beta, gcol=gcol, kb=k * beta,
                                      decay=jnp.where(incl, jnp.exp(gcol - grow), 0.0), strict=strict,
                                      g_last=g_last))

    for u in units:
        kbf = u['k'].astype(BF16)
        u['a'] = jnp.where(u['strict'], _dot_nt(u['kb'].astype(BF16), kbf) * u['decay'], 0.0)
        u['qk'] = (_dot_nt(u['q'].astype(BF16), kbf) * u['decay']).astype(BF16)
    t_inv = _unit_tri_inverses([u['a'] for u in units], rr, cc)
    for u, t in zip(units, t_inv):
        u['rhs'] = jnp.concatenate([u['v'] * u['beta'], u['kb'] * jnp.exp(u['gcol'])], axis=1)
        u['t_off'] = jnp.where(rr == cc, 0.0, t)
    for u in units:
        sol = u['rhs'] + _mm(u['t_off'], u['rhs'])
        u['u_val'] = sol[:, :DN_DV]
        u['w'] = sol[:, DN_DV:].astype(BF16)
        u['qe'] = (u['q'] * jnp.exp(u['gcol'])).astype(BF16)
        u['kdec_t'] = (u['k'] * jnp.exp(u['g_last'] - u['gcol'])).T.astype(BF16)
        u['eg'] = jnp.exp(u['g_last'])

    state = {(d, bat, h): s_ref[d, bat, h] for d in range(2) for bat in range(nbat) for h in range(DN_HEADS)}
    for j in range(nc):
        cur = [u for u in units if u['c'] == (nc - 1 - j if u['d'] == 1 else j)]
        keys = [(u['d'], u['bat'], u['h']) for u in cur]
        sb = [state[key].astype(BF16) for key in keys]
        ws = [_dot(u['w'], s) for u, s in zip(cur, sb)]
        qs = [_dot(u['qe'], s) for u, s in zip(cur, sb)]
        vb = [(u['u_val'] - x).astype(BF16) for u, x in zip(cur, ws)]
        os = [x + _dot(u['qk'], y) for u, x, y in zip(cur, qs, vb)]
        sn = [state[key] * u['eg'] + _dot(u['kdec_t'], y) for u, key, y in zip(cur, keys, vb)]
        for u, key, o, s in zip(cur, keys, os, sn):
            state[key] = s
            o_ref = ob_ref if u['d'] == 1 else of_ref
            o_ref[u['bat'], u['c'] * ch:(u['c'] + 1) * ch, u['h'] * DN_DV:(u['h'] + 1) * DN_DV] = o
    for (d, bat, h), s in state.items():
        s_ref[d, bat, h] = s

    @pl.when(i == n - 1)
    def _():
        fin_ref[...] = s_ref[...]


def dn_scan(qkv3, ba3, s0, al, dtb, nbat, tb):
    b, t, w3 = qkv3.shape
    n = t // tb

    def data_specs(cidx):
        return [pl.BlockSpec((nbat, tb, w3), lambda g, i: (g, cidx(i), 0)),
                pl.BlockSpec((nbat, tb, LANE), lambda g, i: (g, cidx(i), 0))]

    fwd = lambda i: i
    bwd = lambda i: n - 1 - i
    st_spec = pl.BlockSpec((2, nbat, DN_HEADS, DN_DK, DN_DV), lambda g, i: (0, g, 0, 0, 0))
    return pl.pallas_call(
        functools.partial(_dn_kernel, nbat=nbat, tb=tb, shared=(n == 1)),
        grid=(b // nbat, n),
        in_specs=data_specs(fwd) + data_specs(bwd) + [
            pl.BlockSpec((1, LANE), lambda g, i: (0, 0)),
            pl.BlockSpec((1, LANE), lambda g, i: (0, 0)),
            st_spec],
        out_specs=[pl.BlockSpec((nbat, tb, DN_WIDTH), lambda g, i: (g, i, 0)),
                   pl.BlockSpec((nbat, tb, DN_WIDTH), lambda g, i: (g, n - 1 - i, 0)),
                   st_spec],
        out_shape=[jax.ShapeDtypeStruct((b, t, DN_WIDTH), F32),
                   jax.ShapeDtypeStruct((b, t, DN_WIDTH), F32),
                   jax.ShapeDtypeStruct((2, b, DN_HEADS, DN_DK, DN_DV), F32)],
        scratch_shapes=[pltpu.VMEM((2, nbat, DN_HEADS, DN_DK, DN_DV), F32)],
        compiler_params=_cparams(("parallel", "arbitrary")),
        name="dn_scan",
    )(qkv3, ba3, qkv3, ba3, al, dtb, s0)


def _post_kernel(hf_ref, hb_ref, lg_ref, of_ref, ob_ref, dz_ref, ng_ref, yf_ref, yb_ref, su_ref, sd_ref,
                 gw_ref, gb_ref, lru_ref, dn_ref, s5_ref):
    lru_ref[...] = ((hf_ref[0, 0] + hb_ref[0, 0]) * _gelu(lg_ref[...])).astype(BF16)
    o = of_ref[0] + ob_ref[0]
    dz = dz_ref[...]
    for h in range(DN_HEADS):
        sl = slice(h * DN_DV, (h + 1) * DN_DV)
        oh = o[:, sl]
        oh = oh * lax.rsqrt(jnp.mean(oh * oh, axis=-1, keepdims=True) + EPS) * ng_ref[...]
        zh = dz[:, sl]
        dn_ref[:, sl] = (oh * (zh * jax.nn.sigmoid(zh))).astype(BF16)
    y = yf_ref[0, 0] + yb_ref[0, 0] + sd_ref[...] * su_ref[...]
    gy = _gelu(y)
    s5_ref[...] = (gy * jax.nn.sigmoid(_dot(gy.astype(BF16), gw_ref[...]) + gb_ref[...])).astype(BF16)


def mix_post(proj, h_lru, o_f, o_b, y_s5, ng, sd, gw, gb):
    m = proj.shape[0]
    _, b, t, w = h_lru.shape
    tm = min(512, t)
    per = t // tm
    tok = pl.BlockSpec((tm, w), lambda i: (i, 0))

    def pcol(c0):
        return pl.BlockSpec((tm, w), lambda i: (i, c0 // w))

    def dirspec(d):
        return pl.BlockSpec((1, 1, tm, w), lambda i: (d, i // per, i % per, 0))

    seq = pl.BlockSpec((1, tm, w), lambda i: (i // per, i % per, 0))

    def vec(n):
        return pl.BlockSpec((1, n), lambda i: (0, 0))

    out = jax.ShapeDtypeStruct((m, w), BF16)
    return pl.pallas_call(
        _post_kernel,
        grid=(m // tm,),
        in_specs=[dirspec(0), dirspec(1), pcol(COL_LG), seq, seq, pcol(COL_DZ), vec(DN_DV),
                  dirspec(0), dirspec(1), pcol(COL_SU), vec(w), pl.BlockSpec((w, w), lambda i: (0, 0)), vec(w)],
        out_specs=[tok, tok, tok],
        out_shape=[out, out, out],
        compiler_params=_cparams(("parallel",)),
        name="mix_post",
    )(h_lru, h_lru, proj, o_f, o_b, proj, ng, y_s5, y_s5, proj, sd, gw, gb)


def _out_kernel(x_ref, a_ref, b_ref, c_ref, d_ref, w_ref, gate_ref, g_ref, bb_ref, sc_ref, sh_ref,
                x1_ref, u2_ref):
    q = MIXW
    for r0 in range(0, x_ref.shape[0], OUT_SUB):
        rows = slice(r0, r0 + OUT_SUB)
        y = _dot(a_ref[rows, :], w_ref[0:q, :])
        y += _dot(b_ref[rows, :], w_ref[q:2 * q, :])
        y += _dot(c_ref[rows, :], w_ref[2 * q:3 * q, :])
        y += _dot(d_ref[rows, :], w_ref[3 * q:, :])
        z = DEEPNORM_ALPHA * x_ref[rows, :] + gate_ref[0] * y
        x1 = _ln(z) * g_ref[...] + bb_ref[...]
        x1_ref[rows, :] = x1
        u2_ref[rows, :] = (_ln(x1) * (1.0 + sc_ref[0]) + sh_ref[0]).astype(BF16)


def out_proj(x, parts, w, layer, mod, ln_g, ln_b, mod_row, tm=512):
    m, d = x.shape

    def mod_spec(kind):
        return pl.BlockSpec((1, 1, d), lambda i: (kind * MOD_ROWS + mod_row(i * tm), 0, 0))

    vec = pl.BlockSpec((1, d), lambda i: (0, 0))
    part = pl.BlockSpec((tm, MIXW), lambda i: (i, 0))
    return pl.pallas_call(
        _out_kernel,
        grid=(m // tm,),
        in_specs=[pl.BlockSpec((tm, d), lambda i: (i, 0)), part, part, part, part,
                  pl.BlockSpec((None, d, d), lambda i: (layer, 0, 0)),
                  mod_spec(2), vec, vec, mod_spec(4), mod_spec(3)],
        out_specs=[pl.BlockSpec((tm, d), lambda i: (i, 0)), pl.BlockSpec((tm, d), lambda i: (i, 0))],
        out_shape=[jax.ShapeDtypeStruct((m, d), F32), jax.ShapeDtypeStruct((m, d), BF16)],
        compiler_params=_cparams(("parallel",)),
        name="out_proj",
    )(x, *parts, w, mod, ln_g, ln_b, mod, mod)


def _mlp_kernel(u_ref, x_ref, w1_ref, w2_ref, gate_ref, g_ref, b_ref, o_ref, acc_ref):
    f = pl.program_id(1)
    h = _dot(u_ref[...], w1_ref[...])
    h = jnp.square(jnp.maximum(h, 0.0)).astype(BF16)
    part = _dot(h, w2_ref[...])

    @pl.when(f == 0)
    def _():
        acc_ref[...] = part

    @pl.when(f > 0)
    def _():
        acc_ref[...] += part

    @pl.when(f == pl.num_programs(1) - 1)
    def _():
        z = DEEPNORM_ALPHA * x_ref[...] + gate_ref[0] * acc_ref[...]
        o_ref[...] = _ln(z) * g_ref[...] + b_ref[...]


def mlp(u, x, w1, w2, layer, mod, ln_g, ln_b, mod_row, tm=512, tf=1024):
    m, d = x.shape
    ff = w1.shape[2]
    vec = pl.BlockSpec((1, d), lambda i, f: (0, 0))
    return pl.pallas_call(
        _mlp_kernel,
        grid=(m // tm, ff // tf),
        in_specs=[pl.BlockSpec((tm, d), lambda i, f: (i, 0)),
                  pl.BlockSpec((tm, d), lambda i, f: (i, 0)),
                  pl.BlockSpec((None, d, tf), lambda i, f: (layer, 0, f)),
                  pl.BlockSpec((None, tf, d), lambda i, f: (layer, f, 0)),
                  pl.BlockSpec((1, 1, d), lambda i, f: (5 * MOD_ROWS + mod_row(i * tm), 0, 0)),
                  vec, vec],
        out_specs=pl.BlockSpec((tm, d), lambda i, f: (i, 0)),
        out_shape=jax.ShapeDtypeStruct((m, d), F32),
        scratch_shapes=[pltpu.VMEM((tm, d), F32)],
        compiler_params=_cparams(("parallel", "arbitrary")),
        name="mlp",
    )(u, x, w1, w2, mod, ln_g, ln_b)


def _reorder_w_in(w):
    def cols(c0, n):
        return w[..., c0:c0 + n]

    aq, akv, lx, lg = cols(0, 512), cols(512, 512), cols(1024, 512), cols(1536, 512)
    dqkv, dz, ba, su = cols(2048, 1536), cols(3584, 512), cols(4096, N_BA), cols(4096 + N_BA, 512)
    pad = jnp.zeros(w.shape[:2] + (LANE - N_BA,), w.dtype)
    main = jnp.concatenate([dqkv, dz, aq, akv, lx, lg, su], axis=-1).astype(BF16)
    return main, jnp.concatenate([ba, pad], axis=-1).astype(BF16)


def _block_diag(blocks):
    n, r, c = blocks.shape[-3:]
    eye = jnp.eye(n, dtype=blocks.dtype)
    out = blocks[..., :, :, None, :] * eye[:, None, :, None]
    return out.reshape(blocks.shape[:-3] + (n * r, n * c))


def _lru_params(wa, ba, wx, bx, lam):
    wg = jnp.concatenate([_block_diag(wa.astype(BF16)), _block_diag(wx.astype(BF16))], axis=-1)
    bg = jnp.concatenate([ba, bx], axis=-1)[..., None, :]
    sp = jax.nn.softplus(-lam)[..., None, :]
    return wg, bg, sp


def _s5_params(lam_re, lam_im, log_dt, b_re, b_im, c_re, c_im):
    lead = lam_re.shape[:-2]
    dt = jnp.exp(log_dt)[..., None]
    mag = jnp.exp(lam_re * dt)
    abar_re = mag * jnp.cos(lam_im * dt)
    abar_im = mag * jnp.sin(lam_im * dt)
    den = lam_re * lam_re + lam_im * lam_im
    nr = abar_re - 1.0
    ni = abar_im
    f_re = (nr * lam_re + ni * lam_im) / den
    f_im = (ni * lam_re - nr * lam_im) / den
    bb_re = f_re[..., None] * b_re - f_im[..., None] * b_im
    bb_im = f_re[..., None] * b_im + f_im[..., None] * b_re
    to_in = lambda m: _block_diag(jnp.swapaxes(m, -1, -2).astype(BF16))
    bb = jnp.concatenate([to_in(bb_re), to_in(bb_im)], axis=-1)
    n_out = S5_WIDTH // MXU
    gpt = S5_GROUPS // n_out

    def to_out(m, nt):
        return _block_diag(jnp.swapaxes(m[..., nt * gpt:(nt + 1) * gpt, :, :], -1, -2).astype(BF16))

    cc = jnp.stack([jnp.concatenate([to_out(c_re, nt), to_out(-c_im, nt)], axis=-2)
                    for nt in range(n_out)], axis=-3)
    a = jnp.concatenate([abar_re.reshape(lead + (1, -1)), abar_im.reshape(lead + (1, -1))], axis=-1)
    return bb, cc, a


def _lane_row(vals, offset):
    return jnp.zeros((1, LANE), F32).at[0, offset:offset + vals.size].set(vals.reshape(-1))


def kernel(x_prompt, x_sample, cache_attn_k, cache_attn_v, state_rglru, state_delta, state_s5_re, state_s5_im, c, c_ctx, w_ada, b_ada, w_in, w_out, ln1_g, ln1_b, ln2_g, ln2_b, w_mlp1, w_mlp2, q_norm_g, k_norm_g, lru_conv_w, lru_conv_b, lru_wa, lru_ba, lru_wx, lru_bx, lru_lambda, dn_conv_w, dn_a_log, dn_dt_bias, dn_norm_g, s5_lambda_re, s5_lambda_im, s5_log_dt, s5_b_re, s5_b_im, s5_c_re, s5_c_im, s5_d, s5_glu_w, s5_glu_b):
    bp, tp, d = x_prompt.shape
    bs, ts, _ = x_sample.shape
    ctx_row = bs

    cond = jnp.concatenate([c, c_ctx[None, :], jnp.zeros((MOD_ROWS - bs - 1, d), F32)], axis=0)
    mods = ada_mod(cond, w_ada, b_ada)
    w_in_r, w_ba = _reorder_w_in(w_in)
    w_out_b = w_out.astype(BF16)
    w1_b = w_mlp1.astype(BF16)
    w2_b = w_mlp2.astype(BF16)
    cos, sin = rope_tables(ts)
    wg_all, bg_all, sp_all = _lru_params(lru_wa, lru_ba, lru_wx, lru_bx, lru_lambda)
    bb_all, cc_all, a5_all = _s5_params(s5_lambda_re, s5_lambda_im, s5_log_dt, s5_b_re, s5_b_im, s5_c_re, s5_c_im)

    streams = {
        'ctx': dict(x=x_prompt.reshape(bp * tp, d), b=bp, t=tp, nb=8, s5_tt=128, dn_nbat=2, dn_tb=tp,
                    mod_row=lambda tok: ctx_row),
        'lat': dict(x=x_sample.reshape(bs * ts, d), b=bs, t=ts, nb=bs, s5_tt=256, dn_nbat=bs, dn_tb=2 * DN_CHUNK,
                    mod_row=lambda tok: tok // ts),
    }
    ks, vs, lrus, dns, s5rs, s5is = [], [], [], [], [], []
    for l in range(DEPTH):
        mod = mods[l].reshape(MOD_ROWS, N_MOD, d).transpose(1, 0, 2).reshape(N_MOD * MOD_ROWS, 1, d)
        qg = q_norm_g[l].reshape(1, HEAD_DIM)
        kg = k_norm_g[l].reshape(1, HEAD_DIM)
        wg, bg, sp = wg_all[l], bg_all[l], sp_all[l]
        bb, cc, a5 = bb_all[l], cc_all[l], a5_all[l]
        al = _lane_row(dn_a_log[l], 2 * DN_HEADS)
        dtb = _lane_row(dn_dt_bias[l], 2 * DN_HEADS)
        for name, st in streams.items():
            b, t, nb = st['b'], st['t'], st['nb']
            is_ctx = name == 'ctx'
            proj, ba = in_proj(st['x'], mod, w_in_r, w_ba, l, st['mod_row'])
            proj3 = proj.reshape(b, t, N_PROJ)
            ba3 = ba.reshape(b, t, LANE)

            if is_ctx:
                attn, kn, vv = ctx_attention(proj, qg, kg, b, t)
                ks.append(kn.reshape(b, t, ATTN_KV_HEADS, HEAD_DIM))
                vs.append(vv.reshape(b, t, ATTN_KV_HEADS, HEAD_DIM))
                h0_lru = jnp.zeros((2, b, LRU_WIDTH), F32)
                s0_dn = jnp.zeros((2, b, DN_HEADS, DN_DK, DN_DV), F32)
                h0_s5 = jnp.zeros((2, b, 2 * S5_NSTATE), F32)
            else:
                q_s, k_s, v_s = lat_prep(proj, qg, kg, cos, sin, t)
                k_all = jnp.concatenate([cache_attn_k[:, l].reshape(b, -1, KV_WIDTH).astype(BF16),
                                         k_s.reshape(b, t, KV_WIDTH)], axis=1)
                v_all = jnp.concatenate([cache_attn_v[:, l].reshape(b, -1, KV_WIDTH).astype(BF16),
                                         v_s.reshape(b, t, KV_WIDTH)], axis=1)
                attn = lat_attention(q_s, k_all, v_all, t)
                h0_lru = jnp.swapaxes(state_rglru[:, l], 0, 1)
                s0_dn = jnp.swapaxes(state_delta[:, l], 0, 1)
                h0_s5 = jnp.swapaxes(jnp.concatenate([state_s5_re[:, l].reshape(b, 2, S5_NSTATE),
                                                      state_s5_im[:, l].reshape(b, 2, S5_NSTATE)], axis=-1), 0, 1)

            h_lru, lru_fin = lru_scan(proj3, h0_lru, lru_conv_w[l], lru_conv_b[l].reshape(1, -1), wg, bg, sp, nb)
            o_f, o_b, dn_fin = dn_scan(dn_prep(proj3, dn_conv_w[l]), ba3, s0_dn, al, dtb,
                                       st['dn_nbat'], st['dn_tb'])
            y_s5, s5_fin = s5_scan(proj3, h0_s5, bb, cc, a5, nb, st['s5_tt'])
            lru_out, dn_out, s5_out = mix_post(
                proj, h_lru, o_f, o_b, y_s5, dn_norm_g[l].reshape(1, DN_DV), s5_d[l].reshape(1, MIXW),
                s5_glu_w[l].astype(BF16), s5_glu_b[l].reshape(1, MIXW))
            if is_ctx:
                lrus.append(jnp.swapaxes(lru_fin, 0, 1))
                dns.append(jnp.swapaxes(dn_fin, 0, 1))
                s5_fin = jnp.swapaxes(s5_fin, 0, 1)
                s5rs.append(s5_fin[..., :S5_NSTATE].reshape(b, 2, S5_GROUPS, S5_STATE))
                s5is.append(s5_fin[..., S5_NSTATE:].reshape(b, 2, S5_GROUPS, S5_STATE))

            x1, u2 = out_proj(st['x'], [attn, lru_out, dn_out, s5_out], w_out_b, l, mod,
                              ln1_g[l].reshape(1, d), ln1_b[l].reshape(1, d), st['mod_row'])
            st['x'] = mlp(u2, x1, w1_b, w2_b, l, mod, ln2_g[l].reshape(1, d), ln2_b[l].reshape(1, d),
                          st['mod_row'])

    y_prompt = streams['ctx']['x'].reshape(bp, tp, d)
    y_sample = streams['lat']['x'].reshape(bs, ts, d)
    return (y_prompt, y_sample, jnp.stack(ks, axis=1), jnp.stack(vs, axis=1), jnp.stack(lrus, axis=1),
            jnp.stack(dns, axis=1), jnp.stack(s5rs, axis=1), jnp.stack(s5is, axis=1))
```

```python
import functools

import jax
import jax.numpy as jnp
from jax import lax
from jax.experimental import pallas as pl
from jax.experimental.pallas import tpu as pltpu

F32 = jnp.float32
BF16 = jnp.bfloat16

D_MODEL = 2048
DEPTH = 2
GRID_W = 64
CONV_W = 4
EPS = 1e-6
ROPE_THETA = 10000.0
N_MOD = 6
HEAD_DIM = 128
ATTN_WIDTH = D_MODEL // 4
ATTN_HEADS = ATTN_WIDTH // HEAD_DIM
ATTN_KV_HEADS = ATTN_HEADS // 2
ATTN_GROUP = ATTN_HEADS // ATTN_KV_HEADS
KV_WIDTH = ATTN_KV_HEADS * HEAD_DIM
ATTN_SCALE = HEAD_DIM ** -0.5
LRU_WIDTH = D_MODEL // 4
LRU_BLOCKS = 8
LRU_C = 8.0
DN_DK = 128
DN_DV = 128
DN_WIDTH = D_MODEL // 4
DN_HEADS = DN_WIDTH // DN_DV
DN_CHUNK = 64
S5_WIDTH = D_MODEL // 4
S5_CH = 16
S5_GROUPS = S5_WIDTH // S5_CH
S5_STATE = 64
S5_NSTATE = S5_GROUPS * S5_STATE
DEEPNORM_ALPHA = (2 * DEPTH) ** 0.25
MIXW = D_MODEL // 4

COL_DQ = 0
COL_DZ = 1536
COL_Q = 2048
COL_KV = 2560
COL_LX = 3072
COL_LG = 3584
COL_SU = 4096
N_PROJ = 4608
N_BA = 2 * 2 * DN_HEADS
LANE = 128
MXU = 256

MOD_ROWS = 8
PITCH_PAD = 4
OUT_SUB = 256
IN_SUB = 256
VMEM_LIMIT = 56 * 1024 * 1024


def _cparams(sem):
    return pltpu.CompilerParams(dimension_semantics=sem, vmem_limit_bytes=VMEM_LIMIT)


def _ln(x):
    mu = jnp.mean(x, axis=-1, keepdims=True)
    xc = x - mu
    var = jnp.mean(xc * xc, axis=-1, keepdims=True)
    return xc * lax.rsqrt(var + EPS)


def _softplus(x):
    return jnp.maximum(x, 0.0) + jnp.log1p(jnp.exp(-jnp.abs(x)))


def _gelu(x):
    return 0.5 * x * (1.0 + jnp.tanh(0.7978845608028654 * (x + 0.044715 * (x * x * x))))


def _dot(a, b):
    return jnp.dot(a, b, preferred_element_type=F32)


def _dot_nt(a, b):
    return lax.dot_general(a, b, (((1,), (1,)), ((), ())), preferred_element_type=F32)


def _ada_kernel(c_ref, w_ref, b_ref, o_ref):
    cs = c_ref[...]
    s = cs * jax.nn.sigmoid(cs)
    o_ref[0] = _dot(s.astype(BF16), w_ref[0].astype(BF16)) + b_ref[0]


def ada_mod(cond, w_ada, b_ada, tn=1024):
    depth, d, n = w_ada.shape
    return pl.pallas_call(
        _ada_kernel,
        grid=(depth, n // tn),
        in_specs=[pl.BlockSpec((MOD_ROWS, d), lambda l, j: (0, 0)),
                  pl.BlockSpec((1, d, tn), lambda l, j: (l, 0, j)),
                  pl.BlockSpec((1, 1, tn), lambda l, j: (l, 0, j))],
        out_specs=pl.BlockSpec((1, MOD_ROWS, tn), lambda l, j: (l, 0, j)),
        out_shape=jax.ShapeDtypeStruct((depth, MOD_ROWS, n), F32),
        compiler_params=_cparams(("parallel", "parallel")),
        name="ada_mod",
    )(cond, w_ada, b_ada.reshape(depth, 1, n))


def _in_kernel(x_ref, sc_ref, sh_ref, w_ref, wba_ref, o_ref, ba_ref, u_ref):
    j = pl.program_id(1)

    @pl.when(j == 0)
    def _():
        for r0 in range(0, x_ref.shape[0], IN_SUB):
            rows = slice(r0, r0 + IN_SUB)
            u = (_ln(x_ref[rows, :]) * (1.0 + sc_ref[0]) + sh_ref[0]).astype(BF16)
            u_ref[rows, :] = u
            o_ref[rows, :] = _dot(u, w_ref[...])
            ba_ref[rows, :] = _dot(u, wba_ref[...])

    @pl.when(j > 0)
    def _():
        o_ref[...] = _dot(u_ref[...], w_ref[...])


def in_proj(x, mod, w, w_ba, layer, mod_row, tm=1024, tn=768):
    m, d = x.shape
    n = w.shape[2]

    def mod_spec(kind):
        return pl.BlockSpec((1, 1, d), lambda i, j: (kind * MOD_ROWS + mod_row(i * tm), 0, 0))

    return pl.pallas_call(
        _in_kernel,
        grid=(m // tm, n // tn),
        in_specs=[pl.BlockSpec((tm, d), lambda i, j: (i, 0)),
                  mod_spec(1), mod_spec(0),
                  pl.BlockSpec((None, d, tn), lambda i, j: (layer, 0, j)),
                  pl.BlockSpec((None, d, LANE), lambda i, j: (layer, 0, 0))],
        out_specs=[pl.BlockSpec((tm, tn), lambda i, j: (i, j)),
                   pl.BlockSpec((tm, LANE), lambda i, j: (i, 0))],
        out_shape=[jax.ShapeDtypeStruct((m, n), F32), jax.ShapeDtypeStruct((m, LANE), F32)],
        scratch_shapes=[pltpu.VMEM((tm, d), BF16)],
        compiler_params=_cparams(("parallel", "arbitrary")),
        name="in_proj",
    )(x, mod, mod, w, w_ba)


def _rms_heads(x, g, heads):
    outs = []
    for h in range(heads):
        xh = x[:, h * HEAD_DIM:(h + 1) * HEAD_DIM]
        outs.append(xh * lax.rsqrt(jnp.mean(xh * xh, axis=-1, keepdims=True) + EPS) * g)
    return outs


def _softmax_av(q, k, v, scale):
    s = _dot_nt(q, k)
    if scale is not None:
        s = s * scale
    m = jnp.max(s, axis=-1, keepdims=True)
    p = jnp.exp(s - m)
    l = jnp.sum(p, axis=-1, keepdims=True)
    return _dot(p.astype(BF16), v) / l


def _ctx_attn_kernel(q_ref, kv_ref, qg_ref, kg_ref, o_ref, kn_ref, v_ref):
    qs = _rms_heads(q_ref[...], qg_ref[...], ATTN_HEADS)
    kv = kv_ref[...]
    ks = _rms_heads(kv[:, :KV_WIDTH], kg_ref[...], ATTN_KV_HEADS)
    v = kv[:, KV_WIDTH:]
    v_ref[...] = v
    t = q_ref.shape[0]
    for kh in range(ATTN_KV_HEADS):
        kn_ref[:, kh * HEAD_DIM:(kh + 1) * HEAD_DIM] = ks[kh]
        q2 = jnp.concatenate([qs[kh * ATTN_GROUP + g] for g in range(ATTN_GROUP)], axis=0).astype(BF16)
        o = _softmax_av(q2, ks[kh].astype(BF16), v[:, kh * HEAD_DIM:(kh + 1) * HEAD_DIM].astype(BF16), ATTN_SCALE)
        for g in range(ATTN_GROUP):
            h = kh * ATTN_GROUP + g
            o_ref[:, h * HEAD_DIM:(h + 1) * HEAD_DIM] = o[g * t:(g + 1) * t].astype(BF16)


def ctx_attention(proj, qg, kg, batch, seq):
    return pl.pallas_call(
        _ctx_attn_kernel,
        grid=(batch,),
        in_specs=[pl.BlockSpec((seq, ATTN_WIDTH), lambda b: (b, COL_Q // ATTN_WIDTH)),
                  pl.BlockSpec((seq, 2 * KV_WIDTH), lambda b: (b, COL_KV // (2 * KV_WIDTH))),
                  pl.BlockSpec((1, HEAD_DIM), lambda b: (0, 0)),
                  pl.BlockSpec((1, HEAD_DIM), lambda b: (0, 0))],
        out_specs=[pl.BlockSpec((seq, ATTN_WIDTH), lambda b: (b, 0)),
                   pl.BlockSpec((seq, KV_WIDTH), lambda b: (b, 0)),
                   pl.BlockSpec((seq, KV_WIDTH), lambda b: (b, 0))],
        out_shape=[jax.ShapeDtypeStruct((batch * seq, ATTN_WIDTH), BF16),
                   jax.ShapeDtypeStruct((batch * seq, KV_WIDTH), F32),
                   jax.ShapeDtypeStruct((batch * seq, KV_WIDTH), F32)],
        compiler_params=_cparams(("parallel",)),
        name="ctx_attention",
    )(proj, proj, qg, kg)


def _rope(x, cos, sin, heads):
    w = x.shape[-1]
    lane = lax.broadcasted_iota(jnp.int32, x.shape, 1)
    quarter = HEAD_DIM // 4
    partner = jnp.where((lane % (2 * quarter)) < quarter,
                        pltpu.roll(x, w - quarter, 1), pltpu.roll(x, quarter, 1))
    cos_t = jnp.concatenate([cos] * heads, axis=1)
    sin_t = jnp.concatenate([sin] * heads, axis=1)
    return x * cos_t + partner * sin_t


def _lat_prep_kernel(q_ref, kv_ref, qg_ref, kg_ref, cos_ref, sin_ref, qo_ref, ko_ref, vo_ref):
    qn = jnp.concatenate(_rms_heads(q_ref[...], qg_ref[...], ATTN_HEADS), axis=1)
    kv = kv_ref[...]
    kn = jnp.concatenate(_rms_heads(kv[:, :KV_WIDTH], kg_ref[...], ATTN_KV_HEADS), axis=1)
    cos = cos_ref[...]
    sin = sin_ref[...]
    qo_ref[...] = (_rope(qn, cos, sin, ATTN_HEADS) * ATTN_SCALE).astype(BF16)
    ko_ref[...] = _rope(kn, cos, sin, ATTN_KV_HEADS).astype(BF16)
    vo_ref[...] = kv[:, KV_WIDTH:].astype(BF16)


def lat_prep(proj, qg, kg, cos, sin, seq, tm=512):
    m = proj.shape[0]
    per = seq // tm
    return pl.pallas_call(
        _lat_prep_kernel,
        grid=(m // tm,),
        in_specs=[pl.BlockSpec((tm, ATTN_WIDTH), lambda i: (i, COL_Q // ATTN_WIDTH)),
                  pl.BlockSpec((tm, 2 * KV_WIDTH), lambda i: (i, COL_KV // (2 * KV_WIDTH))),
                  pl.BlockSpec((1, HEAD_DIM), lambda i: (0, 0)),
                  pl.BlockSpec((1, HEAD_DIM), lambda i: (0, 0)),
                  pl.BlockSpec((tm, HEAD_DIM), lambda i: (i % per, 0)),
                  pl.BlockSpec((tm, HEAD_DIM), lambda i: (i % per, 0))],
        out_specs=[pl.BlockSpec((tm, ATTN_WIDTH), lambda i: (i, 0)),
                   pl.BlockSpec((tm, KV_WIDTH), lambda i: (i, 0)),
                   pl.BlockSpec((tm, KV_WIDTH), lambda i: (i, 0))],
        out_shape=[jax.ShapeDtypeStruct((m, ATTN_WIDTH), BF16),
                   jax.ShapeDtypeStruct((m, KV_WIDTH), BF16),
                   jax.ShapeDtypeStruct((m, KV_WIDTH), BF16)],
        compiler_params=_cparams(("parallel",)),
        name="lat_prep",
    )(proj, proj, qg, kg, cos, sin)


def _lat_attn_kernel(q_ref, k_ref, v_ref, o_ref):
    k = k_ref[0]
    v = v_ref[0]
    for g in range(ATTN_GROUP):
        sl = slice(g * HEAD_DIM, (g + 1) * HEAD_DIM)
        o_ref[:, sl] = _softmax_av(q_ref[:, sl], k, v, None).astype(BF16)


def lat_attention(q, k_all, v_all, seq, tq=256):
    b, s, _ = k_all.shape
    nq = seq // tq
    gw = ATTN_GROUP * HEAD_DIM
    return pl.pallas_call(
        _lat_attn_kernel,
        grid=(b, ATTN_KV_HEADS, nq),
        in_specs=[pl.BlockSpec((tq, gw), lambda bi, kh, qi: (bi * nq + qi, kh)),
                  pl.BlockSpec((1, s, HEAD_DIM), lambda bi, kh, qi: (bi, 0, kh)),
                  pl.BlockSpec((1, s, HEAD_DIM), lambda bi, kh, qi: (bi, 0, kh))],
        out_specs=pl.BlockSpec((tq, gw), lambda bi, kh, qi: (bi * nq + qi, kh)),
        out_shape=jax.ShapeDtypeStruct((b * seq, ATTN_WIDTH), BF16),
        compiler_params=_cparams(("parallel", "parallel", "arbitrary")),
        name="lat_attention",
    )(q, k_all, v_all)


def rope_tables(seq):
    t = jnp.arange(seq)
    row = (t // GRID_W).astype(F32)
    col = (t % GRID_W).astype(F32)
    quarter = HEAD_DIM // 4
    inv_freq = jnp.power(ROPE_THETA, -jnp.arange(quarter, dtype=F32) / quarter)
    ar = row[:, None] * inv_freq[None, :]
    ac = col[:, None] * inv_freq[None, :]
    cos = jnp.concatenate([jnp.cos(ar), jnp.cos(ar), jnp.cos(ac), jnp.cos(ac)], axis=1)
    sin = jnp.concatenate([-jnp.sin(ar), jnp.sin(ar), -jnp.sin(ac), jnp.sin(ac)], axis=1)
    return cos, sin


def _chunk_index(d, i, n):
    return i + d * (n - 1 - 2 * i)


def _lru_kernel(x_ref, xp_ref, xn_ref, cw_ref, cb_ref, wg_ref, bg_ref, sp_ref, h0_ref,
                h_ref, fin_ref, xpad_ref, a_ref, b_ref, hc_ref, *, nb, tt):
    d = pl.program_id(0)
    i = pl.program_id(2)
    n = pl.num_programs(2)
    ci = _chunk_index(d, i, n)
    pitch = tt + PITCH_PAD
    rows = nb * pitch
    nsl = LRU_WIDTH // LANE

    @pl.when(i == 0)
    def _():
        xpad_ref[...] = jnp.zeros_like(xpad_ref)
        hc_ref[...] = h0_ref[0]

    for s in range(nb):
        base = 8 + s * pitch
        xpad_ref[base:base + tt, :] = x_ref[s]
        xpad_ref[base - 1:base, :] = jnp.where(ci == 0, 0.0, xp_ref[s, 7:8, :])
        xpad_ref[base + tt:base + tt + 2, :] = jnp.where(ci == n - 1, 0.0, xn_ref[s, 0:2, :])
    w = cw_ref[...]
    xc = cb_ref[...] + sum(xpad_ref[7 + j:7 + j + rows, :] * w[j:j + 1, :] for j in range(CONV_W))
    pre = _dot(xc.astype(BF16), wg_ref[0]) + bg_ref[0]
    r = jax.nn.sigmoid(pre[:, :LRU_WIDTH])
    ig = jax.nn.sigmoid(pre[:, LRU_WIDTH:])
    a = jnp.exp((-LRU_C) * r * sp_ref[0])
    inp = jnp.sqrt(1.0 - a * a) * (ig * xc)
    for c in range(nsl):
        a_ref[c] = a[:, c * LANE:(c + 1) * LANE]
        b_ref[c] = inp[:, c * LANE:(c + 1) * LANE]

    def body(t, carry):
        row = t + d * (tt - 1 - 2 * t)
        out = []
        for c in range(nsl):
            idx = (c, pl.ds(row, nb, stride=pitch), slice(None))
            h = a_ref[idx] * carry[c] + b_ref[idx]
            b_ref[idx] = h
            out.append(h)
        return tuple(out)

    carry = tuple(hc_ref[:, c * LANE:(c + 1) * LANE] for c in range(nsl))
    carry = lax.fori_loop(0, tt, body, carry, unroll=8)
    for c in range(nsl):
        hc_ref[:, c * LANE:(c + 1) * LANE] = carry[c]
        for s in range(nb):
            h_ref[0, s, :, c * LANE:(c + 1) * LANE] = b_ref[c, s * pitch:s * pitch + tt, :]

    @pl.when(i == n - 1)
    def _():
        fin_ref[0] = hc_ref[...]


def lru_scan(proj3, h0, cw, cb, wg, bg, sp, nb, tt=128):
    b, t, _ = proj3.shape
    w = LRU_WIDTH
    n = t // tt
    t8 = tt // 8
    rows = nb * (tt + PITCH_PAD)
    col = COL_LX // w

    def cidx(d, i):
        return _chunk_index(d, i, n)

    return pl.pallas_call(
        functools.partial(_lru_kernel, nb=nb, tt=tt),
        grid=(2, b // nb, n),
        in_specs=[pl.BlockSpec((nb, tt, w), lambda d, g, i: (g, cidx(d, i), col)),
                  pl.BlockSpec((nb, 8, w), lambda d, g, i: (g, jnp.maximum(cidx(d, i) * t8 - 1, 0), col)),
                  pl.BlockSpec((nb, 8, w), lambda d, g, i: (g, jnp.minimum((cidx(d, i) + 1) * t8, t // 8 - 1), col)),
                  pl.BlockSpec((CONV_W, w), lambda d, g, i: (0, 0)),
                  pl.BlockSpec((1, w), lambda d, g, i: (0, 0)),
                  pl.BlockSpec((1, w, 2 * w), lambda d, g, i: (d, 0, 0)),
                  pl.BlockSpec((1, 1, 2 * w), lambda d, g, i: (d, 0, 0)),
                  pl.BlockSpec((1, 1, w), lambda d, g, i: (d, 0, 0)),
                  pl.BlockSpec((1, nb, w), lambda d, g, i: (d, g, 0))],
        out_specs=[pl.BlockSpec((1, nb, tt, w), lambda d, g, i: (d, g, cidx(d, i), 0)),
                   pl.BlockSpec((1, nb, w), lambda d, g, i: (d, g, 0))],
        out_shape=[jax.ShapeDtypeStruct((2, b, t, w), F32), jax.ShapeDtypeStruct((2, b, w), F32)],
        scratch_shapes=[pltpu.VMEM((rows + 16, w), F32),
                        pltpu.VMEM((w // LANE, rows, LANE), F32),
                        pltpu.VMEM((w // LANE, rows, LANE), F32),
                        pltpu.VMEM((nb, w), F32)],
        compiler_params=_cparams(("parallel", "parallel", "arbitrary")),
        name="lru_scan",
    )(proj3, proj3, proj3, cw, cb, wg, bg, sp, h0)


def _s5_kernel(u_ref, bb_ref, cc_ref, a_ref, h0_ref, y_ref, fin_ref, up_ref, s_ref, hc_ref, *, nb, tt):
    d = pl.program_id(0)
    i = pl.program_id(2)
    n = pl.num_programs(2)
    pitch = tt + PITCH_PAD
    nsl = S5_NSTATE // LANE

    @pl.when(i == 0)
    def _():
        up_ref[...] = jnp.zeros_like(up_ref)
        hc_ref[...] = h0_ref[0]

    for s in range(nb):
        up_ref[s * pitch:s * pitch + tt, :] = u_ref[s]
    u2 = up_ref[...].astype(BF16)
    per_k = MXU // S5_CH * S5_STATE // MXU
    for nt in range(2 * S5_NSTATE // MXU):
        kt = (nt % (S5_NSTATE // MXU)) // per_k
        tile = _dot(u2[:, kt * MXU:(kt + 1) * MXU], bb_ref[0, kt * MXU:(kt + 1) * MXU, nt * MXU:(nt + 1) * MXU])
        s_ref[2 * nt] = tile[:, :LANE]
        s_ref[2 * nt + 1] = tile[:, LANE:]

    a_all = a_ref[0]
    group = 8
    for c0 in range(0, nsl, group):
        cs = list(range(c0, c0 + group))
        ar = [jnp.broadcast_to(a_all[:, c * LANE:(c + 1) * LANE], (nb, LANE)) for c in cs]
        ai = [jnp.broadcast_to(a_all[:, S5_NSTATE + c * LANE:S5_NSTATE + (c + 1) * LANE], (nb, LANE)) for c in cs]

        def body(t, carry, cs=cs, ar=ar, ai=ai):
            row = t + d * (tt - 1 - 2 * t)
            out = []
            for j, c in enumerate(cs):
                hr, hi = carry[2 * j], carry[2 * j + 1]
                ire = (c, pl.ds(row, nb, stride=pitch), slice(None))
                iim = (nsl + c, pl.ds(row, nb, stride=pitch), slice(None))
                nr = ar[j] * hr - ai[j] * hi + s_ref[ire]
                ni = ar[j] * hi + ai[j] * hr + s_ref[iim]
                s_ref[ire] = nr
                s_ref[iim] = ni
                out += [nr, ni]
            return tuple(out)

        carry = []
        for c in cs:
            carry += [hc_ref[:, c * LANE:(c + 1) * LANE], hc_ref[:, S5_NSTATE + c * LANE:S5_NSTATE + (c + 1) * LANE]]
        carry = lax.fori_loop(0, tt, body, tuple(carry), unroll=4)
        for j, c in enumerate(cs):
            hc_ref[:, c * LANE:(c + 1) * LANE] = carry[2 * j]
            hc_ref[:, S5_NSTATE + c * LANE:S5_NSTATE + (c + 1) * LANE] = carry[2 * j + 1]

    n_out = S5_WIDTH // MXU
    per_n = nsl // n_out
    for nt in range(n_out):
        slabs = [per_n * nt + k for k in range(per_n)] + [nsl + per_n * nt + k for k in range(per_n)]
        lhs = jnp.concatenate([s_ref[c] for c in slabs], axis=1).astype(BF16)
        y = _dot(lhs, cc_ref[0, nt])
        for s in range(nb):
            y_ref[0, s, :, nt * MXU:(nt + 1) * MXU] = y[s * pitch:s * pitch + tt]

    @pl.when(i == n - 1)
    def _():
        fin_ref[0] = hc_ref[...]


def s5_scan(proj3, h0, bb, cc, a, nb, tt=128):
    b, t, _ = proj3.shape
    w = S5_WIDTH
    n = t // tt
    rows = nb * (tt + PITCH_PAD)
    ns2 = 2 * S5_NSTATE

    def cidx(d, i):
        return _chunk_index(d, i, n)

    return pl.pallas_call(
        functools.partial(_s5_kernel, nb=nb, tt=tt),
        grid=(2, b // nb, n),
        in_specs=[pl.BlockSpec((nb, tt, w), lambda d, g, i: (g, cidx(d, i), COL_SU // w)),
                  pl.BlockSpec((1, w, ns2), lambda d, g, i: (d, 0, 0)),
                  pl.BlockSpec((1,) + cc.shape[1:], lambda d, g, i: (d, 0, 0, 0)),
                  pl.BlockSpec((1, 1, ns2), lambda d, g, i: (d, 0, 0)),
                  pl.BlockSpec((1, nb, ns2), lambda d, g, i: (d, g, 0))],
        out_specs=[pl.BlockSpec((1, nb, tt, w), lambda d, g, i: (d, g, cidx(d, i), 0)),
                   pl.BlockSpec((1, nb, ns2), lambda d, g, i: (d, g, 0))],
        out_shape=[jax.ShapeDtypeStruct((2, b, t, w), F32), jax.ShapeDtypeStruct((2, b, ns2), F32)],
        scratch_shapes=[pltpu.VMEM((rows, w), F32),
                        pltpu.VMEM((ns2 // LANE, rows, LANE), F32),
                        pltpu.VMEM((nb, ns2), F32)],
        compiler_params=_cparams(("parallel", "parallel", "arbitrary")),
        name="s5_scan",
    )(proj3, bb, cc, a, h0)


def _split3(x):
    x1 = x.astype(BF16)
    r1 = x - x1.astype(F32)
    x2 = r1.astype(BF16)
    x3 = (r1 - x2.astype(F32)).astype(BF16)
    return x1, x2, x3


def _mm(x, y):
    return _dot(x.astype(BF16), y.astype(BF16))


def _unit_tri_inverses(mats, rr, cc):
    same16 = (rr // 16) == (cc // 16)
    same32 = (rr // 32) == (cc // 32)
    p = [jnp.where(same16, -a, 0.0) for a in mats]
    t = [jnp.where(rr == cc, 1.0, x) for x in p]
    for _ in range(3):
        p = [_mm(x, x) for x in p]
        t = [x + _mm(x, y) for x, y in zip(t, p)]
    for mask in (same32 & jnp.logical_not(same16), jnp.logical_not(same32)):
        te = [_mm(x, jnp.where(mask, a, 0.0)) for x, a in zip(t, mats)]
        t = [x - _mm(y, x) for x, y in zip(t, te)]
    return t


def _dn_prep_kernel(x_ref, xp_ref, xn_ref, cw_ref, o_ref, xpad_ref, *, tb):
    i = pl.program_id(1)
    n = pl.num_programs(1)
    hw = DN_HEADS * DN_DK
    xpad_ref[8:8 + tb, :] = x_ref[0]
    xpad_ref[0:8, :] = jnp.where(i == 0, 0.0, xp_ref[0])
    xpad_ref[8 + tb:16 + tb, :] = jnp.where(i == n - 1, 0.0, xn_ref[0])
    w = cw_ref[...]
    xc = sum(xpad_ref[7 + j:7 + j + tb, :] * w[j:j + 1, :] for j in range(CONV_W))
    qkv = xc * jax.nn.sigmoid(xc)
    for h in range(DN_HEADS):
        q = qkv[:, h * DN_DK:(h + 1) * DN_DK]
        k = qkv[:, hw + h * DN_DK:hw + (h + 1) * DN_DK]
        o_ref[0, :, h * DN_DK:(h + 1) * DN_DK] = (
            q * lax.rsqrt(jnp.sum(q * q, axis=-1, keepdims=True) + EPS) * (DN_DK ** -0.5))
        o_ref[0, :, hw + h * DN_DK:hw + (h + 1) * DN_DK] = (
            k * lax.rsqrt(jnp.sum(k * k, axis=-1, keepdims=True) + EPS))
    o_ref[0, :, 2 * hw:] = qkv[:, 2 * hw:]


def dn_prep(proj3, cw, tb=256):
    b, t, _ = proj3.shape
    n = t // tb
    t8 = tb // 8
    w3 = 3 * DN_WIDTH
    cq = COL_DQ // w3
    return pl.pallas_call(
        functools.partial(_dn_prep_kernel, tb=tb),
        grid=(b, n),
        in_specs=[pl.BlockSpec((1, tb, w3), lambda bi, i: (bi, i, cq)),
                  pl.BlockSpec((1, 8, w3), lambda bi, i: (bi, jnp.maximum(i * t8 - 1, 0), cq)),
                  pl.BlockSpec((1, 8, w3), lambda bi, i: (bi, jnp.minimum((i + 1) * t8, t // 8 - 1), cq)),
                  pl.BlockSpec((CONV_W, w3), lambda bi, i: (0, 0))],
        out_specs=pl.BlockSpec((1, tb, w3), lambda bi, i: (bi, i, 0)),
        out_shape=jax.ShapeDtypeStruct((b, t, w3), F32),
        scratch_shapes=[pltpu.VMEM((tb + 16, w3), F32)],
        compiler_params=_cparams(("parallel", "parallel")),
        name="dn_prep",
    )(proj3, proj3, proj3, cw)


def _dn_kernel(xf_ref, baf_ref, xb_ref, bab_ref, al_ref, dtb_ref, s0_ref, of_ref, ob_ref, fin_ref, s_ref,
               *, nbat, tb, shared):
    i = pl.program_id(1)
    n = pl.num_programs(1)
    ch = DN_CHUNK
    nc = tb // ch
    hw = DN_HEADS * DN_DK

    @pl.when(i == 0)
    def _():
        s_ref[...] = s0_ref[...]

    rb = lax.broadcasted_iota(jnp.int32, (tb, tb), 0)
    cb = lax.broadcasted_iota(jnp.int32, (tb, tb), 1)
    rr = lax.broadcasted_iota(jnp.int32, (ch, ch), 0)
    cc = lax.broadcasted_iota(jnp.int32, (ch, ch), 1)
    al = al_ref[...]
    dtb = dtb_ref[...]

    def qkv_heads(x_ref, bat):
        out = {}
        for c in range(nc):
            rows = slice(c * ch, (c + 1) * ch)
            for h in range(DN_HEADS):
                out[(c, h)] = (x_ref[bat, rows, h * DN_DK:(h + 1) * DN_DK],
                               x_ref[bat, rows, hw + h * DN_DK:hw + (h + 1) * DN_DK],
                               x_ref[bat, rows, 2 * hw + h * DN_DV:2 * hw + (h + 1) * DN_DV])
        return out

    units = []
    for d in range(2):
        reverse = d == 1
        x_ref, ba_ref = (xb_ref, bab_ref) if reverse else (xf_ref, baf_ref)
        tri = ((rb // ch) == (cb // ch)) & ((cb >= rb) if reverse else (cb <= rb))
        tri = jnp.where(tri, 1.0, 0.0).astype(BF16)
        incl = (cc >= rr) if reverse else (cc <= rr)
        strict = (cc > rr) if reverse else (cc < rr)
        if d == 0 or not shared:
            heads = [qkv_heads(x_ref, bat) for bat in range(nbat)]
        for bat in range(nbat):
            ba = ba_ref[bat]
            beta_all = jax.nn.sigmoid(ba)
            g_all = -jnp.exp(al) * _softplus(ba + dtb)
            gc_all = sum(_dot(tri, piece) for piece in _split3(g_all))
            for c in range(nc):
                r0 = c * ch
                gct = gc_all[r0:r0 + ch, :].T
                for h in range(DN_HEADS):
                    lane = d * DN_HEADS + h
                    beta = beta_all[r0:r0 + ch, lane:lane + 1]
                    gcol = gc_all[r0:r0 + ch, 2 * DN_HEADS + lane:2 * DN_HEADS + lane + 1]
                    grow = gct[2 * DN_HEADS + lane:2 * DN_HEADS + lane + 1, :]
                    q, k, v = heads[bat][(c, h)]
                    g_last = gcol[0:1, :] if reverse else gcol[ch - 1:ch, :]
                    units.append(dict(d=d, bat=bat, c=c, h=h, q=q, k=k, v=v, beta=beta, gcol=gcol, kb=k * beta,
                                      decay=jnp.where(incl, jnp.exp(gcol - grow), 0.0), strict=strict,
                                      g_last=g_last))

    for u in units:
        kbf = u['k'].astype(BF16)
        u['a'] = jnp.where(u['strict'], _dot_nt(u['kb'].astype(BF16), kbf) * u['decay'], 0.0)
        u['qk'] = (_dot_nt(u['q'].astype(BF16), kbf) * u['decay']).astype(BF16)
    t_inv = _unit_tri_inverses([u['a'] for u in units], rr, cc)
    for u, t in zip(units, t_inv):
        u['rhs'] = jnp.concatenate([u['v'] * u['beta'], u['kb'] * jnp.exp(u['gcol'])], axis=1)
        u['t_off'] = jnp.where(rr == cc, 0.0, t)
    for u in units:
        sol = u['rhs'] + _mm(u['t_off'], u['rhs'])
        u['u_val'] = sol[:, :DN_DV]
        u['w'] = sol[:, DN_DV:].astype(BF16)
        u['qe'] = (u['q'] * jnp.exp(u['gcol'])).astype(BF16)
        u['kdec_t'] = (u['k'] * jnp.exp(u['g_last'] - u['gcol'])).T.astype(BF16)
        u['eg'] = jnp.exp(u['g_last'])

    state = {(d, bat, h): s_ref[d, bat, h] for d in range(2) for bat in range(nbat) for h in range(DN_HEADS)}
    for j in range(nc):
        cur = [u for u in units if u['c'] == (nc - 1 - j if u['d'] == 1 else j)]
        keys = [(u['d'], u['bat'], u['h']) for u in cur]
        sb = [state[key].astype(BF16) for key in keys]
        ws = [_dot(u['w'], s) for u, s in zip(cur, sb)]
        qs = [_dot(u['qe'], s) for u, s in zip(cur, sb)]
        vb = [(u['u_val'] - x).astype(BF16) for u, x in zip(cur, ws)]
        os = [x + _dot(u['qk'], y) for u, x, y in zip(cur, qs, vb)]
        sn = [state[key] * u['eg'] + _dot(u['kdec_t'], y) for u, key, y in zip(cur, keys, vb)]
        for u, key, o, s in zip(cur, keys, os, sn):
            state[key] = s
            o_ref = ob_ref if u['d'] == 1 else of_ref
            o_ref[u['bat'], u['c'] * ch:(u['c'] + 1) * ch, u['h'] * DN_DV:(u['h'] + 1) * DN_DV] = o
    for (d, bat, h), s in state.items():
        s_ref[d, bat, h] = s

    @pl.when(i == n - 1)
    def _():
        fin_ref[...] = s_ref[...]


def dn_scan(qkv3, ba3, s0, al, dtb, nbat, tb):
    b, t, w3 = qkv3.shape
    n = t // tb

    def data_specs(cidx):
        return [pl.BlockSpec((nbat, tb, w3), lambda g, i: (g, cidx(i), 0)),
                pl.BlockSpec((nbat, tb, LANE), lambda g, i: (g, cidx(i), 0))]

    fwd = lambda i: i
    bwd = lambda i: n - 1 - i
    st_spec = pl.BlockSpec((2, nbat, DN_HEADS, DN_DK, DN_DV), lambda g, i: (0, g, 0, 0, 0))
    return pl.pallas_call(
        functools.partial(_dn_kernel, nbat=nbat, tb=tb, shared=(n == 1)),
        grid=(b // nbat, n),
        in_specs=data_specs(fwd) + data_specs(bwd) + [
            pl.BlockSpec((1, LANE), lambda g, i: (0, 0)),
            pl.BlockSpec((1, LANE), lambda g, i: (0, 0)),
            st_spec],
        out_specs=[pl.BlockSpec((nbat, tb, DN_WIDTH), lambda g, i: (g, i, 0)),
                   pl.BlockSpec((nbat, tb, DN_WIDTH), lambda g, i: (g, n - 1 - i, 0)),
                   st_spec],
        out_shape=[jax.ShapeDtypeStruct((b, t, DN_WIDTH), F32),
                   jax.ShapeDtypeStruct((b, t, DN_WIDTH), F32),
                   jax.ShapeDtypeStruct((2, b, DN_HEADS, DN_DK, DN_DV), F32)],
        scratch_shapes=[pltpu.VMEM((2, nbat, DN_HEADS, DN_DK, DN_DV), F32)],
        compiler_params=_cparams(("parallel", "arbitrary")),
        name="dn_scan",
    )(qkv3, ba3, qkv3, ba3, al, dtb, s0)


def _post_kernel(hf_ref, hb_ref, lg_ref, of_ref, ob_ref, dz_ref, ng_ref, yf_ref, yb_ref, su_ref, sd_ref,
                 gw_ref, gb_ref, lru_ref, dn_ref, s5_ref):
    lru_ref[...] = ((hf_ref[0, 0] + hb_ref[0, 0]) * _gelu(lg_ref[...])).astype(BF16)
    o = of_ref[0] + ob_ref[0]
    dz = dz_ref[...]
    for h in range(DN_HEADS):
        sl = slice(h * DN_DV, (h + 1) * DN_DV)
        oh = o[:, sl]
        oh = oh * lax.rsqrt(jnp.mean(oh * oh, axis=-1, keepdims=True) + EPS) * ng_ref[...]
        zh = dz[:, sl]
        dn_ref[:, sl] = (oh * (zh * jax.nn.sigmoid(zh))).astype(BF16)
    y = yf_ref[0, 0] + yb_ref[0, 0] + sd_ref[...] * su_ref[...]
    gy = _gelu(y)
    s5_ref[...] = (gy * jax.nn.sigmoid(_dot(gy.astype(BF16), gw_ref[...]) + gb_ref[...])).astype(BF16)


def mix_post(proj, h_lru, o_f, o_b, y_s5, ng, sd, gw, gb):
    m = proj.shape[0]
    _, b, t, w = h_lru.shape
    tm = min(512, t)
    per = t // tm
    tok = pl.BlockSpec((tm, w), lambda i: (i, 0))

    def pcol(c0):
        return pl.BlockSpec((tm, w), lambda i: (i, c0 // w))

    def dirspec(d):
        return pl.BlockSpec((1, 1, tm, w), lambda i: (d, i // per, i % per, 0))

    seq = pl.BlockSpec((1, tm, w), lambda i: (i // per, i % per, 0))

    def vec(n):
        return pl.BlockSpec((1, n), lambda i: (0, 0))

    out = jax.ShapeDtypeStruct((m, w), BF16)
    return pl.pallas_call(
        _post_kernel,
        grid=(m // tm,),
        in_specs=[dirspec(0), dirspec(1), pcol(COL_LG), seq, seq, pcol(COL_DZ), vec(DN_DV),
                  dirspec(0), dirspec(1), pcol(COL_SU), vec(w), pl.BlockSpec((w, w), lambda i: (0, 0)), vec(w)],
        out_specs=[tok, tok, tok],
        out_shape=[out, out, out],
        compiler_params=_cparams(("parallel",)),
        name="mix_post",
    )(h_lru, h_lru, proj, o_f, o_b, proj, ng, y_s5, y_s5, proj, sd, gw, gb)


def _out_kernel(x_ref, a_ref, b_ref, c_ref, d_ref, w_ref, gate_ref, g_ref, bb_ref, sc_ref, sh_ref,
                x1_ref, u2_ref):
    q = MIXW
    for r0 in range(0, x_ref.shape[0], OUT_SUB):
        rows = slice(r0, r0 + OUT_SUB)
        y = _dot(a_ref[rows, :], w_ref[0:q, :])
        y += _dot(b_ref[rows, :], w_ref[q:2 * q, :])
        y += _dot(c_ref[rows, :], w_ref[2 * q:3 * q, :])
        y += _dot(d_ref[rows, :], w_ref[3 * q:, :])
        z = DEEPNORM_ALPHA * x_ref[rows, :] + gate_ref[0] * y
        x1 = _ln(z) * g_ref[...] + bb_ref[...]
        x1_ref[rows, :] = x1
        u2_ref[rows, :] = (_ln(x1) * (1.0 + sc_ref[0]) + sh_ref[0]).astype(BF16)


def out_proj(x, parts, w, layer, mod, ln_g, ln_b, mod_row, tm=512):
    m, d = x.shape

    def mod_spec(kind):
        return pl.BlockSpec((1, 1, d), lambda i: (kind * MOD_ROWS + mod_row(i * tm), 0, 0))

    vec = pl.BlockSpec((1, d), lambda i: (0, 0))
    part = pl.BlockSpec((tm, MIXW), lambda i: (i, 0))
    return pl.pallas_call(
        _out_kernel,
        grid=(m // tm,),
        in_specs=[pl.BlockSpec((tm, d), lambda i: (i, 0)), part, part, part, part,
                  pl.BlockSpec((None, d, d), lambda i: (layer, 0, 0)),
                  mod_spec(2), vec, vec, mod_spec(4), mod_spec(3)],
        out_specs=[pl.BlockSpec((tm, d), lambda i: (i, 0)), pl.BlockSpec((tm, d), lambda i: (i, 0))],
        out_shape=[jax.ShapeDtypeStruct((m, d), F32), jax.ShapeDtypeStruct((m, d), BF16)],
        compiler_params=_cparams(("parallel",)),
        name="out_proj",
    )(x, *parts, w, mod, ln_g, ln_b, mod, mod)


def _mlp_kernel(u_ref, x_ref, w1_ref, w2_ref, gate_ref, g_ref, b_ref, o_ref, acc_ref):
    f = pl.program_id(1)

    @pl.when(f == 0)
    def _():
        acc_ref[...] = jnp.zeros_like(acc_ref)

    h = _dot(u_ref[...], w1_ref[...])
    h = jnp.square(jnp.maximum(h, 0.0)).astype(BF16)
    acc_ref[...] += _dot(h, w2_ref[...])

    @pl.when(f == pl.num_programs(1) - 1)
    def _():
        z = DEEPNORM_ALPHA * x_ref[...] + gate_ref[0] * acc_ref[...]
        o_ref[...] = _ln(z) * g_ref[...] + b_ref[...]


def mlp(u, x, w1, w2, layer, mod, ln_g, ln_b, mod_row, tm=512, tf=1024):
    m, d = x.shape
    ff = w1.shape[2]
    vec = pl.BlockSpec((1, d), lambda i, f: (0, 0))
    return pl.pallas_call(
        _mlp_kernel,
        grid=(m // tm, ff // tf),
        in_specs=[pl.BlockSpec((tm, d), lambda i, f: (i, 0)),
                  pl.BlockSpec((tm, d), lambda i, f: (i, 0)),
                  pl.BlockSpec((None, d, tf), lambda i, f: (layer, 0, f)),
                  pl.BlockSpec((None, tf, d), lambda i, f: (layer, f, 0)),
                  pl.BlockSpec((1, 1, d), lambda i, f: (5 * MOD_ROWS + mod_row(i * tm), 0, 0)),
                  vec, vec],
        out_specs=pl.BlockSpec((tm, d), lambda i, f: (i, 0)),
        out_shape=jax.ShapeDtypeStruct((m, d), F32),
        scratch_shapes=[pltpu.VMEM((tm, d), F32)],
        compiler_params=_cparams(("parallel", "arbitrary")),
        name="mlp",
    )(u, x, w1, w2, mod, ln_g, ln_b)


def _reorder_w_in(w):
    def cols(c0, n):
        return w[..., c0:c0 + n]

    aq, akv, lx, lg = cols(0, 512), cols(512, 512), cols(1024, 512), cols(1536, 512)
    dqkv, dz, ba, su = cols(2048, 1536), cols(3584, 512), cols(4096, N_BA), cols(4096 + N_BA, 512)
    pad = jnp.zeros(w.shape[:2] + (LANE - N_BA,), w.dtype)
    main = jnp.concatenate([dqkv, dz, aq, akv, lx, lg, su], axis=-1).astype(BF16)
    return main, jnp.concatenate([ba, pad], axis=-1).astype(BF16)


def _block_diag(blocks):
    n, r, c = blocks.shape[-3:]
    eye = jnp.eye(n, dtype=blocks.dtype)
    out = blocks[..., :, :, None, :] * eye[:, None, :, None]
    return out.reshape(blocks.shape[:-3] + (n * r, n * c))


def _lru_params(wa, ba, wx, bx, lam):
    wg = jnp.concatenate([_block_diag(wa.astype(BF16)), _block_diag(wx.astype(BF16))], axis=-1)
    bg = jnp.concatenate([ba, bx], axis=-1)[..., None, :]
    sp = jax.nn.softplus(-lam)[..., None, :]
    return wg, bg, sp


def _s5_params(lam_re, lam_im, log_dt, b_re, b_im, c_re, c_im):
    lead = lam_re.shape[:-2]
    dt = jnp.exp(log_dt)[..., None]
    mag = jnp.exp(lam_re * dt)
    abar_re = mag * jnp.cos(lam_im * dt)
    abar_im = mag * jnp.sin(lam_im * dt)
    den = lam_re * lam_re + lam_im * lam_im
    nr = abar_re - 1.0
    ni = abar_im
    f_re = (nr * lam_re + ni * lam_im) / den
    f_im = (ni * lam_re - nr * lam_im) / den
    bb_re = f_re[..., None] * b_re - f_im[..., None] * b_im
    bb_im = f_re[..., None] * b_im + f_im[..., None] * b_re
    to_in = lambda m: _block_diag(jnp.swapaxes(m, -1, -2).astype(BF16))
    bb = jnp.concatenate([to_in(bb_re), to_in(bb_im)], axis=-1)
    n_out = S5_WIDTH // MXU
    gpt = S5_GROUPS // n_out

    def to_out(m, nt):
        return _block_diag(jnp.swapaxes(m[..., nt * gpt:(nt + 1) * gpt, :, :], -1, -2).astype(BF16))

    cc = jnp.stack([jnp.concatenate([to_out(c_re, nt), to_out(-c_im, nt)], axis=-2)
                    for nt in range(n_out)], axis=-3)
    a = jnp.concatenate([abar_re.reshape(lead + (1, -1)), abar_im.reshape(lead + (1, -1))], axis=-1)
    return bb, cc, a


def _lane_row(vals, offset):
    return jnp.zeros((1, LANE), F32).at[0, offset:offset + vals.size].set(vals.reshape(-1))


def kernel(x_prompt, x_sample, cache_attn_k, cache_attn_v, state_rglru, state_delta, state_s5_re, state_s5_im, c, c_ctx, w_ada, b_ada, w_in, w_out, ln1_g, ln1_b, ln2_g, ln2_b, w_mlp1, w_mlp2, q_norm_g, k_norm_g, lru_conv_w, lru_conv_b, lru_wa, lru_ba, lru_wx, lru_bx, lru_lambda, dn_conv_w, dn_a_log, dn_dt_bias, dn_norm_g, s5_lambda_re, s5_lambda_im, s5_log_dt, s5_b_re, s5_b_im, s5_c_re, s5_c_im, s5_d, s5_glu_w, s5_glu_b):
    bp, tp, d = x_prompt.shape
    bs, ts, _ = x_sample.shape
    ctx_row = bs

    cond = jnp.concatenate([c, c_ctx[None, :], jnp.zeros((MOD_ROWS - bs - 1, d), F32)], axis=0)
    mods = ada_mod(cond, w_ada, b_ada)
    w_in_r, w_ba = _reorder_w_in(w_in)
    w_out_b = w_out.astype(BF16)
    w1_b = w_mlp1.astype(BF16)
    w2_b = w_mlp2.astype(BF16)
    cos, sin = rope_tables(ts)
    wg_all, bg_all, sp_all = _lru_params(lru_wa, lru_ba, lru_wx, lru_bx, lru_lambda)
    bb_all, cc_all, a5_all = _s5_params(s5_lambda_re, s5_lambda_im, s5_log_dt, s5_b_re, s5_b_im, s5_c_re, s5_c_im)

    streams = {
        'ctx': dict(x=x_prompt.reshape(bp * tp, d), b=bp, t=tp, nb=8, s5_tt=128, dn_nbat=2, dn_tb=tp,
                    mod_row=lambda tok: ctx_row),
        'lat': dict(x=x_sample.reshape(bs * ts, d), b=bs, t=ts, nb=bs, s5_tt=256, dn_nbat=bs, dn_tb=2 * DN_CHUNK,
                    mod_row=lambda tok: tok // ts),
    }
    ks, vs, lrus, dns, s5rs, s5is = [], [], [], [], [], []
    for l in range(DEPTH):
        mod = mods[l].reshape(MOD_ROWS, N_MOD, d).transpose(1, 0, 2).reshape(N_MOD * MOD_ROWS, 1, d)
        qg = q_norm_g[l].reshape(1, HEAD_DIM)
        kg = k_norm_g[l].reshape(1, HEAD_DIM)
        wg, bg, sp = wg_all[l], bg_all[l], sp_all[l]
        bb, cc, a5 = bb_all[l], cc_all[l], a5_all[l]
        al = _lane_row(dn_a_log[l], 2 * DN_HEADS)
        dtb = _lane_row(dn_dt_bias[l], 2 * DN_HEADS)
        for name, st in streams.items():
            b, t, nb = st['b'], st['t'], st['nb']
            is_ctx = name == 'ctx'
            proj, ba = in_proj(st['x'], mod, w_in_r, w_ba, l, st['mod_row'])
            proj3 = proj.reshape(b, t, N_PROJ)
            ba3 = ba.reshape(b, t, LANE)

            if is_ctx:
                attn, kn, vv = ctx_attention(proj, qg, kg, b, t)
                ks.append(kn.reshape(b, t, ATTN_KV_HEADS, HEAD_DIM))
                vs.append(vv.reshape(b, t, ATTN_KV_HEADS, HEAD_DIM))
                h0_lru = jnp.zeros((2, b, LRU_WIDTH), F32)
                s0_dn = jnp.zeros((2, b, DN_HEADS, DN_DK, DN_DV), F32)
                h0_s5 = jnp.zeros((2, b, 2 * S5_NSTATE), F32)
            else:
                q_s, k_s, v_s = lat_prep(proj, qg, kg, cos, sin, t)
                k_all = jnp.concatenate([cache_attn_k[:, l].reshape(b, -1, KV_WIDTH).astype(BF16),
                                         k_s.reshape(b, t, KV_WIDTH)], axis=1)
                v_all = jnp.concatenate([cache_attn_v[:, l].reshape(b, -1, KV_WIDTH).astype(BF16),
                                         v_s.reshape(b, t, KV_WIDTH)], axis=1)
                attn = lat_attention(q_s, k_all, v_all, t)
                h0_lru = jnp.swapaxes(state_rglru[:, l], 0, 1)
                s0_dn = jnp.swapaxes(state_delta[:, l], 0, 1)
                h0_s5 = jnp.swapaxes(jnp.concatenate([state_s5_re[:, l].reshape(b, 2, S5_NSTATE),
                                                      state_s5_im[:, l].reshape(b, 2, S5_NSTATE)], axis=-1), 0, 1)

            h_lru, lru_fin = lru_scan(proj3, h0_lru, lru_conv_w[l], lru_conv_b[l].reshape(1, -1), wg, bg, sp, nb)
            o_f, o_b, dn_fin = dn_scan(dn_prep(proj3, dn_conv_w[l]), ba3, s0_dn, al, dtb,
                                       st['dn_nbat'], st['dn_tb'])
            y_s5, s5_fin = s5_scan(proj3, h0_s5, bb, cc, a5, nb, st['s5_tt'])
            lru_out, dn_out, s5_out = mix_post(
                proj, h_lru, o_f, o_b, y_s5, dn_norm_g[l].reshape(1, DN_DV), s5_d[l].reshape(1, MIXW),
                s5_glu_w[l].astype(BF16), s5_glu_b[l].reshape(1, MIXW))
            if is_ctx:
                lrus.append(jnp.swapaxes(lru_fin, 0, 1))
                dns.append(jnp.swapaxes(dn_fin, 0, 1))
                s5_fin = jnp.swapaxes(s5_fin, 0, 1)
                s5rs.append(s5_fin[..., :S5_NSTATE].reshape(b, 2, S5_GROUPS, S5_STATE))
                s5is.append(s5_fin[..., S5_NSTATE:].reshape(b, 2, S5_GROUPS, S5_STATE))

            x1, u2 = out_proj(st['x'], [attn, lru_out, dn_out, s5_out], w_out_b, l, mod,
                              ln1_g[l].reshape(1, d), ln1_b[l].reshape(1, d), st['mod_row'])
            st['x'] = mlp(u2, x1, w1_b, w2_b, l, mod, ln2_g[l].reshape(1, d), ln2_b[l].reshape(1, d),
                          st['mod_row'])

    y_prompt = streams['ctx']['x'].reshape(bp, tp, d)
    y_sample = streams['lat']['x'].reshape(bs, ts, d)
    return (y_prompt, y_sample, jnp.stack(ks, axis=1), jnp.stack(vs, axis=1), jnp.stack(lrus, axis=1),
            jnp.stack(dns, axis=1), jnp.stack(s5rs, axis=1), jnp.stack(s5is, axis=1))
```

```python
import functools

import jax
import jax.numpy as jnp
from jax import lax
from jax.experimental import pallas as pl
from jax.experimental.pallas import tpu as pltpu

F32 = jnp.float32
BF16 = jnp.bfloat16

D_MODEL = 2048
DEPTH = 2
GRID_W = 64
CONV_W = 4
EPS = 1e-6
ROPE_THETA = 10000.0
N_MOD = 6
HEAD_DIM = 128
ATTN_WIDTH = D_MODEL // 4
ATTN_HEADS = ATTN_WIDTH // HEAD_DIM
ATTN_KV_HEADS = ATTN_HEADS // 2
ATTN_GROUP = ATTN_HEADS // ATTN_KV_HEADS
KV_WIDTH = ATTN_KV_HEADS * HEAD_DIM
ATTN_SCALE = HEAD_DIM ** -0.5
LRU_WIDTH = D_MODEL // 4
LRU_BLOCKS = 8
LRU_C = 8.0
DN_DK = 128
DN_DV = 128
DN_WIDTH = D_MODEL // 4
DN_HEADS = DN_WIDTH // DN_DV
DN_CHUNK = 64
S5_WIDTH = D_MODEL // 4
S5_CH = 16
S5_GROUPS = S5_WIDTH // S5_CH
S5_STATE = 64
S5_NSTATE = S5_GROUPS * S5_STATE
DEEPNORM_ALPHA = (2 * DEPTH) ** 0.25
MIXW = D_MODEL // 4

COL_DQ = 0
COL_DZ = 1536
COL_Q = 2048
COL_KV = 2560
COL_LX = 3072
COL_LG = 3584
COL_SU = 4096
N_PROJ = 4608
N_BA = 2 * 2 * DN_HEADS
LANE = 128
SUBLANES = 8
MXU = 256

MOD_ROWS = 8
PITCH_PAD = 4
OUT_SUB = 256
IN_SUB = 256
VMEM_LIMIT = 56 * 1024 * 1024


def _cparams(sem):
    return pltpu.CompilerParams(dimension_semantics=sem, vmem_limit_bytes=VMEM_LIMIT)


def _ln(x):
    mu = jnp.mean(x, axis=-1, keepdims=True)
    xc = x - mu
    var = jnp.mean(xc * xc, axis=-1, keepdims=True)
    return xc * lax.rsqrt(var + EPS)


def _softplus(x):
    return jnp.maximum(x, 0.0) + jnp.log1p(jnp.exp(-jnp.abs(x)))


def _gelu(x):
    return 0.5 * x * (1.0 + jnp.tanh(0.7978845608028654 * (x + 0.044715 * (x * x * x))))


def _dot(a, b):
    return jnp.dot(a, b, preferred_element_type=F32)


def _dot_nt(a, b):
    return lax.dot_general(a, b, (((1,), (1,)), ((), ())), preferred_element_type=F32)


def _ada_kernel(c_ref, w_ref, b_ref, o_ref):
    cs = c_ref[...]
    s = cs * jax.nn.sigmoid(cs)
    o_ref[0] = _dot(s.astype(BF16), w_ref[0].astype(BF16)) + b_ref[0]


def ada_mod(cond, w_ada, b_ada, tn=1024):
    depth, d, n = w_ada.shape
    return pl.pallas_call(
        _ada_kernel,
        grid=(depth, n // tn),
        in_specs=[pl.BlockSpec((MOD_ROWS, d), lambda l, j: (0, 0)),
                  pl.BlockSpec((1, d, tn), lambda l, j: (l, 0, j)),
                  pl.BlockSpec((1, 1, tn), lambda l, j: (l, 0, j))],
        out_specs=pl.BlockSpec((1, MOD_ROWS, tn), lambda l, j: (l, 0, j)),
        out_shape=jax.ShapeDtypeStruct((depth, MOD_ROWS, n), F32),
        compiler_params=_cparams(("parallel", "parallel")),
        name="ada_mod",
    )(cond, w_ada, b_ada.reshape(depth, 1, n))


def _in_kernel(x_ref, sc_ref, sh_ref, w_ref, wba_ref, o_ref, ba_ref, u_ref):
    j = pl.program_id(1)

    @pl.when(j == 0)
    def _():
        for r0 in range(0, x_ref.shape[0], IN_SUB):
            rows = slice(r0, r0 + IN_SUB)
            u = (_ln(x_ref[rows, :]) * (1.0 + sc_ref[0]) + sh_ref[0]).astype(BF16)
            u_ref[rows, :] = u
            o_ref[rows, :] = _dot(u, w_ref[...])
            ba_ref[rows, :] = _dot(u, wba_ref[...])

    @pl.when(j > 0)
    def _():
        o_ref[...] = _dot(u_ref[...], w_ref[...])


def in_proj(x, mod, w, w_ba, layer, mod_row, tm=1024, tn=768):
    m, d = x.shape
    n = w.shape[2]

    def mod_spec(kind):
        return pl.BlockSpec((1, 1, d), lambda i, j: (kind * MOD_ROWS + mod_row(i * tm), 0, 0))

    return pl.pallas_call(
        _in_kernel,
        grid=(m // tm, n // tn),
        in_specs=[pl.BlockSpec((tm, d), lambda i, j: (i, 0)),
                  mod_spec(1), mod_spec(0),
                  pl.BlockSpec((None, d, tn), lambda i, j: (layer, 0, j)),
                  pl.BlockSpec((None, d, LANE), lambda i, j: (layer, 0, 0))],
        out_specs=[pl.BlockSpec((tm, tn), lambda i, j: (i, j)),
                   pl.BlockSpec((tm, LANE), lambda i, j: (i, 0))],
        out_shape=[jax.ShapeDtypeStruct((m, n), F32), jax.ShapeDtypeStruct((m, LANE), F32)],
        scratch_shapes=[pltpu.VMEM((tm, d), BF16)],
        compiler_params=_cparams(("parallel", "arbitrary")),
        name="in_proj",
    )(x, mod, mod, w, w_ba)


def _rms_heads(x, g, heads):
    outs = []
    for h in range(heads):
        xh = x[:, h * HEAD_DIM:(h + 1) * HEAD_DIM]
        outs.append(xh * lax.rsqrt(jnp.mean(xh * xh, axis=-1, keepdims=True) + EPS) * g)
    return outs


def _softmax_av(q, k, v, scale):
    s = _dot_nt(q, k)
    if scale is not None:
        s = s * scale
    m = jnp.max(s, axis=-1, keepdims=True)
    p = jnp.exp(s - m)
    l = jnp.sum(p, axis=-1, keepdims=True)
    return _dot(p.astype(BF16), v) / l


def _ctx_attn_kernel(q_ref, kv_ref, qg_ref, kg_ref, o_ref, kn_ref, v_ref):
    qs = _rms_heads(q_ref[...], qg_ref[...], ATTN_HEADS)
    kv = kv_ref[...]
    ks = _rms_heads(kv[:, :KV_WIDTH], kg_ref[...], ATTN_KV_HEADS)
    v = kv[:, KV_WIDTH:]
    v_ref[...] = v
    t = q_ref.shape[0]
    for kh in range(ATTN_KV_HEADS):
        kn_ref[:, kh * HEAD_DIM:(kh + 1) * HEAD_DIM] = ks[kh]
        q2 = jnp.concatenate([qs[kh * ATTN_GROUP + g] for g in range(ATTN_GROUP)], axis=0).astype(BF16)
        o = _softmax_av(q2, ks[kh].astype(BF16), v[:, kh * HEAD_DIM:(kh + 1) * HEAD_DIM].astype(BF16), ATTN_SCALE)
        for g in range(ATTN_GROUP):
            h = kh * ATTN_GROUP + g
            o_ref[:, h * HEAD_DIM:(h + 1) * HEAD_DIM] = o[g * t:(g + 1) * t].astype(BF16)


def ctx_attention(proj, qg, kg, batch, seq):
    return pl.pallas_call(
        _ctx_attn_kernel,
        grid=(batch,),
        in_specs=[pl.BlockSpec((seq, ATTN_WIDTH), lambda b: (b, COL_Q // ATTN_WIDTH)),
                  pl.BlockSpec((seq, 2 * KV_WIDTH), lambda b: (b, COL_KV // (2 * KV_WIDTH))),
                  pl.BlockSpec((1, HEAD_DIM), lambda b: (0, 0)),
                  pl.BlockSpec((1, HEAD_DIM), lambda b: (0, 0))],
        out_specs=[pl.BlockSpec((seq, ATTN_WIDTH), lambda b: (b, 0)),
                   pl.BlockSpec((seq, KV_WIDTH), lambda b: (b, 0)),
                   pl.BlockSpec((seq, KV_WIDTH), lambda b: (b, 0))],
        out_shape=[jax.ShapeDtypeStruct((batch * seq, ATTN_WIDTH), BF16),
                   jax.ShapeDtypeStruct((batch * seq, KV_WIDTH), F32),
                   jax.ShapeDtypeStruct((batch * seq, KV_WIDTH), F32)],
        compiler_params=_cparams(("parallel",)),
        name="ctx_attention",
    )(proj, proj, qg, kg)


def _rope(x, cos, sin, heads):
    w = x.shape[-1]
    lane = lax.broadcasted_iota(jnp.int32, x.shape, 1)
    quarter = HEAD_DIM // 4
    partner = jnp.where((lane % (2 * quarter)) < quarter,
                        pltpu.roll(x, w - quarter, 1), pltpu.roll(x, quarter, 1))
    cos_t = jnp.concatenate([cos] * heads, axis=1)
    sin_t = jnp.concatenate([sin] * heads, axis=1)
    return x * cos_t + partner * sin_t


def _lat_prep_kernel(q_ref, kv_ref, qg_ref, kg_ref, cos_ref, sin_ref, qo_ref, ko_ref, vo_ref):
    qn = jnp.concatenate(_rms_heads(q_ref[...], qg_ref[...], ATTN_HEADS), axis=1)
    kv = kv_ref[...]
    kn = jnp.concatenate(_rms_heads(kv[:, :KV_WIDTH], kg_ref[...], ATTN_KV_HEADS), axis=1)
    cos = cos_ref[...]
    sin = sin_ref[...]
    qo_ref[...] = (_rope(qn, cos, sin, ATTN_HEADS) * ATTN_SCALE).astype(BF16)
    ko_ref[...] = _rope(kn, cos, sin, ATTN_KV_HEADS).astype(BF16)
    vo_ref[...] = kv[:, KV_WIDTH:].astype(BF16)


def lat_prep(proj, qg, kg, cos, sin, seq, tm=512):
    m = proj.shape[0]
    per = seq // tm
    return pl.pallas_call(
        _lat_prep_kernel,
        grid=(m // tm,),
        in_specs=[pl.BlockSpec((tm, ATTN_WIDTH), lambda i: (i, COL_Q // ATTN_WIDTH)),
                  pl.BlockSpec((tm, 2 * KV_WIDTH), lambda i: (i, COL_KV // (2 * KV_WIDTH))),
                  pl.BlockSpec((1, HEAD_DIM), lambda i: (0, 0)),
                  pl.BlockSpec((1, HEAD_DIM), lambda i: (0, 0)),
                  pl.BlockSpec((tm, HEAD_DIM), lambda i: (i % per, 0)),
                  pl.BlockSpec((tm, HEAD_DIM), lambda i: (i % per, 0))],
        out_specs=[pl.BlockSpec((tm, ATTN_WIDTH), lambda i: (i, 0)),
                   pl.BlockSpec((tm, KV_WIDTH), lambda i: (i, 0)),
                   pl.BlockSpec((tm, KV_WIDTH), lambda i: (i, 0))],
        out_shape=[jax.ShapeDtypeStruct((m, ATTN_WIDTH), BF16),
                   jax.ShapeDtypeStruct((m, KV_WIDTH), BF16),
                   jax.ShapeDtypeStruct((m, KV_WIDTH), BF16)],
        compiler_params=_cparams(("parallel",)),
        name="lat_prep",
    )(proj, proj, qg, kg, cos, sin)


def _lat_attn_kernel(q_ref, k_ref, v_ref, o_ref):
    k = k_ref[0]
    v = v_ref[0]
    for g in range(ATTN_GROUP):
        sl = slice(g * HEAD_DIM, (g + 1) * HEAD_DIM)
        o_ref[:, sl] = _softmax_av(q_ref[:, sl], k, v, None).astype(BF16)


def lat_attention(q, k_all, v_all, seq, tq=256):
    b, s, _ = k_all.shape
    nq = seq // tq
    gw = ATTN_GROUP * HEAD_DIM
    return pl.pallas_call(
        _lat_attn_kernel,
        grid=(b, ATTN_KV_HEADS, nq),
        in_specs=[pl.BlockSpec((tq, gw), lambda bi, kh, qi: (bi * nq + qi, kh)),
                  pl.BlockSpec((1, s, HEAD_DIM), lambda bi, kh, qi: (bi, 0, kh)),
                  pl.BlockSpec((1, s, HEAD_DIM), lambda bi, kh, qi: (bi, 0, kh))],
        out_specs=pl.BlockSpec((tq, gw), lambda bi, kh, qi: (bi * nq + qi, kh)),
        out_shape=jax.ShapeDtypeStruct((b * seq, ATTN_WIDTH), BF16),
        compiler_params=_cparams(("parallel", "parallel", "arbitrary")),
        name="lat_attention",
    )(q, k_all, v_all)


def rope_tables(seq):
    t = jnp.arange(seq)
    row = (t // GRID_W).astype(F32)
    col = (t % GRID_W).astype(F32)
    quarter = HEAD_DIM // 4
    inv_freq = jnp.power(ROPE_THETA, -jnp.arange(quarter, dtype=F32) / quarter)
    ar = row[:, None] * inv_freq[None, :]
    ac = col[:, None] * inv_freq[None, :]
    cos = jnp.concatenate([jnp.cos(ar), jnp.cos(ar), jnp.cos(ac), jnp.cos(ac)], axis=1)
    sin = jnp.concatenate([-jnp.sin(ar), jnp.sin(ar), -jnp.sin(ac), jnp.sin(ac)], axis=1)
    return cos, sin


def _chunk_index(d, i, n):
    return i + d * (n - 1 - 2 * i)


def _lru_kernel(x_ref, xp_ref, xn_ref, cw_ref, cb_ref, wg_ref, bg_ref, sp_ref, h0_ref,
                h_ref, fin_ref, xpad_ref, a_ref, b_ref, hc_ref, *, nb, tt):
    d = pl.program_id(0)
    i = pl.program_id(2)
    n = pl.num_programs(2)
    ci = _chunk_index(d, i, n)
    pitch = tt + PITCH_PAD
    rows = nb * pitch
    nsl = LRU_WIDTH // LANE

    @pl.when(i == 0)
    def _():
        xpad_ref[...] = jnp.zeros_like(xpad_ref)
        hc_ref[...] = h0_ref[0]

    for s in range(nb):
        base = 8 + s * pitch
        xpad_ref[base:base + tt, :] = x_ref[s]
        xpad_ref[base - 1:base, :] = jnp.where(ci == 0, 0.0, xp_ref[s, 7:8, :])
        xpad_ref[base + tt:base + tt + 2, :] = jnp.where(ci == n - 1, 0.0, xn_ref[s, 0:2, :])
    w = cw_ref[...]
    xc = cb_ref[...] + sum(xpad_ref[7 + j:7 + j + rows, :] * w[j:j + 1, :] for j in range(CONV_W))
    pre = _dot(xc.astype(BF16), wg_ref[0]) + bg_ref[0]
    r = jax.nn.sigmoid(pre[:, :LRU_WIDTH])
    ig = jax.nn.sigmoid(pre[:, LRU_WIDTH:])
    a = jnp.exp((-LRU_C) * r * sp_ref[0])
    inp = jnp.sqrt(1.0 - a * a) * (ig * xc)
    for c in range(nsl):
        a_ref[c] = a[:, c * LANE:(c + 1) * LANE]
        b_ref[c] = inp[:, c * LANE:(c + 1) * LANE]

    def body(t, carry):
        row = t + d * (tt - 1 - 2 * t)
        out = []
        for c in range(nsl):
            idx = (c, pl.ds(row, nb, stride=pitch), slice(None))
            h = a_ref[idx] * carry[c] + b_ref[idx]
            b_ref[idx] = h
            out.append(h)
        return tuple(out)

    carry = tuple(hc_ref[:, c * LANE:(c + 1) * LANE] for c in range(nsl))
    carry = lax.fori_loop(0, tt, body, carry, unroll=8)
    for c in range(nsl):
        hc_ref[:, c * LANE:(c + 1) * LANE] = carry[c]
        for s in range(nb):
            h_ref[0, s, :, c * LANE:(c + 1) * LANE] = b_ref[c, s * pitch:s * pitch + tt, :]

    @pl.when(i == n - 1)
    def _():
        fin_ref[0] = hc_ref[...]


def lru_scan(proj3, h0, cw, cb, wg, bg, sp, nb, tt=128):
    b, t, _ = proj3.shape
    w = LRU_WIDTH
    n = t // tt
    t8 = tt // 8
    rows = nb * (tt + PITCH_PAD)
    col = COL_LX // w

    def cidx(d, i):
        return _chunk_index(d, i, n)

    return pl.pallas_call(
        functools.partial(_lru_kernel, nb=nb, tt=tt),
        grid=(2, b // nb, n),
        in_specs=[pl.BlockSpec((nb, tt, w), lambda d, g, i: (g, cidx(d, i), col)),
                  pl.BlockSpec((nb, 8, w), lambda d, g, i: (g, jnp.maximum(cidx(d, i) * t8 - 1, 0), col)),
                  pl.BlockSpec((nb, 8, w), lambda d, g, i: (g, jnp.minimum((cidx(d, i) + 1) * t8, t // 8 - 1), col)),
                  pl.BlockSpec((CONV_W, w), lambda d, g, i: (0, 0)),
                  pl.BlockSpec((1, w), lambda d, g, i: (0, 0)),
                  pl.BlockSpec((1, w, 2 * w), lambda d, g, i: (d, 0, 0)),
                  pl.BlockSpec((1, 1, 2 * w), lambda d, g, i: (d, 0, 0)),
                  pl.BlockSpec((1, 1, w), lambda d, g, i: (d, 0, 0)),
                  pl.BlockSpec((1, nb, w), lambda d, g, i: (d, g, 0))],
        out_specs=[pl.BlockSpec((1, nb, tt, w), lambda d, g, i: (d, g, cidx(d, i), 0)),
                   pl.BlockSpec((1, nb, w), lambda d, g, i: (d, g, 0))],
        out_shape=[jax.ShapeDtypeStruct((2, b, t, w), F32), jax.ShapeDtypeStruct((2, b, w), F32)],
        scratch_shapes=[pltpu.VMEM((rows + 16, w), F32),
                        pltpu.VMEM((w // LANE, rows, LANE), F32),
                        pltpu.VMEM((w // LANE, rows, LANE), F32),
                        pltpu.VMEM((nb, w), F32)],
        compiler_params=_cparams(("parallel", "parallel", "arbitrary")),
        name="lru_scan",
    )(proj3, proj3, proj3, cw, cb, wg, bg, sp, h0)


def _s5_kernel(u_ref, bb_ref, cc_ref, a_ref, h0_ref, y_ref, fin_ref, up_ref, s_ref, hc_ref, *, nb, tt):
    d = pl.program_id(0)
    i = pl.program_id(2)
    n = pl.num_programs(2)
    pitch = tt + PITCH_PAD
    nsl = S5_NSTATE // LANE

    @pl.when(i == 0)
    def _():
        up_ref[...] = jnp.zeros_like(up_ref)
        hc_ref[...] = h0_ref[0]

    for s in range(nb):
        up_ref[s * pitch:s * pitch + tt, :] = u_ref[s]
    u2 = up_ref[...].astype(BF16)
    per_k = MXU // S5_CH * S5_STATE // MXU
    vs = SUBLANES // nb
    rows_seq = nb * pitch
    npk = nsl // vs

    def slab(c):
        return (c // vs, slice((c % vs) * rows_seq, (c % vs + 1) * rows_seq), slice(None))

    for nt in range(2 * S5_NSTATE // MXU):
        kt = (nt % (S5_NSTATE // MXU)) // per_k
        tile = _dot(u2[:, kt * MXU:(kt + 1) * MXU], bb_ref[0, kt * MXU:(kt + 1) * MXU, nt * MXU:(nt + 1) * MXU])
        s_ref[slab(2 * nt)] = tile[:, :LANE]
        s_ref[slab(2 * nt + 1)] = tile[:, LANE:]

    a_all = a_ref[0]

    def packed_rows(src, off, j):
        parts = [jnp.broadcast_to(src[:, off + (j * vs + v) * LANE:off + (j * vs + v + 1) * LANE], (nb, LANE))
                 for v in range(vs)]
        return parts[0] if vs == 1 else jnp.concatenate(parts, axis=0)

    group = 8
    for j0 in range(0, npk, group):
        js = list(range(j0, min(j0 + group, npk)))
        ar = [packed_rows(a_all, 0, j) for j in js]
        ai = [packed_rows(a_all, S5_NSTATE, j) for j in js]

        def body(t, carry, js=js, ar=ar, ai=ai):
            row = t + d * (tt - 1 - 2 * t)
            out = []
            for k, j in enumerate(js):
                hr, hi = carry[2 * k], carry[2 * k + 1]
                ire = (j, pl.ds(row, SUBLANES, stride=pitch), slice(None))
                iim = (npk + j, pl.ds(row, SUBLANES, stride=pitch), slice(None))
                nr = ar[k] * hr - ai[k] * hi + s_ref[ire]
                ni = ar[k] * hi + ai[k] * hr + s_ref[iim]
                s_ref[ire] = nr
                s_ref[iim] = ni
                out += [nr, ni]
            return tuple(out)

        carry = []
        for j in js:
            carry += [packed_rows(hc_ref, 0, j), packed_rows(hc_ref, S5_NSTATE, j)]
        carry = lax.fori_loop(0, tt, body, tuple(carry), unroll=4)
        for k, j in enumerate(js):
            for v in range(vs):
                c = j * vs + v
                hc_ref[:, c * LANE:(c + 1) * LANE] = carry[2 * k][v * nb:(v + 1) * nb]
                hc_ref[:, S5_NSTATE + c * LANE:S5_NSTATE + (c + 1) * LANE] = carry[2 * k + 1][v * nb:(v + 1) * nb]

    n_out = S5_WIDTH // MXU
    per_n = nsl // n_out
    for nt in range(n_out):
        slabs = [per_n * nt + k for k in range(per_n)] + [nsl + per_n * nt + k for k in range(per_n)]
        lhs = jnp.concatenate([s_ref[slab(c)] for c in slabs], axis=1).astype(BF16)
        y = _dot(lhs, cc_ref[0, nt])
        for s in range(nb):
            y_ref[0, s, :, nt * MXU:(nt + 1) * MXU] = y[s * pitch:s * pitch + tt]

    @pl.when(i == n - 1)
    def _():
        fin_ref[0] = hc_ref[...]


def s5_scan(proj3, h0, bb, cc, a, nb, tt=128):
    b, t, _ = proj3.shape
    w = S5_WIDTH
    n = t // tt
    rows = nb * (tt + PITCH_PAD)
    ns2 = 2 * S5_NSTATE

    def cidx(d, i):
        return _chunk_index(d, i, n)

    return pl.pallas_call(
        functools.partial(_s5_kernel, nb=nb, tt=tt),
        grid=(2, b // nb, n),
        in_specs=[pl.BlockSpec((nb, tt, w), lambda d, g, i: (g, cidx(d, i), COL_SU // w)),
                  pl.BlockSpec((1, w, ns2), lambda d, g, i: (d, 0, 0)),
                  pl.BlockSpec((1,) + cc.shape[1:], lambda d, g, i: (d, 0, 0, 0)),
                  pl.BlockSpec((1, 1, ns2), lambda d, g, i: (d, 0, 0)),
                  pl.BlockSpec((1, nb, ns2), lambda d, g, i: (d, g, 0))],
        out_specs=[pl.BlockSpec((1, nb, tt, w), lambda d, g, i: (d, g, cidx(d, i), 0)),
                   pl.BlockSpec((1, nb, ns2), lambda d, g, i: (d, g, 0))],
        out_shape=[jax.ShapeDtypeStruct((2, b, t, w), F32), jax.ShapeDtypeStruct((2, b, ns2), F32)],
        scratch_shapes=[pltpu.VMEM((rows, w), F32),
                        pltpu.VMEM((ns2 // LANE // (SUBLANES // nb), SUBLANES // nb * rows, LANE), F32),
                        pltpu.VMEM((nb, ns2), F32)],
        compiler_params=_cparams(("parallel", "parallel", "arbitrary")),
        name="s5_scan",
    )(proj3, bb, cc, a, h0)


def _split3(x):
    x1 = x.astype(BF16)
    r1 = x - x1.astype(F32)
    x2 = r1.astype(BF16)
    x3 = (r1 - x2.astype(F32)).astype(BF16)
    return x1, x2, x3


def _mm(x, y):
    return _dot(x.astype(BF16), y.astype(BF16))


def _unit_tri_inverses(mats, rr, cc):
    same16 = (rr // 16) == (cc // 16)
    same32 = (rr // 32) == (cc // 32)
    p = [jnp.where(same16, -a, 0.0) for a in mats]
    t = [jnp.where(rr == cc, 1.0, x) for x in p]
    for _ in range(3):
        p = [_mm(x, x) for x in p]
        t = [x + _mm(x, y) for x, y in zip(t, p)]
    for mask in (same32 & jnp.logical_not(same16), jnp.logical_not(same32)):
        te = [_mm(x, jnp.where(mask, a, 0.0)) for x, a in zip(t, mats)]
        t = [x - _mm(y, x) for x, y in zip(t, te)]
    return t


def _dn_prep_kernel(x_ref, xp_ref, xn_ref, cw_ref, o_ref, xpad_ref, *, tb):
    i = pl.program_id(1)
    n = pl.num_programs(1)
    hw = DN_HEADS * DN_DK
    xpad_ref[8:8 + tb, :] = x_ref[0]
    xpad_ref[0:8, :] = jnp.where(i == 0, 0.0, xp_ref[0])
    xpad_ref[8 + tb:16 + tb, :] = jnp.where(i == n - 1, 0.0, xn_ref[0])
    w = cw_ref[...]
    xc = sum(xpad_ref[7 + j:7 + j + tb, :] * w[j:j + 1, :] for j in range(CONV_W))
    qkv = xc * jax.nn.sigmoid(xc)
    for h in range(DN_HEADS):
        q = qkv[:, h * DN_DK:(h + 1) * DN_DK]
        k = qkv[:, hw + h * DN_DK:hw + (h + 1) * DN_DK]
        o_ref[0, :, h * DN_DK:(h + 1) * DN_DK] = (
            q * lax.rsqrt(jnp.sum(q * q, axis=-1, keepdims=True) + EPS) * (DN_DK ** -0.5))
        o_ref[0, :, hw + h * DN_DK:hw + (h + 1) * DN_DK] = (
            k * lax.rsqrt(jnp.sum(k * k, axis=-1, keepdims=True) + EPS))
    o_ref[0, :, 2 * hw:] = qkv[:, 2 * hw:]


def dn_prep(proj3, cw, tb=256):
    b, t, _ = proj3.shape
    n = t // tb
    t8 = tb // 8
    w3 = 3 * DN_WIDTH
    cq = COL_DQ // w3
    return pl.pallas_call(
        functools.partial(_dn_prep_kernel, tb=tb),
        grid=(b, n),
        in_specs=[pl.BlockSpec((1, tb, w3), lambda bi, i: (bi, i, cq)),
                  pl.BlockSpec((1, 8, w3), lambda bi, i: (bi, jnp.maximum(i * t8 - 1, 0), cq)),
                  pl.BlockSpec((1, 8, w3), lambda bi, i: (bi, jnp.minimum((i + 1) * t8, t // 8 - 1), cq)),
                  pl.BlockSpec((CONV_W, w3), lambda bi, i: (0, 0))],
        out_specs=pl.BlockSpec((1, tb, w3), lambda bi, i: (bi, i, 0)),
        out_shape=jax.ShapeDtypeStruct((b, t, w3), F32),
        scratch_shapes=[pltpu.VMEM((tb + 16, w3), F32)],
        compiler_params=_cparams(("parallel", "parallel")),
        name="dn_prep",
    )(proj3, proj3, proj3, cw)


def _dn_kernel(xf_ref, baf_ref, xb_ref, bab_ref, al_ref, dtb_ref, s0_ref, of_ref, ob_ref, fin_ref, s_ref,
               *, nbat, tb):
    i = pl.program_id(1)
    n = pl.num_programs(1)
    ch = DN_CHUNK
    nc = tb // ch
    hw = DN_HEADS * DN_DK

    @pl.when(i == 0)
    def _():
        s_ref[...] = s0_ref[...]

    rb = lax.broadcasted_iota(jnp.int32, (tb, tb), 0)
    cb = lax.broadcasted_iota(jnp.int32, (tb, tb), 1)
    rr = lax.broadcasted_iota(jnp.int32, (ch, ch), 0)
    cc = lax.broadcasted_iota(jnp.int32, (ch, ch), 1)
    al = al_ref[...]
    dtb = dtb_ref[...]

    x_refs = (xf_ref, xb_ref)
    beta_tab, gc_tab, masks = {}, {}, {}
    for d in range(2):
        reverse = d == 1
        ba_ref = bab_ref if reverse else baf_ref
        tri = ((rb // ch) == (cb // ch)) & ((cb >= rb) if reverse else (cb <= rb))
        tri = jnp.where(tri, 1.0, 0.0).astype(BF16)
        masks[d] = ((cc >= rr) if reverse else (cc <= rr), (cc > rr) if reverse else (cc < rr))
        for bat in range(nbat):
            ba = ba_ref[bat]
            beta_tab[(d, bat)] = jax.nn.sigmoid(ba)
            g_all = -jnp.exp(al) * _softplus(ba + dtb)
            gc_tab[(d, bat)] = sum(_dot(tri, piece) for piece in _split3(g_all))
    units = [dict(d=d, bat=bat, c=c, h=h) for d in range(2) for bat in range(nbat) for c in range(nc)
             for h in range(DN_HEADS)]

    def rows_of(u):
        return slice(u['c'] * ch, (u['c'] + 1) * ch)

    def q_of(u):
        return x_refs[u['d']][u['bat'], rows_of(u), u['h'] * DN_DK:(u['h'] + 1) * DN_DK]

    def k_of(u):
        return x_refs[u['d']][u['bat'], rows_of(u), hw + u['h'] * DN_DK:hw + (u['h'] + 1) * DN_DK]

    def v_of(u):
        return x_refs[u['d']][u['bat'], rows_of(u), 2 * hw + u['h'] * DN_DV:2 * hw + (u['h'] + 1) * DN_DV]

    def beta_of(u):
        lane = u['d'] * DN_HEADS + u['h']
        return beta_tab[(u['d'], u['bat'])][rows_of(u), lane:lane + 1]

    def gcol_of(u):
        lane = 2 * DN_HEADS + u['d'] * DN_HEADS + u['h']
        return gc_tab[(u['d'], u['bat'])][rows_of(u), lane:lane + 1]

    def glast_of(u):
        gcol = gcol_of(u)
        return gcol[0:1, :] if u['d'] == 1 else gcol[ch - 1:ch, :]

    gct = {}
    for u in units:
        key = (u['d'], u['bat'], u['c'])
        if key not in gct:
            gct[key] = gc_tab[(u['d'], u['bat'])][rows_of(u), :].T
        lane = 2 * DN_HEADS + u['d'] * DN_HEADS + u['h']
        incl, strict = masks[u['d']]
        decay = jnp.where(incl, jnp.exp(gcol_of(u) - gct[key][lane:lane + 1, :]), 0.0)
        k = k_of(u)
        kbf = k.astype(BF16)
        u['a'] = jnp.where(strict, _dot_nt((k * beta_of(u)).astype(BF16), kbf) * decay, 0.0)
        u['qk'] = (_dot_nt(q_of(u).astype(BF16), kbf) * decay).astype(BF16)
    t_inv = _unit_tri_inverses([u['a'] for u in units], rr, cc)
    for u, t in zip(units, t_inv):
        beta = beta_of(u)
        rhs = jnp.concatenate([v_of(u) * beta, k_of(u) * beta * jnp.exp(gcol_of(u))], axis=1)
        sol = rhs + _mm(jnp.where(rr == cc, 0.0, t), rhs)
        u['u_val'] = sol[:, :DN_DV]
        u['w'] = sol[:, DN_DV:].astype(BF16)

    state = {(d, bat, h): s_ref[d, bat, h] for d in range(2) for bat in range(nbat) for h in range(DN_HEADS)}
    for j in range(nc):
        cur = [u for u in units if u['c'] == (nc - 1 - j if u['d'] == 1 else j)]
        keys = [(u['d'], u['bat'], u['h']) for u in cur]
        sb = [state[key].astype(BF16) for key in keys]
        ws = [_dot(u['w'], s) for u, s in zip(cur, sb)]
        qs = [_dot((q_of(u) * jnp.exp(gcol_of(u))).astype(BF16), s) for u, s in zip(cur, sb)]
        vb = [(u['u_val'] - x).astype(BF16) for u, x in zip(cur, ws)]
        os = [x + _dot(u['qk'], y) for u, x, y in zip(cur, qs, vb)]
        kdec_t = [(k_of(u) * jnp.exp(glast_of(u) - gcol_of(u))).T.astype(BF16) for u in cur]
        sn = [state[key] * jnp.exp(glast_of(u)) + _dot(kt, y) for u, key, kt, y in zip(cur, keys, kdec_t, vb)]
        for u, key, o, s in zip(cur, keys, os, sn):
            state[key] = s
            o_ref = ob_ref if u['d'] == 1 else of_ref
            o_ref[u['bat'], u['c'] * ch:(u['c'] + 1) * ch, u['h'] * DN_DV:(u['h'] + 1) * DN_DV] = o
    for (d, bat, h), s in state.items():
        s_ref[d, bat, h] = s

    @pl.when(i == n - 1)
    def _():
        fin_ref[...] = s_ref[...]


def dn_scan(qkv3, ba3, s0, al, dtb, nbat, tb):
    b, t, w3 = qkv3.shape
    n = t // tb

    def data_specs(cidx):
        return [pl.BlockSpec((nbat, tb, w3), lambda g, i: (g, cidx(i), 0)),
                pl.BlockSpec((nbat, tb, LANE), lambda g, i: (g, cidx(i), 0))]

    fwd = lambda i: i
    bwd = lambda i: n - 1 - i
    st_spec = pl.BlockSpec((2, nbat, DN_HEADS, DN_DK, DN_DV), lambda g, i: (0, g, 0, 0, 0))
    return pl.pallas_call(
        functools.partial(_dn_kernel, nbat=nbat, tb=tb),
        grid=(b // nbat, n),
        in_specs=data_specs(fwd) + data_specs(bwd) + [
            pl.BlockSpec((1, LANE), lambda g, i: (0, 0)),
            pl.BlockSpec((1, LANE), lambda g, i: (0, 0)),
            st_spec],
        out_specs=[pl.BlockSpec((nbat, tb, DN_WIDTH), lambda g, i: (g, i, 0)),
                   pl.BlockSpec((nbat, tb, DN_WIDTH), lambda g, i: (g, n - 1 - i, 0)),
                   st_spec],
        out_shape=[jax.ShapeDtypeStruct((b, t, DN_WIDTH), F32),
                   jax.ShapeDtypeStruct((b, t, DN_WIDTH), F32),
                   jax.ShapeDtypeStruct((2, b, DN_HEADS, DN_DK, DN_DV), F32)],
        scratch_shapes=[pltpu.VMEM((2, nbat, DN_HEADS, DN_DK, DN_DV), F32)],
        compiler_params=_cparams(("parallel", "arbitrary")),
        name="dn_scan",
    )(qkv3, ba3, qkv3, ba3, al, dtb, s0)


def _post_kernel(hf_ref, hb_ref, lg_ref, of_ref, ob_ref, dz_ref, ng_ref, yf_ref, yb_ref, su_ref, sd_ref,
                 gw_ref, gb_ref, lru_ref, dn_ref, s5_ref):
    lru_ref[...] = ((hf_ref[0, 0] + hb_ref[0, 0]) * _gelu(lg_ref[...])).astype(BF16)
    o = of_ref[0] + ob_ref[0]
    dz = dz_ref[...]
    for h in range(DN_HEADS):
        sl = slice(h * DN_DV, (h + 1) * DN_DV)
        oh = o[:, sl]
        oh = oh * lax.rsqrt(jnp.mean(oh * oh, axis=-1, keepdims=True) + EPS) * ng_ref[...]
        zh = dz[:, sl]
        dn_ref[:, sl] = (oh * (zh * jax.nn.sigmoid(zh))).astype(BF16)
    y = yf_ref[0, 0] + yb_ref[0, 0] + sd_ref[...] * su_ref[...]
    gy = _gelu(y)
    s5_ref[...] = (gy * jax.nn.sigmoid(_dot(gy.astype(BF16), gw_ref[...]) + gb_ref[...])).astype(BF16)


def mix_post(proj, h_lru, o_f, o_b, y_s5, ng, sd, gw, gb):
    m = proj.shape[0]
    _, b, t, w = h_lru.shape
    tm = min(512, t)
    per = t // tm
    tok = pl.BlockSpec((tm, w), lambda i: (i, 0))

    def pcol(c0):
        return pl.BlockSpec((tm, w), lambda i: (i, c0 // w))

    def dirspec(d):
        return pl.BlockSpec((1, 1, tm, w), lambda i: (d, i // per, i % per, 0))

    seq = pl.BlockSpec((1, tm, w), lambda i: (i // per, i % per, 0))

    def vec(n):
        return pl.BlockSpec((1, n), lambda i: (0, 0))

    out = jax.ShapeDtypeStruct((m, w), BF16)
    return pl.pallas_call(
        _post_kernel,
        grid=(m // tm,),
        in_specs=[dirspec(0), dirspec(1), pcol(COL_LG), seq, seq, pcol(COL_DZ), vec(DN_DV),
                  dirspec(0), dirspec(1), pcol(COL_SU), vec(w), pl.BlockSpec((w, w), lambda i: (0, 0)), vec(w)],
        out_specs=[tok, tok, tok],
        out_shape=[out, out, out],
        compiler_params=_cparams(("parallel",)),
        name="mix_post",
    )(h_lru, h_lru, proj, o_f, o_b, proj, ng, y_s5, y_s5, proj, sd, gw, gb)


def _out_kernel(x_ref, a_ref, b_ref, c_ref, d_ref, w_ref, gate_ref, g_ref, bb_ref, sc_ref, sh_ref,
                x1_ref, u2_ref):
    q = MIXW
    for r0 in range(0, x_ref.shape[0], OUT_SUB):
        rows = slice(r0, r0 + OUT_SUB)
        y = _dot(a_ref[rows, :], w_ref[0:q, :])
        y += _dot(b_ref[rows, :], w_ref[q:2 * q, :])
        y += _dot(c_ref[rows, :], w_ref[2 * q:3 * q, :])
        y += _dot(d_ref[rows, :], w_ref[3 * q:, :])
        z = DEEPNORM_ALPHA * x_ref[rows, :] + gate_ref[0] * y
        x1 = _ln(z) * g_ref[...] + bb_ref[...]
        x1_ref[rows, :] = x1
        u2_ref[rows, :] = (_ln(x1) * (1.0 + sc_ref[0]) + sh_ref[0]).astype(BF16)


def out_proj(x, parts, w, layer, mod, ln_g, ln_b, mod_row, tm=512):
    m, d = x.shape

    def mod_spec(kind):
        return pl.BlockSpec((1, 1, d), lambda i: (kind * MOD_ROWS + mod_row(i * tm), 0, 0))

    vec = pl.BlockSpec((1, d), lambda i: (0, 0))
    part = pl.BlockSpec((tm, MIXW), lambda i: (i, 0))
    return pl.pallas_call(
        _out_kernel,
        grid=(m // tm,),
        in_specs=[pl.BlockSpec((tm, d), lambda i: (i, 0)), part, part, part, part,
                  pl.BlockSpec((None, d, d), lambda i: (layer, 0, 0)),
                  mod_spec(2), vec, vec, mod_spec(4), mod_spec(3)],
        out_specs=[pl.BlockSpec((tm, d), lambda i: (i, 0)), pl.BlockSpec((tm, d), lambda i: (i, 0))],
        out_shape=[jax.ShapeDtypeStruct((m, d), F32), jax.ShapeDtypeStruct((m, d), BF16)],
        compiler_params=_cparams(("parallel",)),
        name="out_proj",
    )(x, *parts, w, mod, ln_g, ln_b, mod, mod)


def _mlp_kernel(u_ref, x_ref, w1_ref, w2_ref, gate_ref, g_ref, b_ref, o_ref, acc_ref):
    f = pl.program_id(1)

    @pl.when(f == 0)
    def _():
        acc_ref[...] = jnp.zeros_like(acc_ref)

    h = _dot(u_ref[...], w1_ref[...])
    h = jnp.square(jnp.maximum(h, 0.0)).astype(BF16)
    acc_ref[...] += _dot(h, w2_ref[...])

    @pl.when(f == pl.num_programs(1) - 1)
    def _():
        z = DEEPNORM_ALPHA * x_ref[...] + gate_ref[0] * acc_ref[...]
        o_ref[...] = _ln(z) * g_ref[...] + b_ref[...]


def mlp(u, x, w1, w2, layer, mod, ln_g, ln_b, mod_row, tm=512, tf=1024):
    m, d = x.shape
    ff = w1.shape[2]
    vec = pl.BlockSpec((1, d), lambda i, f: (0, 0))
    return pl.pallas_call(
        _mlp_kernel,
        grid=(m // tm, ff // tf),
        in_specs=[pl.BlockSpec((tm, d), lambda i, f: (i, 0)),
                  pl.BlockSpec((tm, d), lambda i, f: (i, 0)),
                  pl.BlockSpec((None, d, tf), lambda i, f: (layer, 0, f)),
                  pl.BlockSpec((None, tf, d), lambda i, f: (layer, f, 0)),
                  pl.BlockSpec((1, 1, d), lambda i, f: (5 * MOD_ROWS + mod_row(i * tm), 0, 0)),
                  vec, vec],
        out_specs=pl.BlockSpec((tm, d), lambda i, f: (i, 0)),
        out_shape=jax.ShapeDtypeStruct((m, d), F32),
        scratch_shapes=[pltpu.VMEM((tm, d), F32)],
        compiler_params=_cparams(("parallel", "arbitrary")),
        name="mlp",
    )(u, x, w1, w2, mod, ln_g, ln_b)


def _reorder_w_in(w):
    def cols(c0, n):
        return w[..., c0:c0 + n]

    aq, akv, lx, lg = cols(0, 512), cols(512, 512), cols(1024, 512), cols(1536, 512)
    dqkv, dz, ba, su = cols(2048, 1536), cols(3584, 512), cols(4096, N_BA), cols(4096 + N_BA, 512)
    pad = jnp.zeros(w.shape[:2] + (LANE - N_BA,), w.dtype)
    main = jnp.concatenate([dqkv, dz, aq, akv, lx, lg, su], axis=-1).astype(BF16)
    return main, jnp.concatenate([ba, pad], axis=-1).astype(BF16)


def _block_diag(blocks):
    n, r, c = blocks.shape[-3:]
    eye = jnp.eye(n, dtype=blocks.dtype)
    out = blocks[..., :, :, None, :] * eye[:, None, :, None]
    return out.reshape(blocks.shape[:-3] + (n * r, n * c))


def _lru_params(wa, ba, wx, bx, lam):
    wg = jnp.concatenate([_block_diag(wa.astype(BF16)), _block_diag(wx.astype(BF16))], axis=-1)
    bg = jnp.concatenate([ba, bx], axis=-1)[..., None, :]
    sp = jax.nn.softplus(-lam)[..., None, :]
    return wg, bg, sp


def _s5_params(lam_re, lam_im, log_dt, b_re, b_im, c_re, c_im):
    lead = lam_re.shape[:-2]
    dt = jnp.exp(log_dt)[..., None]
    mag = jnp.exp(lam_re * dt)
    abar_re = mag * jnp.cos(lam_im * dt)
    abar_im = mag * jnp.sin(lam_im * dt)
    den = lam_re * lam_re + lam_im * lam_im
    nr = abar_re - 1.0
    ni = abar_im
    f_re = (nr * lam_re + ni * lam_im) / den
    f_im = (ni * lam_re - nr * lam_im) / den
    bb_re = f_re[..., None] * b_re - f_im[..., None] * b_im
    bb_im = f_re[..., None] * b_im + f_im[..., None] * b_re
    to_in = lambda m: _block_diag(jnp.swapaxes(m, -1, -2).astype(BF16))
    bb = jnp.concatenate([to_in(bb_re), to_in(bb_im)], axis=-1)
    n_out = S5_WIDTH // MXU
    gpt = S5_GROUPS // n_out

    def to_out(m, nt):
        return _block_diag(jnp.swapaxes(m[..., nt * gpt:(nt + 1) * gpt, :, :], -1, -2).astype(BF16))

    cc = jnp.stack([jnp.concatenate([to_out(c_re, nt), to_out(-c_im, nt)], axis=-2)
                    for nt in range(n_out)], axis=-3)
    a = jnp.concatenate([abar_re.reshape(lead + (1, -1)), abar_im.reshape(lead + (1, -1))], axis=-1)
    return bb, cc, a


def _lane_row(vals, offset):
    return jnp.zeros((1, LANE), F32).at[0, offset:offset + vals.size].set(vals.reshape(-1))


def kernel(x_prompt, x_sample, cache_attn_k, cache_attn_v, state_rglru, state_delta, state_s5_re, state_s5_im, c, c_ctx, w_ada, b_ada, w_in, w_out, ln1_g, ln1_b, ln2_g, ln2_b, w_mlp1, w_mlp2, q_norm_g, k_norm_g, lru_conv_w, lru_conv_b, lru_wa, lru_ba, lru_wx, lru_bx, lru_lambda, dn_conv_w, dn_a_log, dn_dt_bias, dn_norm_g, s5_lambda_re, s5_lambda_im, s5_log_dt, s5_b_re, s5_b_im, s5_c_re, s5_c_im, s5_d, s5_glu_w, s5_glu_b):
    bp, tp, d = x_prompt.shape
    bs, ts, _ = x_sample.shape
    ctx_row = bs

    cond = jnp.concatenate([c, c_ctx[None, :], jnp.zeros((MOD_ROWS - bs - 1, d), F32)], axis=0)
    mods = ada_mod(cond, w_ada, b_ada)
    w_in_r, w_ba = _reorder_w_in(w_in)
    w_out_b = w_out.astype(BF16)
    w1_b = w_mlp1.astype(BF16)
    w2_b = w_mlp2.astype(BF16)
    cos, sin = rope_tables(ts)
    wg_all, bg_all, sp_all = _lru_params(lru_wa, lru_ba, lru_wx, lru_bx, lru_lambda)
    bb_all, cc_all, a5_all = _s5_params(s5_lambda_re, s5_lambda_im, s5_log_dt, s5_b_re, s5_b_im, s5_c_re, s5_c_im)

    streams = {
        'ctx': dict(x=x_prompt.reshape(bp * tp, d), b=bp, t=tp, nb=8, s5_tt=128, dn_nbat=2, dn_tb=tp,
                    mod_row=lambda tok: ctx_row),
        'lat': dict(x=x_sample.reshape(bs * ts, d), b=bs, t=ts, nb=bs, s5_tt=256, dn_nbat=bs, dn_tb=2 * DN_CHUNK,
                    mod_row=lambda tok: tok // ts),
    }
    ks, vs, lrus, dns, s5rs, s5is = [], [], [], [], [], []
    for l in range(DEPTH):
        mod = mods[l].reshape(MOD_ROWS, N_MOD, d).transpose(1, 0, 2).reshape(N_MOD * MOD_ROWS, 1, d)
        qg = q_norm_g[l].reshape(1, HEAD_DIM)
        kg = k_norm_g[l].reshape(1, HEAD_DIM)
        wg, bg, sp = wg_all[l], bg_all[l], sp_all[l]
        bb, cc, a5 = bb_all[l], cc_all[l], a5_all[l]
        al = _lane_row(dn_a_log[l], 2 * DN_HEADS)
        dtb = _lane_row(dn_dt_bias[l], 2 * DN_HEADS)
        for name, st in streams.items():
            b, t, nb = st['b'], st['t'], st['nb']
            is_ctx = name == 'ctx'
            proj, ba = in_proj(st['x'], mod, w_in_r, w_ba, l, st['mod_row'])
            proj3 = proj.reshape(b, t, N_PROJ)
            ba3 = ba.reshape(b, t, LANE)

            if is_ctx:
                attn, kn, vv = ctx_attention(proj, qg, kg, b, t)
                ks.append(kn.reshape(b, t, ATTN_KV_HEADS, HEAD_DIM))
                vs.append(vv.reshape(b, t, ATTN_KV_HEADS, HEAD_DIM))
                h0_lru = jnp.zeros((2, b, LRU_WIDTH), F32)
                s0_dn = jnp.zeros((2, b, DN_HEADS, DN_DK, DN_DV), F32)
                h0_s5 = jnp.zeros((2, b, 2 * S5_NSTATE), F32)
            else:
                q_s, k_s, v_s = lat_prep(proj, qg, kg, cos, sin, t)
                k_all = jnp.concatenate([cache_attn_k[:, l].reshape(b, -1, KV_WIDTH).astype(BF16),
                                         k_s.reshape(b, t, KV_WIDTH)], axis=1)
                v_all = jnp.concatenate([cache_attn_v[:, l].reshape(b, -1, KV_WIDTH).astype(BF16),
                                         v_s.reshape(b, t, KV_WIDTH)], axis=1)
                attn = lat_attention(q_s, k_all, v_all, t)
                h0_lru = jnp.swapaxes(state_rglru[:, l], 0, 1)
                s0_dn = jnp.swapaxes(state_delta[:, l], 0, 1)
                h0_s5 = jnp.swapaxes(jnp.concatenate([state_s5_re[:, l].reshape(b, 2, S5_NSTATE),
                                                      state_s5_im[:, l].reshape(b, 2, S5_NSTATE)], axis=-1), 0, 1)

            h_lru, lru_fin = lru_scan(proj3, h0_lru, lru_conv_w[l], lru_conv_b[l].reshape(1, -1), wg, bg, sp, nb)
            o_f, o_b, dn_fin = dn_scan(dn_prep(proj3, dn_conv_w[l]), ba3, s0_dn, al, dtb,
                                       st['dn_nbat'], st['dn_tb'])
            y_s5, s5_fin = s5_scan(proj3, h0_s5, bb, cc, a5, nb, st['s5_tt'])
            lru_out, dn_out, s5_out = mix_post(
                proj, h_lru, o_f, o_b, y_s5, dn_norm_g[l].reshape(1, DN_DV), s5_d[l].reshape(1, MIXW),
                s5_glu_w[l].astype(BF16), s5_glu_b[l].reshape(1, MIXW))
            if is_ctx:
                lrus.append(jnp.swapaxes(lru_fin, 0, 1))
                dns.append(jnp.swapaxes(dn_fin, 0, 1))
                s5_fin = jnp.swapaxes(s5_fin, 0, 1)
                s5rs.append(s5_fin[..., :S5_NSTATE].reshape(b, 2, S5_GROUPS, S5_STATE))
                s5is.append(s5_fin[..., S5_NSTATE:].reshape(b, 2, S5_GROUPS, S5_STATE))

            x1, u2 = out_proj(st['x'], [attn, lru_out, dn_out, s5_out], w_out_b, l, mod,
                              ln1_g[l].reshape(1, d), ln1_b[l].reshape(1, d), st['mod_row'])
            st['x'] = mlp(u2, x1, w1_b, w2_b, l, mod, ln2_g[l].reshape(1, d), ln2_b[l].reshape(1, d),
                          st['mod_row'])

    y_prompt = streams['ctx']['x'].reshape(bp, tp, d)
    y_sample = streams['lat']['x'].reshape(bs, ts, d)
    return (y_prompt, y_sample, jnp.stack(ks, axis=1), jnp.stack(vs, axis=1), jnp.stack(lrus, axis=1),
            jnp.stack(dns, axis=1), jnp.stack(s5rs, axis=1), jnp.stack(s5is, axis=1))
```

```python
import functools

import jax
import jax.numpy as jnp
from jax import lax
from jax.experimental import pallas as pl
from jax.experimental.pallas import tpu as pltpu

F32 = jnp.float32
BF16 = jnp.bfloat16

D_MODEL = 2048
DEPTH = 2
GRID_W = 64
CONV_W = 4
EPS = 1e-6
ROPE_THETA = 10000.0
N_MOD = 6
HEAD_DIM = 128
ATTN_WIDTH = D_MODEL // 4
ATTN_HEADS = ATTN_WIDTH // HEAD_DIM
ATTN_KV_HEADS = ATTN_HEADS // 2
ATTN_GROUP = ATTN_HEADS // ATTN_KV_HEADS
KV_WIDTH = ATTN_KV_HEADS * HEAD_DIM
ATTN_SCALE = HEAD_DIM ** -0.5
LRU_WIDTH = D_MODEL // 4
LRU_BLOCKS = 8
LRU_C = 8.0
DN_DK = 128
DN_DV = 128
DN_WIDTH = D_MODEL // 4
DN_HEADS = DN_WIDTH // DN_DV
DN_CHUNK = 64
S5_WIDTH = D_MODEL // 4
S5_CH = 16
S5_GROUPS = S5_WIDTH // S5_CH
S5_STATE = 64
S5_NSTATE = S5_GROUPS * S5_STATE
DEEPNORM_ALPHA = (2 * DEPTH) ** 0.25
MIXW = D_MODEL // 4

COL_DQ = 0
COL_DZ = 1536
COL_Q = 2048
COL_KV = 2560
COL_LX = 3072
COL_LG = 3584
COL_SU = 4096
N_PROJ = 4608
N_BA = 2 * 2 * DN_HEADS
LANE = 128
SUBLANES = 8
MXU = 256

MOD_ROWS = 8
PITCH_PAD = 4
OUT_SUB = 256
IN_SUB = 256
MLP_SUB = 256
VMEM_LIMIT = 56 * 1024 * 1024


def _cparams(sem):
    return pltpu.CompilerParams(dimension_semantics=sem, vmem_limit_bytes=VMEM_LIMIT)


def _ln(x):
    mu = jnp.mean(x, axis=-1, keepdims=True)
    xc = x - mu
    var = jnp.mean(xc * xc, axis=-1, keepdims=True)
    return xc * lax.rsqrt(var + EPS)


def _softplus(x):
    return jnp.maximum(x, 0.0) + jnp.log1p(jnp.exp(-jnp.abs(x)))


def _gelu(x):
    return 0.5 * x * (1.0 + jnp.tanh(0.7978845608028654 * (x + 0.044715 * (x * x * x))))


def _dot(a, b):
    return jnp.dot(a, b, preferred_element_type=F32)


def _dot_nt(a, b):
    return lax.dot_general(a, b, (((1,), (1,)), ((), ())), preferred_element_type=F32)


def _ada_kernel(c_ref, w_ref, b_ref, o_ref):
    cs = c_ref[...]
    s = cs * jax.nn.sigmoid(cs)
    o_ref[0] = _dot(s.astype(BF16), w_ref[0].astype(BF16)) + b_ref[0]


def ada_mod(cond, w_ada, b_ada, tn=1024):
    depth, d, n = w_ada.shape
    return pl.pallas_call(
        _ada_kernel,
        grid=(depth, n // tn),
        in_specs=[pl.BlockSpec((MOD_ROWS, d), lambda l, j: (0, 0)),
                  pl.BlockSpec((1, d, tn), lambda l, j: (l, 0, j)),
                  pl.BlockSpec((1, 1, tn), lambda l, j: (l, 0, j))],
        out_specs=pl.BlockSpec((1, MOD_ROWS, tn), lambda l, j: (l, 0, j)),
        out_shape=jax.ShapeDtypeStruct((depth, MOD_ROWS, n), F32),
        compiler_params=_cparams(("parallel", "parallel")),
        name="ada_mod",
    )(cond, w_ada, b_ada.reshape(depth, 1, n))


def _in_kernel(x_ref, sc_ref, sh_ref, w_ref, wba_ref, o_ref, ba_ref, u_ref):
    j = pl.program_id(1)

    @pl.when(j == 0)
    def _():
        for r0 in range(0, x_ref.shape[0], IN_SUB):
            rows = slice(r0, r0 + IN_SUB)
            u = (_ln(x_ref[rows, :]) * (1.0 + sc_ref[0]) + sh_ref[0]).astype(BF16)
            u_ref[rows, :] = u
            o_ref[rows, :] = _dot(u, w_ref[...])
            ba_ref[rows, :] = _dot(u, wba_ref[...])

    @pl.when(j > 0)
    def _():
        o_ref[...] = _dot(u_ref[...], w_ref[...])


def in_proj(x, mod, w, w_ba, layer, mod_row, tm=1024, tn=768):
    m, d = x.shape
    n = w.shape[2]

    def mod_spec(kind):
        return pl.BlockSpec((1, 1, d), lambda i, j: (kind * MOD_ROWS + mod_row(i * tm), 0, 0))

    return pl.pallas_call(
        _in_kernel,
        grid=(m // tm, n // tn),
        in_specs=[pl.BlockSpec((tm, d), lambda i, j: (i, 0)),
                  mod_spec(1), mod_spec(0),
                  pl.BlockSpec((None, d, tn), lambda i, j: (layer, 0, j)),
                  pl.BlockSpec((None, d, LANE), lambda i, j: (layer, 0, 0))],
        out_specs=[pl.BlockSpec((tm, tn), lambda i, j: (i, j)),
                   pl.BlockSpec((tm, LANE), lambda i, j: (i, 0))],
        out_shape=[jax.ShapeDtypeStruct((m, n), F32), jax.ShapeDtypeStruct((m, LANE), F32)],
        scratch_shapes=[pltpu.VMEM((tm, d), BF16)],
        compiler_params=_cparams(("parallel", "arbitrary")),
        name="in_proj",
    )(x, mod, mod, w, w_ba)


def _rms_heads(x, g, heads):
    outs = []
    for h in range(heads):
        xh = x[:, h * HEAD_DIM:(h + 1) * HEAD_DIM]
        outs.append(xh * lax.rsqrt(jnp.mean(xh * xh, axis=-1, keepdims=True) + EPS) * g)
    return outs


def _softmax_av(q, k, v, scale):
    s = _dot_nt(q, k)
    if scale is not None:
        s = s * scale
    m = jnp.max(s, axis=-1, keepdims=True)
    p = jnp.exp(s - m)
    l = jnp.sum(p, axis=-1, keepdims=True)
    return _dot(p.astype(BF16), v) / l


def _ctx_attn_kernel(q_ref, kv_ref, qg_ref, kg_ref, o_ref, kn_ref, v_ref):
    qs = _rms_heads(q_ref[...], qg_ref[...], ATTN_HEADS)
    kv = kv_ref[...]
    ks = _rms_heads(kv[:, :KV_WIDTH], kg_ref[...], ATTN_KV_HEADS)
    v = kv[:, KV_WIDTH:]
    v_ref[...] = v
    t = q_ref.shape[0]
    for kh in range(ATTN_KV_HEADS):
        kn_ref[:, kh * HEAD_DIM:(kh + 1) * HEAD_DIM] = ks[kh]
        q2 = jnp.concatenate([qs[kh * ATTN_GROUP + g] for g in range(ATTN_GROUP)], axis=0).astype(BF16)
        o = _softmax_av(q2, ks[kh].astype(BF16), v[:, kh * HEAD_DIM:(kh + 1) * HEAD_DIM].astype(BF16), ATTN_SCALE)
        for g in range(ATTN_GROUP):
            h = kh * ATTN_GROUP + g
            o_ref[:, h * HEAD_DIM:(h + 1) * HEAD_DIM] = o[g * t:(g + 1) * t].astype(BF16)


def ctx_attention(proj, qg, kg, batch, seq):
    return pl.pallas_call(
        _ctx_attn_kernel,
        grid=(batch,),
        in_specs=[pl.BlockSpec((seq, ATTN_WIDTH), lambda b: (b, COL_Q // ATTN_WIDTH)),
                  pl.BlockSpec((seq, 2 * KV_WIDTH), lambda b: (b, COL_KV // (2 * KV_WIDTH))),
                  pl.BlockSpec((1, HEAD_DIM), lambda b: (0, 0)),
                  pl.BlockSpec((1, HEAD_DIM), lambda b: (0, 0))],
        out_specs=[pl.BlockSpec((seq, ATTN_WIDTH), lambda b: (b, 0)),
                   pl.BlockSpec((seq, KV_WIDTH), lambda b: (b, 0)),
                   pl.BlockSpec((seq, KV_WIDTH), lambda b: (b, 0))],
        out_shape=[jax.ShapeDtypeStruct((batch * seq, ATTN_WIDTH), BF16),
                   jax.ShapeDtypeStruct((batch * seq, KV_WIDTH), F32),
                   jax.ShapeDtypeStruct((batch * seq, KV_WIDTH), F32)],
        compiler_params=_cparams(("parallel",)),
        name="ctx_attention",
    )(proj, proj, qg, kg)


def _rope(x, cos, sin, heads):
    w = x.shape[-1]
    lane = lax.broadcasted_iota(jnp.int32, x.shape, 1)
    quarter = HEAD_DIM // 4
    partner = jnp.where((lane % (2 * quarter)) < quarter,
                        pltpu.roll(x, w - quarter, 1), pltpu.roll(x, quarter, 1))
    cos_t = jnp.concatenate([cos] * heads, axis=1)
    sin_t = jnp.concatenate([sin] * heads, axis=1)
    return x * cos_t + partner * sin_t


def _lat_prep_kernel(q_ref, kv_ref, qg_ref, kg_ref, cos_ref, sin_ref, qo_ref, ko_ref, vo_ref):
    qn = jnp.concatenate(_rms_heads(q_ref[...], qg_ref[...], ATTN_HEADS), axis=1)
    kv = kv_ref[...]
    kn = jnp.concatenate(_rms_heads(kv[:, :KV_WIDTH], kg_ref[...], ATTN_KV_HEADS), axis=1)
    cos = cos_ref[...]
    sin = sin_ref[...]
    qo_ref[...] = (_rope(qn, cos, sin, ATTN_HEADS) * ATTN_SCALE).astype(BF16)
    ko_ref[...] = _rope(kn, cos, sin, ATTN_KV_HEADS).astype(BF16)
    vo_ref[...] = kv[:, KV_WIDTH:].astype(BF16)


def lat_prep(proj, qg, kg, cos, sin, seq, tm=512):
    m = proj.shape[0]
    per = seq // tm
    return pl.pallas_call(
        _lat_prep_kernel,
        grid=(m // tm,),
        in_specs=[pl.BlockSpec((tm, ATTN_WIDTH), lambda i: (i, COL_Q // ATTN_WIDTH)),
                  pl.BlockSpec((tm, 2 * KV_WIDTH), lambda i: (i, COL_KV // (2 * KV_WIDTH))),
                  pl.BlockSpec((1, HEAD_DIM), lambda i: (0, 0)),
                  pl.BlockSpec((1, HEAD_DIM), lambda i: (0, 0)),
                  pl.BlockSpec((tm, HEAD_DIM), lambda i: (i % per, 0)),
                  pl.BlockSpec((tm, HEAD_DIM), lambda i: (i % per, 0))],
        out_specs=[pl.BlockSpec((tm, ATTN_WIDTH), lambda i: (i, 0)),
                   pl.BlockSpec((tm, KV_WIDTH), lambda i: (i, 0)),
                   pl.BlockSpec((tm, KV_WIDTH), lambda i: (i, 0))],
        out_shape=[jax.ShapeDtypeStruct((m, ATTN_WIDTH), BF16),
                   jax.ShapeDtypeStruct((m, KV_WIDTH), BF16),
                   jax.ShapeDtypeStruct((m, KV_WIDTH), BF16)],
        compiler_params=_cparams(("parallel",)),
        name="lat_prep",
    )(proj, proj, qg, kg, cos, sin)


def _lat_attn_kernel(q_ref, k_ref, v_ref, o_ref):
    k = k_ref[0]
    v = v_ref[0]
    for g in range(ATTN_GROUP):
        sl = slice(g * HEAD_DIM, (g + 1) * HEAD_DIM)
        o_ref[:, sl] = _softmax_av(q_ref[:, sl], k, v, None).astype(BF16)


def lat_attention(q, k_all, v_all, seq, tq=256):
    b, s, _ = k_all.shape
    nq = seq // tq
    gw = ATTN_GROUP * HEAD_DIM
    return pl.pallas_call(
        _lat_attn_kernel,
        grid=(b, ATTN_KV_HEADS, nq),
        in_specs=[pl.BlockSpec((tq, gw), lambda bi, kh, qi: (bi * nq + qi, kh)),
                  pl.BlockSpec((1, s, HEAD_DIM), lambda bi, kh, qi: (bi, 0, kh)),
                  pl.BlockSpec((1, s, HEAD_DIM), lambda bi, kh, qi: (bi, 0, kh))],
        out_specs=pl.BlockSpec((tq, gw), lambda bi, kh, qi: (bi * nq + qi, kh)),
        out_shape=jax.ShapeDtypeStruct((b * seq, ATTN_WIDTH), BF16),
        compiler_params=_cparams(("parallel", "parallel", "arbitrary")),
        name="lat_attention",
    )(q, k_all, v_all)


def rope_tables(seq):
    t = jnp.arange(seq)
    row = (t // GRID_W).astype(F32)
    col = (t % GRID_W).astype(F32)
    quarter = HEAD_DIM // 4
    inv_freq = jnp.power(ROPE_THETA, -jnp.arange(quarter, dtype=F32) / quarter)
    ar = row[:, None] * inv_freq[None, :]
    ac = col[:, None] * inv_freq[None, :]
    cos = jnp.concatenate([jnp.cos(ar), jnp.cos(ar), jnp.cos(ac), jnp.cos(ac)], axis=1)
    sin = jnp.concatenate([-jnp.sin(ar), jnp.sin(ar), -jnp.sin(ac), jnp.sin(ac)], axis=1)
    return cos, sin


def _chunk_index(d, i, n):
    return i + d * (n - 1 - 2 * i)


def _lru_kernel(x_ref, xp_ref, xn_ref, cw_ref, cb_ref, wg_ref, bg_ref, sp_ref, h0_ref,
                h_ref, fin_ref, xpad_ref, a_ref, b_ref, hc_ref, *, nb, tt):
    d = pl.program_id(0)
    i = pl.program_id(2)
    n = pl.num_programs(2)
    ci = _chunk_index(d, i, n)
    pitch = tt + PITCH_PAD
    rows = nb * pitch
    nsl = LRU_WIDTH // LANE

    @pl.when(i == 0)
    def _():
        xpad_ref[...] = jnp.zeros_like(xpad_ref)
        hc_ref[...] = h0_ref[0]

    for s in range(nb):
        base = 8 + s * pitch
        xpad_ref[base:base + tt, :] = x_ref[s]
        xpad_ref[base - 1:base, :] = jnp.where(ci == 0, 0.0, xp_ref[s, 7:8, :])
        xpad_ref[base + tt:base + tt + 2, :] = jnp.where(ci == n - 1, 0.0, xn_ref[s, 0:2, :])
    w = cw_ref[...]
    xc = cb_ref[...] + sum(xpad_ref[7 + j:7 + j + rows, :] * w[j:j + 1, :] for j in range(CONV_W))
    pre = _dot(xc.astype(BF16), wg_ref[0]) + bg_ref[0]
    r = jax.nn.sigmoid(pre[:, :LRU_WIDTH])
    ig = jax.nn.sigmoid(pre[:, LRU_WIDTH:])
    a = jnp.exp((-LRU_C) * r * sp_ref[0])
    inp = jnp.sqrt(1.0 - a * a) * (ig * xc)
    for c in range(nsl):
        a_ref[c] = a[:, c * LANE:(c + 1) * LANE]
        b_ref[c] = inp[:, c * LANE:(c + 1) * LANE]

    def body(t, carry):
        row = t + d * (tt - 1 - 2 * t)
        out = []
        for c in range(nsl):
            idx = (c, pl.ds(row, nb, stride=pitch), slice(None))
            h = a_ref[idx] * carry[c] + b_ref[idx]
            b_ref[idx] = h
            out.append(h)
        return tuple(out)

    carry = tuple(hc_ref[:, c * LANE:(c + 1) * LANE] for c in range(nsl))
    carry = lax.fori_loop(0, tt, body, carry, unroll=8)
    for c in range(nsl):
        hc_ref[:, c * LANE:(c + 1) * LANE] = carry[c]
        for s in range(nb):
            h_ref[0, s, :, c * LANE:(c + 1) * LANE] = b_ref[c, s * pitch:s * pitch + tt, :]

    @pl.when(i == n - 1)
    def _():
        fin_ref[0] = hc_ref[...]


def lru_scan(proj3, h0, cw, cb, wg, bg, sp, nb, tt=128):
    b, t, _ = proj3.shape
    w = LRU_WIDTH
    n = t // tt
    t8 = tt // 8
    rows = nb * (tt + PITCH_PAD)
    col = COL_LX // w

    def cidx(d, i):
        return _chunk_index(d, i, n)

    return pl.pallas_call(
        functools.partial(_lru_kernel, nb=nb, tt=tt),
        grid=(2, b // nb, n),
        in_specs=[pl.BlockSpec((nb, tt, w), lambda d, g, i: (g, cidx(d, i), col)),
                  pl.BlockSpec((nb, 8, w), lambda d, g, i: (g, jnp.maximum(cidx(d, i) * t8 - 1, 0), col)),
                  pl.BlockSpec((nb, 8, w), lambda d, g, i: (g, jnp.minimum((cidx(d, i) + 1) * t8, t // 8 - 1), col)),
                  pl.BlockSpec((CONV_W, w), lambda d, g, i: (0, 0)),
                  pl.BlockSpec((1, w), lambda d, g, i: (0, 0)),
                  pl.BlockSpec((1, w, 2 * w), lambda d, g, i: (d, 0, 0)),
                  pl.BlockSpec((1, 1, 2 * w), lambda d, g, i: (d, 0, 0)),
                  pl.BlockSpec((1, 1, w), lambda d, g, i: (d, 0, 0)),
                  pl.BlockSpec((1, nb, w), lambda d, g, i: (d, g, 0))],
        out_specs=[pl.BlockSpec((1, nb, tt, w), lambda d, g, i: (d, g, cidx(d, i), 0)),
                   pl.BlockSpec((1, nb, w), lambda d, g, i: (d, g, 0))],
        out_shape=[jax.ShapeDtypeStruct((2, b, t, w), F32), jax.ShapeDtypeStruct((2, b, w), F32)],
        scratch_shapes=[pltpu.VMEM((rows + 16, w), F32),
                        pltpu.VMEM((w // LANE, rows, LANE), F32),
                        pltpu.VMEM((w // LANE, rows, LANE), F32),
                        pltpu.VMEM((nb, w), F32)],
        compiler_params=_cparams(("parallel", "parallel", "arbitrary")),
        name="lru_scan",
    )(proj3, proj3, proj3, cw, cb, wg, bg, sp, h0)


def _s5_kernel(u_ref, bb_ref, cc_ref, a_ref, h0_ref, y_ref, fin_ref, up_ref, s_ref, hc_ref, *, nb, tt):
    d = pl.program_id(0)
    i = pl.program_id(2)
    n = pl.num_programs(2)
    pitch = tt + PITCH_PAD
    nsl = S5_NSTATE // LANE

    @pl.when(i == 0)
    def _():
        up_ref[...] = jnp.zeros_like(up_ref)
        hc_ref[...] = h0_ref[0]

    for s in range(nb):
        up_ref[s * pitch:s * pitch + tt, :] = u_ref[s]
    u2 = up_ref[...].astype(BF16)
    per_k = MXU // S5_CH * S5_STATE // MXU
    vs = SUBLANES // nb
    rows_seq = nb * pitch
    npk = nsl // vs

    def slab(c):
        return (c // vs, slice((c % vs) * rows_seq, (c % vs + 1) * rows_seq), slice(None))

    for nt in range(2 * S5_NSTATE // MXU):
        kt = (nt % (S5_NSTATE // MXU)) // per_k
        tile = _dot(u2[:, kt * MXU:(kt + 1) * MXU], bb_ref[0, kt * MXU:(kt + 1) * MXU, nt * MXU:(nt + 1) * MXU])
        s_ref[slab(2 * nt)] = tile[:, :LANE]
        s_ref[slab(2 * nt + 1)] = tile[:, LANE:]

    a_all = a_ref[0]

    def packed_rows(src, off, j):
        parts = [jnp.broadcast_to(src[:, off + (j * vs + v) * LANE:off + (j * vs + v + 1) * LANE], (nb, LANE))
                 for v in range(vs)]
        return parts[0] if vs == 1 else jnp.concatenate(parts, axis=0)

    group = 8
    for j0 in range(0, npk, group):
        js = list(range(j0, min(j0 + group, npk)))
        ar = [packed_rows(a_all, 0, j) for j in js]
        ai = [packed_rows(a_all, S5_NSTATE, j) for j in js]

        def body(t, carry, js=js, ar=ar, ai=ai):
            row = t + d * (tt - 1 - 2 * t)
            out = []
            for k, j in enumerate(js):
                hr, hi = carry[2 * k], carry[2 * k + 1]
                ire = (j, pl.ds(row, SUBLANES, stride=pitch), slice(None))
                iim = (npk + j, pl.ds(row, SUBLANES, stride=pitch), slice(None))
                nr = ar[k] * hr - ai[k] * hi + s_ref[ire]
                ni = ar[k] * hi + ai[k] * hr + s_ref[iim]
                s_ref[ire] = nr
                s_ref[iim] = ni
                out += [nr, ni]
            return tuple(out)

        carry = []
        for j in js:
            carry += [packed_rows(hc_ref, 0, j), packed_rows(hc_ref, S5_NSTATE, j)]
        carry = lax.fori_loop(0, tt, body, tuple(carry), unroll=4)
        for k, j in enumerate(js):
            for v in range(vs):
                c = j * vs + v
                hc_ref[:, c * LANE:(c + 1) * LANE] = carry[2 * k][v * nb:(v + 1) * nb]
                hc_ref[:, S5_NSTATE + c * LANE:S5_NSTATE + (c + 1) * LANE] = carry[2 * k + 1][v * nb:(v + 1) * nb]

    n_out = S5_WIDTH // MXU
    per_n = nsl // n_out
    for nt in range(n_out):
        slabs = [per_n * nt + k for k in range(per_n)] + [nsl + per_n * nt + k for k in range(per_n)]
        lhs = jnp.concatenate([s_ref[slab(c)] for c in slabs], axis=1).astype(BF16)
        y = _dot(lhs, cc_ref[0, nt])
        for s in range(nb):
            y_ref[0, s, :, nt * MXU:(nt + 1) * MXU] = y[s * pitch:s * pitch + tt]

    @pl.when(i == n - 1)
    def _():
        fin_ref[0] = hc_ref[...]


def s5_scan(proj3, h0, bb, cc, a, nb, tt=128):
    b, t, _ = proj3.shape
    w = S5_WIDTH
    n = t // tt
    rows = nb * (tt + PITCH_PAD)
    ns2 = 2 * S5_NSTATE

    def cidx(d, i):
        return _chunk_index(d, i, n)

    return pl.pallas_call(
        functools.partial(_s5_kernel, nb=nb, tt=tt),
        grid=(2, b // nb, n),
        in_specs=[pl.BlockSpec((nb, tt, w), lambda d, g, i: (g, cidx(d, i), COL_SU // w)),
                  pl.BlockSpec((1, w, ns2), lambda d, g, i: (d, 0, 0)),
                  pl.BlockSpec((1,) + cc.shape[1:], lambda d, g, i: (d, 0, 0, 0)),
                  pl.BlockSpec((1, 1, ns2), lambda d, g, i: (d, 0, 0)),
                  pl.BlockSpec((1, nb, ns2), lambda d, g, i: (d, g, 0))],
        out_specs=[pl.BlockSpec((1, nb, tt, w), lambda d, g, i: (d, g, cidx(d, i), 0)),
                   pl.BlockSpec((1, nb, ns2), lambda d, g, i: (d, g, 0))],
        out_shape=[jax.ShapeDtypeStruct((2, b, t, w), F32), jax.ShapeDtypeStruct((2, b, ns2), F32)],
        scratch_shapes=[pltpu.VMEM((rows, w), F32),
                        pltpu.VMEM((ns2 // LANE // (SUBLANES // nb), SUBLANES // nb * rows, LANE), F32),
                        pltpu.VMEM((nb, ns2), F32)],
        compiler_params=_cparams(("parallel", "parallel", "arbitrary")),
        name="s5_scan",
    )(proj3, bb, cc, a, h0)


def _split3(x):
    x1 = x.astype(BF16)
    r1 = x - x1.astype(F32)
    x2 = r1.astype(BF16)
    x3 = (r1 - x2.astype(F32)).astype(BF16)
    return x1, x2, x3


def _mm(x, y):
    return _dot(x.astype(BF16), y.astype(BF16))


def _unit_tri_inverses(mats, rr, cc):
    same16 = (rr // 16) == (cc // 16)
    same32 = (rr // 32) == (cc // 32)
    p = [jnp.where(same16, -a, 0.0) for a in mats]
    t = [jnp.where(rr == cc, 1.0, x) for x in p]
    for _ in range(3):
        p = [_mm(x, x) for x in p]
        t = [x + _mm(x, y) for x, y in zip(t, p)]
    for mask in (same32 & jnp.logical_not(same16), jnp.logical_not(same32)):
        te = [_mm(x, jnp.where(mask, a, 0.0)) for x, a in zip(t, mats)]
        t = [x - _mm(y, x) for x, y in zip(t, te)]
    return t


def _dn_prep_kernel(x_ref, xp_ref, xn_ref, cw_ref, o_ref, xpad_ref, *, tb):
    i = pl.program_id(1)
    n = pl.num_programs(1)
    hw = DN_HEADS * DN_DK
    xpad_ref[8:8 + tb, :] = x_ref[0]
    xpad_ref[0:8, :] = jnp.where(i == 0, 0.0, xp_ref[0])
    xpad_ref[8 + tb:16 + tb, :] = jnp.where(i == n - 1, 0.0, xn_ref[0])
    w = cw_ref[...]
    xc = sum(xpad_ref[7 + j:7 + j + tb, :] * w[j:j + 1, :] for j in range(CONV_W))
    qkv = xc * jax.nn.sigmoid(xc)
    for h in range(DN_HEADS):
        q = qkv[:, h * DN_DK:(h + 1) * DN_DK]
        k = qkv[:, hw + h * DN_DK:hw + (h + 1) * DN_DK]
        o_ref[0, :, h * DN_DK:(h + 1) * DN_DK] = (
            q * lax.rsqrt(jnp.sum(q * q, axis=-1, keepdims=True) + EPS) * (DN_DK ** -0.5))
        o_ref[0, :, hw + h * DN_DK:hw + (h + 1) * DN_DK] = (
            k * lax.rsqrt(jnp.sum(k * k, axis=-1, keepdims=True) + EPS))
    o_ref[0, :, 2 * hw:] = qkv[:, 2 * hw:]


def dn_prep(proj3, cw, tb=256):
    b, t, _ = proj3.shape
    n = t // tb
    t8 = tb // 8
    w3 = 3 * DN_WIDTH
    cq = COL_DQ // w3
    return pl.pallas_call(
        functools.partial(_dn_prep_kernel, tb=tb),
        grid=(b, n),
        in_specs=[pl.BlockSpec((1, tb, w3), lambda bi, i: (bi, i, cq)),
                  pl.BlockSpec((1, 8, w3), lambda bi, i: (bi, jnp.maximum(i * t8 - 1, 0), cq)),
                  pl.BlockSpec((1, 8, w3), lambda bi, i: (bi, jnp.minimum((i + 1) * t8, t // 8 - 1), cq)),
                  pl.BlockSpec((CONV_W, w3), lambda bi, i: (0, 0))],
        out_specs=pl.BlockSpec((1, tb, w3), lambda bi, i: (bi, i, 0)),
        out_shape=jax.ShapeDtypeStruct((b, t, w3), F32),
        scratch_shapes=[pltpu.VMEM((tb + 16, w3), F32)],
        compiler_params=_cparams(("parallel", "parallel")),
        name="dn_prep",
    )(proj3, proj3, proj3, cw)


def _dn_kernel(xf_ref, baf_ref, xb_ref, bab_ref, al_ref, dtb_ref, s0_ref, of_ref, ob_ref, fin_ref, s_ref,
               *, nbat, tb):
    i = pl.program_id(1)
    n = pl.num_programs(1)
    ch = DN_CHUNK
    nc = tb // ch
    hw = DN_HEADS * DN_DK

    @pl.when(i == 0)
    def _():
        s_ref[...] = s0_ref[...]

    rb = lax.broadcasted_iota(jnp.int32, (tb, tb), 0)
    cb = lax.broadcasted_iota(jnp.int32, (tb, tb), 1)
    rr = lax.broadcasted_iota(jnp.int32, (ch, ch), 0)
    cc = lax.broadcasted_iota(jnp.int32, (ch, ch), 1)
    al = al_ref[...]
    dtb = dtb_ref[...]

    x_refs = (xf_ref, xb_ref)
    beta_tab, gc_tab, masks = {}, {}, {}
    for d in range(2):
        reverse = d == 1
        ba_ref = bab_ref if reverse else baf_ref
        tri = ((rb // ch) == (cb // ch)) & ((cb >= rb) if reverse else (cb <= rb))
        tri = jnp.where(tri, 1.0, 0.0).astype(BF16)
        masks[d] = ((cc >= rr) if reverse else (cc <= rr), (cc > rr) if reverse else (cc < rr))
        for bat in range(nbat):
            ba = ba_ref[bat]
            beta_tab[(d, bat)] = jax.nn.sigmoid(ba)
            g_all = -jnp.exp(al) * _softplus(ba + dtb)
            gc_tab[(d, bat)] = sum(_dot(tri, piece) for piece in _split3(g_all))
    units = [dict(d=d, bat=bat, c=c, h=h) for d in range(2) for bat in range(nbat) for c in range(nc)
             for h in range(DN_HEADS)]

    def rows_of(u):
        return slice(u['c'] * ch, (u['c'] + 1) * ch)

    def q_of(u):
        return x_refs[u['d']][u['bat'], rows_of(u), u['h'] * DN_DK:(u['h'] + 1) * DN_DK]

    def k_of(u):
        return x_refs[u['d']][u['bat'], rows_of(u), hw + u['h'] * DN_DK:hw + (u['h'] + 1) * DN_DK]

    def v_of(u):
        return x_refs[u['d']][u['bat'], rows_of(u), 2 * hw + u['h'] * DN_DV:2 * hw + (u['h'] + 1) * DN_DV]

    def beta_of(u):
        lane = u['d'] * DN_HEADS + u['h']
        return beta_tab[(u['d'], u['bat'])][rows_of(u), lane:lane + 1]

    def gcol_of(u):
        lane = 2 * DN_HEADS + u['d'] * DN_HEADS + u['h']
        return gc_tab[(u['d'], u['bat'])][rows_of(u), lane:lane + 1]

    def glast_of(u):
        gcol = gcol_of(u)
        return gcol[0:1, :] if u['d'] == 1 else gcol[ch - 1:ch, :]

    gct = {}
    for u in units:
        key = (u['d'], u['bat'], u['c'])
        if key not in gct:
            gct[key] = gc_tab[(u['d'], u['bat'])][rows_of(u), :].T
        lane = 2 * DN_HEADS + u['d'] * DN_HEADS + u['h']
        incl, strict = masks[u['d']]
        decay = jnp.where(incl, jnp.exp(gcol_of(u) - gct[key][lane:lane + 1, :]), 0.0)
        k = k_of(u)
        kbf = k.astype(BF16)
        u['a'] = jnp.where(strict, _dot_nt((k * beta_of(u)).astype(BF16), kbf) * decay, 0.0)
        u['qk'] = (_dot_nt(q_of(u).astype(BF16), kbf) * decay).astype(BF16)
    t_inv = _unit_tri_inverses([u['a'] for u in units], rr, cc)
    for u, t in zip(units, t_inv):
        beta = beta_of(u)
        rhs = jnp.concatenate([v_of(u) * beta, k_of(u) * beta * jnp.exp(gcol_of(u))], axis=1)
        sol = rhs + _mm(jnp.where(rr == cc, 0.0, t), rhs)
        u['u_val'] = sol[:, :DN_DV]
        u['w'] = sol[:, DN_DV:].astype(BF16)

    state = {(d, bat, h): s_ref[d, bat, h] for d in range(2) for bat in range(nbat) for h in range(DN_HEADS)}
    for j in range(nc):
        cur = [u for u in units if u['c'] == (nc - 1 - j if u['d'] == 1 else j)]
        keys = [(u['d'], u['bat'], u['h']) for u in cur]
        sb = [state[key].astype(BF16) for key in keys]
        ws = [_dot(u['w'], s) for u, s in zip(cur, sb)]
        qs = [_dot((q_of(u) * jnp.exp(gcol_of(u))).astype(BF16), s) for u, s in zip(cur, sb)]
        vb = [(u['u_val'] - x).astype(BF16) for u, x in zip(cur, ws)]
        os = [x + _dot(u['qk'], y) for u, x, y in zip(cur, qs, vb)]
        kdec_t = [(k_of(u) * jnp.exp(glast_of(u) - gcol_of(u))).T.astype(BF16) for u in cur]
        sn = [state[key] * jnp.exp(glast_of(u)) + _dot(kt, y) for u, key, kt, y in zip(cur, keys, kdec_t, vb)]
        for u, key, o, s in zip(cur, keys, os, sn):
            state[key] = s
            o_ref = ob_ref if u['d'] == 1 else of_ref
            o_ref[u['bat'], u['c'] * ch:(u['c'] + 1) * ch, u['h'] * DN_DV:(u['h'] + 1) * DN_DV] = o
    for (d, bat, h), s in state.items():
        s_ref[d, bat, h] = s

    @pl.when(i == n - 1)
    def _():
        fin_ref[...] = s_ref[...]


def dn_scan(qkv3, ba3, s0, al, dtb, nbat, tb):
    b, t, w3 = qkv3.shape
    n = t // tb

    def data_specs(cidx):
        return [pl.BlockSpec((nbat, tb, w3), lambda g, i: (g, cidx(i), 0)),
                pl.BlockSpec((nbat, tb, LANE), lambda g, i: (g, cidx(i), 0))]

    fwd = lambda i: i
    bwd = lambda i: n - 1 - i
    st_spec = pl.BlockSpec((2, nbat, DN_HEADS, DN_DK, DN_DV), lambda g, i: (0, g, 0, 0, 0))
    return pl.pallas_call(
        functools.partial(_dn_kernel, nbat=nbat, tb=tb),
        grid=(b // nbat, n),
        in_specs=data_specs(fwd) + data_specs(bwd) + [
            pl.BlockSpec((1, LANE), lambda g, i: (0, 0)),
            pl.BlockSpec((1, LANE), lambda g, i: (0, 0)),
            st_spec],
        out_specs=[pl.BlockSpec((nbat, tb, DN_WIDTH), lambda g, i: (g, i, 0)),
                   pl.BlockSpec((nbat, tb, DN_WIDTH), lambda g, i: (g, n - 1 - i, 0)),
                   st_spec],
        out_shape=[jax.ShapeDtypeStruct((b, t, DN_WIDTH), F32),
                   jax.ShapeDtypeStruct((b, t, DN_WIDTH), F32),
                   jax.ShapeDtypeStruct((2, b, DN_HEADS, DN_DK, DN_DV), F32)],
        scratch_shapes=[pltpu.VMEM((2, nbat, DN_HEADS, DN_DK, DN_DV), F32)],
        compiler_params=_cparams(("parallel", "arbitrary")),
        name="dn_scan",
    )(qkv3, ba3, qkv3, ba3, al, dtb, s0)


def _post_kernel(hf_ref, hb_ref, lg_ref, of_ref, ob_ref, dz_ref, ng_ref, yf_ref, yb_ref, su_ref, sd_ref,
                 gw_ref, gb_ref, lru_ref, dn_ref, s5_ref):
    lru_ref[...] = ((hf_ref[0, 0] + hb_ref[0, 0]) * _gelu(lg_ref[...])).astype(BF16)
    o = of_ref[0] + ob_ref[0]
    dz = dz_ref[...]
    for h in range(DN_HEADS):
        sl = slice(h * DN_DV, (h + 1) * DN_DV)
        oh = o[:, sl]
        oh = oh * lax.rsqrt(jnp.mean(oh * oh, axis=-1, keepdims=True) + EPS) * ng_ref[...]
        zh = dz[:, sl]
        dn_ref[:, sl] = (oh * (zh * jax.nn.sigmoid(zh))).astype(BF16)
    y = yf_ref[0, 0] + yb_ref[0, 0] + sd_ref[...] * su_ref[...]
    gy = _gelu(y)
    s5_ref[...] = (gy * jax.nn.sigmoid(_dot(gy.astype(BF16), gw_ref[...]) + gb_ref[...])).astype(BF16)


def mix_post(proj, h_lru, o_f, o_b, y_s5, ng, sd, gw, gb):
    m = proj.shape[0]
    _, b, t, w = h_lru.shape
    tm = min(512, t)
    per = t // tm
    tok = pl.BlockSpec((tm, w), lambda i: (i, 0))

    def pcol(c0):
        return pl.BlockSpec((tm, w), lambda i: (i, c0 // w))

    def dirspec(d):
        return pl.BlockSpec((1, 1, tm, w), lambda i: (d, i // per, i % per, 0))

    seq = pl.BlockSpec((1, tm, w), lambda i: (i // per, i % per, 0))

    def vec(n):
        return pl.BlockSpec((1, n), lambda i: (0, 0))

    out = jax.ShapeDtypeStruct((m, w), BF16)
    return pl.pallas_call(
        _post_kernel,
        grid=(m // tm,),
        in_specs=[dirspec(0), dirspec(1), pcol(COL_LG), seq, seq, pcol(COL_DZ), vec(DN_DV),
                  dirspec(0), dirspec(1), pcol(COL_SU), vec(w), pl.BlockSpec((w, w), lambda i: (0, 0)), vec(w)],
        out_specs=[tok, tok, tok],
        out_shape=[out, out, out],
        compiler_params=_cparams(("parallel",)),
        name="mix_post",
    )(h_lru, h_lru, proj, o_f, o_b, proj, ng, y_s5, y_s5, proj, sd, gw, gb)


def _out_kernel(x_ref, a_ref, b_ref, c_ref, d_ref, w_ref, gate_ref, g_ref, bb_ref, x1_ref):
    q = MIXW
    for r0 in range(0, x_ref.shape[0], OUT_SUB):
        rows = slice(r0, r0 + OUT_SUB)
        y = _dot(a_ref[rows, :], w_ref[0:q, :])
        y += _dot(b_ref[rows, :], w_ref[q:2 * q, :])
        y += _dot(c_ref[rows, :], w_ref[2 * q:3 * q, :])
        y += _dot(d_ref[rows, :], w_ref[3 * q:, :])
        z = DEEPNORM_ALPHA * x_ref[rows, :] + gate_ref[0] * y
        x1_ref[rows, :] = _ln(z) * g_ref[...] + bb_ref[...]


def out_proj(x, parts, w, layer, mod, ln_g, ln_b, mod_row, tm=512):
    m, d = x.shape
    vec = pl.BlockSpec((1, d), lambda i: (0, 0))
    part = pl.BlockSpec((tm, MIXW), lambda i: (i, 0))
    return pl.pallas_call(
        _out_kernel,
        grid=(m // tm,),
        in_specs=[pl.BlockSpec((tm, d), lambda i: (i, 0)), part, part, part, part,
                  pl.BlockSpec((None, d, d), lambda i: (layer, 0, 0)),
                  pl.BlockSpec((1, 1, d), lambda i: (2 * MOD_ROWS + mod_row(i * tm), 0, 0)), vec, vec],
        out_specs=pl.BlockSpec((tm, d), lambda i: (i, 0)),
        out_shape=jax.ShapeDtypeStruct((m, d), F32),
        compiler_params=_cparams(("parallel",)),
        name="out_proj",
    )(x, *parts, w, mod, ln_g, ln_b)


def _mlp_kernel(x_ref, w1_ref, w2_ref, sc_ref, sh_ref, gate_ref, g_ref, b_ref, o_ref, u_ref, acc_ref):
    f = pl.program_id(1)
    last = pl.num_programs(1) - 1
    tm = x_ref.shape[0]

    def hidden(u):
        return jnp.square(jnp.maximum(_dot(u, w1_ref[...]), 0.0)).astype(BF16)

    @pl.when(f == 0)
    def _():
        for r0 in range(0, tm, MLP_SUB):
            rows = slice(r0, r0 + MLP_SUB)
            u = (_ln(x_ref[rows, :]) * (1.0 + sc_ref[0]) + sh_ref[0]).astype(BF16)
            u_ref[rows, :] = u
            acc_ref[rows, :] = _dot(hidden(u), w2_ref[...])

    @pl.when((f > 0) & (f < last))
    def _():
        acc_ref[...] += _dot(hidden(u_ref[...]), w2_ref[...])

    @pl.when(f == last)
    def _():
        for r0 in range(0, tm, MLP_SUB):
            rows = slice(r0, r0 + MLP_SUB)
            a = acc_ref[rows, :] + _dot(hidden(u_ref[rows, :]), w2_ref[...])
            z = DEEPNORM_ALPHA * x_ref[rows, :] + gate_ref[0] * a
            o_ref[rows, :] = _ln(z) * g_ref[...] + b_ref[...]


def mlp(x, w1, w2, layer, mod, ln_g, ln_b, mod_row, tm=512, tf=1024):
    m, d = x.shape
    ff = w1.shape[2]
    assert ff // tf >= 2
    vec = pl.BlockSpec((1, d), lambda i, f: (0, 0))

    def mod_spec(kind):
        return pl.BlockSpec((1, 1, d), lambda i, f: (kind * MOD_ROWS + mod_row(i * tm), 0, 0))

    return pl.pallas_call(
        _mlp_kernel,
        grid=(m // tm, ff // tf),
        in_specs=[pl.BlockSpec((tm, d), lambda i, f: (i, 0)),
                  pl.BlockSpec((None, d, tf), lambda i, f: (layer, 0, f)),
                  pl.BlockSpec((None, tf, d), lambda i, f: (layer, f, 0)),
                  mod_spec(4), mod_spec(3), mod_spec(5), vec, vec],
        out_specs=pl.BlockSpec((tm, d), lambda i, f: (i, 0)),
        out_shape=jax.ShapeDtypeStruct((m, d), F32),
        scratch_shapes=[pltpu.VMEM((tm, d), BF16), pltpu.VMEM((tm, d), F32)],
        compiler_params=_cparams(("parallel", "arbitrary")),
        name="mlp",
    )(x, w1, w2, mod, mod, mod, ln_g, ln_b)


def _reorder_w_in(w):
    def cols(c0, n):
        return w[..., c0:c0 + n]

    aq, akv, lx, lg = cols(0, 512), cols(512, 512), cols(1024, 512), cols(1536, 512)
    dqkv, dz, ba, su = cols(2048, 1536), cols(3584, 512), cols(4096, N_BA), cols(4096 + N_BA, 512)
    pad = jnp.zeros(w.shape[:2] + (LANE - N_BA,), w.dtype)
    main = jnp.concatenate([dqkv, dz, aq, akv, lx, lg, su], axis=-1).astype(BF16)
    return main, jnp.concatenate([ba, pad], axis=-1).astype(BF16)


def _block_diag(blocks):
    n, r, c = blocks.shape[-3:]
    eye = jnp.eye(n, dtype=blocks.dtype)
    out = blocks[..., :, :, None, :] * eye[:, None, :, None]
    return out.reshape(blocks.shape[:-3] + (n * r, n * c))


def _lru_params(wa, ba, wx, bx, lam):
    wg = jnp.concatenate([_block_diag(wa.astype(BF16)), _block_diag(wx.astype(BF16))], axis=-1)
    bg = jnp.concatenate([ba, bx], axis=-1)[..., None, :]
    sp = jax.nn.softplus(-lam)[..., None, :]
    return wg, bg, sp


def _s5_params(lam_re, lam_im, log_dt, b_re, b_im, c_re, c_im):
    lead = lam_re.shape[:-2]
    dt = jnp.exp(log_dt)[..., None]
    mag = jnp.exp(lam_re * dt)
    abar_re = mag * jnp.cos(lam_im * dt)
    abar_im = mag * jnp.sin(lam_im * dt)
    den = lam_re * lam_re + lam_im * lam_im
    nr = abar_re - 1.0
    ni = abar_im
    f_re = (nr * lam_re + ni * lam_im) / den
    f_im = (ni * lam_re - nr * lam_im) / den
    bb_re = f_re[..., None] * b_re - f_im[..., None] * b_im
    bb_im = f_re[..., None] * b_im + f_im[..., None] * b_re
    to_in = lambda m: _block_diag(jnp.swapaxes(m, -1, -2).astype(BF16))
    bb = jnp.concatenate([to_in(bb_re), to_in(bb_im)], axis=-1)
    n_out = S5_WIDTH // MXU
    gpt = S5_GROUPS // n_out

    def to_out(m, nt):
        return _block_diag(jnp.swapaxes(m[..., nt * gpt:(nt + 1) * gpt, :, :], -1, -2).astype(BF16))

    cc = jnp.stack([jnp.concatenate([to_out(c_re, nt), to_out(-c_im, nt)], axis=-2)
                    for nt in range(n_out)], axis=-3)
    a = jnp.concatenate([abar_re.reshape(lead + (1, -1)), abar_im.reshape(lead + (1, -1))], axis=-1)
    return bb, cc, a


def _lane_row(vals, offset):
    return jnp.zeros((1, LANE), F32).at[0, offset:offset + vals.size].set(vals.reshape(-1))


def kernel(x_prompt, x_sample, cache_attn_k, cache_attn_v, state_rglru, state_delta, state_s5_re, state_s5_im, c, c_ctx, w_ada, b_ada, w_in, w_out, ln1_g, ln1_b, ln2_g, ln2_b, w_mlp1, w_mlp2, q_norm_g, k_norm_g, lru_conv_w, lru_conv_b, lru_wa, lru_ba, lru_wx, lru_bx, lru_lambda, dn_conv_w, dn_a_log, dn_dt_bias, dn_norm_g, s5_lambda_re, s5_lambda_im, s5_log_dt, s5_b_re, s5_b_im, s5_c_re, s5_c_im, s5_d, s5_glu_w, s5_glu_b):
    bp, tp, d = x_prompt.shape
    bs, ts, _ = x_sample.shape
    ctx_row = bs

    cond = jnp.concatenate([c, c_ctx[None, :], jnp.zeros((MOD_ROWS - bs - 1, d), F32)], axis=0)
    mods = ada_mod(cond, w_ada, b_ada)
    w_in_r, w_ba = _reorder_w_in(w_in)
    w_out_b = w_out.astype(BF16)
    w1_b = w_mlp1.astype(BF16)
    w2_b = w_mlp2.astype(BF16)
    cos, sin = rope_tables(ts)
    wg_all, bg_all, sp_all = _lru_params(lru_wa, lru_ba, lru_wx, lru_bx, lru_lambda)
    bb_all, cc_all, a5_all = _s5_params(s5_lambda_re, s5_lambda_im, s5_log_dt, s5_b_re, s5_b_im, s5_c_re, s5_c_im)

    streams = {
        'ctx': dict(x=x_prompt.reshape(bp * tp, d), b=bp, t=tp, nb=8, s5_tt=128, dn_nbat=2, dn_tb=tp,
                    mod_row=lambda tok: ctx_row),
        'lat': dict(x=x_sample.reshape(bs * ts, d), b=bs, t=ts, nb=bs, s5_tt=256, dn_nbat=bs, dn_tb=2 * DN_CHUNK,
                    mod_row=lambda tok: tok // ts),
    }
    ks, vs, lrus, dns, s5rs, s5is = [], [], [], [], [], []
    for l in range(DEPTH):
        mod = mods[l].reshape(MOD_ROWS, N_MOD, d).transpose(1, 0, 2).reshape(N_MOD * MOD_ROWS, 1, d)
        qg = q_norm_g[l].reshape(1, HEAD_DIM)
        kg = k_norm_g[l].reshape(1, HEAD_DIM)
        wg, bg, sp = wg_all[l], bg_all[l], sp_all[l]
        bb, cc, a5 = bb_all[l], cc_all[l], a5_all[l]
        al = _lane_row(dn_a_log[l], 2 * DN_HEADS)
        dtb = _lane_row(dn_dt_bias[l], 2 * DN_HEADS)
        for name, st in streams.items():
            b, t, nb = st['b'], st['t'], st['nb']
            is_ctx = name == 'ctx'
            proj, ba = in_proj(st['x'], mod, w_in_r, w_ba, l, st['mod_row'])
            proj3 = proj.reshape(b, t, N_PROJ)
            ba3 = ba.reshape(b, t, LANE)

            if is_ctx:
                attn, kn, vv = ctx_attention(proj, qg, kg, b, t)
                ks.append(kn.reshape(b, t, ATTN_KV_HEADS, HEAD_DIM))
                vs.append(vv.reshape(b, t, ATTN_KV_HEADS, HEAD_DIM))
                h0_lru = jnp.zeros((2, b, LRU_WIDTH), F32)
                s0_dn = jnp.zeros((2, b, DN_HEADS, DN_DK, DN_DV), F32)
                h0_s5 = jnp.zeros((2, b, 2 * S5_NSTATE), F32)
            else:
                q_s, k_s, v_s = lat_prep(proj, qg, kg, cos, sin, t)
                k_all = jnp.concatenate([cache_attn_k[:, l].reshape(b, -1, KV_WIDTH).astype(BF16),
                                         k_s.reshape(b, t, KV_WIDTH)], axis=1)
                v_all = jnp.concatenate([cache_attn_v[:, l].reshape(b, -1, KV_WIDTH).astype(BF16),
                                         v_s.reshape(b, t, KV_WIDTH)], axis=1)
                attn = lat_attention(q_s, k_all, v_all, t)
                h0_lru = jnp.swapaxes(state_rglru[:, l], 0, 1)
                s0_dn = jnp.swapaxes(state_delta[:, l], 0, 1)
                h0_s5 = jnp.swapaxes(jnp.concatenate([state_s5_re[:, l].reshape(b, 2, S5_NSTATE),
                                                      state_s5_im[:, l].reshape(b, 2, S5_NSTATE)], axis=-1), 0, 1)

            h_lru, lru_fin = lru_scan(proj3, h0_lru, lru_conv_w[l], lru_conv_b[l].reshape(1, -1), wg, bg, sp, nb)
            o_f, o_b, dn_fin = dn_scan(dn_prep(proj3, dn_conv_w[l]), ba3, s0_dn, al, dtb,
                                       st['dn_nbat'], st['dn_tb'])
            y_s5, s5_fin = s5_scan(proj3, h0_s5, bb, cc, a5, nb, st['s5_tt'])
            lru_out, dn_out, s5_out = mix_post(
                proj, h_lru, o_f, o_b, y_s5, dn_norm_g[l].reshape(1, DN_DV), s5_d[l].reshape(1, MIXW),
                s5_glu_w[l].astype(BF16), s5_glu_b[l].reshape(1, MIXW))
            if is_ctx:
                lrus.append(jnp.swapaxes(lru_fin, 0, 1))
                dns.append(jnp.swapaxes(dn_fin, 0, 1))
                s5_fin = jnp.swapaxes(s5_fin, 0, 1)
                s5rs.append(s5_fin[..., :S5_NSTATE].reshape(b, 2, S5_GROUPS, S5_STATE))
                s5is.append(s5_fin[..., S5_NSTATE:].reshape(b, 2, S5_GROUPS, S5_STATE))

            x1 = out_proj(st['x'], [attn, lru_out, dn_out, s5_out], w_out_b, l, mod,
                          ln1_g[l].reshape(1, d), ln1_b[l].reshape(1, d), st['mod_row'])
            st['x'] = mlp(x1, w1_b, w2_b, l, mod, ln2_g[l].reshape(1, d), ln2_b[l].reshape(1, d), st['mod_row'])

    y_prompt = streams['ctx']['x'].reshape(bp, tp, d)
    y_sample = streams['lat']['x'].reshape(bs, ts, d)
    return (y_prompt, y_sample, jnp.stack(ks, axis=1), jnp.stack(vs, axis=1), jnp.stack(lrus, axis=1),
            jnp.stack(dns, axis=1), jnp.stack(s5rs, axis=1), jnp.stack(s5is, axis=1))
```

```python
import functools

import jax
import jax.numpy as jnp
from jax import lax
from jax.experimental import pallas as pl
from jax.experimental.pallas import tpu as pltpu

F32 = jnp.float32
BF16 = jnp.bfloat16

D_MODEL = 2048
DEPTH = 2
GRID_W = 64
CONV_W = 4
EPS = 1e-6
ROPE_THETA = 10000.0
N_MOD = 6
HEAD_DIM = 128
ATTN_WIDTH = D_MODEL // 4
ATTN_HEADS = ATTN_WIDTH // HEAD_DIM
ATTN_KV_HEADS = ATTN_HEADS // 2
ATTN_GROUP = ATTN_HEADS // ATTN_KV_HEADS
KV_WIDTH = ATTN_KV_HEADS * HEAD_DIM
ATTN_SCALE = HEAD_DIM ** -0.5
LRU_WIDTH = D_MODEL // 4
LRU_BLOCKS = 8
LRU_C = 8.0
DN_DK = 128
DN_DV = 128
DN_WIDTH = D_MODEL // 4
DN_HEADS = DN_WIDTH // DN_DV
DN_CHUNK = 64
S5_WIDTH = D_MODEL // 4
S5_CH = 16
S5_GROUPS = S5_WIDTH // S5_CH
S5_STATE = 64
S5_NSTATE = S5_GROUPS * S5_STATE
DEEPNORM_ALPHA = (2 * DEPTH) ** 0.25
MIXW = D_MODEL // 4

COL_DQ = 0
COL_DZ = 1536
COL_Q = 2048
COL_KV = 2560
COL_LX = 3072
COL_LG = 3584
COL_SU = 4096
N_PROJ = 4608
N_BA = 2 * 2 * DN_HEADS
LANE = 128
SUBLANES = 8
MXU = 256

MOD_ROWS = 8
PITCH_PAD = 4
OUT_SUB = 256
IN_SUB = 256
MLP_SUB = 256
VMEM_LIMIT = 56 * 1024 * 1024


def _cparams(sem):
    return pltpu.CompilerParams(dimension_semantics=sem, vmem_limit_bytes=VMEM_LIMIT)


def _ln(x):
    mu = jnp.mean(x, axis=-1, keepdims=True)
    xc = x - mu
    var = jnp.mean(xc * xc, axis=-1, keepdims=True)
    return xc * lax.rsqrt(var + EPS)


def _softplus(x):
    return jnp.maximum(x, 0.0) + jnp.log1p(jnp.exp(-jnp.abs(x)))


def _gelu(x):
    return 0.5 * x * (1.0 + jnp.tanh(0.7978845608028654 * (x + 0.044715 * (x * x * x))))


def _dot(a, b):
    return jnp.dot(a, b, preferred_element_type=F32)


def _dot_nt(a, b):
    return lax.dot_general(a, b, (((1,), (1,)), ((), ())), preferred_element_type=F32)


def _ada_kernel(c_ref, w_ref, b_ref, o_ref):
    cs = c_ref[...]
    s = cs * jax.nn.sigmoid(cs)
    o_ref[0] = _dot(s.astype(BF16), w_ref[0].astype(BF16)) + b_ref[0]


def ada_mod(cond, w_ada, b_ada, tn=1024):
    depth, d, n = w_ada.shape
    return pl.pallas_call(
        _ada_kernel,
        grid=(depth, n // tn),
        in_specs=[pl.BlockSpec((MOD_ROWS, d), lambda l, j: (0, 0)),
                  pl.BlockSpec((1, d, tn), lambda l, j: (l, 0, j)),
                  pl.BlockSpec((1, 1, tn), lambda l, j: (l, 0, j))],
        out_specs=pl.BlockSpec((1, MOD_ROWS, tn), lambda l, j: (l, 0, j)),
        out_shape=jax.ShapeDtypeStruct((depth, MOD_ROWS, n), F32),
        compiler_params=_cparams(("parallel", "parallel")),
        name="ada_mod",
    )(cond, w_ada, b_ada.reshape(depth, 1, n))


def _in_kernel(x_ref, sc_ref, sh_ref, w_ref, wba_ref, o_ref, ba_ref, u_ref):
    j = pl.program_id(1)

    @pl.when(j == 0)
    def _():
        for r0 in range(0, x_ref.shape[0], IN_SUB):
            rows = slice(r0, r0 + IN_SUB)
            u = (_ln(x_ref[rows, :]) * (1.0 + sc_ref[0]) + sh_ref[0]).astype(BF16)
            u_ref[rows, :] = u
            o_ref[rows, :] = _dot(u, w_ref[...])
            ba_ref[rows, :] = _dot(u, wba_ref[...])

    @pl.when(j > 0)
    def _():
        o_ref[...] = _dot(u_ref[...], w_ref[...])


def in_proj(x, mod, w, w_ba, layer, mod_row, tm=1024, tn=768):
    m, d = x.shape
    n = w.shape[2]

    def mod_spec(kind):
        return pl.BlockSpec((1, 1, d), lambda i, j: (kind * MOD_ROWS + mod_row(i * tm), 0, 0))

    return pl.pallas_call(
        _in_kernel,
        grid=(m // tm, n // tn),
        in_specs=[pl.BlockSpec((tm, d), lambda i, j: (i, 0)),
                  mod_spec(1), mod_spec(0),
                  pl.BlockSpec((None, d, tn), lambda i, j: (layer, 0, j)),
                  pl.BlockSpec((None, d, LANE), lambda i, j: (layer, 0, 0))],
        out_specs=[pl.BlockSpec((tm, tn), lambda i, j: (i, j)),
                   pl.BlockSpec((tm, LANE), lambda i, j: (i, 0))],
        out_shape=[jax.ShapeDtypeStruct((m, n), F32), jax.ShapeDtypeStruct((m, LANE), F32)],
        scratch_shapes=[pltpu.VMEM((tm, d), BF16)],
        compiler_params=_cparams(("parallel", "arbitrary")),
        name="in_proj",
    )(x, mod, mod, w, w_ba)


def _rms_heads(x, g, heads):
    outs = []
    for h in range(heads):
        xh = x[:, h * HEAD_DIM:(h + 1) * HEAD_DIM]
        outs.append(xh * lax.rsqrt(jnp.mean(xh * xh, axis=-1, keepdims=True) + EPS) * g)
    return outs


def _softmax_av(q, k, v, scale):
    s = _dot_nt(q, k)
    if scale is not None:
        s = s * scale
    m = jnp.max(s, axis=-1, keepdims=True)
    p = jnp.exp(s - m)
    l = jnp.sum(p, axis=-1, keepdims=True)
    return _dot(p.astype(BF16), v) / l


def _ctx_attn_kernel(q_ref, kv_ref, qg_ref, kg_ref, o_ref, kn_ref, v_ref):
    qs = _rms_heads(q_ref[...], qg_ref[...], ATTN_HEADS)
    kv = kv_ref[...]
    ks = _rms_heads(kv[:, :KV_WIDTH], kg_ref[...], ATTN_KV_HEADS)
    v = kv[:, KV_WIDTH:]
    v_ref[...] = v
    t = q_ref.shape[0]
    for kh in range(ATTN_KV_HEADS):
        kn_ref[:, kh * HEAD_DIM:(kh + 1) * HEAD_DIM] = ks[kh]
        q2 = jnp.concatenate([qs[kh * ATTN_GROUP + g] for g in range(ATTN_GROUP)], axis=0).astype(BF16)
        o = _softmax_av(q2, ks[kh].astype(BF16), v[:, kh * HEAD_DIM:(kh + 1) * HEAD_DIM].astype(BF16), ATTN_SCALE)
        for g in range(ATTN_GROUP):
            h = kh * ATTN_GROUP + g
            o_ref[:, h * HEAD_DIM:(h + 1) * HEAD_DIM] = o[g * t:(g + 1) * t].astype(BF16)


def ctx_attention(proj, qg, kg, batch, seq):
    return pl.pallas_call(
        _ctx_attn_kernel,
        grid=(batch,),
        in_specs=[pl.BlockSpec((seq, ATTN_WIDTH), lambda b: (b, COL_Q // ATTN_WIDTH)),
                  pl.BlockSpec((seq, 2 * KV_WIDTH), lambda b: (b, COL_KV // (2 * KV_WIDTH))),
                  pl.BlockSpec((1, HEAD_DIM), lambda b: (0, 0)),
                  pl.BlockSpec((1, HEAD_DIM), lambda b: (0, 0))],
        out_specs=[pl.BlockSpec((seq, ATTN_WIDTH), lambda b: (b, 0)),
                   pl.BlockSpec((seq, KV_WIDTH), lambda b: (b, 0)),
                   pl.BlockSpec((seq, KV_WIDTH), lambda b: (b, 0))],
        out_shape=[jax.ShapeDtypeStruct((batch * seq, ATTN_WIDTH), BF16),
                   jax.ShapeDtypeStruct((batch * seq, KV_WIDTH), F32),
                   jax.ShapeDtypeStruct((batch * seq, KV_WIDTH), F32)],
        compiler_params=_cparams(("parallel",)),
        name="ctx_attention",
    )(proj, proj, qg, kg)


def _rope(x, cos, sin, heads):
    w = x.shape[-1]
    lane = lax.broadcasted_iota(jnp.int32, x.shape, 1)
    quarter = HEAD_DIM // 4
    partner = jnp.where((lane % (2 * quarter)) < quarter,
                        pltpu.roll(x, w - quarter, 1), pltpu.roll(x, quarter, 1))
    cos_t = jnp.concatenate([cos] * heads, axis=1)
    sin_t = jnp.concatenate([sin] * heads, axis=1)
    return x * cos_t + partner * sin_t


def _lat_prep_kernel(q_ref, kv_ref, qg_ref, kg_ref, cos_ref, sin_ref, qo_ref, ko_ref, vo_ref):
    qn = jnp.concatenate(_rms_heads(q_ref[...], qg_ref[...], ATTN_HEADS), axis=1)
    kv = kv_ref[...]
    kn = jnp.concatenate(_rms_heads(kv[:, :KV_WIDTH], kg_ref[...], ATTN_KV_HEADS), axis=1)
    cos = cos_ref[...]
    sin = sin_ref[...]
    qo_ref[...] = (_rope(qn, cos, sin, ATTN_HEADS) * ATTN_SCALE).astype(BF16)
    ko_ref[...] = _rope(kn, cos, sin, ATTN_KV_HEADS).astype(BF16)
    vo_ref[...] = kv[:, KV_WIDTH:].astype(BF16)


def lat_prep(proj, qg, kg, cos, sin, seq, tm=512):
    m = proj.shape[0]
    per = seq // tm
    return pl.pallas_call(
        _lat_prep_kernel,
        grid=(m // tm,),
        in_specs=[pl.BlockSpec((tm, ATTN_WIDTH), lambda i: (i, COL_Q // ATTN_WIDTH)),
                  pl.BlockSpec((tm, 2 * KV_WIDTH), lambda i: (i, COL_KV // (2 * KV_WIDTH))),
                  pl.BlockSpec((1, HEAD_DIM), lambda i: (0, 0)),
                  pl.BlockSpec((1, HEAD_DIM), lambda i: (0, 0)),
                  pl.BlockSpec((tm, HEAD_DIM), lambda i: (i % per, 0)),
                  pl.BlockSpec((tm, HEAD_DIM), lambda i: (i % per, 0))],
        out_specs=[pl.BlockSpec((tm, ATTN_WIDTH), lambda i: (i, 0)),
                   pl.BlockSpec((tm, KV_WIDTH), lambda i: (i, 0)),
                   pl.BlockSpec((tm, KV_WIDTH), lambda i: (i, 0))],
        out_shape=[jax.ShapeDtypeStruct((m, ATTN_WIDTH), BF16),
                   jax.ShapeDtypeStruct((m, KV_WIDTH), BF16),
                   jax.ShapeDtypeStruct((m, KV_WIDTH), BF16)],
        compiler_params=_cparams(("parallel",)),
        name="lat_prep",
    )(proj, proj, qg, kg, cos, sin)


def _lat_attn_kernel(q_ref, k_ref, v_ref, o_ref):
    k = k_ref[0]
    v = v_ref[0]
    half = q_ref.shape[0] // 2
    sls = [(slice(r0, r0 + half), slice(g * HEAD_DIM, (g + 1) * HEAD_DIM))
           for r0 in (0, half) for g in range(ATTN_GROUP)]
    ss = [_dot_nt(q_ref[sl], k) for sl in sls]
    ps = [jnp.exp(s - jnp.max(s, axis=-1, keepdims=True)) for s in ss]
    ls = [jnp.sum(p, axis=-1, keepdims=True) for p in ps]
    os = [_dot(p.astype(BF16), v) for p in ps]
    for sl, o, l in zip(sls, os, ls):
        o_ref[sl] = (o / l).astype(BF16)


def lat_attention(q, k_all, v_all, seq, tq=256):
    b, s, _ = k_all.shape
    nq = seq // tq
    gw = ATTN_GROUP * HEAD_DIM
    return pl.pallas_call(
        _lat_attn_kernel,
        grid=(b, ATTN_KV_HEADS, nq),
        in_specs=[pl.BlockSpec((tq, gw), lambda bi, kh, qi: (bi * nq + qi, kh)),
                  pl.BlockSpec((1, s, HEAD_DIM), lambda bi, kh, qi: (bi, 0, kh)),
                  pl.BlockSpec((1, s, HEAD_DIM), lambda bi, kh, qi: (bi, 0, kh))],
        out_specs=pl.BlockSpec((tq, gw), lambda bi, kh, qi: (bi * nq + qi, kh)),
        out_shape=jax.ShapeDtypeStruct((b * seq, ATTN_WIDTH), BF16),
        compiler_params=_cparams(("parallel", "parallel", "arbitrary")),
        name="lat_attention",
    )(q, k_all, v_all)


def rope_tables(seq):
    t = jnp.arange(seq)
    row = (t // GRID_W).astype(F32)
    col = (t % GRID_W).astype(F32)
    quarter = HEAD_DIM // 4
    inv_freq = jnp.power(ROPE_THETA, -jnp.arange(quarter, dtype=F32) / quarter)
    ar = row[:, None] * inv_freq[None, :]
    ac = col[:, None] * inv_freq[None, :]
    cos = jnp.concatenate([jnp.cos(ar), jnp.cos(ar), jnp.cos(ac), jnp.cos(ac)], axis=1)
    sin = jnp.concatenate([-jnp.sin(ar), jnp.sin(ar), -jnp.sin(ac), jnp.sin(ac)], axis=1)
    return cos, sin


def _chunk_index(d, i, n):
    return i + d * (n - 1 - 2 * i)


def _lru_kernel(x_ref, xp_ref, xn_ref, cw_ref, cb_ref, wg_ref, bg_ref, sp_ref, h0_ref,
                h_ref, fin_ref, xpad_ref, a_ref, b_ref, hc_ref, *, nb, tt):
    d = pl.program_id(0)
    i = pl.program_id(2)
    n = pl.num_programs(2)
    ci = _chunk_index(d, i, n)
    pitch = tt + PITCH_PAD
    rows = nb * pitch
    nsl = LRU_WIDTH // LANE

    @pl.when(i == 0)
    def _():
        xpad_ref[...] = jnp.zeros_like(xpad_ref)
        hc_ref[...] = h0_ref[0]

    for s in range(nb):
        base = 8 + s * pitch
        xpad_ref[base:base + tt, :] = x_ref[s]
        xpad_ref[base - 1:base, :] = jnp.where(ci == 0, 0.0, xp_ref[s, 7:8, :])
        xpad_ref[base + tt:base + tt + 2, :] = jnp.where(ci == n - 1, 0.0, xn_ref[s, 0:2, :])
    w = cw_ref[...]
    xc = cb_ref[...] + sum(xpad_ref[7 + j:7 + j + rows, :] * w[j:j + 1, :] for j in range(CONV_W))
    pre = _dot(xc.astype(BF16), wg_ref[0]) + bg_ref[0]
    r = jax.nn.sigmoid(pre[:, :LRU_WIDTH])
    ig = jax.nn.sigmoid(pre[:, LRU_WIDTH:])
    a = jnp.exp((-LRU_C) * r * sp_ref[0])
    inp = jnp.sqrt(1.0 - a * a) * (ig * xc)
    for c in range(nsl):
        a_ref[c] = a[:, c * LANE:(c + 1) * LANE]
        b_ref[c] = inp[:, c * LANE:(c + 1) * LANE]

    def body(t, carry):
        row = t + d * (tt - 1 - 2 * t)
        out = []
        for c in range(nsl):
            idx = (c, pl.ds(row, nb, stride=pitch), slice(None))
            h = a_ref[idx] * carry[c] + b_ref[idx]
            b_ref[idx] = h
            out.append(h)
        return tuple(out)

    carry = tuple(hc_ref[:, c * LANE:(c + 1) * LANE] for c in range(nsl))
    carry = lax.fori_loop(0, tt, body, carry, unroll=8)
    for c in range(nsl):
        hc_ref[:, c * LANE:(c + 1) * LANE] = carry[c]
        for s in range(nb):
            h_ref[0, s, :, c * LANE:(c + 1) * LANE] = b_ref[c, s * pitch:s * pitch + tt, :]

    @pl.when(i == n - 1)
    def _():
        fin_ref[0] = hc_ref[...]


def lru_scan(proj3, h0, cw, cb, wg, bg, sp, nb, tt=128):
    b, t, _ = proj3.shape
    w = LRU_WIDTH
    n = t // tt
    t8 = tt // 8
    rows = nb * (tt + PITCH_PAD)
    col = COL_LX // w

    def cidx(d, i):
        return _chunk_index(d, i, n)

    return pl.pallas_call(
        functools.partial(_lru_kernel, nb=nb, tt=tt),
        grid=(2, b // nb, n),
        in_specs=[pl.BlockSpec((nb, tt, w), lambda d, g, i: (g, cidx(d, i), col)),
                  pl.BlockSpec((nb, 8, w), lambda d, g, i: (g, jnp.maximum(cidx(d, i) * t8 - 1, 0), col)),
                  pl.BlockSpec((nb, 8, w), lambda d, g, i: (g, jnp.minimum((cidx(d, i) + 1) * t8, t // 8 - 1), col)),
                  pl.BlockSpec((CONV_W, w), lambda d, g, i: (0, 0)),
                  pl.BlockSpec((1, w), lambda d, g, i: (0, 0)),
                  pl.BlockSpec((1, w, 2 * w), lambda d, g, i: (d, 0, 0)),
                  pl.BlockSpec((1, 1, 2 * w), lambda d, g, i: (d, 0, 0)),
                  pl.BlockSpec((1, 1, w), lambda d, g, i: (d, 0, 0)),
                  pl.BlockSpec((1, nb, w), lambda d, g, i: (d, g, 0))],
        out_specs=[pl.BlockSpec((1, nb, tt, w), lambda d, g, i: (d, g, cidx(d, i), 0)),
                   pl.BlockSpec((1, nb, w), lambda d, g, i: (d, g, 0))],
        out_shape=[jax.ShapeDtypeStruct((2, b, t, w), F32), jax.ShapeDtypeStruct((2, b, w), F32)],
        scratch_shapes=[pltpu.VMEM((rows + 16, w), F32),
                        pltpu.VMEM((w // LANE, rows, LANE), F32),
                        pltpu.VMEM((w // LANE, rows, LANE), F32),
                        pltpu.VMEM((nb, w), F32)],
        compiler_params=_cparams(("parallel", "parallel", "arbitrary")),
        name="lru_scan",
    )(proj3, proj3, proj3, cw, cb, wg, bg, sp, h0)


def _s5_kernel(u_ref, bb_ref, cc_ref, a_ref, h0_ref, y_ref, fin_ref, up_ref, s_ref, hc_ref, *, nb, tt):
    d = pl.program_id(0)
    i = pl.program_id(2)
    n = pl.num_programs(2)
    pitch = tt + PITCH_PAD
    nsl = S5_NSTATE // LANE

    @pl.when(i == 0)
    def _():
        up_ref[...] = jnp.zeros_like(up_ref)
        hc_ref[...] = h0_ref[0]

    for s in range(nb):
        up_ref[s * pitch:s * pitch + tt, :] = u_ref[s]
    u2 = up_ref[...].astype(BF16)
    per_k = MXU // S5_CH * S5_STATE // MXU
    vs = SUBLANES // nb
    rows_seq = nb * pitch
    npk = nsl // vs

    def slab(c):
        return (c // vs, slice((c % vs) * rows_seq, (c % vs + 1) * rows_seq), slice(None))

    for nt in range(2 * S5_NSTATE // MXU):
        kt = (nt % (S5_NSTATE // MXU)) // per_k
        tile = _dot(u2[:, kt * MXU:(kt + 1) * MXU], bb_ref[0, kt * MXU:(kt + 1) * MXU, nt * MXU:(nt + 1) * MXU])
        s_ref[slab(2 * nt)] = tile[:, :LANE]
        s_ref[slab(2 * nt + 1)] = tile[:, LANE:]

    a_all = a_ref[0]

    def packed_rows(src, off, j):
        parts = [jnp.broadcast_to(src[:, off + (j * vs + v) * LANE:off + (j * vs + v + 1) * LANE], (nb, LANE))
                 for v in range(vs)]
        return parts[0] if vs == 1 else jnp.concatenate(parts, axis=0)

    group = 8
    for j0 in range(0, npk, group):
        js = list(range(j0, min(j0 + group, npk)))
        ar = [packed_rows(a_all, 0, j) for j in js]
        ai = [packed_rows(a_all, S5_NSTATE, j) for j in js]

        def body(t, carry, js=js, ar=ar, ai=ai):
            row = t + d * (tt - 1 - 2 * t)
            out = []
            for k, j in enumerate(js):
                hr, hi = carry[2 * k], carry[2 * k + 1]
                ire = (j, pl.ds(row, SUBLANES, stride=pitch), slice(None))
                iim = (npk + j, pl.ds(row, SUBLANES, stride=pitch), slice(None))
                nr = ar[k] * hr - ai[k] * hi + s_ref[ire]
                ni = ar[k] * hi + ai[k] * hr + s_ref[iim]
                s_ref[ire] = nr
                s_ref[iim] = ni
                out += [nr, ni]
            return tuple(out)

        carry = []
        for j in js:
            carry += [packed_rows(hc_ref, 0, j), packed_rows(hc_ref, S5_NSTATE, j)]
        carry = lax.fori_loop(0, tt, body, tuple(carry), unroll=4)
        for k, j in enumerate(js):
            for v in range(vs):
                c = j * vs + v
                hc_ref[:, c * LANE:(c + 1) * LANE] = carry[2 * k][v * nb:(v + 1) * nb]
                hc_ref[:, S5_NSTATE + c * LANE:S5_NSTATE + (c + 1) * LANE] = carry[2 * k + 1][v * nb:(v + 1) * nb]

    n_out = S5_WIDTH // MXU
    per_n = nsl // n_out
    for nt in range(n_out):
        slabs = [per_n * nt + k for k in range(per_n)] + [nsl + per_n * nt + k for k in range(per_n)]
        lhs = jnp.concatenate([s_ref[slab(c)] for c in slabs], axis=1).astype(BF16)
        y = _dot(lhs, cc_ref[0, nt])
        for s in range(nb):
            y_ref[0, s, :, nt * MXU:(nt + 1) * MXU] = y[s * pitch:s * pitch + tt]

    @pl.when(i == n - 1)
    def _():
        fin_ref[0] = hc_ref[...]


def s5_scan(proj3, h0, bb, cc, a, nb, tt=128):
    b, t, _ = proj3.shape
    w = S5_WIDTH
    n = t // tt
    rows = nb * (tt + PITCH_PAD)
    ns2 = 2 * S5_NSTATE

    def cidx(d, i):
        return _chunk_index(d, i, n)

    return pl.pallas_call(
        functools.partial(_s5_kernel, nb=nb, tt=tt),
        grid=(2, b // nb, n),
        in_specs=[pl.BlockSpec((nb, tt, w), lambda d, g, i: (g, cidx(d, i), COL_SU // w)),
                  pl.BlockSpec((1, w, ns2), lambda d, g, i: (d, 0, 0)),
                  pl.BlockSpec((1,) + cc.shape[1:], lambda d, g, i: (d, 0, 0, 0)),
                  pl.BlockSpec((1, 1, ns2), lambda d, g, i: (d, 0, 0)),
                  pl.BlockSpec((1, nb, ns2), lambda d, g, i: (d, g, 0))],
        out_specs=[pl.BlockSpec((1, nb, tt, w), lambda d, g, i: (d, g, cidx(d, i), 0)),
                   pl.BlockSpec((1, nb, ns2), lambda d, g, i: (d, g, 0))],
        out_shape=[jax.ShapeDtypeStruct((2, b, t, w), F32), jax.ShapeDtypeStruct((2, b, ns2), F32)],
        scratch_shapes=[pltpu.VMEM((rows, w), F32),
                        pltpu.VMEM((ns2 // LANE // (SUBLANES // nb), SUBLANES // nb * rows, LANE), F32),
                        pltpu.VMEM((nb, ns2), F32)],
        compiler_params=_cparams(("parallel", "parallel", "arbitrary")),
        name="s5_scan",
    )(proj3, bb, cc, a, h0)


def _split3(x):
    x1 = x.astype(BF16)
    r1 = x - x1.astype(F32)
    x2 = r1.astype(BF16)
    x3 = (r1 - x2.astype(F32)).astype(BF16)
    return x1, x2, x3


def _mm(x, y):
    return _dot(x.astype(BF16), y.astype(BF16))


def _unit_tri_inverses(mats, rr, cc):
    same16 = (rr // 16) == (cc // 16)
    same32 = (rr // 32) == (cc // 32)
    p = [jnp.where(same16, -a, 0.0) for a in mats]
    t = [jnp.where(rr == cc, 1.0, x) for x in p]
    for _ in range(3):
        p = [_mm(x, x) for x in p]
        t = [x + _mm(x, y) for x, y in zip(t, p)]
    for mask in (same32 & jnp.logical_not(same16), jnp.logical_not(same32)):
        te = [_mm(x, jnp.where(mask, a, 0.0)) for x, a in zip(t, mats)]
        t = [x - _mm(y, x) for x, y in zip(t, te)]
    return t


def _dn_prep_kernel(x_ref, xp_ref, xn_ref, cw_ref, o_ref, xpad_ref, *, tb):
    i = pl.program_id(1)
    n = pl.num_programs(1)
    hw = DN_HEADS * DN_DK
    xpad_ref[8:8 + tb, :] = x_ref[0]
    xpad_ref[0:8, :] = jnp.where(i == 0, 0.0, xp_ref[0])
    xpad_ref[8 + tb:16 + tb, :] = jnp.where(i == n - 1, 0.0, xn_ref[0])
    w = cw_ref[...]
    xc = sum(xpad_ref[7 + j:7 + j + tb, :] * w[j:j + 1, :] for j in range(CONV_W))
    qkv = xc * jax.nn.sigmoid(xc)
    for h in range(DN_HEADS):
        q = qkv[:, h * DN_DK:(h + 1) * DN_DK]
        k = qkv[:, hw + h * DN_DK:hw + (h + 1) * DN_DK]
        o_ref[0, :, h * DN_DK:(h + 1) * DN_DK] = (
            q * lax.rsqrt(jnp.sum(q * q, axis=-1, keepdims=True) + EPS) * (DN_DK ** -0.5))
        o_ref[0, :, hw + h * DN_DK:hw + (h + 1) * DN_DK] = (
            k * lax.rsqrt(jnp.sum(k * k, axis=-1, keepdims=True) + EPS))
    o_ref[0, :, 2 * hw:] = qkv[:, 2 * hw:]


def dn_prep(proj3, cw, tb=256):
    b, t, _ = proj3.shape
    n = t // tb
    t8 = tb // 8
    w3 = 3 * DN_WIDTH
    cq = COL_DQ // w3
    return pl.pallas_call(
        functools.partial(_dn_prep_kernel, tb=tb),
        grid=(b, n),
        in_specs=[pl.BlockSpec((1, tb, w3), lambda bi, i: (bi, i, cq)),
                  pl.BlockSpec((1, 8, w3), lambda bi, i: (bi, jnp.maximum(i * t8 - 1, 0), cq)),
                  pl.BlockSpec((1, 8, w3), lambda bi, i: (bi, jnp.minimum((i + 1) * t8, t // 8 - 1), cq)),
                  pl.BlockSpec((CONV_W, w3), lambda bi, i: (0, 0))],
        out_specs=pl.BlockSpec((1, tb, w3), lambda bi, i: (bi, i, 0)),
        out_shape=jax.ShapeDtypeStruct((b, t, w3), F32),
        scratch_shapes=[pltpu.VMEM((tb + 16, w3), F32)],
        compiler_params=_cparams(("parallel", "parallel")),
        name="dn_prep",
    )(proj3, proj3, proj3, cw)


def _dn_kernel(xf_ref, baf_ref, xb_ref, bab_ref, al_ref, dtb_ref, s0_ref, of_ref, ob_ref, fin_ref, s_ref,
               *, nbat, tb):
    i = pl.program_id(1)
    n = pl.num_programs(1)
    ch = DN_CHUNK
    nc = tb // ch
    hw = DN_HEADS * DN_DK

    @pl.when(i == 0)
    def _():
        s_ref[...] = s0_ref[...]

    rb = lax.broadcasted_iota(jnp.int32, (tb, tb), 0)
    cb = lax.broadcasted_iota(jnp.int32, (tb, tb), 1)
    rr = lax.broadcasted_iota(jnp.int32, (ch, ch), 0)
    cc = lax.broadcasted_iota(jnp.int32, (ch, ch), 1)
    al = al_ref[...]
    dtb = dtb_ref[...]

    x_refs = (xf_ref, xb_ref)
    beta_tab, gc_tab, masks = {}, {}, {}
    for d in range(2):
        reverse = d == 1
        ba_ref = bab_ref if reverse else baf_ref
        tri = ((rb // ch) == (cb // ch)) & ((cb >= rb) if reverse else (cb <= rb))
        tri = jnp.where(tri, 1.0, 0.0).astype(BF16)
        masks[d] = ((cc >= rr) if reverse else (cc <= rr), (cc > rr) if reverse else (cc < rr))
        for bat in range(nbat):
            ba = ba_ref[bat]
            beta_tab[(d, bat)] = jax.nn.sigmoid(ba)
            g_all = -jnp.exp(al) * _softplus(ba + dtb)
            gc_tab[(d, bat)] = sum(_dot(tri, piece) for piece in _split3(g_all))
    units = [dict(d=d, bat=bat, c=c, h=h) for d in range(2) for bat in range(nbat) for c in range(nc)
             for h in range(DN_HEADS)]

    def rows_of(u):
        return slice(u['c'] * ch, (u['c'] + 1) * ch)

    def q_of(u):
        return x_refs[u['d']][u['bat'], rows_of(u), u['h'] * DN_DK:(u['h'] + 1) * DN_DK]

    def k_of(u):
        return x_refs[u['d']][u['bat'], rows_of(u), hw + u['h'] * DN_DK:hw + (u['h'] + 1) * DN_DK]

    def v_of(u):
        return x_refs[u['d']][u['bat'], rows_of(u), 2 * hw + u['h'] * DN_DV:2 * hw + (u['h'] + 1) * DN_DV]

    def beta_of(u):
        lane = u['d'] * DN_HEADS + u['h']
        return beta_tab[(u['d'], u['bat'])][rows_of(u), lane:lane + 1]

    def gcol_of(u):
        lane = 2 * DN_HEADS + u['d'] * DN_HEADS + u['h']
        return gc_tab[(u['d'], u['bat'])][rows_of(u), lane:lane + 1]

    def glast_of(u):
        gcol = gcol_of(u)
        return gcol[0:1, :] if u['d'] == 1 else gcol[ch - 1:ch, :]

    gct = {}
    for u in units:
        key = (u['d'], u['bat'], u['c'])
        if key not in gct:
            gct[key] = gc_tab[(u['d'], u['bat'])][rows_of(u), :].T
        lane = 2 * DN_HEADS + u['d'] * DN_HEADS + u['h']
        incl, strict = masks[u['d']]
        decay = jnp.where(incl, jnp.exp(gcol_of(u) - gct[key][lane:lane + 1, :]), 0.0)
        k = k_of(u)
        kbf = k.astype(BF16)
        u['a'] = jnp.where(strict, _dot_nt((k * beta_of(u)).astype(BF16), kbf) * decay, 0.0)
        u['qk'] = (_dot_nt(q_of(u).astype(BF16), kbf) * decay).astype(BF16)
    t_inv = _unit_tri_inverses([u['a'] for u in units], rr, cc)
    for u, t in zip(units, t_inv):
        beta = beta_of(u)
        rhs = jnp.concatenate([v_of(u) * beta, k_of(u) * beta * jnp.exp(gcol_of(u))], axis=1)
        sol = rhs + _mm(jnp.where(rr == cc, 0.0, t), rhs)
        u['u_val'] = sol[:, :DN_DV]
        u['w'] = sol[:, DN_DV:].astype(BF16)

    state = {(d, bat, h): s_ref[d, bat, h] for d in range(2) for bat in range(nbat) for h in range(DN_HEADS)}
    for j in range(nc):
        cur = [u for u in units if u['c'] == (nc - 1 - j if u['d'] == 1 else j)]
        keys = [(u['d'], u['bat'], u['h']) for u in cur]
        sb = [state[key].astype(BF16) for key in keys]
        ws = [_dot(u['w'], s) for u, s in zip(cur, sb)]
        qs = [_dot((q_of(u) * jnp.exp(gcol_of(u))).astype(BF16), s) for u, s in zip(cur, sb)]
        vb = [(u['u_val'] - x).astype(BF16) for u, x in zip(cur, ws)]
        os = [x + _dot(u['qk'], y) for u, x, y in zip(cur, qs, vb)]
        kdec_t = [(k_of(u) * jnp.exp(glast_of(u) - gcol_of(u))).T.astype(BF16) for u in cur]
        sn = [state[key] * jnp.exp(glast_of(u)) + _dot(kt, y) for u, key, kt, y in zip(cur, keys, kdec_t, vb)]
        for u, key, o, s in zip(cur, keys, os, sn):
            state[key] = s
            o_ref = ob_ref if u['d'] == 1 else of_ref
            o_ref[u['bat'], u['c'] * ch:(u['c'] + 1) * ch, u['h'] * DN_DV:(u['h'] + 1) * DN_DV] = o
    for (d, bat, h), s in state.items():
        s_ref[d, bat, h] = s

    @pl.when(i == n - 1)
    def _():
        fin_ref[...] = s_ref[...]


def dn_scan(qkv3, ba3, s0, al, dtb, nbat, tb):
    b, t, w3 = qkv3.shape
    n = t // tb

    def data_specs(cidx):
        return [pl.BlockSpec((nbat, tb, w3), lambda g, i: (g, cidx(i), 0)),
                pl.BlockSpec((nbat, tb, LANE), lambda g, i: (g, cidx(i), 0))]

    fwd = lambda i: i
    bwd = lambda i: n - 1 - i
    st_spec = pl.BlockSpec((2, nbat, DN_HEADS, DN_DK, DN_DV), lambda g, i: (0, g, 0, 0, 0))
    return pl.pallas_call(
        functools.partial(_dn_kernel, nbat=nbat, tb=tb),
        grid=(b // nbat, n),
        in_specs=data_specs(fwd) + data_specs(bwd) + [
            pl.BlockSpec((1, LANE), lambda g, i: (0, 0)),
            pl.BlockSpec((1, LANE), lambda g, i: (0, 0)),
            st_spec],
        out_specs=[pl.BlockSpec((nbat, tb, DN_WIDTH), lambda g, i: (g, i, 0)),
                   pl.BlockSpec((nbat, tb, DN_WIDTH), lambda g, i: (g, n - 1 - i, 0)),
                   st_spec],
        out_shape=[jax.ShapeDtypeStruct((b, t, DN_WIDTH), F32),
                   jax.ShapeDtypeStruct((b, t, DN_WIDTH), F32),
                   jax.ShapeDtypeStruct((2, b, DN_HEADS, DN_DK, DN_DV), F32)],
        scratch_shapes=[pltpu.VMEM((2, nbat, DN_HEADS, DN_DK, DN_DV), F32)],
        compiler_params=_cparams(("parallel", "arbitrary")),
        name="dn_scan",
    )(qkv3, ba3, qkv3, ba3, al, dtb, s0)


def _post_kernel(hf_ref, hb_ref, lg_ref, of_ref, ob_ref, dz_ref, ng_ref, yf_ref, yb_ref, su_ref, sd_ref,
                 gw_ref, gb_ref, lru_ref, dn_ref, s5_ref):
    lru_ref[...] = ((hf_ref[0, 0] + hb_ref[0, 0]) * _gelu(lg_ref[...])).astype(BF16)
    o = of_ref[0] + ob_ref[0]
    dz = dz_ref[...]
    for h in range(DN_HEADS):
        sl = slice(h * DN_DV, (h + 1) * DN_DV)
        oh = o[:, sl]
        oh = oh * lax.rsqrt(jnp.mean(oh * oh, axis=-1, keepdims=True) + EPS) * ng_ref[...]
        zh = dz[:, sl]
        dn_ref[:, sl] = (oh * (zh * jax.nn.sigmoid(zh))).astype(BF16)
    y = yf_ref[0, 0] + yb_ref[0, 0] + sd_ref[...] * su_ref[...]
    gy = _gelu(y)
    s5_ref[...] = (gy * jax.nn.sigmoid(_dot(gy.astype(BF16), gw_ref[...]) + gb_ref[...])).astype(BF16)


def mix_post(proj, h_lru, o_f, o_b, y_s5, ng, sd, gw, gb):
    m = proj.shape[0]
    _, b, t, w = h_lru.shape
    tm = min(512, t)
    per = t // tm
    tok = pl.BlockSpec((tm, w), lambda i: (i, 0))

    def pcol(c0):
        return pl.BlockSpec((tm, w), lambda i: (i, c0 // w))

    def dirspec(d):
        return pl.BlockSpec((1, 1, tm, w), lambda i: (d, i // per, i % per, 0))

    seq = pl.BlockSpec((1, tm, w), lambda i: (i // per, i % per, 0))

    def vec(n):
        return pl.BlockSpec((1, n), lambda i: (0, 0))

    out = jax.ShapeDtypeStruct((m, w), BF16)
    return pl.pallas_call(
        _post_kernel,
        grid=(m // tm,),
        in_specs=[dirspec(0), dirspec(1), pcol(COL_LG), seq, seq, pcol(COL_DZ), vec(DN_DV),
                  dirspec(0), dirspec(1), pcol(COL_SU), vec(w), pl.BlockSpec((w, w), lambda i: (0, 0)), vec(w)],
        out_specs=[tok, tok, tok],
        out_shape=[out, out, out],
        compiler_params=_cparams(("parallel",)),
        name="mix_post",
    )(h_lru, h_lru, proj, o_f, o_b, proj, ng, y_s5, y_s5, proj, sd, gw, gb)


def _out_kernel(x_ref, a_ref, b_ref, c_ref, d_ref, w_ref, gate_ref, g_ref, bb_ref, x1_ref):
    q = MIXW
    for r0 in range(0, x_ref.shape[0], OUT_SUB):
        rows = slice(r0, r0 + OUT_SUB)
        y = _dot(a_ref[rows, :], w_ref[0:q, :])
        y += _dot(b_ref[rows, :], w_ref[q:2 * q, :])
        y += _dot(c_ref[rows, :], w_ref[2 * q:3 * q, :])
        y += _dot(d_ref[rows, :], w_ref[3 * q:, :])
        z = DEEPNORM_ALPHA * x_ref[rows, :] + gate_ref[0] * y
        x1_ref[rows, :] = _ln(z) * g_ref[...] + bb_ref[...]


def out_proj(x, parts, w, layer, mod, ln_g, ln_b, mod_row, tm=512):
    m, d = x.shape
    vec = pl.BlockSpec((1, d), lambda i: (0, 0))
    part = pl.BlockSpec((tm, MIXW), lambda i: (i, 0))
    return pl.pallas_call(
        _out_kernel,
        grid=(m // tm,),
        in_specs=[pl.BlockSpec((tm, d), lambda i: (i, 0)), part, part, part, part,
                  pl.BlockSpec((None, d, d), lambda i: (layer, 0, 0)),
                  pl.BlockSpec((1, 1, d), lambda i: (2 * MOD_ROWS + mod_row(i * tm), 0, 0)), vec, vec],
        out_specs=pl.BlockSpec((tm, d), lambda i: (i, 0)),
        out_shape=jax.ShapeDtypeStruct((m, d), F32),
        compiler_params=_cparams(("parallel",)),
        name="out_proj",
    )(x, *parts, w, mod, ln_g, ln_b)


def _mlp_kernel(x_ref, w1_ref, w2_ref, sc_ref, sh_ref, gate_ref, g_ref, b_ref, o_ref, u_ref, acc_ref):
    f = pl.program_id(1)
    last = pl.num_programs(1) - 1
    tm = x_ref.shape[0]

    def hidden(u):
        return jnp.square(jnp.maximum(_dot(u, w1_ref[...]), 0.0)).astype(BF16)

    @pl.when(f == 0)
    def _():
        for r0 in range(0, tm, MLP_SUB):
            rows = slice(r0, r0 + MLP_SUB)
            u = (_ln(x_ref[rows, :]) * (1.0 + sc_ref[0]) + sh_ref[0]).astype(BF16)
            u_ref[rows, :] = u
            acc_ref[rows, :] = _dot(hidden(u), w2_ref[...])

    @pl.when((f > 0) & (f < last))
    def _():
        acc_ref[...] += _dot(hidden(u_ref[...]), w2_ref[...])

    @pl.when(f == last)
    def _():
        for r0 in range(0, tm, MLP_SUB):
            rows = slice(r0, r0 + MLP_SUB)
            a = acc_ref[rows, :] + _dot(hidden(u_ref[rows, :]), w2_ref[...])
            z = DEEPNORM_ALPHA * x_ref[rows, :] + gate_ref[0] * a
            o_ref[rows, :] = _ln(z) * g_ref[...] + b_ref[...]


def mlp(x, w1, w2, layer, mod, ln_g, ln_b, mod_row, tm=512, tf=1024):
    m, d = x.shape
    ff = w1.shape[2]
    assert ff // tf >= 2
    vec = pl.BlockSpec((1, d), lambda i, f: (0, 0))

    def mod_spec(kind):
        return pl.BlockSpec((1, 1, d), lambda i, f: (kind * MOD_ROWS + mod_row(i * tm), 0, 0))

    return pl.pallas_call(
        _mlp_kernel,
        grid=(m // tm, ff // tf),
        in_specs=[pl.BlockSpec((tm, d), lambda i, f: (i, 0)),
                  pl.BlockSpec((None, d, tf), lambda i, f: (layer, 0, f)),
                  pl.BlockSpec((None, tf, d), lambda i, f: (layer, f, 0)),
                  mod_spec(4), mod_spec(3), mod_spec(5), vec, vec],
        out_specs=pl.BlockSpec((tm, d), lambda i, f: (i, 0)),
        out_shape=jax.ShapeDtypeStruct((m, d), F32),
        scratch_shapes=[pltpu.VMEM((tm, d), BF16), pltpu.VMEM((tm, d), F32)],
        compiler_params=_cparams(("parallel", "arbitrary")),
        name="mlp",
    )(x, w1, w2, mod, mod, mod, ln_g, ln_b)


def _reorder_w_in(w):
    def cols(c0, n):
        return w[..., c0:c0 + n]

    aq, akv, lx, lg = cols(0, 512), cols(512, 512), cols(1024, 512), cols(1536, 512)
    dqkv, dz, ba, su = cols(2048, 1536), cols(3584, 512), cols(4096, N_BA), cols(4096 + N_BA, 512)
    pad = jnp.zeros(w.shape[:2] + (LANE - N_BA,), w.dtype)
    main = jnp.concatenate([dqkv, dz, aq, akv, lx, lg, su], axis=-1).astype(BF16)
    return main, jnp.concatenate([ba, pad], axis=-1).astype(BF16)


def _block_diag(blocks):
    n, r, c = blocks.shape[-3:]
    eye = jnp.eye(n, dtype=blocks.dtype)
    out = blocks[..., :, :, None, :] * eye[:, None, :, None]
    return out.reshape(blocks.shape[:-3] + (n * r, n * c))


def _lru_params(wa, ba, wx, bx, lam):
    wg = jnp.concatenate([_block_diag(wa.astype(BF16)), _block_diag(wx.astype(BF16))], axis=-1)
    bg = jnp.concatenate([ba, bx], axis=-1)[..., None, :]
    sp = jax.nn.softplus(-lam)[..., None, :]
    return wg, bg, sp


def _s5_params(lam_re, lam_im, log_dt, b_re, b_im, c_re, c_im):
    lead = lam_re.shape[:-2]
    dt = jnp.exp(log_dt)[..., None]
    mag = jnp.exp(lam_re * dt)
    abar_re = mag * jnp.cos(lam_im * dt)
    abar_im = mag * jnp.sin(lam_im * dt)
    den = lam_re * lam_re + lam_im * lam_im
    nr = abar_re - 1.0
    ni = abar_im
    f_re = (nr * lam_re + ni * lam_im) / den
    f_im = (ni * lam_re - nr * lam_im) / den
    bb_re = f_re[..., None] * b_re - f_im[..., None] * b_im
    bb_im = f_re[..., None] * b_im + f_im[..., None] * b_re
    to_in = lambda m: _block_diag(jnp.swapaxes(m, -1, -2).astype(BF16))
    bb = jnp.concatenate([to_in(bb_re), to_in(bb_im)], axis=-1)
    n_out = S5_WIDTH // MXU
    gpt = S5_GROUPS // n_out

    def to_out(m, nt):
        return _block_diag(jnp.swapaxes(m[..., nt * gpt:(nt + 1) * gpt, :, :], -1, -2).astype(BF16))

    cc = jnp.stack([jnp.concatenate([to_out(c_re, nt), to_out(-c_im, nt)], axis=-2)
                    for nt in range(n_out)], axis=-3)
    a = jnp.concatenate([abar_re.reshape(lead + (1, -1)), abar_im.reshape(lead + (1, -1))], axis=-1)
    return bb, cc, a


def _lane_row(vals, offset):
    return jnp.zeros((1, LANE), F32).at[0, offset:offset + vals.size].set(vals.reshape(-1))


def kernel(x_prompt, x_sample, cache_attn_k, cache_attn_v, state_rglru, state_delta, state_s5_re, state_s5_im, c, c_ctx, w_ada, b_ada, w_in, w_out, ln1_g, ln1_b, ln2_g, ln2_b, w_mlp1, w_mlp2, q_norm_g, k_norm_g, lru_conv_w, lru_conv_b, lru_wa, lru_ba, lru_wx, lru_bx, lru_lambda, dn_conv_w, dn_a_log, dn_dt_bias, dn_norm_g, s5_lambda_re, s5_lambda_im, s5_log_dt, s5_b_re, s5_b_im, s5_c_re, s5_c_im, s5_d, s5_glu_w, s5_glu_b):
    bp, tp, d = x_prompt.shape
    bs, ts, _ = x_sample.shape
    ctx_row = bs

    cond = jnp.concatenate([c, c_ctx[None, :], jnp.zeros((MOD_ROWS - bs - 1, d), F32)], axis=0)
    mods = ada_mod(cond, w_ada, b_ada)
    w_in_r, w_ba = _reorder_w_in(w_in)
    w_out_b = w_out.astype(BF16)
    w1_b = w_mlp1.astype(BF16)
    w2_b = w_mlp2.astype(BF16)
    cos, sin = rope_tables(ts)
    wg_all, bg_all, sp_all = _lru_params(lru_wa, lru_ba, lru_wx, lru_bx, lru_lambda)
    bb_all, cc_all, a5_all = _s5_params(s5_lambda_re, s5_lambda_im, s5_log_dt, s5_b_re, s5_b_im, s5_c_re, s5_c_im)

    streams = {
        'ctx': dict(x=x_prompt.reshape(bp * tp, d), b=bp, t=tp, nb=8, s5_tt=128, dn_nbat=2, dn_tb=tp,
                    mod_row=lambda tok: ctx_row),
        'lat': dict(x=x_sample.reshape(bs * ts, d), b=bs, t=ts, nb=bs, s5_tt=256, dn_nbat=bs, dn_tb=2 * DN_CHUNK,
                    mod_row=lambda tok: tok // ts),
    }
    ks, vs, lrus, dns, s5rs, s5is = [], [], [], [], [], []
    for l in range(DEPTH):
        mod = mods[l].reshape(MOD_ROWS, N_MOD, d).transpose(1, 0, 2).reshape(N_MOD * MOD_ROWS, 1, d)
        qg = q_norm_g[l].reshape(1, HEAD_DIM)
        kg = k_norm_g[l].reshape(1, HEAD_DIM)
        wg, bg, sp = wg_all[l], bg_all[l], sp_all[l]
        bb, cc, a5 = bb_all[l], cc_all[l], a5_all[l]
        al = _lane_row(dn_a_log[l], 2 * DN_HEADS)
        dtb = _lane_row(dn_dt_bias[l], 2 * DN_HEADS)
        for name, st in streams.items():
            b, t, nb = st['b'], st['t'], st['nb']
            is_ctx = name == 'ctx'
            proj, ba = in_proj(st['x'], mod, w_in_r, w_ba, l, st['mod_row'])
            proj3 = proj.reshape(b, t, N_PROJ)
            ba3 = ba.reshape(b, t, LANE)

            if is_ctx:
                attn, kn, vv = ctx_attention(proj, qg, kg, b, t)
                ks.append(kn.reshape(b, t, ATTN_KV_HEADS, HEAD_DIM))
                vs.append(vv.reshape(b, t, ATTN_KV_HEADS, HEAD_DIM))
                h0_lru = jnp.zeros((2, b, LRU_WIDTH), F32)
                s0_dn = jnp.zeros((2, b, DN_HEADS, DN_DK, DN_DV), F32)
                h0_s5 = jnp.zeros((2, b, 2 * S5_NSTATE), F32)
            else:
                q_s, k_s, v_s = lat_prep(proj, qg, kg, cos, sin, t)
                k_all = jnp.concatenate([cache_attn_k[:, l].reshape(b, -1, KV_WIDTH).astype(BF16),
                                         k_s.reshape(b, t, KV_WIDTH)], axis=1)
                v_all = jnp.concatenate([cache_attn_v[:, l].reshape(b, -1, KV_WIDTH).astype(BF16),
                                         v_s.reshape(b, t, KV_WIDTH)], axis=1)
                attn = lat_attention(q_s, k_all, v_all, t)
                h0_lru = jnp.swapaxes(state_rglru[:, l], 0, 1)
                s0_dn = jnp.swapaxes(state_delta[:, l], 0, 1)
                h0_s5 = jnp.swapaxes(jnp.concatenate([state_s5_re[:, l].reshape(b, 2, S5_NSTATE),
                                                      state_s5_im[:, l].reshape(b, 2, S5_NSTATE)], axis=-1), 0, 1)

            h_lru, lru_fin = lru_scan(proj3, h0_lru, lru_conv_w[l], lru_conv_b[l].reshape(1, -1), wg, bg, sp, nb)
            o_f, o_b, dn_fin = dn_scan(dn_prep(proj3, dn_conv_w[l]), ba3, s0_dn, al, dtb,
                                       st['dn_nbat'], st['dn_tb'])
            y_s5, s5_fin = s5_scan(proj3, h0_s5, bb, cc, a5, nb, st['s5_tt'])
            lru_out, dn_out, s5_out = mix_post(
                proj, h_lru, o_f, o_b, y_s5, dn_norm_g[l].reshape(1, DN_DV), s5_d[l].reshape(1, MIXW),
                s5_glu_w[l].astype(BF16), s5_glu_b[l].reshape(1, MIXW))
            if is_ctx:
                lrus.append(jnp.swapaxes(lru_fin, 0, 1))
                dns.append(jnp.swapaxes(dn_fin, 0, 1))
                s5_fin = jnp.swapaxes(s5_fin, 0, 1)
                s5rs.append(s5_fin[..., :S5_NSTATE].reshape(b, 2, S5_GROUPS, S5_STATE))
                s5is.append(s5_fin[..., S5_NSTATE:].reshape(b, 2, S5_GROUPS, S5_STATE))

            x1 = out_proj(st['x'], [attn, lru_out, dn_out, s5_out], w_out_b, l, mod,
                          ln1_g[l].reshape(1, d), ln1_b[l].reshape(1, d), st['mod_row'])
            st['x'] = mlp(x1, w1_b, w2_b, l, mod, ln2_g[l].reshape(1, d), ln2_b[l].reshape(1, d), st['mod_row'])

    y_prompt = streams['ctx']['x'].reshape(bp, tp, d)
    y_sample = streams['lat']['x'].reshape(bs, ts, d)
    return (y_prompt, y_sample, jnp.stack(ks, axis=1), jnp.stack(vs, axis=1), jnp.stack(lrus, axis=1),
            jnp.stack(dns, axis=1), jnp.stack(s5rs, axis=1), jnp.stack(s5is, axis=1))
```

```python
import functools

import jax
import jax.numpy as jnp
from jax import lax
from jax.experimental import pallas as pl
from jax.experimental.pallas import tpu as pltpu

F32 = jnp.float32
BF16 = jnp.bfloat16

D_MODEL = 2048
DEPTH = 2
GRID_W = 64
CONV_W = 4
EPS = 1e-6
ROPE_THETA = 10000.0
N_MOD = 6
HEAD_DIM = 128
ATTN_WIDTH = D_MODEL // 4
ATTN_HEADS = ATTN_WIDTH // HEAD_DIM
ATTN_KV_HEADS = ATTN_HEADS // 2
ATTN_GROUP = ATTN_HEADS // ATTN_KV_HEADS
KV_WIDTH = ATTN_KV_HEADS * HEAD_DIM
ATTN_SCALE = HEAD_DIM ** -0.5
LRU_WIDTH = D_MODEL // 4
LRU_BLOCKS = 8
LRU_C = 8.0
DN_DK = 128
DN_DV = 128
DN_WIDTH = D_MODEL // 4
DN_HEADS = DN_WIDTH // DN_DV
DN_CHUNK = 64
S5_WIDTH = D_MODEL // 4
S5_CH = 16
S5_GROUPS = S5_WIDTH // S5_CH
S5_STATE = 64
S5_NSTATE = S5_GROUPS * S5_STATE
DEEPNORM_ALPHA = (2 * DEPTH) ** 0.25
MIXW = D_MODEL // 4

COL_DQ = 0
COL_DZ = 1536
COL_Q = 2048
COL_KV = 2560
COL_LX = 3072
COL_LG = 3584
COL_SU = 4096
N_PROJ = 4608
N_BA = 2 * 2 * DN_HEADS
LANE = 128
SUBLANES = 8
MXU = 256

MOD_ROWS = 8
PITCH_PAD = 4
OUT_SUB = 256
IN_SUB = 256
MLP_SUB = 256
MLP_MID = 512
VMEM_LIMIT = 56 * 1024 * 1024


def _cparams(sem):
    return pltpu.CompilerParams(dimension_semantics=sem, vmem_limit_bytes=VMEM_LIMIT)


def _ln(x):
    mu = jnp.mean(x, axis=-1, keepdims=True)
    xc = x - mu
    var = jnp.mean(xc * xc, axis=-1, keepdims=True)
    return xc * lax.rsqrt(var + EPS)


def _softplus(x):
    return jnp.maximum(x, 0.0) + jnp.log1p(jnp.exp(-jnp.abs(x)))


def _gelu(x):
    return 0.5 * x * (1.0 + jnp.tanh(0.7978845608028654 * (x + 0.044715 * (x * x * x))))


def _dot(a, b):
    return jnp.dot(a, b, preferred_element_type=F32)


def _dot_nt(a, b):
    return lax.dot_general(a, b, (((1,), (1,)), ((), ())), preferred_element_type=F32)


def _ada_kernel(c_ref, w_ref, b_ref, o_ref):
    cs = c_ref[...]
    s = cs * jax.nn.sigmoid(cs)
    o_ref[0] = _dot(s.astype(BF16), w_ref[0].astype(BF16)) + b_ref[0]


def ada_mod(cond, w_ada, b_ada, tn=1024):
    depth, d, n = w_ada.shape
    return pl.pallas_call(
        _ada_kernel,
        grid=(depth, n // tn),
        in_specs=[pl.BlockSpec((MOD_ROWS, d), lambda l, j: (0, 0)),
                  pl.BlockSpec((1, d, tn), lambda l, j: (l, 0, j)),
                  pl.BlockSpec((1, 1, tn), lambda l, j: (l, 0, j))],
        out_specs=pl.BlockSpec((1, MOD_ROWS, tn), lambda l, j: (l, 0, j)),
        out_shape=jax.ShapeDtypeStruct((depth, MOD_ROWS, n), F32),
        compiler_params=_cparams(("parallel", "parallel")),
        name="ada_mod",
    )(cond, w_ada, b_ada.reshape(depth, 1, n))


def _in_kernel(x_ref, sc_ref, sh_ref, w_ref, wba_ref, o_ref, ba_ref, u_ref):
    j = pl.program_id(1)

    @pl.when(j == 0)
    def _():
        for r0 in range(0, x_ref.shape[0], IN_SUB):
            rows = slice(r0, r0 + IN_SUB)
            u = (_ln(x_ref[rows, :]) * (1.0 + sc_ref[0]) + sh_ref[0]).astype(BF16)
            u_ref[rows, :] = u
            o_ref[rows, :] = _dot(u, w_ref[...])
            ba_ref[rows, :] = _dot(u, wba_ref[...])

    @pl.when(j > 0)
    def _():
        o_ref[...] = _dot(u_ref[...], w_ref[...])


def in_proj(x, mod, w, w_ba, layer, mod_row, tm=1024, tn=768):
    m, d = x.shape
    n = w.shape[2]

    def mod_spec(kind):
        return pl.BlockSpec((1, 1, d), lambda i, j: (kind * MOD_ROWS + mod_row(i * tm), 0, 0))

    return pl.pallas_call(
        _in_kernel,
        grid=(m // tm, n // tn),
        in_specs=[pl.BlockSpec((tm, d), lambda i, j: (i, 0)),
                  mod_spec(1), mod_spec(0),
                  pl.BlockSpec((None, d, tn), lambda i, j: (layer, 0, j)),
                  pl.BlockSpec((None, d, LANE), lambda i, j: (layer, 0, 0))],
        out_specs=[pl.BlockSpec((tm, tn), lambda i, j: (i, j)),
                   pl.BlockSpec((tm, LANE), lambda i, j: (i, 0))],
        out_shape=[jax.ShapeDtypeStruct((m, n), F32), jax.ShapeDtypeStruct((m, LANE), F32)],
        scratch_shapes=[pltpu.VMEM((tm, d), BF16)],
        compiler_params=_cparams(("parallel", "arbitrary")),
        name="in_proj",
    )(x, mod, mod, w, w_ba)


def _rms_heads(x, g, heads):
    outs = []
    for h in range(heads):
        xh = x[:, h * HEAD_DIM:(h + 1) * HEAD_DIM]
        outs.append(xh * lax.rsqrt(jnp.mean(xh * xh, axis=-1, keepdims=True) + EPS) * g)
    return outs


def _softmax_av(q, k, v, scale):
    s = _dot_nt(q, k)
    if scale is not None:
        s = s * scale
    m = jnp.max(s, axis=-1, keepdims=True)
    p = jnp.exp(s - m)
    l = jnp.sum(p, axis=-1, keepdims=True)
    return _dot(p.astype(BF16), v) / l


def _ctx_attn_kernel(q_ref, kv_ref, qg_ref, kg_ref, o_ref, kn_ref, v_ref):
    qs = _rms_heads(q_ref[...], qg_ref[...], ATTN_HEADS)
    kv = kv_ref[...]
    ks = _rms_heads(kv[:, :KV_WIDTH], kg_ref[...], ATTN_KV_HEADS)
    v = kv[:, KV_WIDTH:]
    v_ref[...] = v
    t = q_ref.shape[0]
    for kh in range(ATTN_KV_HEADS):
        kn_ref[:, kh * HEAD_DIM:(kh + 1) * HEAD_DIM] = ks[kh]
        q2 = jnp.concatenate([qs[kh * ATTN_GROUP + g] for g in range(ATTN_GROUP)], axis=0).astype(BF16)
        o = _softmax_av(q2, ks[kh].astype(BF16), v[:, kh * HEAD_DIM:(kh + 1) * HEAD_DIM].astype(BF16), ATTN_SCALE)
        for g in range(ATTN_GROUP):
            h = kh * ATTN_GROUP + g
            o_ref[:, h * HEAD_DIM:(h + 1) * HEAD_DIM] = o[g * t:(g + 1) * t].astype(BF16)


def ctx_attention(proj, qg, kg, batch, seq):
    return pl.pallas_call(
        _ctx_attn_kernel,
        grid=(batch,),
        in_specs=[pl.BlockSpec((seq, ATTN_WIDTH), lambda b: (b, COL_Q // ATTN_WIDTH)),
                  pl.BlockSpec((seq, 2 * KV_WIDTH), lambda b: (b, COL_KV // (2 * KV_WIDTH))),
                  pl.BlockSpec((1, HEAD_DIM), lambda b: (0, 0)),
                  pl.BlockSpec((1, HEAD_DIM), lambda b: (0, 0))],
        out_specs=[pl.BlockSpec((seq, ATTN_WIDTH), lambda b: (b, 0)),
                   pl.BlockSpec((seq, KV_WIDTH), lambda b: (b, 0)),
                   pl.BlockSpec((seq, KV_WIDTH), lambda b: (b, 0))],
        out_shape=[jax.ShapeDtypeStruct((batch * seq, ATTN_WIDTH), BF16),
                   jax.ShapeDtypeStruct((batch * seq, KV_WIDTH), F32),
                   jax.ShapeDtypeStruct((batch * seq, KV_WIDTH), F32)],
        compiler_params=_cparams(("parallel",)),
        name="ctx_attention",
    )(proj, proj, qg, kg)


def _rope(x, cos, sin, heads):
    w = x.shape[-1]
    lane = lax.broadcasted_iota(jnp.int32, x.shape, 1)
    quarter = HEAD_DIM // 4
    partner = jnp.where((lane % (2 * quarter)) < quarter,
                        pltpu.roll(x, w - quarter, 1), pltpu.roll(x, quarter, 1))
    cos_t = jnp.concatenate([cos] * heads, axis=1)
    sin_t = jnp.concatenate([sin] * heads, axis=1)
    return x * cos_t + partner * sin_t


def _lat_prep_kernel(q_ref, kv_ref, qg_ref, kg_ref, cos_ref, sin_ref, qo_ref, ko_ref, vo_ref):
    qn = jnp.concatenate(_rms_heads(q_ref[...], qg_ref[...], ATTN_HEADS), axis=1)
    kv = kv_ref[...]
    kn = jnp.concatenate(_rms_heads(kv[:, :KV_WIDTH], kg_ref[...], ATTN_KV_HEADS), axis=1)
    cos = cos_ref[...]
    sin = sin_ref[...]
    qo_ref[...] = (_rope(qn, cos, sin, ATTN_HEADS) * ATTN_SCALE).astype(BF16)
    ko_ref[...] = _rope(kn, cos, sin, ATTN_KV_HEADS).astype(BF16)
    vo_ref[...] = kv[:, KV_WIDTH:].astype(BF16)


def lat_prep(proj, qg, kg, cos, sin, seq, tm=512):
    m = proj.shape[0]
    per = seq // tm
    return pl.pallas_call(
        _lat_prep_kernel,
        grid=(m // tm,),
        in_specs=[pl.BlockSpec((tm, ATTN_WIDTH), lambda i: (i, COL_Q // ATTN_WIDTH)),
                  pl.BlockSpec((tm, 2 * KV_WIDTH), lambda i: (i, COL_KV // (2 * KV_WIDTH))),
                  pl.BlockSpec((1, HEAD_DIM), lambda i: (0, 0)),
                  pl.BlockSpec((1, HEAD_DIM), lambda i: (0, 0)),
                  pl.BlockSpec((tm, HEAD_DIM), lambda i: (i % per, 0)),
                  pl.BlockSpec((tm, HEAD_DIM), lambda i: (i % per, 0))],
        out_specs=[pl.BlockSpec((tm, ATTN_WIDTH), lambda i: (i, 0)),
                   pl.BlockSpec((tm, KV_WIDTH), lambda i: (i, 0)),
                   pl.BlockSpec((tm, KV_WIDTH), lambda i: (i, 0))],
        out_shape=[jax.ShapeDtypeStruct((m, ATTN_WIDTH), BF16),
                   jax.ShapeDtypeStruct((m, KV_WIDTH), BF16),
                   jax.ShapeDtypeStruct((m, KV_WIDTH), BF16)],
        compiler_params=_cparams(("parallel",)),
        name="lat_prep",
    )(proj, proj, qg, kg, cos, sin)


def _lat_attn_kernel(q_ref, k_ref, v_ref, o_ref):
    k = k_ref[0]
    v = v_ref[0]
    half = q_ref.shape[0] // 2
    sls = [(slice(r0, r0 + half), slice(g * HEAD_DIM, (g + 1) * HEAD_DIM))
           for r0 in (0, half) for g in range(ATTN_GROUP)]
    ss = [_dot_nt(q_ref[sl], k) for sl in sls]
    ps = [jnp.exp(s - jnp.max(s, axis=-1, keepdims=True)) for s in ss]
    ls = [jnp.sum(p, axis=-1, keepdims=True) for p in ps]
    os = [_dot(p.astype(BF16), v) for p in ps]
    for sl, o, l in zip(sls, os, ls):
        o_ref[sl] = (o / l).astype(BF16)


def lat_attention(q, k_all, v_all, seq, tq=256):
    b, s, _ = k_all.shape
    nq = seq // tq
    gw = ATTN_GROUP * HEAD_DIM
    return pl.pallas_call(
        _lat_attn_kernel,
        grid=(b, ATTN_KV_HEADS, nq),
        in_specs=[pl.BlockSpec((tq, gw), lambda bi, kh, qi: (bi * nq + qi, kh)),
                  pl.BlockSpec((1, s, HEAD_DIM), lambda bi, kh, qi: (bi, 0, kh)),
                  pl.BlockSpec((1, s, HEAD_DIM), lambda bi, kh, qi: (bi, 0, kh))],
        out_specs=pl.BlockSpec((tq, gw), lambda bi, kh, qi: (bi * nq + qi, kh)),
        out_shape=jax.ShapeDtypeStruct((b * seq, ATTN_WIDTH), BF16),
        compiler_params=_cparams(("parallel", "parallel", "arbitrary")),
        name="lat_attention",
    )(q, k_all, v_all)


def rope_tables(seq):
    t = jnp.arange(seq)
    row = (t // GRID_W).astype(F32)
    col = (t % GRID_W).astype(F32)
    quarter = HEAD_DIM // 4
    inv_freq = jnp.power(ROPE_THETA, -jnp.arange(quarter, dtype=F32) / quarter)
    ar = row[:, None] * inv_freq[None, :]
    ac = col[:, None] * inv_freq[None, :]
    cos = jnp.concatenate([jnp.cos(ar), jnp.cos(ar), jnp.cos(ac), jnp.cos(ac)], axis=1)
    sin = jnp.concatenate([-jnp.sin(ar), jnp.sin(ar), -jnp.sin(ac), jnp.sin(ac)], axis=1)
    return cos, sin


def _chunk_index(d, i, n):
    return i + d * (n - 1 - 2 * i)


def _lru_kernel(x_ref, xp_ref, xn_ref, cw_ref, cb_ref, wg_ref, bg_ref, sp_ref, h0_ref,
                h_ref, fin_ref, xpad_ref, a_ref, b_ref, hc_ref, *, nb, tt):
    d = pl.program_id(0)
    i = pl.program_id(2)
    n = pl.num_programs(2)
    ci = _chunk_index(d, i, n)
    pitch = tt + PITCH_PAD
    rows = nb * pitch
    nsl = LRU_WIDTH // LANE

    @pl.when(i == 0)
    def _():
        xpad_ref[...] = jnp.zeros_like(xpad_ref)
        hc_ref[...] = h0_ref[0]

    for s in range(nb):
        base = 8 + s * pitch
        xpad_ref[base:base + tt, :] = x_ref[s]
        xpad_ref[base - 1:base, :] = jnp.where(ci == 0, 0.0, xp_ref[s, 7:8, :])
        xpad_ref[base + tt:base + tt + 2, :] = jnp.where(ci == n - 1, 0.0, xn_ref[s, 0:2, :])
    w = cw_ref[...]
    xc = cb_ref[...] + sum(xpad_ref[7 + j:7 + j + rows, :] * w[j:j + 1, :] for j in range(CONV_W))
    pre = _dot(xc.astype(BF16), wg_ref[0]) + bg_ref[0]
    r = jax.nn.sigmoid(pre[:, :LRU_WIDTH])
    ig = jax.nn.sigmoid(pre[:, LRU_WIDTH:])
    a = jnp.exp((-LRU_C) * r * sp_ref[0])
    inp = jnp.sqrt(1.0 - a * a) * (ig * xc)
    for c in range(nsl):
        a_ref[c] = a[:, c * LANE:(c + 1) * LANE]
        b_ref[c] = inp[:, c * LANE:(c + 1) * LANE]

    def body(t, carry):
        row = t + d * (tt - 1 - 2 * t)
        out = []
        for c in range(nsl):
            idx = (c, pl.ds(row, nb, stride=pitch), slice(None))
            h = a_ref[idx] * carry[c] + b_ref[idx]
            b_ref[idx] = h
            out.append(h)
        return tuple(out)

    carry = tuple(hc_ref[:, c * LANE:(c + 1) * LANE] for c in range(nsl))
    carry = lax.fori_loop(0, tt, body, carry, unroll=8)
    for c in range(nsl):
        hc_ref[:, c * LANE:(c + 1) * LANE] = carry[c]
        for s in range(nb):
            h_ref[0, s, :, c * LANE:(c + 1) * LANE] = b_ref[c, s * pitch:s * pitch + tt, :]

    @pl.when(i == n - 1)
    def _():
        fin_ref[0] = hc_ref[...]


def lru_scan(proj3, h0, cw, cb, wg, bg, sp, nb, tt=128):
    b, t, _ = proj3.shape
    w = LRU_WIDTH
    n = t // tt
    t8 = tt // 8
    rows = nb * (tt + PITCH_PAD)
    col = COL_LX // w

    def cidx(d, i):
        return _chunk_index(d, i, n)

    return pl.pallas_call(
        functools.partial(_lru_kernel, nb=nb, tt=tt),
        grid=(2, b // nb, n),
        in_specs=[pl.BlockSpec((nb, tt, w), lambda d, g, i: (g, cidx(d, i), col)),
                  pl.BlockSpec((nb, 8, w), lambda d, g, i: (g, jnp.maximum(cidx(d, i) * t8 - 1, 0), col)),
                  pl.BlockSpec((nb, 8, w), lambda d, g, i: (g, jnp.minimum((cidx(d, i) + 1) * t8, t // 8 - 1), col)),
                  pl.BlockSpec((CONV_W, w), lambda d, g, i: (0, 0)),
                  pl.BlockSpec((1, w), lambda d, g, i: (0, 0)),
                  pl.BlockSpec((1, w, 2 * w), lambda d, g, i: (d, 0, 0)),
                  pl.BlockSpec((1, 1, 2 * w), lambda d, g, i: (d, 0, 0)),
                  pl.BlockSpec((1, 1, w), lambda d, g, i: (d, 0, 0)),
                  pl.BlockSpec((1, nb, w), lambda d, g, i: (d, g, 0))],
        out_specs=[pl.BlockSpec((1, nb, tt, w), lambda d, g, i: (d, g, cidx(d, i), 0)),
                   pl.BlockSpec((1, nb, w), lambda d, g, i: (d, g, 0))],
        out_shape=[jax.ShapeDtypeStruct((2, b, t, w), F32), jax.ShapeDtypeStruct((2, b, w), F32)],
        scratch_shapes=[pltpu.VMEM((rows + 16, w), F32),
                        pltpu.VMEM((w // LANE, rows, LANE), F32),
                        pltpu.VMEM((w // LANE, rows, LANE), F32),
                        pltpu.VMEM((nb, w), F32)],
        compiler_params=_cparams(("parallel", "parallel", "arbitrary")),
        name="lru_scan",
    )(proj3, proj3, proj3, cw, cb, wg, bg, sp, h0)


def _s5_kernel(u_ref, bb_ref, cc_ref, a_ref, h0_ref, y_ref, fin_ref, up_ref, s_ref, hc_ref, *, nb, tt):
    d = pl.program_id(0)
    i = pl.program_id(2)
    n = pl.num_programs(2)
    pitch = tt + PITCH_PAD
    nsl = S5_NSTATE // LANE

    @pl.when(i == 0)
    def _():
        up_ref[...] = jnp.zeros_like(up_ref)
        hc_ref[...] = h0_ref[0]

    for s in range(nb):
        up_ref[s * pitch:s * pitch + tt, :] = u_ref[s]
    u2 = up_ref[...].astype(BF16)
    per_k = MXU // S5_CH * S5_STATE // MXU
    vs = SUBLANES // nb
    rows_seq = nb * pitch
    npk = nsl // vs

    def slab(c):
        return (c // vs, slice((c % vs) * rows_seq, (c % vs + 1) * rows_seq), slice(None))

    for nt in range(2 * S5_NSTATE // MXU):
        kt = (nt % (S5_NSTATE // MXU)) // per_k
        tile = _dot(u2[:, kt * MXU:(kt + 1) * MXU], bb_ref[0, kt * MXU:(kt + 1) * MXU, nt * MXU:(nt + 1) * MXU])
        s_ref[slab(2 * nt)] = tile[:, :LANE]
        s_ref[slab(2 * nt + 1)] = tile[:, LANE:]

    a_all = a_ref[0]

    def packed_rows(src, off, j):
        parts = [jnp.broadcast_to(src[:, off + (j * vs + v) * LANE:off + (j * vs + v + 1) * LANE], (nb, LANE))
                 for v in range(vs)]
        return parts[0] if vs == 1 else jnp.concatenate(parts, axis=0)

    group = 8
    for j0 in range(0, npk, group):
        js = list(range(j0, min(j0 + group, npk)))
        ar = [packed_rows(a_all, 0, j) for j in js]
        ai = [packed_rows(a_all, S5_NSTATE, j) for j in js]

        def body(t, carry, js=js, ar=ar, ai=ai):
            row = t + d * (tt - 1 - 2 * t)
            out = []
            for k, j in enumerate(js):
                hr, hi = carry[2 * k], carry[2 * k + 1]
                ire = (j, pl.ds(row, SUBLANES, stride=pitch), slice(None))
                iim = (npk + j, pl.ds(row, SUBLANES, stride=pitch), slice(None))
                nr = ar[k] * hr - ai[k] * hi + s_ref[ire]
                ni = ar[k] * hi + ai[k] * hr + s_ref[iim]
                s_ref[ire] = nr
                s_ref[iim] = ni
                out += [nr, ni]
            return tuple(out)

        carry = []
        for j in js:
            carry += [packed_rows(hc_ref, 0, j), packed_rows(hc_ref, S5_NSTATE, j)]
        carry = lax.fori_loop(0, tt, body, tuple(carry), unroll=4)
        for k, j in enumerate(js):
            for v in range(vs):
                c = j * vs + v
                hc_ref[:, c * LANE:(c + 1) * LANE] = carry[2 * k][v * nb:(v + 1) * nb]
                hc_ref[:, S5_NSTATE + c * LANE:S5_NSTATE + (c + 1) * LANE] = carry[2 * k + 1][v * nb:(v + 1) * nb]

    n_out = S5_WIDTH // MXU
    per_n = nsl // n_out
    for nt in range(n_out):
        slabs = [per_n * nt + k for k in range(per_n)] + [nsl + per_n * nt + k for k in range(per_n)]
        lhs = jnp.concatenate([s_ref[slab(c)] for c in slabs], axis=1).astype(BF16)
        y = _dot(lhs, cc_ref[0, nt])
        for s in range(nb):
            y_ref[0, s, :, nt * MXU:(nt + 1) * MXU] = y[s * pitch:s * pitch + tt]

    @pl.when(i == n - 1)
    def _():
        fin_ref[0] = hc_ref[...]


def s5_scan(proj3, h0, bb, cc, a, nb, tt=128):
    b, t, _ = proj3.shape
    w = S5_WIDTH
    n = t // tt
    rows = nb * (tt + PITCH_PAD)
    ns2 = 2 * S5_NSTATE

    def cidx(d, i):
        return _chunk_index(d, i, n)

    return pl.pallas_call(
        functools.partial(_s5_kernel, nb=nb, tt=tt),
        grid=(2, b // nb, n),
        in_specs=[pl.BlockSpec((nb, tt, w), lambda d, g, i: (g, cidx(d, i), COL_SU // w)),
                  pl.BlockSpec((1, w, ns2), lambda d, g, i: (d, 0, 0)),
                  pl.BlockSpec((1,) + cc.shape[1:], lambda d, g, i: (d, 0, 0, 0)),
                  pl.BlockSpec((1, 1, ns2), lambda d, g, i: (d, 0, 0)),
                  pl.BlockSpec((1, nb, ns2), lambda d, g, i: (d, g, 0))],
        out_specs=[pl.BlockSpec((1, nb, tt, w), lambda d, g, i: (d, g, cidx(d, i), 0)),
                   pl.BlockSpec((1, nb, ns2), lambda d, g, i: (d, g, 0))],
        out_shape=[jax.ShapeDtypeStruct((2, b, t, w), F32), jax.ShapeDtypeStruct((2, b, ns2), F32)],
        scratch_shapes=[pltpu.VMEM((rows, w), F32),
                        pltpu.VMEM((ns2 // LANE // (SUBLANES // nb), SUBLANES // nb * rows, LANE), F32),
                        pltpu.VMEM((nb, ns2), F32)],
        compiler_params=_cparams(("parallel", "parallel", "arbitrary")),
        name="s5_scan",
    )(proj3, bb, cc, a, h0)


def _split3(x):
    x1 = x.astype(BF16)
    r1 = x - x1.astype(F32)
    x2 = r1.astype(BF16)
    x3 = (r1 - x2.astype(F32)).astype(BF16)
    return x1, x2, x3


def _mm(x, y):
    return _dot(x.astype(BF16), y.astype(BF16))


def _unit_tri_inverses(mats, rr, cc):
    same16 = (rr // 16) == (cc // 16)
    same32 = (rr // 32) == (cc // 32)
    p = [jnp.where(same16, -a, 0.0) for a in mats]
    t = [jnp.where(rr == cc, 1.0, x) for x in p]
    for _ in range(3):
        p = [_mm(x, x) for x in p]
        t = [x + _mm(x, y) for x, y in zip(t, p)]
    for mask in (same32 & jnp.logical_not(same16), jnp.logical_not(same32)):
        te = [_mm(x, jnp.where(mask, a, 0.0)) for x, a in zip(t, mats)]
        t = [x - _mm(y, x) for x, y in zip(t, te)]
    return t


def _dn_prep_kernel(x_ref, xp_ref, xn_ref, cw_ref, o_ref, xpad_ref, *, tb):
    i = pl.program_id(1)
    n = pl.num_programs(1)
    hw = DN_HEADS * DN_DK
    xpad_ref[8:8 + tb, :] = x_ref[0]
    xpad_ref[0:8, :] = jnp.where(i == 0, 0.0, xp_ref[0])
    xpad_ref[8 + tb:16 + tb, :] = jnp.where(i == n - 1, 0.0, xn_ref[0])
    w = cw_ref[...]
    xc = sum(xpad_ref[7 + j:7 + j + tb, :] * w[j:j + 1, :] for j in range(CONV_W))
    qkv = xc * jax.nn.sigmoid(xc)
    for h in range(DN_HEADS):
        q = qkv[:, h * DN_DK:(h + 1) * DN_DK]
        k = qkv[:, hw + h * DN_DK:hw + (h + 1) * DN_DK]
        o_ref[0, :, h * DN_DK:(h + 1) * DN_DK] = (
            q * lax.rsqrt(jnp.sum(q * q, axis=-1, keepdims=True) + EPS) * (DN_DK ** -0.5))
        o_ref[0, :, hw + h * DN_DK:hw + (h + 1) * DN_DK] = (
            k * lax.rsqrt(jnp.sum(k * k, axis=-1, keepdims=True) + EPS))
    o_ref[0, :, 2 * hw:] = qkv[:, 2 * hw:]


def dn_prep(proj3, cw, tb=256):
    b, t, _ = proj3.shape
    n = t // tb
    t8 = tb // 8
    w3 = 3 * DN_WIDTH
    cq = COL_DQ // w3
    return pl.pallas_call(
        functools.partial(_dn_prep_kernel, tb=tb),
        grid=(b, n),
        in_specs=[pl.BlockSpec((1, tb, w3), lambda bi, i: (bi, i, cq)),
                  pl.BlockSpec((1, 8, w3), lambda bi, i: (bi, jnp.maximum(i * t8 - 1, 0), cq)),
                  pl.BlockSpec((1, 8, w3), lambda bi, i: (bi, jnp.minimum((i + 1) * t8, t // 8 - 1), cq)),
                  pl.BlockSpec((CONV_W, w3), lambda bi, i: (0, 0))],
        out_specs=pl.BlockSpec((1, tb, w3), lambda bi, i: (bi, i, 0)),
        out_shape=jax.ShapeDtypeStruct((b, t, w3), F32),
        scratch_shapes=[pltpu.VMEM((tb + 16, w3), F32)],
        compiler_params=_cparams(("parallel", "parallel")),
        name="dn_prep",
    )(proj3, proj3, proj3, cw)


def _dn_kernel(xf_ref, baf_ref, xb_ref, bab_ref, al_ref, dtb_ref, s0_ref, of_ref, ob_ref, fin_ref, s_ref,
               *, nbat, tb):
    i = pl.program_id(1)
    n = pl.num_programs(1)
    ch = DN_CHUNK
    nc = tb // ch
    hw = DN_HEADS * DN_DK

    @pl.when(i == 0)
    def _():
        s_ref[...] = s0_ref[...]

    rb = lax.broadcasted_iota(jnp.int32, (tb, tb), 0)
    cb = lax.broadcasted_iota(jnp.int32, (tb, tb), 1)
    rr = lax.broadcasted_iota(jnp.int32, (ch, ch), 0)
    cc = lax.broadcasted_iota(jnp.int32, (ch, ch), 1)
    al = al_ref[...]
    dtb = dtb_ref[...]

    x_refs = (xf_ref, xb_ref)
    beta_tab, gc_tab, masks = {}, {}, {}
    for d in range(2):
        reverse = d == 1
        ba_ref = bab_ref if reverse else baf_ref
        tri = ((rb // ch) == (cb // ch)) & ((cb >= rb) if reverse else (cb <= rb))
        tri = jnp.where(tri, 1.0, 0.0).astype(BF16)
        masks[d] = ((cc >= rr) if reverse else (cc <= rr), (cc > rr) if reverse else (cc < rr))
        for bat in range(nbat):
            ba = ba_ref[bat]
            beta_tab[(d, bat)] = jax.nn.sigmoid(ba)
            g_all = -jnp.exp(al) * _softplus(ba + dtb)
            gc_tab[(d, bat)] = sum(_dot(tri, piece) for piece in _split3(g_all))
    units = [dict(d=d, bat=bat, c=c, h=h) for d in range(2) for bat in range(nbat) for c in range(nc)
             for h in range(DN_HEADS)]

    def rows_of(u):
        return slice(u['c'] * ch, (u['c'] + 1) * ch)

    def q_of(u):
        return x_refs[u['d']][u['bat'], rows_of(u), u['h'] * DN_DK:(u['h'] + 1) * DN_DK]

    def k_of(u):
        return x_refs[u['d']][u['bat'], rows_of(u), hw + u['h'] * DN_DK:hw + (u['h'] + 1) * DN_DK]

    def v_of(u):
        return x_refs[u['d']][u['bat'], rows_of(u), 2 * hw + u['h'] * DN_DV:2 * hw + (u['h'] + 1) * DN_DV]

    def beta_of(u):
        lane = u['d'] * DN_HEADS + u['h']
        return beta_tab[(u['d'], u['bat'])][rows_of(u), lane:lane + 1]

    def gcol_of(u):
        lane = 2 * DN_HEADS + u['d'] * DN_HEADS + u['h']
        return gc_tab[(u['d'], u['bat'])][rows_of(u), lane:lane + 1]

    def glast_of(u):
        gcol = gcol_of(u)
        return gcol[0:1, :] if u['d'] == 1 else gcol[ch - 1:ch, :]

    gct = {}
    for u in units:
        key = (u['d'], u['bat'], u['c'])
        if key not in gct:
            gct[key] = gc_tab[(u['d'], u['bat'])][rows_of(u), :].T
        lane = 2 * DN_HEADS + u['d'] * DN_HEADS + u['h']
        incl, strict = masks[u['d']]
        decay = jnp.where(incl, jnp.exp(gcol_of(u) - gct[key][lane:lane + 1, :]), 0.0)
        k = k_of(u)
        kbf = k.astype(BF16)
        u['a'] = jnp.where(strict, _dot_nt((k * beta_of(u)).astype(BF16), kbf) * decay, 0.0)
        u['qk'] = (_dot_nt(q_of(u).astype(BF16), kbf) * decay).astype(BF16)
    t_inv = _unit_tri_inverses([u['a'] for u in units], rr, cc)
    for u, t in zip(units, t_inv):
        beta = beta_of(u)
        rhs = jnp.concatenate([v_of(u) * beta, k_of(u) * beta * jnp.exp(gcol_of(u))], axis=1)
        sol = rhs + _mm(jnp.where(rr == cc, 0.0, t), rhs)
        u['u_val'] = sol[:, :DN_DV]
        u['w'] = sol[:, DN_DV:].astype(BF16)

    state = {(d, bat, h): s_ref[d, bat, h] for d in range(2) for bat in range(nbat) for h in range(DN_HEADS)}
    for j in range(nc):
        cur = [u for u in units if u['c'] == (nc - 1 - j if u['d'] == 1 else j)]
        keys = [(u['d'], u['bat'], u['h']) for u in cur]
        sb = [state[key].astype(BF16) for key in keys]
        ws = [_dot(u['w'], s) for u, s in zip(cur, sb)]
        qs = [_dot((q_of(u) * jnp.exp(gcol_of(u))).astype(BF16), s) for u, s in zip(cur, sb)]
        vb = [(u['u_val'] - x).astype(BF16) for u, x in zip(cur, ws)]
        os = [x + _dot(u['qk'], y) for u, x, y in zip(cur, qs, vb)]
        kdec_t = [(k_of(u) * jnp.exp(glast_of(u) - gcol_of(u))).T.astype(BF16) for u in cur]
        sn = [state[key] * jnp.exp(glast_of(u)) + _dot(kt, y) for u, key, kt, y in zip(cur, keys, kdec_t, vb)]
        for u, key, o, s in zip(cur, keys, os, sn):
            state[key] = s
            o_ref = ob_ref if u['d'] == 1 else of_ref
            o_ref[u['bat'], u['c'] * ch:(u['c'] + 1) * ch, u['h'] * DN_DV:(u['h'] + 1) * DN_DV] = o
    for (d, bat, h), s in state.items():
        s_ref[d, bat, h] = s

    @pl.when(i == n - 1)
    def _():
        fin_ref[...] = s_ref[...]


def dn_scan(qkv3, ba3, s0, al, dtb, nbat, tb):
    b, t, w3 = qkv3.shape
    n = t // tb

    def data_specs(cidx):
        return [pl.BlockSpec((nbat, tb, w3), lambda g, i: (g, cidx(i), 0)),
                pl.BlockSpec((nbat, tb, LANE), lambda g, i: (g, cidx(i), 0))]

    fwd = lambda i: i
    bwd = lambda i: n - 1 - i
    st_spec = pl.BlockSpec((2, nbat, DN_HEADS, DN_DK, DN_DV), lambda g, i: (0, g, 0, 0, 0))
    return pl.pallas_call(
        functools.partial(_dn_kernel, nbat=nbat, tb=tb),
        grid=(b // nbat, n),
        in_specs=data_specs(fwd) + data_specs(bwd) + [
            pl.BlockSpec((1, LANE), lambda g, i: (0, 0)),
            pl.BlockSpec((1, LANE), lambda g, i: (0, 0)),
            st_spec],
        out_specs=[pl.BlockSpec((nbat, tb, DN_WIDTH), lambda g, i: (g, i, 0)),
                   pl.BlockSpec((nbat, tb, DN_WIDTH), lambda g, i: (g, n - 1 - i, 0)),
                   st_spec],
        out_shape=[jax.ShapeDtypeStruct((b, t, DN_WIDTH), F32),
                   jax.ShapeDtypeStruct((b, t, DN_WIDTH), F32),
                   jax.ShapeDtypeStruct((2, b, DN_HEADS, DN_DK, DN_DV), F32)],
        scratch_shapes=[pltpu.VMEM((2, nbat, DN_HEADS, DN_DK, DN_DV), F32)],
        compiler_params=_cparams(("parallel", "arbitrary")),
        name="dn_scan",
    )(qkv3, ba3, qkv3, ba3, al, dtb, s0)


def _post_kernel(hf_ref, hb_ref, lg_ref, of_ref, ob_ref, dz_ref, ng_ref, yf_ref, yb_ref, su_ref, sd_ref,
                 gw_ref, gb_ref, lru_ref, dn_ref, s5_ref):
    lru_ref[...] = ((hf_ref[0, 0] + hb_ref[0, 0]) * _gelu(lg_ref[...])).astype(BF16)
    o = of_ref[0] + ob_ref[0]
    dz = dz_ref[...]
    for h in range(DN_HEADS):
        sl = slice(h * DN_DV, (h + 1) * DN_DV)
        oh = o[:, sl]
        oh = oh * lax.rsqrt(jnp.mean(oh * oh, axis=-1, keepdims=True) + EPS) * ng_ref[...]
        zh = dz[:, sl]
        dn_ref[:, sl] = (oh * (zh * jax.nn.sigmoid(zh))).astype(BF16)
    y = yf_ref[0, 0] + yb_ref[0, 0] + sd_ref[...] * su_ref[...]
    gy = _gelu(y)
    s5_ref[...] = (gy * jax.nn.sigmoid(_dot(gy.astype(BF16), gw_ref[...]) + gb_ref[...])).astype(BF16)


def mix_post(proj, h_lru, o_f, o_b, y_s5, ng, sd, gw, gb):
    m = proj.shape[0]
    _, b, t, w = h_lru.shape
    tm = min(512, t)
    per = t // tm
    tok = pl.BlockSpec((tm, w), lambda i: (i, 0))

    def pcol(c0):
        return pl.BlockSpec((tm, w), lambda i: (i, c0 // w))

    def dirspec(d):
        return pl.BlockSpec((1, 1, tm, w), lambda i: (d, i // per, i % per, 0))

    seq = pl.BlockSpec((1, tm, w), lambda i: (i // per, i % per, 0))

    def vec(n):
        return pl.BlockSpec((1, n), lambda i: (0, 0))

    out = jax.ShapeDtypeStruct((m, w), BF16)
    return pl.pallas_call(
        _post_kernel,
        grid=(m // tm,),
        in_specs=[dirspec(0), dirspec(1), pcol(COL_LG), seq, seq, pcol(COL_DZ), vec(DN_DV),
                  dirspec(0), dirspec(1), pcol(COL_SU), vec(w), pl.BlockSpec((w, w), lambda i: (0, 0)), vec(w)],
        out_specs=[tok, tok, tok],
        out_shape=[out, out, out],
        compiler_params=_cparams(("parallel",)),
        name="mix_post",
    )(h_lru, h_lru, proj, o_f, o_b, proj, ng, y_s5, y_s5, proj, sd, gw, gb)


def _out_kernel(x_ref, a_ref, b_ref, c_ref, d_ref, w_ref, gate_ref, g_ref, bb_ref, x1_ref):
    q = MIXW
    for r0 in range(0, x_ref.shape[0], OUT_SUB):
        rows = slice(r0, r0 + OUT_SUB)
        y = _dot(a_ref[rows, :], w_ref[0:q, :])
        y += _dot(b_ref[rows, :], w_ref[q:2 * q, :])
        y += _dot(c_ref[rows, :], w_ref[2 * q:3 * q, :])
        y += _dot(d_ref[rows, :], w_ref[3 * q:, :])
        z = DEEPNORM_ALPHA * x_ref[rows, :] + gate_ref[0] * y
        x1_ref[rows, :] = _ln(z) * g_ref[...] + bb_ref[...]


def out_proj(x, parts, w, layer, mod, ln_g, ln_b, mod_row, tm=512):
    m, d = x.shape
    vec = pl.BlockSpec((1, d), lambda i: (0, 0))
    part = pl.BlockSpec((tm, MIXW), lambda i: (i, 0))
    return pl.pallas_call(
        _out_kernel,
        grid=(m // tm,),
        in_specs=[pl.BlockSpec((tm, d), lambda i: (i, 0)), part, part, part, part,
                  pl.BlockSpec((None, d, d), lambda i: (layer, 0, 0)),
                  pl.BlockSpec((1, 1, d), lambda i: (2 * MOD_ROWS + mod_row(i * tm), 0, 0)), vec, vec],
        out_specs=pl.BlockSpec((tm, d), lambda i: (i, 0)),
        out_shape=jax.ShapeDtypeStruct((m, d), F32),
        compiler_params=_cparams(("parallel",)),
        name="out_proj",
    )(x, *parts, w, mod, ln_g, ln_b)


def _mlp_kernel(x_ref, w1_ref, w2_ref, sc_ref, sh_ref, gate_ref, g_ref, b_ref, o_ref, u_ref):
    f = pl.program_id(1)
    last = pl.num_programs(1) - 1
    tm = x_ref.shape[0]

    def hidden(u):
        return jnp.square(jnp.maximum(_dot(u, w1_ref[...]), 0.0)).astype(BF16)

    @pl.when(f == 0)
    def _():
        for r0 in range(0, tm, MLP_SUB):
            rows = slice(r0, r0 + MLP_SUB)
            u = (_ln(x_ref[rows, :]) * (1.0 + sc_ref[0]) + sh_ref[0]).astype(BF16)
            u_ref[rows, :] = u
            o_ref[rows, :] = _dot(hidden(u), w2_ref[...])

    @pl.when((f > 0) & (f < last))
    def _():
        for r0 in range(0, tm, MLP_MID):
            rows = slice(r0, r0 + MLP_MID)
            o_ref[rows, :] += _dot(hidden(u_ref[rows, :]), w2_ref[...])

    @pl.when(f == last)
    def _():
        for r0 in range(0, tm, MLP_SUB):
            rows = slice(r0, r0 + MLP_SUB)
            a = o_ref[rows, :] + _dot(hidden(u_ref[rows, :]), w2_ref[...])
            z = DEEPNORM_ALPHA * x_ref[rows, :] + gate_ref[0] * a
            o_ref[rows, :] = _ln(z) * g_ref[...] + b_ref[...]


def mlp(x, w1, w2, layer, mod, ln_g, ln_b, mod_row, tm=1024, tf=512):
    m, d = x.shape
    ff = w1.shape[2]
    assert ff // tf >= 2
    vec = pl.BlockSpec((1, d), lambda i, f: (0, 0))

    def mod_spec(kind):
        return pl.BlockSpec((1, 1, d), lambda i, f: (kind * MOD_ROWS + mod_row(i * tm), 0, 0))

    return pl.pallas_call(
        _mlp_kernel,
        grid=(m // tm, ff // tf),
        in_specs=[pl.BlockSpec((tm, d), lambda i, f: (i, 0)),
                  pl.BlockSpec((None, d, tf), lambda i, f: (layer, 0, f)),
                  pl.BlockSpec((None, tf, d), lambda i, f: (layer, f, 0)),
                  mod_spec(4), mod_spec(3), mod_spec(5), vec, vec],
        out_specs=pl.BlockSpec((tm, d), lambda i, f: (i, 0)),
        out_shape=jax.ShapeDtypeStruct((m, d), F32),
        scratch_shapes=[pltpu.VMEM((tm, d), BF16)],
        compiler_params=_cparams(("parallel", "arbitrary")),
        name="mlp",
    )(x, w1, w2, mod, mod, mod, ln_g, ln_b)


def _reorder_w_in(w):
    def cols(c0, n):
        return w[..., c0:c0 + n]

    aq, akv, lx, lg = cols(0, 512), cols(512, 512), cols(1024, 512), cols(1536, 512)
    dqkv, dz, ba, su = cols(2048, 1536), cols(3584, 512), cols(4096, N_BA), cols(4096 + N_BA, 512)
    pad = jnp.zeros(w.shape[:2] + (LANE - N_BA,), w.dtype)
    main = jnp.concatenate([dqkv, dz, aq, akv, lx, lg, su], axis=-1).astype(BF16)
    return main, jnp.concatenate([ba, pad], axis=-1).astype(BF16)


def _block_diag(blocks):
    n, r, c = blocks.shape[-3:]
    eye = jnp.eye(n, dtype=blocks.dtype)
    out = blocks[..., :, :, None, :] * eye[:, None, :, None]
    return out.reshape(blocks.shape[:-3] + (n * r, n * c))


def _lru_params(wa, ba, wx, bx, lam):
    wg = jnp.concatenate([_block_diag(wa.astype(BF16)), _block_diag(wx.astype(BF16))], axis=-1)
    bg = jnp.concatenate([ba, bx], axis=-1)[..., None, :]
    sp = jax.nn.softplus(-lam)[..., None, :]
    return wg, bg, sp


def _s5_params(lam_re, lam_im, log_dt, b_re, b_im, c_re, c_im):
    lead = lam_re.shape[:-2]
    dt = jnp.exp(log_dt)[..., None]
    mag = jnp.exp(lam_re * dt)
    abar_re = mag * jnp.cos(lam_im * dt)
    abar_im = mag * jnp.sin(lam_im * dt)
    den = lam_re * lam_re + lam_im * lam_im
    nr = abar_re - 1.0
    ni = abar_im
    f_re = (nr * lam_re + ni * lam_im) / den
    f_im = (ni * lam_re - nr * lam_im) / den
    bb_re = f_re[..., None] * b_re - f_im[..., None] * b_im
    bb_im = f_re[..., None] * b_im + f_im[..., None] * b_re
    to_in = lambda m: _block_diag(jnp.swapaxes(m, -1, -2).astype(BF16))
    bb = jnp.concatenate([to_in(bb_re), to_in(bb_im)], axis=-1)
    n_out = S5_WIDTH // MXU
    gpt = S5_GROUPS // n_out

    def to_out(m, nt):
        return _block_diag(jnp.swapaxes(m[..., nt * gpt:(nt + 1) * gpt, :, :], -1, -2).astype(BF16))

    cc = jnp.stack([jnp.concatenate([to_out(c_re, nt), to_out(-c_im, nt)], axis=-2)
                    for nt in range(n_out)], axis=-3)
    a = jnp.concatenate([abar_re.reshape(lead + (1, -1)), abar_im.reshape(lead + (1, -1))], axis=-1)
    return bb, cc, a


def _lane_row(vals, offset):
    return jnp.zeros((1, LANE), F32).at[0, offset:offset + vals.size].set(vals.reshape(-1))


def kernel(x_prompt, x_sample, cache_attn_k, cache_attn_v, state_rglru, state_delta, state_s5_re, state_s5_im, c, c_ctx, w_ada, b_ada, w_in, w_out, ln1_g, ln1_b, ln2_g, ln2_b, w_mlp1, w_mlp2, q_norm_g, k_norm_g, lru_conv_w, lru_conv_b, lru_wa, lru_ba, lru_wx, lru_bx, lru_lambda, dn_conv_w, dn_a_log, dn_dt_bias, dn_norm_g, s5_lambda_re, s5_lambda_im, s5_log_dt, s5_b_re, s5_b_im, s5_c_re, s5_c_im, s5_d, s5_glu_w, s5_glu_b):
    bp, tp, d = x_prompt.shape
    bs, ts, _ = x_sample.shape
    ctx_row = bs

    cond = jnp.concatenate([c, c_ctx[None, :], jnp.zeros((MOD_ROWS - bs - 1, d), F32)], axis=0)
    mods = ada_mod(cond, w_ada, b_ada)
    w_in_r, w_ba = _reorder_w_in(w_in)
    w_out_b = w_out.astype(BF16)
    w1_b = w_mlp1.astype(BF16)
    w2_b = w_mlp2.astype(BF16)
    cos, sin = rope_tables(ts)
    wg_all, bg_all, sp_all = _lru_params(lru_wa, lru_ba, lru_wx, lru_bx, lru_lambda)
    bb_all, cc_all, a5_all = _s5_params(s5_lambda_re, s5_lambda_im, s5_log_dt, s5_b_re, s5_b_im, s5_c_re, s5_c_im)

    streams = {
        'ctx': dict(x=x_prompt.reshape(bp * tp, d), b=bp, t=tp, nb=8, s5_tt=128, dn_nbat=2, dn_tb=tp,
                    mod_row=lambda tok: ctx_row),
        'lat': dict(x=x_sample.reshape(bs * ts, d), b=bs, t=ts, nb=bs, s5_tt=256, dn_nbat=bs, dn_tb=2 * DN_CHUNK,
                    mod_row=lambda tok: tok // ts),
    }
    ks, vs, lrus, dns, s5rs, s5is = [], [], [], [], [], []
    for l in range(DEPTH):
        mod = mods[l].reshape(MOD_ROWS, N_MOD, d).transpose(1, 0, 2).reshape(N_MOD * MOD_ROWS, 1, d)
        qg = q_norm_g[l].reshape(1, HEAD_DIM)
        kg = k_norm_g[l].reshape(1, HEAD_DIM)
        wg, bg, sp = wg_all[l], bg_all[l], sp_all[l]
        bb, cc, a5 = bb_all[l], cc_all[l], a5_all[l]
        al = _lane_row(dn_a_log[l], 2 * DN_HEADS)
        dtb = _lane_row(dn_dt_bias[l], 2 * DN_HEADS)
        for name, st in streams.items():
            b, t, nb = st['b'], st['t'], st['nb']
            is_ctx = name == 'ctx'
            proj, ba = in_proj(st['x'], mod, w_in_r, w_ba, l, st['mod_row'])
            proj3 = proj.reshape(b, t, N_PROJ)
            ba3 = ba.reshape(b, t, LANE)

            if is_ctx:
                attn, kn, vv = ctx_attention(proj, qg, kg, b, t)
                ks.append(kn.reshape(b, t, ATTN_KV_HEADS, HEAD_DIM))
                vs.append(vv.reshape(b, t, ATTN_KV_HEADS, HEAD_DIM))
                h0_lru = jnp.zeros((2, b, LRU_WIDTH), F32)
                s0_dn = jnp.zeros((2, b, DN_HEADS, DN_DK, DN_DV), F32)
                h0_s5 = jnp.zeros((2, b, 2 * S5_NSTATE), F32)
            else:
                q_s, k_s, v_s = lat_prep(proj, qg, kg, cos, sin, t)
                k_all = jnp.concatenate([cache_attn_k[:, l].reshape(b, -1, KV_WIDTH).astype(BF16),
                                         k_s.reshape(b, t, KV_WIDTH)], axis=1)
                v_all = jnp.concatenate([cache_attn_v[:, l].reshape(b, -1, KV_WIDTH).astype(BF16),
                                         v_s.reshape(b, t, KV_WIDTH)], axis=1)
                attn = lat_attention(q_s, k_all, v_all, t)
                h0_lru = jnp.swapaxes(state_rglru[:, l], 0, 1)
                s0_dn = jnp.swapaxes(state_delta[:, l], 0, 1)
                h0_s5 = jnp.swapaxes(jnp.concatenate([state_s5_re[:, l].reshape(b, 2, S5_NSTATE),
                                                      state_s5_im[:, l].reshape(b, 2, S5_NSTATE)], axis=-1), 0, 1)

            h_lru, lru_fin = lru_scan(proj3, h0_lru, lru_conv_w[l], lru_conv_b[l].reshape(1, -1), wg, bg, sp, nb)
            o_f, o_b, dn_fin = dn_scan(dn_prep(proj3, dn_conv_w[l]), ba3, s0_dn, al, dtb,
                                       st['dn_nbat'], st['dn_tb'])
            y_s5, s5_fin = s5_scan(proj3, h0_s5, bb, cc, a5, nb, st['s5_tt'])
            lru_out, dn_out, s5_out = mix_post(
                proj, h_lru, o_f, o_b, y_s5, dn_norm_g[l].reshape(1, DN_DV), s5_d[l].reshape(1, MIXW),
                s5_glu_w[l].astype(BF16), s5_glu_b[l].reshape(1, MIXW))
            if is_ctx:
                lrus.append(jnp.swapaxes(lru_fin, 0, 1))
                dns.append(jnp.swapaxes(dn_fin, 0, 1))
                s5_fin = jnp.swapaxes(s5_fin, 0, 1)
                s5rs.append(s5_fin[..., :S5_NSTATE].reshape(b, 2, S5_GROUPS, S5_STATE))
                s5is.append(s5_fin[..., S5_NSTATE:].reshape(b, 2, S5_GROUPS, S5_STATE))

            x1 = out_proj(st['x'], [attn, lru_out, dn_out, s5_out], w_out_b, l, mod,
                          ln1_g[l].reshape(1, d), ln1_b[l].reshape(1, d), st['mod_row'])
            st['x'] = mlp(x1, w1_b, w2_b, l, mod, ln2_g[l].reshape(1, d), ln2_b[l].reshape(1, d), st['mod_row'])

    y_prompt = streams['ctx']['x'].reshape(bp, tp, d)
    y_sample = streams['lat']['x'].reshape(bs, ts, d)
    return (y_prompt, y_sample, jnp.stack(ks, axis=1), jnp.stack(vs, axis=1), jnp.stack(lrus, axis=1),
            jnp.stack(dns, axis=1), jnp.stack(s5rs, axis=1), jnp.stack(s5is, axis=1))
```

```python
import functools

import jax
import jax.numpy as jnp
from jax import lax
from jax.experimental import pallas as pl
from jax.experimental.pallas import tpu as pltpu

F32 = jnp.float32
BF16 = jnp.bfloat16

D_MODEL = 2048
DEPTH = 2
GRID_W = 64
CONV_W = 4
EPS = 1e-6
ROPE_THETA = 10000.0
N_MOD = 6
HEAD_DIM = 128
ATTN_WIDTH = D_MODEL // 4
ATTN_HEADS = ATTN_WIDTH // HEAD_DIM
ATTN_KV_HEADS = ATTN_HEADS // 2
ATTN_GROUP = ATTN_HEADS // ATTN_KV_HEADS
KV_WIDTH = ATTN_KV_HEADS * HEAD_DIM
ATTN_SCALE = HEAD_DIM ** -0.5
LRU_WIDTH = D_MODEL // 4
LRU_BLOCKS = 8
LRU_C = 8.0
DN_DK = 128
DN_DV = 128
DN_WIDTH = D_MODEL // 4
DN_HEADS = DN_WIDTH // DN_DV
DN_CHUNK = 64
S5_WIDTH = D_MODEL // 4
S5_CH = 16
S5_GROUPS = S5_WIDTH // S5_CH
S5_STATE = 64
S5_NSTATE = S5_GROUPS * S5_STATE
DEEPNORM_ALPHA = (2 * DEPTH) ** 0.25
MIXW = D_MODEL // 4

COL_DQ = 0
COL_DZ = 1536
COL_Q = 2048
COL_KV = 2560
COL_LX = 3072
COL_LG = 3584
COL_SU = 4096
N_PROJ = 4608
N_BA = 2 * 2 * DN_HEADS
LANE = 128
SUBLANES = 8
MXU = 256

MOD_ROWS = 8
PITCH_PAD = 4
OUT_SUB = 256
IN_SUB = 256
MLP_SUB = 256
VMEM_LIMIT = 56 * 1024 * 1024


def _cparams(sem):
    return pltpu.CompilerParams(dimension_semantics=sem, vmem_limit_bytes=VMEM_LIMIT)


def _ln(x):
    mu = jnp.mean(x, axis=-1, keepdims=True)
    xc = x - mu
    var = jnp.mean(xc * xc, axis=-1, keepdims=True)
    return xc * lax.rsqrt(var + EPS)


def _softplus(x):
    return jnp.maximum(x, 0.0) + jnp.log1p(jnp.exp(-jnp.abs(x)))


def _gelu(x):
    return 0.5 * x * (1.0 + jnp.tanh(0.7978845608028654 * (x + 0.044715 * (x * x * x))))


def _dot(a, b):
    return jnp.dot(a, b, preferred_element_type=F32)


def _dot_nt(a, b):
    return lax.dot_general(a, b, (((1,), (1,)), ((), ())), preferred_element_type=F32)


def _ada_kernel(c_ref, w_ref, b_ref, o_ref):
    cs = c_ref[...]
    s = cs * jax.nn.sigmoid(cs)
    o_ref[0] = _dot(s.astype(BF16), w_ref[0].astype(BF16)) + b_ref[0]


def ada_mod(cond, w_ada, b_ada, tn=1024):
    depth, d, n = w_ada.shape
    return pl.pallas_call(
        _ada_kernel,
        grid=(depth, n // tn),
        in_specs=[pl.BlockSpec((MOD_ROWS, d), lambda l, j: (0, 0)),
                  pl.BlockSpec((1, d, tn), lambda l, j: (l, 0, j)),
                  pl.BlockSpec((1, 1, tn), lambda l, j: (l, 0, j))],
        out_specs=pl.BlockSpec((1, MOD_ROWS, tn), lambda l, j: (l, 0, j)),
        out_shape=jax.ShapeDtypeStruct((depth, MOD_ROWS, n), F32),
        compiler_params=_cparams(("parallel", "parallel")),
        name="ada_mod",
    )(cond, w_ada, b_ada.reshape(depth, 1, n))


def _in_kernel(x_ref, sc_ref, sh_ref, w_ref, wba_ref, o_ref, ba_ref, u_ref):
    j = pl.program_id(1)

    @pl.when(j == 0)
    def _():
        for r0 in range(0, x_ref.shape[0], IN_SUB):
            rows = slice(r0, r0 + IN_SUB)
            u = (_ln(x_ref[rows, :]) * (1.0 + sc_ref[0]) + sh_ref[0]).astype(BF16)
            u_ref[rows, :] = u
            o_ref[rows, :] = _dot(u, w_ref[...])
            ba_ref[rows, :] = _dot(u, wba_ref[...])

    @pl.when(j > 0)
    def _():
        o_ref[...] = _dot(u_ref[...], w_ref[...])


def in_proj(x, mod, w, w_ba, layer, mod_row, tm=1024, tn=768):
    m, d = x.shape
    n = w.shape[2]

    def mod_spec(kind):
        return pl.BlockSpec((1, 1, d), lambda i, j: (kind * MOD_ROWS + mod_row(i * tm), 0, 0))

    return pl.pallas_call(
        _in_kernel,
        grid=(m // tm, n // tn),
        in_specs=[pl.BlockSpec((tm, d), lambda i, j: (i, 0)),
                  mod_spec(1), mod_spec(0),
                  pl.BlockSpec((None, d, tn), lambda i, j: (layer, 0, j)),
                  pl.BlockSpec((None, d, LANE), lambda i, j: (layer, 0, 0))],
        out_specs=[pl.BlockSpec((tm, tn), lambda i, j: (i, j)),
                   pl.BlockSpec((tm, LANE), lambda i, j: (i, 0))],
        out_shape=[jax.ShapeDtypeStruct((m, n), F32), jax.ShapeDtypeStruct((m, LANE), F32)],
        scratch_shapes=[pltpu.VMEM((tm, d), BF16)],
        compiler_params=_cparams(("parallel", "arbitrary")),
        name="in_proj",
    )(x, mod, mod, w, w_ba)


def _rms_heads(x, g, heads):
    outs = []
    for h in range(heads):
        xh = x[:, h * HEAD_DIM:(h + 1) * HEAD_DIM]
        outs.append(xh * lax.rsqrt(jnp.mean(xh * xh, axis=-1, keepdims=True) + EPS) * g)
    return outs


def _softmax_av(q, k, v, scale):
    s = _dot_nt(q, k)
    if scale is not None:
        s = s * scale
    m = jnp.max(s, axis=-1, keepdims=True)
    p = jnp.exp(s - m)
    l = jnp.sum(p, axis=-1, keepdims=True)
    return _dot(p.astype(BF16), v) / l


def _ctx_attn_kernel(q_ref, kv_ref, qg_ref, kg_ref, o_ref, kn_ref, v_ref):
    qs = _rms_heads(q_ref[...], qg_ref[...], ATTN_HEADS)
    kv = kv_ref[...]
    ks = _rms_heads(kv[:, :KV_WIDTH], kg_ref[...], ATTN_KV_HEADS)
    v = kv[:, KV_WIDTH:]
    v_ref[...] = v
    t = q_ref.shape[0]
    for kh in range(ATTN_KV_HEADS):
        kn_ref[:, kh * HEAD_DIM:(kh + 1) * HEAD_DIM] = ks[kh]
        q2 = jnp.concatenate([qs[kh * ATTN_GROUP + g] for g in range(ATTN_GROUP)], axis=0).astype(BF16)
        o = _softmax_av(q2, ks[kh].astype(BF16), v[:, kh * HEAD_DIM:(kh + 1) * HEAD_DIM].astype(BF16), ATTN_SCALE)
        for g in range(ATTN_GROUP):
            h = kh * ATTN_GROUP + g
            o_ref[:, h * HEAD_DIM:(h + 1) * HEAD_DIM] = o[g * t:(g + 1) * t].astype(BF16)


def ctx_attention(proj, qg, kg, batch, seq):
    return pl.pallas_call(
        _ctx_attn_kernel,
        grid=(batch,),
        in_specs=[pl.BlockSpec((seq, ATTN_WIDTH), lambda b: (b, COL_Q // ATTN_WIDTH)),
                  pl.BlockSpec((seq, 2 * KV_WIDTH), lambda b: (b, COL_KV // (2 * KV_WIDTH))),
                  pl.BlockSpec((1, HEAD_DIM), lambda b: (0, 0)),
                  pl.BlockSpec((1, HEAD_DIM), lambda b: (0, 0))],
        out_specs=[pl.BlockSpec((seq, ATTN_WIDTH), lambda b: (b, 0)),
                   pl.BlockSpec((seq, KV_WIDTH), lambda b: (b, 0)),
                   pl.BlockSpec((seq, KV_WIDTH), lambda b: (b, 0))],
        out_shape=[jax.ShapeDtypeStruct((batch * seq, ATTN_WIDTH), BF16),
                   jax.ShapeDtypeStruct((batch * seq, KV_WIDTH), F32),
                   jax.ShapeDtypeStruct((batch * seq, KV_WIDTH), F32)],
        compiler_params=_cparams(("parallel",)),
        name="ctx_attention",
    )(proj, proj, qg, kg)


def _rope(x, cos, sin, heads):
    w = x.shape[-1]
    lane = lax.broadcasted_iota(jnp.int32, x.shape, 1)
    quarter = HEAD_DIM // 4
    partner = jnp.where((lane % (2 * quarter)) < quarter,
                        pltpu.roll(x, w - quarter, 1), pltpu.roll(x, quarter, 1))
    cos_t = jnp.concatenate([cos] * heads, axis=1)
    sin_t = jnp.concatenate([sin] * heads, axis=1)
    return x * cos_t + partner * sin_t


def _lat_prep_kernel(q_ref, kv_ref, qg_ref, kg_ref, cos_ref, sin_ref, qo_ref, ko_ref, vo_ref):
    qn = jnp.concatenate(_rms_heads(q_ref[...], qg_ref[...], ATTN_HEADS), axis=1)
    kv = kv_ref[...]
    kn = jnp.concatenate(_rms_heads(kv[:, :KV_WIDTH], kg_ref[...], ATTN_KV_HEADS), axis=1)
    cos = cos_ref[...]
    sin = sin_ref[...]
    qo_ref[...] = (_rope(qn, cos, sin, ATTN_HEADS) * ATTN_SCALE).astype(BF16)
    ko_ref[...] = _rope(kn, cos, sin, ATTN_KV_HEADS).astype(BF16)
    vo_ref[...] = kv[:, KV_WIDTH:].astype(BF16)


def lat_prep(proj, qg, kg, cos, sin, seq, tm=1024):
    m = proj.shape[0]
    per = seq // tm
    return pl.pallas_call(
        _lat_prep_kernel,
        grid=(m // tm,),
        in_specs=[pl.BlockSpec((tm, ATTN_WIDTH), lambda i: (i, COL_Q // ATTN_WIDTH)),
                  pl.BlockSpec((tm, 2 * KV_WIDTH), lambda i: (i, COL_KV // (2 * KV_WIDTH))),
                  pl.BlockSpec((1, HEAD_DIM), lambda i: (0, 0)),
                  pl.BlockSpec((1, HEAD_DIM), lambda i: (0, 0)),
                  pl.BlockSpec((tm, HEAD_DIM), lambda i: (i % per, 0)),
                  pl.BlockSpec((tm, HEAD_DIM), lambda i: (i % per, 0))],
        out_specs=[pl.BlockSpec((tm, ATTN_WIDTH), lambda i: (i, 0)),
                   pl.BlockSpec((tm, KV_WIDTH), lambda i: (i, 0)),
                   pl.BlockSpec((tm, KV_WIDTH), lambda i: (i, 0))],
        out_shape=[jax.ShapeDtypeStruct((m, ATTN_WIDTH), BF16),
                   jax.ShapeDtypeStruct((m, KV_WIDTH), BF16),
                   jax.ShapeDtypeStruct((m, KV_WIDTH), BF16)],
        compiler_params=_cparams(("parallel",)),
        name="lat_prep",
    )(proj, proj, qg, kg, cos, sin)


def _lat_attn_kernel(q_ref, k_ref, v_ref, o_ref):
    k = k_ref[0]
    v = v_ref[0]
    half = q_ref.shape[0] // 2
    sls = [(slice(r0, r0 + half), slice(g * HEAD_DIM, (g + 1) * HEAD_DIM))
           for r0 in (0, half) for g in range(ATTN_GROUP)]
    ss = [_dot_nt(q_ref[sl], k) for sl in sls]
    ps = [jnp.exp(s - jnp.max(s, axis=-1, keepdims=True)) for s in ss]
    ls = [jnp.sum(p, axis=-1, keepdims=True) for p in ps]
    os = [_dot(p.astype(BF16), v) for p in ps]
    for sl, o, l in zip(sls, os, ls):
        o_ref[sl] = (o / l).astype(BF16)


def lat_attention(q, k_all, v_all, seq, tq=256):
    b, s, _ = k_all.shape
    nq = seq // tq
    gw = ATTN_GROUP * HEAD_DIM
    return pl.pallas_call(
        _lat_attn_kernel,
        grid=(b, ATTN_KV_HEADS, nq),
        in_specs=[pl.BlockSpec((tq, gw), lambda bi, kh, qi: (bi * nq + qi, kh)),
                  pl.BlockSpec((1, s, HEAD_DIM), lambda bi, kh, qi: (bi, 0, kh)),
                  pl.BlockSpec((1, s, HEAD_DIM), lambda bi, kh, qi: (bi, 0, kh))],
        out_specs=pl.BlockSpec((tq, gw), lambda bi, kh, qi: (bi * nq + qi, kh)),
        out_shape=jax.ShapeDtypeStruct((b * seq, ATTN_WIDTH), BF16),
        compiler_params=_cparams(("parallel", "parallel", "arbitrary")),
        name="lat_attention",
    )(q, k_all, v_all)


def rope_tables(seq):
    t = jnp.arange(seq)
    row = (t // GRID_W).astype(F32)
    col = (t % GRID_W).astype(F32)
    quarter = HEAD_DIM // 4
    inv_freq = jnp.power(ROPE_THETA, -jnp.arange(quarter, dtype=F32) / quarter)
    ar = row[:, None] * inv_freq[None, :]
    ac = col[:, None] * inv_freq[None, :]
    cos = jnp.concatenate([jnp.cos(ar), jnp.cos(ar), jnp.cos(ac), jnp.cos(ac)], axis=1)
    sin = jnp.concatenate([-jnp.sin(ar), jnp.sin(ar), -jnp.sin(ac), jnp.sin(ac)], axis=1)
    return cos, sin


def _chunk_index(d, i, n):
    return i + d * (n - 1 - 2 * i)


def _lru_kernel(x_ref, xp_ref, xn_ref, cw_ref, cb_ref, wg_ref, bg_ref, sp_ref, h0_ref,
                h_ref, fin_ref, xpad_ref, a_ref, b_ref, hc_ref, *, nb, tt):
    d = pl.program_id(0)
    i = pl.program_id(2)
    n = pl.num_programs(2)
    ci = _chunk_index(d, i, n)
    pitch = tt + PITCH_PAD
    rows = nb * pitch
    nsl = LRU_WIDTH // LANE

    @pl.when(i == 0)
    def _():
        xpad_ref[...] = jnp.zeros_like(xpad_ref)
        hc_ref[...] = h0_ref[0]

    for s in range(nb):
        base = 8 + s * pitch
        xpad_ref[base:base + tt, :] = x_ref[s]
        xpad_ref[base - 1:base, :] = jnp.where(ci == 0, 0.0, xp_ref[s, 7:8, :])
        xpad_ref[base + tt:base + tt + 2, :] = jnp.where(ci == n - 1, 0.0, xn_ref[s, 0:2, :])
    w = cw_ref[...]
    xc = cb_ref[...] + sum(xpad_ref[7 + j:7 + j + rows, :] * w[j:j + 1, :] for j in range(CONV_W))
    pre = _dot(xc.astype(BF16), wg_ref[0]) + bg_ref[0]
    r = jax.nn.sigmoid(pre[:, :LRU_WIDTH])
    ig = jax.nn.sigmoid(pre[:, LRU_WIDTH:])
    a = jnp.exp((-LRU_C) * r * sp_ref[0])
    inp = jnp.sqrt(1.0 - a * a) * (ig * xc)
    for c in range(nsl):
        a_ref[c] = a[:, c * LANE:(c + 1) * LANE]
        b_ref[c] = inp[:, c * LANE:(c + 1) * LANE]

    def body(t, carry):
        row = t + d * (tt - 1 - 2 * t)
        out = []
        for c in range(nsl):
            idx = (c, pl.ds(row, nb, stride=pitch), slice(None))
            h = a_ref[idx] * carry[c] + b_ref[idx]
            b_ref[idx] = h
            out.append(h)
        return tuple(out)

    carry = tuple(hc_ref[:, c * LANE:(c + 1) * LANE] for c in range(nsl))
    carry = lax.fori_loop(0, tt, body, carry, unroll=8)
    for c in range(nsl):
        hc_ref[:, c * LANE:(c + 1) * LANE] = carry[c]
        for s in range(nb):
            h_ref[0, s, :, c * LANE:(c + 1) * LANE] = b_ref[c, s * pitch:s * pitch + tt, :]

    @pl.when(i == n - 1)
    def _():
        fin_ref[0] = hc_ref[...]


def lru_scan(proj3, h0, cw, cb, wg, bg, sp, nb, tt=128):
    b, t, _ = proj3.shape
    w = LRU_WIDTH
    n = t // tt
    t8 = tt // 8
    rows = nb * (tt + PITCH_PAD)
    col = COL_LX // w

    def cidx(d, i):
        return _chunk_index(d, i, n)

    return pl.pallas_call(
        functools.partial(_lru_kernel, nb=nb, tt=tt),
        grid=(2, b // nb, n),
        in_specs=[pl.BlockSpec((nb, tt, w), lambda d, g, i: (g, cidx(d, i), col)),
                  pl.BlockSpec((nb, 8, w), lambda d, g, i: (g, jnp.maximum(cidx(d, i) * t8 - 1, 0), col)),
                  pl.BlockSpec((nb, 8, w), lambda d, g, i: (g, jnp.minimum((cidx(d, i) + 1) * t8, t // 8 - 1), col)),
                  pl.BlockSpec((CONV_W, w), lambda d, g, i: (0, 0)),
                  pl.BlockSpec((1, w), lambda d, g, i: (0, 0)),
                  pl.BlockSpec((1, w, 2 * w), lambda d, g, i: (d, 0, 0)),
                  pl.BlockSpec((1, 1, 2 * w), lambda d, g, i: (d, 0, 0)),
                  pl.BlockSpec((1, 1, w), lambda d, g, i: (d, 0, 0)),
                  pl.BlockSpec((1, nb, w), lambda d, g, i: (d, g, 0))],
        out_specs=[pl.BlockSpec((1, nb, tt, w), lambda d, g, i: (d, g, cidx(d, i), 0)),
                   pl.BlockSpec((1, nb, w), lambda d, g, i: (d, g, 0))],
        out_shape=[jax.ShapeDtypeStruct((2, b, t, w), F32), jax.ShapeDtypeStruct((2, b, w), F32)],
        scratch_shapes=[pltpu.VMEM((rows + 16, w), F32),
                        pltpu.VMEM((w // LANE, rows, LANE), F32),
                        pltpu.VMEM((w // LANE, rows, LANE), F32),
                        pltpu.VMEM((nb, w), F32)],
        compiler_params=_cparams(("parallel", "parallel", "arbitrary")),
        name="lru_scan",
    )(proj3, proj3, proj3, cw, cb, wg, bg, sp, h0)


def _s5_kernel(u_ref, bb_ref, cc_ref, a_ref, h0_ref, y_ref, fin_ref, up_ref, s_ref, hc_ref, *, nb, tt):
    d = pl.program_id(0)
    i = pl.program_id(2)
    n = pl.num_programs(2)
    pitch = tt + PITCH_PAD
    nsl = S5_NSTATE // LANE

    @pl.when(i == 0)
    def _():
        up_ref[...] = jnp.zeros_like(up_ref)
        hc_ref[...] = h0_ref[0]

    for s in range(nb):
        up_ref[s * pitch:s * pitch + tt, :] = u_ref[s]
    u2 = up_ref[...].astype(BF16)
    per_k = MXU // S5_CH * S5_STATE // MXU
    vs = SUBLANES // nb
    rows_seq = nb * pitch
    npk = nsl // vs

    def slab(c):
        return (c // vs, slice((c % vs) * rows_seq, (c % vs + 1) * rows_seq), slice(None))

    for nt in range(2 * S5_NSTATE // MXU):
        kt = (nt % (S5_NSTATE // MXU)) // per_k
        tile = _dot(u2[:, kt * MXU:(kt + 1) * MXU], bb_ref[0, kt * MXU:(kt + 1) * MXU, nt * MXU:(nt + 1) * MXU])
        s_ref[slab(2 * nt)] = tile[:, :LANE]
        s_ref[slab(2 * nt + 1)] = tile[:, LANE:]

    a_all = a_ref[0]

    def packed_rows(src, off, j):
        parts = [jnp.broadcast_to(src[:, off + (j * vs + v) * LANE:off + (j * vs + v + 1) * LANE], (nb, LANE))
                 for v in range(vs)]
        return parts[0] if vs == 1 else jnp.concatenate(parts, axis=0)

    group = 8
    for j0 in range(0, npk, group):
        js = list(range(j0, min(j0 + group, npk)))
        ar = [packed_rows(a_all, 0, j) for j in js]
        ai = [packed_rows(a_all, S5_NSTATE, j) for j in js]

        def body(t, carry, js=js, ar=ar, ai=ai):
            row = t + d * (tt - 1 - 2 * t)
            out = []
            for k, j in enumerate(js):
                hr, hi = carry[2 * k], carry[2 * k + 1]
                ire = (j, pl.ds(row, SUBLANES, stride=pitch), slice(None))
                iim = (npk + j, pl.ds(row, SUBLANES, stride=pitch), slice(None))
                nr = ar[k] * hr - ai[k] * hi + s_ref[ire]
                ni = ar[k] * hi + ai[k] * hr + s_ref[iim]
                s_ref[ire] = nr
                s_ref[iim] = ni
                out += [nr, ni]
            return tuple(out)

        carry = []
        for j in js:
            carry += [packed_rows(hc_ref, 0, j), packed_rows(hc_ref, S5_NSTATE, j)]
        carry = lax.fori_loop(0, tt, body, tuple(carry), unroll=4)
        for k, j in enumerate(js):
            for v in range(vs):
                c = j * vs + v
                hc_ref[:, c * LANE:(c + 1) * LANE] = carry[2 * k][v * nb:(v + 1) * nb]
                hc_ref[:, S5_NSTATE + c * LANE:S5_NSTATE + (c + 1) * LANE] = carry[2 * k + 1][v * nb:(v + 1) * nb]

    n_out = S5_WIDTH // MXU
    per_n = nsl // n_out
    for nt in range(n_out):
        slabs = [per_n * nt + k for k in range(per_n)] + [nsl + per_n * nt + k for k in range(per_n)]
        lhs = jnp.concatenate([s_ref[slab(c)] for c in slabs], axis=1).astype(BF16)
        y = _dot(lhs, cc_ref[0, nt])
        for s in range(nb):
            y_ref[0, s, :, nt * MXU:(nt + 1) * MXU] = y[s * pitch:s * pitch + tt]

    @pl.when(i == n - 1)
    def _():
        fin_ref[0] = hc_ref[...]


def s5_scan(proj3, h0, bb, cc, a, nb, tt=128):
    b, t, _ = proj3.shape
    w = S5_WIDTH
    n = t // tt
    rows = nb * (tt + PITCH_PAD)
    ns2 = 2 * S5_NSTATE

    def cidx(d, i):
        return _chunk_index(d, i, n)

    return pl.pallas_call(
        functools.partial(_s5_kernel, nb=nb, tt=tt),
        grid=(2, b // nb, n),
        in_specs=[pl.BlockSpec((nb, tt, w), lambda d, g, i: (g, cidx(d, i), COL_SU // w)),
                  pl.BlockSpec((1, w, ns2), lambda d, g, i: (d, 0, 0)),
                  pl.BlockSpec((1,) + cc.shape[1:], lambda d, g, i: (d, 0, 0, 0)),
                  pl.BlockSpec((1, 1, ns2), lambda d, g, i: (d, 0, 0)),
                  pl.BlockSpec((1, nb, ns2), lambda d, g, i: (d, g, 0))],
        out_specs=[pl.BlockSpec((1, nb, tt, w), lambda d, g, i: (d, g, cidx(d, i), 0)),
                   pl.BlockSpec((1, nb, ns2), lambda d, g, i: (d, g, 0))],
        out_shape=[jax.ShapeDtypeStruct((2, b, t, w), F32), jax.ShapeDtypeStruct((2, b, ns2), F32)],
        scratch_shapes=[pltpu.VMEM((rows, w), F32),
                        pltpu.VMEM((ns2 // LANE // (SUBLANES // nb), SUBLANES // nb * rows, LANE), F32),
                        pltpu.VMEM((nb, ns2), F32)],
        compiler_params=_cparams(("parallel", "parallel", "arbitrary")),
        name="s5_scan",
    )(proj3, bb, cc, a, h0)


def _split3(x):
    x1 = x.astype(BF16)
    r1 = x - x1.astype(F32)
    x2 = r1.astype(BF16)
    x3 = (r1 - x2.astype(F32)).astype(BF16)
    return x1, x2, x3


def _mm(x, y):
    return _dot(x.astype(BF16), y.astype(BF16))


def _unit_tri_inverses(mats, rr, cc):
    same16 = (rr // 16) == (cc // 16)
    same32 = (rr // 32) == (cc // 32)
    p = [jnp.where(same16, -a, 0.0) for a in mats]
    t = [jnp.where(rr == cc, 1.0, x) for x in p]
    for _ in range(3):
        p = [_mm(x, x) for x in p]
        t = [x + _mm(x, y) for x, y in zip(t, p)]
    for mask in (same32 & jnp.logical_not(same16), jnp.logical_not(same32)):
        te = [_mm(x, jnp.where(mask, a, 0.0)) for x, a in zip(t, mats)]
        t = [x - _mm(y, x) for x, y in zip(t, te)]
    return t


def _dn_prep_kernel(x_ref, xp_ref, xn_ref, cw_ref, o_ref, xpad_ref, *, tb):
    i = pl.program_id(1)
    n = pl.num_programs(1)
    hw = DN_HEADS * DN_DK
    xpad_ref[8:8 + tb, :] = x_ref[0]
    xpad_ref[0:8, :] = jnp.where(i == 0, 0.0, xp_ref[0])
    xpad_ref[8 + tb:16 + tb, :] = jnp.where(i == n - 1, 0.0, xn_ref[0])
    w = cw_ref[...]
    xc = sum(xpad_ref[7 + j:7 + j + tb, :] * w[j:j + 1, :] for j in range(CONV_W))
    qkv = xc * jax.nn.sigmoid(xc)
    for h in range(DN_HEADS):
        q = qkv[:, h * DN_DK:(h + 1) * DN_DK]
        k = qkv[:, hw + h * DN_DK:hw + (h + 1) * DN_DK]
        o_ref[0, :, h * DN_DK:(h + 1) * DN_DK] = (
            q * lax.rsqrt(jnp.sum(q * q, axis=-1, keepdims=True) + EPS) * (DN_DK ** -0.5))
        o_ref[0, :, hw + h * DN_DK:hw + (h + 1) * DN_DK] = (
            k * lax.rsqrt(jnp.sum(k * k, axis=-1, keepdims=True) + EPS))
    o_ref[0, :, 2 * hw:] = qkv[:, 2 * hw:]


def dn_prep(proj3, cw, tb=256):
    b, t, _ = proj3.shape
    n = t // tb
    t8 = tb // 8
    w3 = 3 * DN_WIDTH
    cq = COL_DQ // w3
    return pl.pallas_call(
        functools.partial(_dn_prep_kernel, tb=tb),
        grid=(b, n),
        in_specs=[pl.BlockSpec((1, tb, w3), lambda bi, i: (bi, i, cq)),
                  pl.BlockSpec((1, 8, w3), lambda bi, i: (bi, jnp.maximum(i * t8 - 1, 0), cq)),
                  pl.BlockSpec((1, 8, w3), lambda bi, i: (bi, jnp.minimum((i + 1) * t8, t // 8 - 1), cq)),
                  pl.BlockSpec((CONV_W, w3), lambda bi, i: (0, 0))],
        out_specs=pl.BlockSpec((1, tb, w3), lambda bi, i: (bi, i, 0)),
        out_shape=jax.ShapeDtypeStruct((b, t, w3), F32),
        scratch_shapes=[pltpu.VMEM((tb + 16, w3), F32)],
        compiler_params=_cparams(("parallel", "parallel")),
        name="dn_prep",
    )(proj3, proj3, proj3, cw)


def _dn_kernel(xf_ref, baf_ref, xb_ref, bab_ref, al_ref, dtb_ref, s0_ref, of_ref, ob_ref, fin_ref, s_ref,
               *, nbat, tb):
    i = pl.program_id(1)
    n = pl.num_programs(1)
    ch = DN_CHUNK
    nc = tb // ch
    hw = DN_HEADS * DN_DK

    @pl.when(i == 0)
    def _():
        s_ref[...] = s0_ref[...]

    rb = lax.broadcasted_iota(jnp.int32, (tb, tb), 0)
    cb = lax.broadcasted_iota(jnp.int32, (tb, tb), 1)
    rr = lax.broadcasted_iota(jnp.int32, (ch, ch), 0)
    cc = lax.broadcasted_iota(jnp.int32, (ch, ch), 1)
    al = al_ref[...]
    dtb = dtb_ref[...]

    x_refs = (xf_ref, xb_ref)
    beta_tab, gc_tab, masks = {}, {}, {}
    for d in range(2):
        reverse = d == 1
        ba_ref = bab_ref if reverse else baf_ref
        tri = ((rb // ch) == (cb // ch)) & ((cb >= rb) if reverse else (cb <= rb))
        tri = jnp.where(tri, 1.0, 0.0).astype(BF16)
        masks[d] = ((cc >= rr) if reverse else (cc <= rr), (cc > rr) if reverse else (cc < rr))
        for bat in range(nbat):
            ba = ba_ref[bat]
            beta_tab[(d, bat)] = jax.nn.sigmoid(ba)
            g_all = -jnp.exp(al) * _softplus(ba + dtb)
            gc_tab[(d, bat)] = sum(_dot(tri, piece) for piece in _split3(g_all))
    units = [dict(d=d, bat=bat, c=c, h=h) for d in range(2) for bat in range(nbat) for c in range(nc)
             for h in range(DN_HEADS)]

    def rows_of(u):
        return slice(u['c'] * ch, (u['c'] + 1) * ch)

    def q_of(u):
        return x_refs[u['d']][u['bat'], rows_of(u), u['h'] * DN_DK:(u['h'] + 1) * DN_DK]

    def k_of(u):
        return x_refs[u['d']][u['bat'], rows_of(u), hw + u['h'] * DN_DK:hw + (u['h'] + 1) * DN_DK]

    def v_of(u):
        return x_refs[u['d']][u['bat'], rows_of(u), 2 * hw + u['h'] * DN_DV:2 * hw + (u['h'] + 1) * DN_DV]

    def beta_of(u):
        lane = u['d'] * DN_HEADS + u['h']
        return beta_tab[(u['d'], u['bat'])][rows_of(u), lane:lane + 1]

    def gcol_of(u):
        lane = 2 * DN_HEADS + u['d'] * DN_HEADS + u['h']
        return gc_tab[(u['d'], u['bat'])][rows_of(u), lane:lane + 1]

    def glast_of(u):
        gcol = gcol_of(u)
        return gcol[0:1, :] if u['d'] == 1 else gcol[ch - 1:ch, :]

    gct = {}
    for u in units:
        key = (u['d'], u['bat'], u['c'])
        if key not in gct:
            gct[key] = gc_tab[(u['d'], u['bat'])][rows_of(u), :].T
        lane = 2 * DN_HEADS + u['d'] * DN_HEADS + u['h']
        incl, strict = masks[u['d']]
        decay = jnp.where(incl, jnp.exp(gcol_of(u) - gct[key][lane:lane + 1, :]), 0.0)
        k = k_of(u)
        kbf = k.astype(BF16)
        u['a'] = jnp.where(strict, _dot_nt((k * beta_of(u)).astype(BF16), kbf) * decay, 0.0)
        u['qk'] = (_dot_nt(q_of(u).astype(BF16), kbf) * decay).astype(BF16)
    t_inv = _unit_tri_inverses([u['a'] for u in units], rr, cc)
    for u, t in zip(units, t_inv):
        beta = beta_of(u)
        rhs = jnp.concatenate([v_of(u) * beta, k_of(u) * beta * jnp.exp(gcol_of(u))], axis=1)
        sol = rhs + _mm(jnp.where(rr == cc, 0.0, t), rhs)
        u['u_val'] = sol[:, :DN_DV]
        u['w'] = sol[:, DN_DV:].astype(BF16)

    state = {(d, bat, h): s_ref[d, bat, h] for d in range(2) for bat in range(nbat) for h in range(DN_HEADS)}
    for j in range(nc):
        cur = [u for u in units if u['c'] == (nc - 1 - j if u['d'] == 1 else j)]
        keys = [(u['d'], u['bat'], u['h']) for u in cur]
        sb = [state[key].astype(BF16) for key in keys]
        ws = [_dot(u['w'], s) for u, s in zip(cur, sb)]
        qs = [_dot((q_of(u) * jnp.exp(gcol_of(u))).astype(BF16), s) for u, s in zip(cur, sb)]
        vb = [(u['u_val'] - x).astype(BF16) for u, x in zip(cur, ws)]
        os = [x + _dot(u['qk'], y) for u, x, y in zip(cur, qs, vb)]
        kdec_t = [(k_of(u) * jnp.exp(glast_of(u) - gcol_of(u))).T.astype(BF16) for u in cur]
        sn = [state[key] * jnp.exp(glast_of(u)) + _dot(kt, y) for u, key, kt, y in zip(cur, keys, kdec_t, vb)]
        for u, key, o, s in zip(cur, keys, os, sn):
            state[key] = s
            o_ref = ob_ref if u['d'] == 1 else of_ref
            o_ref[u['bat'], u['c'] * ch:(u['c'] + 1) * ch, u['h'] * DN_DV:(u['h'] + 1) * DN_DV] = o
    for (d, bat, h), s in state.items():
        s_ref[d, bat, h] = s

    @pl.when(i == n - 1)
    def _():
        fin_ref[...] = s_ref[...]


def dn_scan(qkv3, ba3, s0, al, dtb, nbat, tb):
    b, t, w3 = qkv3.shape
    n = t // tb

    def data_specs(cidx):
        return [pl.BlockSpec((nbat, tb, w3), lambda g, i: (g, cidx(i), 0)),
                pl.BlockSpec((nbat, tb, LANE), lambda g, i: (g, cidx(i), 0))]

    fwd = lambda i: i
    bwd = lambda i: n - 1 - i
    st_spec = pl.BlockSpec((2, nbat, DN_HEADS, DN_DK, DN_DV), lambda g, i: (0, g, 0, 0, 0))
    return pl.pallas_call(
        functools.partial(_dn_kernel, nbat=nbat, tb=tb),
        grid=(b // nbat, n),
        in_specs=data_specs(fwd) + data_specs(bwd) + [
            pl.BlockSpec((1, LANE), lambda g, i: (0, 0)),
            pl.BlockSpec((1, LANE), lambda g, i: (0, 0)),
            st_spec],
        out_specs=[pl.BlockSpec((nbat, tb, DN_WIDTH), lambda g, i: (g, i, 0)),
                   pl.BlockSpec((nbat, tb, DN_WIDTH), lambda g, i: (g, n - 1 - i, 0)),
                   st_spec],
        out_shape=[jax.ShapeDtypeStruct((b, t, DN_WIDTH), F32),
                   jax.ShapeDtypeStruct((b, t, DN_WIDTH), F32),
                   jax.ShapeDtypeStruct((2, b, DN_HEADS, DN_DK, DN_DV), F32)],
        scratch_shapes=[pltpu.VMEM((2, nbat, DN_HEADS, DN_DK, DN_DV), F32)],
        compiler_params=_cparams(("parallel", "arbitrary")),
        name="dn_scan",
    )(qkv3, ba3, qkv3, ba3, al, dtb, s0)


def _post_kernel(hf_ref, hb_ref, lg_ref, of_ref, ob_ref, dz_ref, ng_ref, yf_ref, yb_ref, su_ref, sd_ref,
                 gw_ref, gb_ref, lru_ref, dn_ref, s5_ref):
    lru_ref[...] = ((hf_ref[0, 0] + hb_ref[0, 0]) * _gelu(lg_ref[...])).astype(BF16)
    o = of_ref[0] + ob_ref[0]
    dz = dz_ref[...]
    for h in range(DN_HEADS):
        sl = slice(h * DN_DV, (h + 1) * DN_DV)
        oh = o[:, sl]
        oh = oh * lax.rsqrt(jnp.mean(oh * oh, axis=-1, keepdims=True) + EPS) * ng_ref[...]
        zh = dz[:, sl]
        dn_ref[:, sl] = (oh * (zh * jax.nn.sigmoid(zh))).astype(BF16)
    y = yf_ref[0, 0] + yb_ref[0, 0] + sd_ref[...] * su_ref[...]
    gy = _gelu(y)
    s5_ref[...] = (gy * jax.nn.sigmoid(_dot(gy.astype(BF16), gw_ref[...]) + gb_ref[...])).astype(BF16)


def mix_post(proj, h_lru, o_f, o_b, y_s5, ng, sd, gw, gb):
    m = proj.shape[0]
    _, b, t, w = h_lru.shape
    tm = min(512, t)
    per = t // tm
    tok = pl.BlockSpec((tm, w), lambda i: (i, 0))

    def pcol(c0):
        return pl.BlockSpec((tm, w), lambda i: (i, c0 // w))

    def dirspec(d):
        return pl.BlockSpec((1, 1, tm, w), lambda i: (d, i // per, i % per, 0))

    seq = pl.BlockSpec((1, tm, w), lambda i: (i // per, i % per, 0))

    def vec(n):
        return pl.BlockSpec((1, n), lambda i: (0, 0))

    out = jax.ShapeDtypeStruct((m, w), BF16)
    return pl.pallas_call(
        _post_kernel,
        grid=(m // tm,),
        in_specs=[dirspec(0), dirspec(1), pcol(COL_LG), seq, seq, pcol(COL_DZ), vec(DN_DV),
                  dirspec(0), dirspec(1), pcol(COL_SU), vec(w), pl.BlockSpec((w, w), lambda i: (0, 0)), vec(w)],
        out_specs=[tok, tok, tok],
        out_shape=[out, out, out],
        compiler_params=_cparams(("parallel",)),
        name="mix_post",
    )(h_lru, h_lru, proj, o_f, o_b, proj, ng, y_s5, y_s5, proj, sd, gw, gb)


def _out_kernel(x_ref, a_ref, b_ref, c_ref, d_ref, w_ref, gate_ref, g_ref, bb_ref, x1_ref):
    q = MIXW
    for r0 in range(0, x_ref.shape[0], OUT_SUB):
        rows = slice(r0, r0 + OUT_SUB)
        y = _dot(a_ref[rows, :], w_ref[0:q, :])
        y += _dot(b_ref[rows, :], w_ref[q:2 * q, :])
        y += _dot(c_ref[rows, :], w_ref[2 * q:3 * q, :])
        y += _dot(d_ref[rows, :], w_ref[3 * q:, :])
        z = DEEPNORM_ALPHA * x_ref[rows, :] + gate_ref[0] * y
        x1_ref[rows, :] = _ln(z) * g_ref[...] + bb_ref[...]


def out_proj(x, parts, w, layer, mod, ln_g, ln_b, mod_row, tm=512):
    m, d = x.shape
    vec = pl.BlockSpec((1, d), lambda i: (0, 0))
    part = pl.BlockSpec((tm, MIXW), lambda i: (i, 0))
    return pl.pallas_call(
        _out_kernel,
        grid=(m // tm,),
        in_specs=[pl.BlockSpec((tm, d), lambda i: (i, 0)), part, part, part, part,
                  pl.BlockSpec((None, d, d), lambda i: (layer, 0, 0)),
                  pl.BlockSpec((1, 1, d), lambda i: (2 * MOD_ROWS + mod_row(i * tm), 0, 0)), vec, vec],
        out_specs=pl.BlockSpec((tm, d), lambda i: (i, 0)),
        out_shape=jax.ShapeDtypeStruct((m, d), F32),
        compiler_params=_cparams(("parallel",)),
        name="out_proj",
    )(x, *parts, w, mod, ln_g, ln_b)


def _mlp_kernel(x_ref, w1_ref, w2_ref, sc_ref, sh_ref, gate_ref, g_ref, b_ref, o_ref, u_ref, acc_ref):
    f = pl.program_id(1)
    last = pl.num_programs(1) - 1
    tm = x_ref.shape[0]

    def hidden(u):
        return jnp.square(jnp.maximum(_dot(u, w1_ref[...]), 0.0)).astype(BF16)

    @pl.when(f == 0)
    def _():
        for r0 in range(0, tm, MLP_SUB):
            rows = slice(r0, r0 + MLP_SUB)
            u = (_ln(x_ref[rows, :]) * (1.0 + sc_ref[0]) + sh_ref[0]).astype(BF16)
            u_ref[rows, :] = u
            acc_ref[rows, :] = _dot(hidden(u), w2_ref[...])

    @pl.when((f > 0) & (f < last))
    def _():
        acc_ref[...] += _dot(hidden(u_ref[...]), w2_ref[...])

    @pl.when(f == last)
    def _():
        for r0 in range(0, tm, MLP_SUB):
            rows = slice(r0, r0 + MLP_SUB)
            a = acc_ref[rows, :] + _dot(hidden(u_ref[rows, :]), w2_ref[...])
            z = DEEPNORM_ALPHA * x_ref[rows, :] + gate_ref[0] * a
            o_ref[rows, :] = _ln(z) * g_ref[...] + b_ref[...]


def mlp(x, w1, w2, layer, mod, ln_g, ln_b, mod_row, tm=512, tf=1024):
    m, d = x.shape
    ff = w1.shape[2]
    assert ff // tf >= 2
    vec = pl.BlockSpec((1, d), lambda i, f: (0, 0))

    def mod_spec(kind):
        return pl.BlockSpec((1, 1, d), lambda i, f: (kind * MOD_ROWS + mod_row(i * tm), 0, 0))

    return pl.pallas_call(
        _mlp_kernel,
        grid=(m // tm, ff // tf),
        in_specs=[pl.BlockSpec((tm, d), lambda i, f: (i, 0)),
                  pl.BlockSpec((None, d, tf), lambda i, f: (layer, 0, f)),
                  pl.BlockSpec((None, tf, d), lambda i, f: (layer, f, 0)),
                  mod_spec(4), mod_spec(3), mod_spec(5), vec, vec],
        out_specs=pl.BlockSpec((tm, d), lambda i, f: (i, 0)),
        out_shape=jax.ShapeDtypeStruct((m, d), F32),
        scratch_shapes=[pltpu.VMEM((tm, d), BF16), pltpu.VMEM((tm, d), F32)],
        compiler_params=_cparams(("parallel", "arbitrary")),
        name="mlp",
    )(x, w1, w2, mod, mod, mod, ln_g, ln_b)


def _reorder_w_in(w):
    def cols(c0, n):
        return w[..., c0:c0 + n]

    aq, akv, lx, lg = cols(0, 512), cols(512, 512), cols(1024, 512), cols(1536, 512)
    dqkv, dz, ba, su = cols(2048, 1536), cols(3584, 512), cols(4096, N_BA), cols(4096 + N_BA, 512)
    pad = jnp.zeros(w.shape[:2] + (LANE - N_BA,), w.dtype)
    main = jnp.concatenate([dqkv, dz, aq, akv, lx, lg, su], axis=-1).astype(BF16)
    return main, jnp.concatenate([ba, pad], axis=-1).astype(BF16)


def _block_diag(blocks):
    n, r, c = blocks.shape[-3:]
    eye = jnp.eye(n, dtype=blocks.dtype)
    out = blocks[..., :, :, None, :] * eye[:, None, :, None]
    return out.reshape(blocks.shape[:-3] + (n * r, n * c))


def _lru_params(wa, ba, wx, bx, lam):
    wg = jnp.concatenate([_block_diag(wa.astype(BF16)), _block_diag(wx.astype(BF16))], axis=-1)
    bg = jnp.concatenate([ba, bx], axis=-1)[..., None, :]
    sp = jax.nn.softplus(-lam)[..., None, :]
    return wg, bg, sp


def _s5_params(lam_re, lam_im, log_dt, b_re, b_im, c_re, c_im):
    lead = lam_re.shape[:-2]
    dt = jnp.exp(log_dt)[..., None]
    mag = jnp.exp(lam_re * dt)
    abar_re = mag * jnp.cos(lam_im * dt)
    abar_im = mag * jnp.sin(lam_im * dt)
    den = lam_re * lam_re + lam_im * lam_im
    nr = abar_re - 1.0
    ni = abar_im
    f_re = (nr * lam_re + ni * lam_im) / den
    f_im = (ni * lam_re - nr * lam_im) / den
    bb_re = f_re[..., None] * b_re - f_im[..., None] * b_im
    bb_im = f_re[..., None] * b_im + f_im[..., None] * b_re
    to_in = lambda m: _block_diag(jnp.swapaxes(m, -1, -2).astype(BF16))
    bb = jnp.concatenate([to_in(bb_re), to_in(bb_im)], axis=-1)
    n_out = S5_WIDTH // MXU
    gpt = S5_GROUPS // n_out

    def to_out(m, nt):
        return _block_diag(jnp.swapaxes(m[..., nt * gpt:(nt + 1) * gpt, :, :], -1, -2).astype(BF16))

    cc = jnp.stack([jnp.concatenate([to_out(c_re, nt), to_out(-c_im, nt)], axis=-2)
                    for nt in range(n_out)], axis=-3)
    a = jnp.concatenate([abar_re.reshape(lead + (1, -1)), abar_im.reshape(lead + (1, -1))], axis=-1)
    return bb, cc, a


def _lane_row(vals, offset):
    return jnp.zeros((1, LANE), F32).at[0, offset:offset + vals.size].set(vals.reshape(-1))


def kernel(x_prompt, x_sample, cache_attn_k, cache_attn_v, state_rglru, state_delta, state_s5_re, state_s5_im, c, c_ctx, w_ada, b_ada, w_in, w_out, ln1_g, ln1_b, ln2_g, ln2_b, w_mlp1, w_mlp2, q_norm_g, k_norm_g, lru_conv_w, lru_conv_b, lru_wa, lru_ba, lru_wx, lru_bx, lru_lambda, dn_conv_w, dn_a_log, dn_dt_bias, dn_norm_g, s5_lambda_re, s5_lambda_im, s5_log_dt, s5_b_re, s5_b_im, s5_c_re, s5_c_im, s5_d, s5_glu_w, s5_glu_b):
    bp, tp, d = x_prompt.shape
    bs, ts, _ = x_sample.shape
    ctx_row = bs

    cond = jnp.concatenate([c, c_ctx[None, :], jnp.zeros((MOD_ROWS - bs - 1, d), F32)], axis=0)
    mods = ada_mod(cond, w_ada, b_ada)
    w_in_r, w_ba = _reorder_w_in(w_in)
    w_out_b = w_out.astype(BF16)
    w1_b = w_mlp1.astype(BF16)
    w2_b = w_mlp2.astype(BF16)
    cos, sin = rope_tables(ts)
    wg_all, bg_all, sp_all = _lru_params(lru_wa, lru_ba, lru_wx, lru_bx, lru_lambda)
    bb_all, cc_all, a5_all = _s5_params(s5_lambda_re, s5_lambda_im, s5_log_dt, s5_b_re, s5_b_im, s5_c_re, s5_c_im)

    streams = {
        'ctx': dict(x=x_prompt.reshape(bp * tp, d), b=bp, t=tp, nb=8, s5_tt=128, dn_nbat=2, dn_tb=tp,
                    mod_row=lambda tok: ctx_row),
        'lat': dict(x=x_sample.reshape(bs * ts, d), b=bs, t=ts, nb=bs, s5_tt=256, dn_nbat=bs, dn_tb=2 * DN_CHUNK,
                    mod_row=lambda tok: tok // ts),
    }
    ks, vs, lrus, dns, s5rs, s5is = [], [], [], [], [], []
    for l in range(DEPTH):
        mod = mods[l].reshape(MOD_ROWS, N_MOD, d).transpose(1, 0, 2).reshape(N_MOD * MOD_ROWS, 1, d)
        qg = q_norm_g[l].reshape(1, HEAD_DIM)
        kg = k_norm_g[l].reshape(1, HEAD_DIM)
        wg, bg, sp = wg_all[l], bg_all[l], sp_all[l]
        bb, cc, a5 = bb_all[l], cc_all[l], a5_all[l]
        al = _lane_row(dn_a_log[l], 2 * DN_HEADS)
        dtb = _lane_row(dn_dt_bias[l], 2 * DN_HEADS)
        for name, st in streams.items():
            b, t, nb = st['b'], st['t'], st['nb']
            is_ctx = name == 'ctx'
            proj, ba = in_proj(st['x'], mod, w_in_r, w_ba, l, st['mod_row'])
            proj3 = proj.reshape(b, t, N_PROJ)
            ba3 = ba.reshape(b, t, LANE)

            if is_ctx:
                attn, kn, vv = ctx_attention(proj, qg, kg, b, t)
                ks.append(kn.reshape(b, t, ATTN_KV_HEADS, HEAD_DIM))
                vs.append(vv.reshape(b, t, ATTN_KV_HEADS, HEAD_DIM))
                h0_lru = jnp.zeros((2, b, LRU_WIDTH), F32)
                s0_dn = jnp.zeros((2, b, DN_HEADS, DN_DK, DN_DV), F32)
                h0_s5 = jnp.zeros((2, b, 2 * S5_NSTATE), F32)
            else:
                q_s, k_s, v_s = lat_prep(proj, qg, kg, cos, sin, t)
                k_all = jnp.concatenate([cache_attn_k[:, l].reshape(b, -1, KV_WIDTH).astype(BF16),
                                         k_s.reshape(b, t, KV_WIDTH)], axis=1)
                v_all = jnp.concatenate([cache_attn_v[:, l].reshape(b, -1, KV_WIDTH).astype(BF16),
                                         v_s.reshape(b, t, KV_WIDTH)], axis=1)
                attn = lat_attention(q_s, k_all, v_all, t)
                h0_lru = jnp.swapaxes(state_rglru[:, l], 0, 1)
                s0_dn = jnp.swapaxes(state_delta[:, l], 0, 1)
                h0_s5 = jnp.swapaxes(jnp.concatenate([state_s5_re[:, l].reshape(b, 2, S5_NSTATE),
                                                      state_s5_im[:, l].reshape(b, 2, S5_NSTATE)], axis=-1), 0, 1)

            h_lru, lru_fin = lru_scan(proj3, h0_lru, lru_conv_w[l], lru_conv_b[l].reshape(1, -1), wg, bg, sp, nb,
                                      st['s5_tt'])
            o_f, o_b, dn_fin = dn_scan(dn_prep(proj3, dn_conv_w[l], min(t, 512)), ba3, s0_dn, al, dtb,
                                       st['dn_nbat'], st['dn_tb'])
            y_s5, s5_fin = s5_scan(proj3, h0_s5, bb, cc, a5, nb, st['s5_tt'])
            lru_out, dn_out, s5_out = mix_post(
                proj, h_lru, o_f, o_b, y_s5, dn_norm_g[l].reshape(1, DN_DV), s5_d[l].reshape(1, MIXW),
                s5_glu_w[l].astype(BF16), s5_glu_b[l].reshape(1, MIXW))
            if is_ctx:
                lrus.append(jnp.swapaxes(lru_fin, 0, 1))
                dns.append(jnp.swapaxes(dn_fin, 0, 1))
                s5_fin = jnp.swapaxes(s5_fin, 0, 1)
                s5rs.append(s5_fin[..., :S5_NSTATE].reshape(b, 2, S5_GROUPS, S5_STATE))
                s5is.append(s5_fin[..., S5_NSTATE:].reshape(b, 2, S5_GROUPS, S5_STATE))

            x1 = out_proj(st['x'], [attn, lru_out, dn_out, s5_out], w_out_b, l, mod,
                          ln1_g[l].reshape(1, d), ln1_b[l].reshape(1, d), st['mod_row'])
            st['x'] = mlp(x1, w1_b, w2_b, l, mod, ln2_g[l].reshape(1, d), ln2_b[l].reshape(1, d), st['mod_row'])

    y_prompt = streams['ctx']['x'].reshape(bp, tp, d)
    y_sample = streams['lat']['x'].reshape(bs, ts, d)
    return (y_prompt, y_sample, jnp.stack(ks, axis=1), jnp.stack(vs, axis=1), jnp.stack(lrus, axis=1),
            jnp.stack(dns, axis=1), jnp.stack(s5rs, axis=1), jnp.stack(s5is, axis=1))
```

```python
import functools

import jax
import jax.numpy as jnp
from jax import lax
from jax.experimental import pallas as pl
from jax.experimental.pallas import tpu as pltpu

F32 = jnp.float32
BF16 = jnp.bfloat16

D_MODEL = 2048
DEPTH = 2
GRID_W = 64
CONV_W = 4
EPS = 1e-6
ROPE_THETA = 10000.0
N_MOD = 6
HEAD_DIM = 128
ATTN_WIDTH = D_MODEL // 4
ATTN_HEADS = ATTN_WIDTH // HEAD_DIM
ATTN_KV_HEADS = ATTN_HEADS // 2
ATTN_GROUP = ATTN_HEADS // ATTN_KV_HEADS
KV_WIDTH = ATTN_KV_HEADS * HEAD_DIM
ATTN_SCALE = HEAD_DIM ** -0.5
LRU_WIDTH = D_MODEL // 4
LRU_BLOCKS = 8
LRU_C = 8.0
DN_DK = 128
DN_DV = 128
DN_WIDTH = D_MODEL // 4
DN_HEADS = DN_WIDTH // DN_DV
DN_CHUNK = 64
S5_WIDTH = D_MODEL // 4
S5_CH = 16
S5_GROUPS = S5_WIDTH // S5_CH
S5_STATE = 64
S5_NSTATE = S5_GROUPS * S5_STATE
DEEPNORM_ALPHA = (2 * DEPTH) ** 0.25
MIXW = D_MODEL // 4

COL_DQ = 0
COL_DZ = 1536
COL_Q = 2048
COL_KV = 2560
COL_LX = 3072
COL_LG = 3584
COL_SU = 4096
N_PROJ = 4608
N_BA = 2 * 2 * DN_HEADS
LANE = 128
SUBLANES = 8
MXU = 256

MOD_ROWS = 8
PITCH_PAD = 4
OUT_SUB = 256
IN_SUB = 256
MLP_SUB = 256
VMEM_LIMIT = 56 * 1024 * 1024


def _cparams(sem):
    return pltpu.CompilerParams(dimension_semantics=sem, vmem_limit_bytes=VMEM_LIMIT)


def _ln(x):
    mu = jnp.mean(x, axis=-1, keepdims=True)
    xc = x - mu
    var = jnp.mean(xc * xc, axis=-1, keepdims=True)
    return xc * lax.rsqrt(var + EPS)


def _softplus(x):
    return jnp.maximum(x, 0.0) + jnp.log1p(jnp.exp(-jnp.abs(x)))


def _gelu(x):
    return 0.5 * x * (1.0 + jnp.tanh(0.7978845608028654 * (x + 0.044715 * (x * x * x))))


def _dot(a, b):
    return jnp.dot(a, b, preferred_element_type=F32)


def _dot_nt(a, b):
    return lax.dot_general(a, b, (((1,), (1,)), ((), ())), preferred_element_type=F32)


def _ada_kernel(c_ref, w_ref, b_ref, o_ref):
    cs = c_ref[...]
    s = cs * jax.nn.sigmoid(cs)
    o_ref[0] = _dot(s.astype(BF16), w_ref[0].astype(BF16)) + b_ref[0]


def ada_mod(cond, w_ada, b_ada, tn=1024):
    depth, d, n = w_ada.shape
    return pl.pallas_call(
        _ada_kernel,
        grid=(depth, n // tn),
        in_specs=[pl.BlockSpec((MOD_ROWS, d), lambda l, j: (0, 0)),
                  pl.BlockSpec((1, d, tn), lambda l, j: (l, 0, j)),
                  pl.BlockSpec((1, 1, tn), lambda l, j: (l, 0, j))],
        out_specs=pl.BlockSpec((1, MOD_ROWS, tn), lambda l, j: (l, 0, j)),
        out_shape=jax.ShapeDtypeStruct((depth, MOD_ROWS, n), F32),
        compiler_params=_cparams(("parallel", "parallel")),
        name="ada_mod",
    )(cond, w_ada, b_ada.reshape(depth, 1, n))


def _in_kernel(x_ref, sc_ref, sh_ref, w_ref, wba_ref, o_ref, ba_ref, u_ref):
    j = pl.program_id(1)

    @pl.when(j == 0)
    def _():
        for r0 in range(0, x_ref.shape[0], IN_SUB):
            rows = slice(r0, r0 + IN_SUB)
            u = (_ln(x_ref[rows, :]) * (1.0 + sc_ref[0]) + sh_ref[0]).astype(BF16)
            u_ref[rows, :] = u
            o_ref[rows, :] = _dot(u, w_ref[...])
            ba_ref[rows, :] = _dot(u, wba_ref[...])

    @pl.when(j > 0)
    def _():
        o_ref[...] = _dot(u_ref[...], w_ref[...])


def in_proj(x, mod, w, w_ba, layer, mod_row, tm=1024, tn=768):
    m, d = x.shape
    n = w.shape[2]

    def mod_spec(kind):
        return pl.BlockSpec((1, 1, d), lambda i, j: (kind * MOD_ROWS + mod_row(i * tm), 0, 0))

    return pl.pallas_call(
        _in_kernel,
        grid=(m // tm, n // tn),
        in_specs=[pl.BlockSpec((tm, d), lambda i, j: (i, 0)),
                  mod_spec(1), mod_spec(0),
                  pl.BlockSpec((None, d, tn), lambda i, j: (layer, 0, j)),
                  pl.BlockSpec((None, d, LANE), lambda i, j: (layer, 0, 0))],
        out_specs=[pl.BlockSpec((tm, tn), lambda i, j: (i, j)),
                   pl.BlockSpec((tm, LANE), lambda i, j: (i, 0))],
        out_shape=[jax.ShapeDtypeStruct((m, n), F32), jax.ShapeDtypeStruct((m, LANE), F32)],
        scratch_shapes=[pltpu.VMEM((tm, d), BF16)],
        compiler_params=_cparams(("parallel", "arbitrary")),
        name="in_proj",
    )(x, mod, mod, w, w_ba)


def _rms_heads(x, g, heads):
    outs = []
    for h in range(heads):
        xh = x[:, h * HEAD_DIM:(h + 1) * HEAD_DIM]
        outs.append(xh * lax.rsqrt(jnp.mean(xh * xh, axis=-1, keepdims=True) + EPS) * g)
    return outs


def _softmax_av(q, k, v, scale):
    s = _dot_nt(q, k)
    if scale is not None:
        s = s * scale
    m = jnp.max(s, axis=-1, keepdims=True)
    p = jnp.exp(s - m)
    l = jnp.sum(p, axis=-1, keepdims=True)
    return _dot(p.astype(BF16), v) / l


def _ctx_attn_kernel(q_ref, kv_ref, qg_ref, kg_ref, o_ref, kn_ref, v_ref):
    qs = _rms_heads(q_ref[...], qg_ref[...], ATTN_HEADS)
    kv = kv_ref[...]
    ks = _rms_heads(kv[:, :KV_WIDTH], kg_ref[...], ATTN_KV_HEADS)
    v = kv[:, KV_WIDTH:]
    v_ref[...] = v
    t = q_ref.shape[0]
    for kh in range(ATTN_KV_HEADS):
        kn_ref[:, kh * HEAD_DIM:(kh + 1) * HEAD_DIM] = ks[kh]
        q2 = jnp.concatenate([qs[kh * ATTN_GROUP + g] for g in range(ATTN_GROUP)], axis=0).astype(BF16)
        o = _softmax_av(q2, ks[kh].astype(BF16), v[:, kh * HEAD_DIM:(kh + 1) * HEAD_DIM].astype(BF16), ATTN_SCALE)
        for g in range(ATTN_GROUP):
            h = kh * ATTN_GROUP + g
            o_ref[:, h * HEAD_DIM:(h + 1) * HEAD_DIM] = o[g * t:(g + 1) * t].astype(BF16)


def ctx_attention(proj, qg, kg, batch, seq):
    return pl.pallas_call(
        _ctx_attn_kernel,
        grid=(batch,),
        in_specs=[pl.BlockSpec((seq, ATTN_WIDTH), lambda b: (b, COL_Q // ATTN_WIDTH)),
                  pl.BlockSpec((seq, 2 * KV_WIDTH), lambda b: (b, COL_KV // (2 * KV_WIDTH))),
                  pl.BlockSpec((1, HEAD_DIM), lambda b: (0, 0)),
                  pl.BlockSpec((1, HEAD_DIM), lambda b: (0, 0))],
        out_specs=[pl.BlockSpec((seq, ATTN_WIDTH), lambda b: (b, 0)),
                   pl.BlockSpec((seq, KV_WIDTH), lambda b: (b, 0)),
                   pl.BlockSpec((seq, KV_WIDTH), lambda b: (b, 0))],
        out_shape=[jax.ShapeDtypeStruct((batch * seq, ATTN_WIDTH), BF16),
                   jax.ShapeDtypeStruct((batch * seq, KV_WIDTH), F32),
                   jax.ShapeDtypeStruct((batch * seq, KV_WIDTH), F32)],
        compiler_params=_cparams(("parallel",)),
        name="ctx_attention",
    )(proj, proj, qg, kg)


def _rope(x, cos, sin, heads):
    w = x.shape[-1]
    lane = lax.broadcasted_iota(jnp.int32, x.shape, 1)
    quarter = HEAD_DIM // 4
    partner = jnp.where((lane % (2 * quarter)) < quarter,
                        pltpu.roll(x, w - quarter, 1), pltpu.roll(x, quarter, 1))
    cos_t = jnp.concatenate([cos] * heads, axis=1)
    sin_t = jnp.concatenate([sin] * heads, axis=1)
    return x * cos_t + partner * sin_t


def _lat_prep_kernel(q_ref, kv_ref, qg_ref, kg_ref, cos_ref, sin_ref, qo_ref, ko_ref, vo_ref):
    qn = jnp.concatenate(_rms_heads(q_ref[...], qg_ref[...], ATTN_HEADS), axis=1)
    kv = kv_ref[...]
    kn = jnp.concatenate(_rms_heads(kv[:, :KV_WIDTH], kg_ref[...], ATTN_KV_HEADS), axis=1)
    cos = cos_ref[...]
    sin = sin_ref[...]
    qo_ref[...] = (_rope(qn, cos, sin, ATTN_HEADS) * ATTN_SCALE).astype(BF16)
    ko_ref[...] = _rope(kn, cos, sin, ATTN_KV_HEADS).astype(BF16)
    vo_ref[...] = kv[:, KV_WIDTH:].astype(BF16)


def lat_prep(proj, qg, kg, cos, sin, seq, tm=1024):
    m = proj.shape[0]
    per = seq // tm
    return pl.pallas_call(
        _lat_prep_kernel,
        grid=(m // tm,),
        in_specs=[pl.BlockSpec((tm, ATTN_WIDTH), lambda i: (i, COL_Q // ATTN_WIDTH)),
                  pl.BlockSpec((tm, 2 * KV_WIDTH), lambda i: (i, COL_KV // (2 * KV_WIDTH))),
                  pl.BlockSpec((1, HEAD_DIM), lambda i: (0, 0)),
                  pl.BlockSpec((1, HEAD_DIM), lambda i: (0, 0)),
                  pl.BlockSpec((tm, HEAD_DIM), lambda i: (i % per, 0)),
                  pl.BlockSpec((tm, HEAD_DIM), lambda i: (i % per, 0))],
        out_specs=[pl.BlockSpec((tm, ATTN_WIDTH), lambda i: (i, 0)),
                   pl.BlockSpec((tm, KV_WIDTH), lambda i: (i, 0)),
                   pl.BlockSpec((tm, KV_WIDTH), lambda i: (i, 0))],
        out_shape=[jax.ShapeDtypeStruct((m, ATTN_WIDTH), BF16),
                   jax.ShapeDtypeStruct((m, KV_WIDTH), BF16),
                   jax.ShapeDtypeStruct((m, KV_WIDTH), BF16)],
        compiler_params=_cparams(("parallel",)),
        name="lat_prep",
    )(proj, proj, qg, kg, cos, sin)


def _lat_attn_kernel(q_ref, k_ref, v_ref, o_ref):
    k = k_ref[0]
    v = v_ref[0]
    half = q_ref.shape[0] // 2
    sls = [(slice(r0, r0 + half), slice(g * HEAD_DIM, (g + 1) * HEAD_DIM))
           for r0 in (0, half) for g in range(ATTN_GROUP)]
    ss = [_dot_nt(q_ref[sl], k) for sl in sls]
    ps = [jnp.exp(s - jnp.max(s, axis=-1, keepdims=True)) for s in ss]
    ls = [jnp.sum(p, axis=-1, keepdims=True) for p in ps]
    os = [_dot(p.astype(BF16), v) for p in ps]
    for sl, o, l in zip(sls, os, ls):
        o_ref[sl] = (o / l).astype(BF16)


def lat_attention(q, k_all, v_all, seq, tq=256):
    b, s, _ = k_all.shape
    nq = seq // tq
    gw = ATTN_GROUP * HEAD_DIM
    return pl.pallas_call(
        _lat_attn_kernel,
        grid=(b, ATTN_KV_HEADS, nq),
        in_specs=[pl.BlockSpec((tq, gw), lambda bi, kh, qi: (bi * nq + qi, kh)),
                  pl.BlockSpec((1, s, HEAD_DIM), lambda bi, kh, qi: (bi, 0, kh)),
                  pl.BlockSpec((1, s, HEAD_DIM), lambda bi, kh, qi: (bi, 0, kh))],
        out_specs=pl.BlockSpec((tq, gw), lambda bi, kh, qi: (bi * nq + qi, kh)),
        out_shape=jax.ShapeDtypeStruct((b * seq, ATTN_WIDTH), BF16),
        compiler_params=_cparams(("parallel", "parallel", "arbitrary")),
        name="lat_attention",
    )(q, k_all, v_all)


def rope_tables(seq):
    t = jnp.arange(seq)
    row = (t // GRID_W).astype(F32)
    col = (t % GRID_W).astype(F32)
    quarter = HEAD_DIM // 4
    inv_freq = jnp.power(ROPE_THETA, -jnp.arange(quarter, dtype=F32) / quarter)
    ar = row[:, None] * inv_freq[None, :]
    ac = col[:, None] * inv_freq[None, :]
    cos = jnp.concatenate([jnp.cos(ar), jnp.cos(ar), jnp.cos(ac), jnp.cos(ac)], axis=1)
    sin = jnp.concatenate([-jnp.sin(ar), jnp.sin(ar), -jnp.sin(ac), jnp.sin(ac)], axis=1)
    return cos, sin


def _chunk_index(d, i, n):
    return i + d * (n - 1 - 2 * i)


def _lru_kernel(x_ref, xp_ref, xn_ref, cw_ref, cb_ref, wg_ref, bg_ref, sp_ref, h0_ref,
                h_ref, fin_ref, xpad_ref, a_ref, b_ref, hc_ref, *, nb, tt):
    d = pl.program_id(0)
    i = pl.program_id(2)
    n = pl.num_programs(2)
    ci = _chunk_index(d, i, n)
    pitch = tt + PITCH_PAD
    rows = nb * pitch
    nsl = LRU_WIDTH // LANE

    @pl.when(i == 0)
    def _():
        xpad_ref[...] = jnp.zeros_like(xpad_ref)
        hc_ref[...] = h0_ref[0]

    for s in range(nb):
        base = 8 + s * pitch
        xpad_ref[base:base + tt, :] = x_ref[s]
        xpad_ref[base - 1:base, :] = jnp.where(ci == 0, 0.0, xp_ref[s, 7:8, :])
        xpad_ref[base + tt:base + tt + 2, :] = jnp.where(ci == n - 1, 0.0, xn_ref[s, 0:2, :])
    w = cw_ref[...]
    xc = cb_ref[...] + sum(xpad_ref[7 + j:7 + j + rows, :] * w[j:j + 1, :] for j in range(CONV_W))
    pre = _dot(xc.astype(BF16), wg_ref[0]) + bg_ref[0]
    r = jax.nn.sigmoid(pre[:, :LRU_WIDTH])
    ig = jax.nn.sigmoid(pre[:, LRU_WIDTH:])
    a = jnp.exp((-LRU_C) * r * sp_ref[0])
    inp = jnp.sqrt(1.0 - a * a) * (ig * xc)
    for c in range(nsl):
        a_ref[c] = a[:, c * LANE:(c + 1) * LANE]
        b_ref[c] = inp[:, c * LANE:(c + 1) * LANE]

    def body(t, carry):
        row = t + d * (tt - 1 - 2 * t)
        out = []
        for c in range(nsl):
            idx = (c, pl.ds(row, nb, stride=pitch), slice(None))
            h = a_ref[idx] * carry[c] + b_ref[idx]
            b_ref[idx] = h
            out.append(h)
        return tuple(out)

    carry = tuple(hc_ref[:, c * LANE:(c + 1) * LANE] for c in range(nsl))
    carry = lax.fori_loop(0, tt, body, carry, unroll=8)
    for c in range(nsl):
        hc_ref[:, c * LANE:(c + 1) * LANE] = carry[c]
        for s in range(nb):
            h_ref[0, s, :, c * LANE:(c + 1) * LANE] = b_ref[c, s * pitch:s * pitch + tt, :]

    @pl.when(i == n - 1)
    def _():
        fin_ref[0] = hc_ref[...]


def lru_scan(proj3, h0, cw, cb, wg, bg, sp, nb, tt=128):
    b, t, _ = proj3.shape
    w = LRU_WIDTH
    n = t // tt
    t8 = tt // 8
    rows = nb * (tt + PITCH_PAD)
    col = COL_LX // w

    def cidx(d, i):
        return _chunk_index(d, i, n)

    return pl.pallas_call(
        functools.partial(_lru_kernel, nb=nb, tt=tt),
        grid=(2, b // nb, n),
        in_specs=[pl.BlockSpec((nb, tt, w), lambda d, g, i: (g, cidx(d, i), col)),
                  pl.BlockSpec((nb, 8, w), lambda d, g, i: (g, jnp.maximum(cidx(d, i) * t8 - 1, 0), col)),
                  pl.BlockSpec((nb, 8, w), lambda d, g, i: (g, jnp.minimum((cidx(d, i) + 1) * t8, t // 8 - 1), col)),
                  pl.BlockSpec((CONV_W, w), lambda d, g, i: (0, 0)),
                  pl.BlockSpec((1, w), lambda d, g, i: (0, 0)),
                  pl.BlockSpec((1, w, 2 * w), lambda d, g, i: (d, 0, 0)),
                  pl.BlockSpec((1, 1, 2 * w), lambda d, g, i: (d, 0, 0)),
                  pl.BlockSpec((1, 1, w), lambda d, g, i: (d, 0, 0)),
                  pl.BlockSpec((1, nb, w), lambda d, g, i: (d, g, 0))],
        out_specs=[pl.BlockSpec((1, nb, tt, w), lambda d, g, i: (d, g, cidx(d, i), 0)),
                   pl.BlockSpec((1, nb, w), lambda d, g, i: (d, g, 0))],
        out_shape=[jax.ShapeDtypeStruct((2, b, t, w), F32), jax.ShapeDtypeStruct((2, b, w), F32)],
        scratch_shapes=[pltpu.VMEM((rows + 16, w), F32),
                        pltpu.VMEM((w // LANE, rows, LANE), F32),
                        pltpu.VMEM((w // LANE, rows, LANE), F32),
                        pltpu.VMEM((nb, w), F32)],
        compiler_params=_cparams(("parallel", "parallel", "arbitrary")),
        name="lru_scan",
    )(proj3, proj3, proj3, cw, cb, wg, bg, sp, h0)


def _s5_kernel(u_ref, bb_ref, cc_ref, a_ref, h0_ref, y_ref, fin_ref, up_ref, s_ref, hc_ref, *, nb, tt):
    d = pl.program_id(0)
    i = pl.program_id(2)
    n = pl.num_programs(2)
    pitch = tt + PITCH_PAD
    nsl = S5_NSTATE // LANE

    @pl.when(i == 0)
    def _():
        up_ref[...] = jnp.zeros_like(up_ref)
        hc_ref[...] = h0_ref[0]

    for s in range(nb):
        up_ref[s * pitch:s * pitch + tt, :] = u_ref[s]
    u2 = up_ref[...].astype(BF16)
    per_k = MXU // S5_CH * S5_STATE // MXU
    vs = SUBLANES // nb
    rows_seq = nb * pitch
    npk = nsl // vs

    def slab(c):
        return (c // vs, slice((c % vs) * rows_seq, (c % vs + 1) * rows_seq), slice(None))

    for nt in range(2 * S5_NSTATE // MXU):
        kt = (nt % (S5_NSTATE // MXU)) // per_k
        tile = _dot(u2[:, kt * MXU:(kt + 1) * MXU], bb_ref[0, kt * MXU:(kt + 1) * MXU, nt * MXU:(nt + 1) * MXU])
        s_ref[slab(2 * nt)] = tile[:, :LANE]
        s_ref[slab(2 * nt + 1)] = tile[:, LANE:]

    a_all = a_ref[0]

    def packed_rows(src, off, j):
        parts = [jnp.broadcast_to(src[:, off + (j * vs + v) * LANE:off + (j * vs + v + 1) * LANE], (nb, LANE))
                 for v in range(vs)]
        return parts[0] if vs == 1 else jnp.concatenate(parts, axis=0)

    group = 8
    for j0 in range(0, npk, group):
        js = list(range(j0, min(j0 + group, npk)))
        ar = [packed_rows(a_all, 0, j) for j in js]
        ai = [packed_rows(a_all, S5_NSTATE, j) for j in js]

        def body(t, carry, js=js, ar=ar, ai=ai):
            row = t + d * (tt - 1 - 2 * t)
            out = []
            for k, j in enumerate(js):
                hr, hi = carry[2 * k], carry[2 * k + 1]
                ire = (j, pl.ds(row, SUBLANES, stride=pitch), slice(None))
                iim = (npk + j, pl.ds(row, SUBLANES, stride=pitch), slice(None))
                nr = ar[k] * hr - ai[k] * hi + s_ref[ire]
                ni = ar[k] * hi + ai[k] * hr + s_ref[iim]
                s_ref[ire] = nr
                s_ref[iim] = ni
                out += [nr, ni]
            return tuple(out)

        carry = []
        for j in js:
            carry += [packed_rows(hc_ref, 0, j), packed_rows(hc_ref, S5_NSTATE, j)]
        carry = lax.fori_loop(0, tt, body, tuple(carry), unroll=4)
        for k, j in enumerate(js):
            for v in range(vs):
                c = j * vs + v
                hc_ref[:, c * LANE:(c + 1) * LANE] = carry[2 * k][v * nb:(v + 1) * nb]
                hc_ref[:, S5_NSTATE + c * LANE:S5_NSTATE + (c + 1) * LANE] = carry[2 * k + 1][v * nb:(v + 1) * nb]

    n_out = S5_WIDTH // MXU
    per_n = nsl // n_out
    for nt in range(n_out):
        slabs = [per_n * nt + k for k in range(per_n)] + [nsl + per_n * nt + k for k in range(per_n)]
        lhs = jnp.concatenate([s_ref[slab(c)] for c in slabs], axis=1).astype(BF16)
        y = _dot(lhs, cc_ref[0, nt])
        for s in range(nb):
            y_ref[0, s, :, nt * MXU:(nt + 1) * MXU] = y[s * pitch:s * pitch + tt]

    @pl.when(i == n - 1)
    def _():
        fin_ref[0] = hc_ref[...]


def s5_scan(proj3, h0, bb, cc, a, nb, tt=128):
    b, t, _ = proj3.shape
    w = S5_WIDTH
    n = t // tt
    rows = nb * (tt + PITCH_PAD)
    ns2 = 2 * S5_NSTATE

    def cidx(d, i):
        return _chunk_index(d, i, n)

    return pl.pallas_call(
        functools.partial(_s5_kernel, nb=nb, tt=tt),
        grid=(2, b // nb, n),
        in_specs=[pl.BlockSpec((nb, tt, w), lambda d, g, i: (g, cidx(d, i), COL_SU // w)),
                  pl.BlockSpec((1, w, ns2), lambda d, g, i: (d, 0, 0)),
                  pl.BlockSpec((1,) + cc.shape[1:], lambda d, g, i: (d, 0, 0, 0)),
                  pl.BlockSpec((1, 1, ns2), lambda d, g, i: (d, 0, 0)),
                  pl.BlockSpec((1, nb, ns2), lambda d, g, i: (d, g, 0))],
        out_specs=[pl.BlockSpec((1, nb, tt, w), lambda d, g, i: (d, g, cidx(d, i), 0)),
                   pl.BlockSpec((1, nb, ns2), lambda d, g, i: (d, g, 0))],
        out_shape=[jax.ShapeDtypeStruct((2, b, t, w), F32), jax.ShapeDtypeStruct((2, b, ns2), F32)],
        scratch_shapes=[pltpu.VMEM((rows, w), F32),
                        pltpu.VMEM((ns2 // LANE // (SUBLANES // nb), SUBLANES // nb * rows, LANE), F32),
                        pltpu.VMEM((nb, ns2), F32)],
        compiler_params=_cparams(("parallel", "parallel", "arbitrary")),
        name="s5_scan",
    )(proj3, bb, cc, a, h0)


def _split3(x):
    x1 = x.astype(BF16)
    r1 = x - x1.astype(F32)
    x2 = r1.astype(BF16)
    x3 = (r1 - x2.astype(F32)).astype(BF16)
    return x1, x2, x3


def _mm(x, y):
    return _dot(x.astype(BF16), y.astype(BF16))


def _unit_tri_inverses(mats, rr, cc):
    same16 = (rr // 16) == (cc // 16)
    same32 = (rr // 32) == (cc // 32)
    p = [jnp.where(same16, -a, 0.0) for a in mats]
    t = [jnp.where(rr == cc, 1.0, x) for x in p]
    for _ in range(3):
        p = [_mm(x, x) for x in p]
        t = [x + _mm(x, y) for x, y in zip(t, p)]
    for mask in (same32 & jnp.logical_not(same16), jnp.logical_not(same32)):
        te = [_mm(x, jnp.where(mask, a, 0.0)) for x, a in zip(t, mats)]
        t = [x - _mm(y, x) for x, y in zip(t, te)]
    return t


def _dn_prep_kernel(x_ref, xp_ref, xn_ref, cw_ref, o_ref, xpad_ref, *, tb):
    i = pl.program_id(1)
    n = pl.num_programs(1)
    hw = DN_HEADS * DN_DK
    xpad_ref[8:8 + tb, :] = x_ref[0]
    xpad_ref[0:8, :] = jnp.where(i == 0, 0.0, xp_ref[0])
    xpad_ref[8 + tb:16 + tb, :] = jnp.where(i == n - 1, 0.0, xn_ref[0])
    w = cw_ref[...]
    xc = sum(xpad_ref[7 + j:7 + j + tb, :] * w[j:j + 1, :] for j in range(CONV_W))
    qkv = xc * jax.nn.sigmoid(xc)
    for h in range(DN_HEADS):
        q = qkv[:, h * DN_DK:(h + 1) * DN_DK]
        k = qkv[:, hw + h * DN_DK:hw + (h + 1) * DN_DK]
        o_ref[0, :, h * DN_DK:(h + 1) * DN_DK] = (
            q * lax.rsqrt(jnp.sum(q * q, axis=-1, keepdims=True) + EPS) * (DN_DK ** -0.5))
        o_ref[0, :, hw + h * DN_DK:hw + (h + 1) * DN_DK] = (
            k * lax.rsqrt(jnp.sum(k * k, axis=-1, keepdims=True) + EPS))
    o_ref[0, :, 2 * hw:] = qkv[:, 2 * hw:]


def dn_prep(proj3, cw, tb=256):
    b, t, _ = proj3.shape
    n = t // tb
    t8 = tb // 8
    w3 = 3 * DN_WIDTH
    cq = COL_DQ // w3
    return pl.pallas_call(
        functools.partial(_dn_prep_kernel, tb=tb),
        grid=(b, n),
        in_specs=[pl.BlockSpec((1, tb, w3), lambda bi, i: (bi, i, cq)),
                  pl.BlockSpec((1, 8, w3), lambda bi, i: (bi, jnp.maximum(i * t8 - 1, 0), cq)),
                  pl.BlockSpec((1, 8, w3), lambda bi, i: (bi, jnp.minimum((i + 1) * t8, t // 8 - 1), cq)),
                  pl.BlockSpec((CONV_W, w3), lambda bi, i: (0, 0))],
        out_specs=pl.BlockSpec((1, tb, w3), lambda bi, i: (bi, i, 0)),
        out_shape=jax.ShapeDtypeStruct((b, t, w3), F32),
        scratch_shapes=[pltpu.VMEM((tb + 16, w3), F32)],
        compiler_params=_cparams(("parallel", "parallel")),
        name="dn_prep",
    )(proj3, proj3, proj3, cw)


def _dn_kernel(xf_ref, baf_ref, xb_ref, bab_ref, al_ref, dtb_ref, s0_ref, of_ref, ob_ref, fin_ref, s_ref,
               *, nbat, tb):
    i = pl.program_id(1)
    n = pl.num_programs(1)
    ch = DN_CHUNK
    nc = tb // ch
    hw = DN_HEADS * DN_DK

    @pl.when(i == 0)
    def _():
        s_ref[...] = s0_ref[...]

    rb = lax.broadcasted_iota(jnp.int32, (tb, tb), 0)
    cb = lax.broadcasted_iota(jnp.int32, (tb, tb), 1)
    rr = lax.broadcasted_iota(jnp.int32, (ch, ch), 0)
    cc = lax.broadcasted_iota(jnp.int32, (ch, ch), 1)
    al = al_ref[...]
    dtb = dtb_ref[...]

    x_refs = (xf_ref, xb_ref)
    beta_tab, gc_tab, masks = {}, {}, {}
    for d in range(2):
        reverse = d == 1
        ba_ref = bab_ref if reverse else baf_ref
        tri = ((rb // ch) == (cb // ch)) & ((cb >= rb) if reverse else (cb <= rb))
        tri = jnp.where(tri, 1.0, 0.0).astype(BF16)
        masks[d] = ((cc >= rr) if reverse else (cc <= rr), (cc > rr) if reverse else (cc < rr))
        for bat in range(nbat):
            ba = ba_ref[bat]
            beta_tab[(d, bat)] = jax.nn.sigmoid(ba)
            g_all = -jnp.exp(al) * _softplus(ba + dtb)
            gc_tab[(d, bat)] = sum(_dot(tri, piece) for piece in _split3(g_all))
    units = [dict(d=d, bat=bat, c=c, h=h) for d in range(2) for bat in range(nbat) for c in range(nc)
             for h in range(DN_HEADS)]

    def rows_of(u):
        return slice(u['c'] * ch, (u['c'] + 1) * ch)

    def q_of(u):
        return x_refs[u['d']][u['bat'], rows_of(u), u['h'] * DN_DK:(u['h'] + 1) * DN_DK]

    def k_of(u):
        return x_refs[u['d']][u['bat'], rows_of(u), hw + u['h'] * DN_DK:hw + (u['h'] + 1) * DN_DK]

    def v_of(u):
        return x_refs[u['d']][u['bat'], rows_of(u), 2 * hw + u['h'] * DN_DV:2 * hw + (u['h'] + 1) * DN_DV]

    def beta_of(u):
        lane = u['d'] * DN_HEADS + u['h']
        return beta_tab[(u['d'], u['bat'])][rows_of(u), lane:lane + 1]

    def gcol_of(u):
        lane = 2 * DN_HEADS + u['d'] * DN_HEADS + u['h']
        return gc_tab[(u['d'], u['bat'])][rows_of(u), lane:lane + 1]

    def glast_of(u):
        gcol = gcol_of(u)
        return gcol[0:1, :] if u['d'] == 1 else gcol[ch - 1:ch, :]

    gct = {}
    half_units = len(units) // 2
    for g0 in range(0, len(units), half_units):
        part = units[g0:g0 + half_units]
        for u in part:
            key = (u['d'], u['bat'], u['c'])
            if key not in gct:
                gct[key] = gc_tab[(u['d'], u['bat'])][rows_of(u), :].T
            lane = 2 * DN_HEADS + u['d'] * DN_HEADS + u['h']
            incl, strict = masks[u['d']]
            decay = jnp.where(incl, jnp.exp(gcol_of(u) - gct[key][lane:lane + 1, :]), 0.0)
            k = k_of(u)
            kbf = k.astype(BF16)
            u['a'] = jnp.where(strict, _dot_nt((k * beta_of(u)).astype(BF16), kbf) * decay, 0.0)
            u['qk'] = (_dot_nt(q_of(u).astype(BF16), kbf) * decay).astype(BF16)
        t_inv = _unit_tri_inverses([u['a'] for u in part], rr, cc)
        for u, t in zip(part, t_inv):
            beta = beta_of(u)
            rhs = jnp.concatenate([v_of(u) * beta, k_of(u) * beta * jnp.exp(gcol_of(u))], axis=1)
            sol = rhs + _mm(jnp.where(rr == cc, 0.0, t), rhs)
            u['u_val'] = sol[:, :DN_DV]
            u['w'] = sol[:, DN_DV:].astype(BF16)

    state = {(d, bat, h): s_ref[d, bat, h] for d in range(2) for bat in range(nbat) for h in range(DN_HEADS)}
    for j in range(nc):
        cur = [u for u in units if u['c'] == (nc - 1 - j if u['d'] == 1 else j)]
        keys = [(u['d'], u['bat'], u['h']) for u in cur]
        sb = [state[key].astype(BF16) for key in keys]
        ws = [_dot(u['w'], s) for u, s in zip(cur, sb)]
        qs = [_dot((q_of(u) * jnp.exp(gcol_of(u))).astype(BF16), s) for u, s in zip(cur, sb)]
        vb = [(u['u_val'] - x).astype(BF16) for u, x in zip(cur, ws)]
        os = [x + _dot(u['qk'], y) for u, x, y in zip(cur, qs, vb)]
        kdec_t = [(k_of(u) * jnp.exp(glast_of(u) - gcol_of(u))).T.astype(BF16) for u in cur]
        sn = [state[key] * jnp.exp(glast_of(u)) + _dot(kt, y) for u, key, kt, y in zip(cur, keys, kdec_t, vb)]
        for u, key, o, s in zip(cur, keys, os, sn):
            state[key] = s
            o_ref = ob_ref if u['d'] == 1 else of_ref
            o_ref[u['bat'], u['c'] * ch:(u['c'] + 1) * ch, u['h'] * DN_DV:(u['h'] + 1) * DN_DV] = o
    for (d, bat, h), s in state.items():
        s_ref[d, bat, h] = s

    @pl.when(i == n - 1)
    def _():
        fin_ref[...] = s_ref[...]


def dn_scan(qkv3, ba3, s0, al, dtb, nbat, tb):
    b, t, w3 = qkv3.shape
    n = t // tb

    def data_specs(cidx):
        return [pl.BlockSpec((nbat, tb, w3), lambda g, i: (g, cidx(i), 0)),
                pl.BlockSpec((nbat, tb, LANE), lambda g, i: (g, cidx(i), 0))]

    fwd = lambda i: i
    bwd = lambda i: n - 1 - i
    st_spec = pl.BlockSpec((2, nbat, DN_HEADS, DN_DK, DN_DV), lambda g, i: (0, g, 0, 0, 0))
    return pl.pallas_call(
        functools.partial(_dn_kernel, nbat=nbat, tb=tb),
        grid=(b // nbat, n),
        in_specs=data_specs(fwd) + data_specs(bwd) + [
            pl.BlockSpec((1, LANE), lambda g, i: (0, 0)),
            pl.BlockSpec((1, LANE), lambda g, i: (0, 0)),
            st_spec],
        out_specs=[pl.BlockSpec((nbat, tb, DN_WIDTH), lambda g, i: (g, i, 0)),
                   pl.BlockSpec((nbat, tb, DN_WIDTH), lambda g, i: (g, n - 1 - i, 0)),
                   st_spec],
        out_shape=[jax.ShapeDtypeStruct((b, t, DN_WIDTH), F32),
                   jax.ShapeDtypeStruct((b, t, DN_WIDTH), F32),
                   jax.ShapeDtypeStruct((2, b, DN_HEADS, DN_DK, DN_DV), F32)],
        scratch_shapes=[pltpu.VMEM((2, nbat, DN_HEADS, DN_DK, DN_DV), F32)],
        compiler_params=_cparams(("parallel", "arbitrary")),
        name="dn_scan",
    )(qkv3, ba3, qkv3, ba3, al, dtb, s0)


def _post_kernel(hf_ref, hb_ref, lg_ref, of_ref, ob_ref, dz_ref, ng_ref, yf_ref, yb_ref, su_ref, sd_ref,
                 gw_ref, gb_ref, lru_ref, dn_ref, s5_ref):
    lru_ref[...] = ((hf_ref[0, 0] + hb_ref[0, 0]) * _gelu(lg_ref[...])).astype(BF16)
    o = of_ref[0] + ob_ref[0]
    dz = dz_ref[...]
    for h in range(DN_HEADS):
        sl = slice(h * DN_DV, (h + 1) * DN_DV)
        oh = o[:, sl]
        oh = oh * lax.rsqrt(jnp.mean(oh * oh, axis=-1, keepdims=True) + EPS) * ng_ref[...]
        zh = dz[:, sl]
        dn_ref[:, sl] = (oh * (zh * jax.nn.sigmoid(zh))).astype(BF16)
    y = yf_ref[0, 0] + yb_ref[0, 0] + sd_ref[...] * su_ref[...]
    gy = _gelu(y)
    s5_ref[...] = (gy * jax.nn.sigmoid(_dot(gy.astype(BF16), gw_ref[...]) + gb_ref[...])).astype(BF16)


def mix_post(proj, h_lru, o_f, o_b, y_s5, ng, sd, gw, gb):
    m = proj.shape[0]
    _, b, t, w = h_lru.shape
    tm = min(512, t)
    per = t // tm
    tok = pl.BlockSpec((tm, w), lambda i: (i, 0))

    def pcol(c0):
        return pl.BlockSpec((tm, w), lambda i: (i, c0 // w))

    def dirspec(d):
        return pl.BlockSpec((1, 1, tm, w), lambda i: (d, i // per, i % per, 0))

    seq = pl.BlockSpec((1, tm, w), lambda i: (i // per, i % per, 0))

    def vec(n):
        return pl.BlockSpec((1, n), lambda i: (0, 0))

    out = jax.ShapeDtypeStruct((m, w), BF16)
    return pl.pallas_call(
        _post_kernel,
        grid=(m // tm,),
        in_specs=[dirspec(0), dirspec(1), pcol(COL_LG), seq, seq, pcol(COL_DZ), vec(DN_DV),
                  dirspec(0), dirspec(1), pcol(COL_SU), vec(w), pl.BlockSpec((w, w), lambda i: (0, 0)), vec(w)],
        out_specs=[tok, tok, tok],
        out_shape=[out, out, out],
        compiler_params=_cparams(("parallel",)),
        name="mix_post",
    )(h_lru, h_lru, proj, o_f, o_b, proj, ng, y_s5, y_s5, proj, sd, gw, gb)


def _out_kernel(x_ref, a_ref, b_ref, c_ref, d_ref, w_ref, gate_ref, g_ref, bb_ref, x1_ref):
    q = MIXW
    for r0 in range(0, x_ref.shape[0], OUT_SUB):
        rows = slice(r0, r0 + OUT_SUB)
        y = _dot(a_ref[rows, :], w_ref[0:q, :])
        y += _dot(b_ref[rows, :], w_ref[q:2 * q, :])
        y += _dot(c_ref[rows, :], w_ref[2 * q:3 * q, :])
        y += _dot(d_ref[rows, :], w_ref[3 * q:, :])
        z = DEEPNORM_ALPHA * x_ref[rows, :] + gate_ref[0] * y
        x1_ref[rows, :] = _ln(z) * g_ref[...] + bb_ref[...]


def out_proj(x, parts, w, layer, mod, ln_g, ln_b, mod_row, tm=512):
    m, d = x.shape
    vec = pl.BlockSpec((1, d), lambda i: (0, 0))
    part = pl.BlockSpec((tm, MIXW), lambda i: (i, 0))
    return pl.pallas_call(
        _out_kernel,
        grid=(m // tm,),
        in_specs=[pl.BlockSpec((tm, d), lambda i: (i, 0)), part, part, part, part,
                  pl.BlockSpec((None, d, d), lambda i: (layer, 0, 0)),
                  pl.BlockSpec((1, 1, d), lambda i: (2 * MOD_ROWS + mod_row(i * tm), 0, 0)), vec, vec],
        out_specs=pl.BlockSpec((tm, d), lambda i: (i, 0)),
        out_shape=jax.ShapeDtypeStruct((m, d), F32),
        compiler_params=_cparams(("parallel",)),
        name="out_proj",
    )(x, *parts, w, mod, ln_g, ln_b)


def _mlp_kernel(x_ref, w1_ref, w2_ref, sc_ref, sh_ref, gate_ref, g_ref, b_ref, o_ref, u_ref, acc_ref):
    f = pl.program_id(1)
    last = pl.num_programs(1) - 1
    tm = x_ref.shape[0]

    def hidden(u):
        return jnp.square(jnp.maximum(_dot(u, w1_ref[...]), 0.0)).astype(BF16)

    @pl.when(f == 0)
    def _():
        for r0 in range(0, tm, MLP_SUB):
            rows = slice(r0, r0 + MLP_SUB)
            u = (_ln(x_ref[rows, :]) * (1.0 + sc_ref[0]) + sh_ref[0]).astype(BF16)
            u_ref[rows, :] = u
            acc_ref[rows, :] = _dot(hidden(u), w2_ref[...])

    @pl.when((f > 0) & (f < last))
    def _():
        acc_ref[...] += _dot(hidden(u_ref[...]), w2_ref[...])

    @pl.when(f == last)
    def _():
        for r0 in range(0, tm, MLP_SUB):
            rows = slice(r0, r0 + MLP_SUB)
            a = acc_ref[rows, :] + _dot(hidden(u_ref[rows, :]), w2_ref[...])
            z = DEEPNORM_ALPHA * x_ref[rows, :] + gate_ref[0] * a
            o_ref[rows, :] = _ln(z) * g_ref[...] + b_ref[...]


def mlp(x, w1, w2, layer, mod, ln_g, ln_b, mod_row, tm=512, tf=1024):
    m, d = x.shape
    ff = w1.shape[2]
    assert ff // tf >= 2
    vec = pl.BlockSpec((1, d), lambda i, f: (0, 0))

    def mod_spec(kind):
        return pl.BlockSpec((1, 1, d), lambda i, f: (kind * MOD_ROWS + mod_row(i * tm), 0, 0))

    return pl.pallas_call(
        _mlp_kernel,
        grid=(m // tm, ff // tf),
        in_specs=[pl.BlockSpec((tm, d), lambda i, f: (i, 0)),
                  pl.BlockSpec((None, d, tf), lambda i, f: (layer, 0, f)),
                  pl.BlockSpec((None, tf, d), lambda i, f: (layer, f, 0)),
                  mod_spec(4), mod_spec(3), mod_spec(5), vec, vec],
        out_specs=pl.BlockSpec((tm, d), lambda i, f: (i, 0)),
        out_shape=jax.ShapeDtypeStruct((m, d), F32),
        scratch_shapes=[pltpu.VMEM((tm, d), BF16), pltpu.VMEM((tm, d), F32)],
        compiler_params=_cparams(("parallel", "arbitrary")),
        name="mlp",
    )(x, w1, w2, mod, mod, mod, ln_g, ln_b)


def _reorder_w_in(w):
    def cols(c0, n):
        return w[..., c0:c0 + n]

    aq, akv, lx, lg = cols(0, 512), cols(512, 512), cols(1024, 512), cols(1536, 512)
    dqkv, dz, ba, su = cols(2048, 1536), cols(3584, 512), cols(4096, N_BA), cols(4096 + N_BA, 512)
    pad = jnp.zeros(w.shape[:2] + (LANE - N_BA,), w.dtype)
    main = jnp.concatenate([dqkv, dz, aq, akv, lx, lg, su], axis=-1).astype(BF16)
    return main, jnp.concatenate([ba, pad], axis=-1).astype(BF16)


def _block_diag(blocks):
    n, r, c = blocks.shape[-3:]
    eye = jnp.eye(n, dtype=blocks.dtype)
    out = blocks[..., :, :, None, :] * eye[:, None, :, None]
    return out.reshape(blocks.shape[:-3] + (n * r, n * c))


def _lru_params(wa, ba, wx, bx, lam):
    wg = jnp.concatenate([_block_diag(wa.astype(BF16)), _block_diag(wx.astype(BF16))], axis=-1)
    bg = jnp.concatenate([ba, bx], axis=-1)[..., None, :]
    sp = jax.nn.softplus(-lam)[..., None, :]
    return wg, bg, sp


def _s5_params(lam_re, lam_im, log_dt, b_re, b_im, c_re, c_im):
    lead = lam_re.shape[:-2]
    dt = jnp.exp(log_dt)[..., None]
    mag = jnp.exp(lam_re * dt)
    abar_re = mag * jnp.cos(lam_im * dt)
    abar_im = mag * jnp.sin(lam_im * dt)
    den = lam_re * lam_re + lam_im * lam_im
    nr = abar_re - 1.0
    ni = abar_im
    f_re = (nr * lam_re + ni * lam_im) / den
    f_im = (ni * lam_re - nr * lam_im) / den
    bb_re = f_re[..., None] * b_re - f_im[..., None] * b_im
    bb_im = f_re[..., None] * b_im + f_im[..., None] * b_re
    to_in = lambda m: _block_diag(jnp.swapaxes(m, -1, -2).astype(BF16))
    bb = jnp.concatenate([to_in(bb_re), to_in(bb_im)], axis=-1)
    n_out = S5_WIDTH // MXU
    gpt = S5_GROUPS // n_out

    def to_out(m, nt):
        return _block_diag(jnp.swapaxes(m[..., nt * gpt:(nt + 1) * gpt, :, :], -1, -2).astype(BF16))

    cc = jnp.stack([jnp.concatenate([to_out(c_re, nt), to_out(-c_im, nt)], axis=-2)
                    for nt in range(n_out)], axis=-3)
    a = jnp.concatenate([abar_re.reshape(lead + (1, -1)), abar_im.reshape(lead + (1, -1))], axis=-1)
    return bb, cc, a


def _lane_row(vals, offset):
    return jnp.zeros((1, LANE), F32).at[0, offset:offset + vals.size].set(vals.reshape(-1))


def kernel(x_prompt, x_sample, cache_attn_k, cache_attn_v, state_rglru, state_delta, state_s5_re, state_s5_im, c, c_ctx, w_ada, b_ada, w_in, w_out, ln1_g, ln1_b, ln2_g, ln2_b, w_mlp1, w_mlp2, q_norm_g, k_norm_g, lru_conv_w, lru_conv_b, lru_wa, lru_ba, lru_wx, lru_bx, lru_lambda, dn_conv_w, dn_a_log, dn_dt_bias, dn_norm_g, s5_lambda_re, s5_lambda_im, s5_log_dt, s5_b_re, s5_b_im, s5_c_re, s5_c_im, s5_d, s5_glu_w, s5_glu_b):
    bp, tp, d = x_prompt.shape
    bs, ts, _ = x_sample.shape
    ctx_row = bs

    cond = jnp.concatenate([c, c_ctx[None, :], jnp.zeros((MOD_ROWS - bs - 1, d), F32)], axis=0)
    mods = ada_mod(cond, w_ada, b_ada)
    w_in_r, w_ba = _reorder_w_in(w_in)
    w_out_b = w_out.astype(BF16)
    w1_b = w_mlp1.astype(BF16)
    w2_b = w_mlp2.astype(BF16)
    cos, sin = rope_tables(ts)
    wg_all, bg_all, sp_all = _lru_params(lru_wa, lru_ba, lru_wx, lru_bx, lru_lambda)
    bb_all, cc_all, a5_all = _s5_params(s5_lambda_re, s5_lambda_im, s5_log_dt, s5_b_re, s5_b_im, s5_c_re, s5_c_im)

    streams = {
        'ctx': dict(x=x_prompt.reshape(bp * tp, d), b=bp, t=tp, nb=8, s5_tt=128, dn_nbat=2, dn_tb=tp,
                    mod_row=lambda tok: ctx_row),
        'lat': dict(x=x_sample.reshape(bs * ts, d), b=bs, t=ts, nb=bs, s5_tt=256, dn_nbat=bs, dn_tb=2 * DN_CHUNK,
                    mod_row=lambda tok: tok // ts),
    }
    ks, vs, lrus, dns, s5rs, s5is = [], [], [], [], [], []
    for l in range(DEPTH):
        mod = mods[l].reshape(MOD_ROWS, N_MOD, d).transpose(1, 0, 2).reshape(N_MOD * MOD_ROWS, 1, d)
        qg = q_norm_g[l].reshape(1, HEAD_DIM)
        kg = k_norm_g[l].reshape(1, HEAD_DIM)
        wg, bg, sp = wg_all[l], bg_all[l], sp_all[l]
        bb, cc, a5 = bb_all[l], cc_all[l], a5_all[l]
        al = _lane_row(dn_a_log[l], 2 * DN_HEADS)
        dtb = _lane_row(dn_dt_bias[l], 2 * DN_HEADS)
        for name, st in streams.items():
            b, t, nb = st['b'], st['t'], st['nb']
            is_ctx = name == 'ctx'
            proj, ba = in_proj(st['x'], mod, w_in_r, w_ba, l, st['mod_row'])
            proj3 = proj.reshape(b, t, N_PROJ)
            ba3 = ba.reshape(b, t, LANE)

            if is_ctx:
                attn, kn, vv = ctx_attention(proj, qg, kg, b, t)
                ks.append(kn.reshape(b, t, ATTN_KV_HEADS, HEAD_DIM))
                vs.append(vv.reshape(b, t, ATTN_KV_HEADS, HEAD_DIM))
                h0_lru = jnp.zeros((2, b, LRU_WIDTH), F32)
                s0_dn = jnp.zeros((2, b, DN_HEADS, DN_DK, DN_DV), F32)
                h0_s5 = jnp.zeros((2, b, 2 * S5_NSTATE), F32)
            else:
                q_s, k_s, v_s = lat_prep(proj, qg, kg, cos, sin, t)
                k_all = jnp.concatenate([cache_attn_k[:, l].reshape(b, -1, KV_WIDTH).astype(BF16),
                                         k_s.reshape(b, t, KV_WIDTH)], axis=1)
                v_all = jnp.concatenate([cache_attn_v[:, l].reshape(b, -1, KV_WIDTH).astype(BF16),
                                         v_s.reshape(b, t, KV_WIDTH)], axis=1)
                attn = lat_attention(q_s, k_all, v_all, t)
                h0_lru = jnp.swapaxes(state_rglru[:, l], 0, 1)
                s0_dn = jnp.swapaxes(state_delta[:, l], 0, 1)
                h0_s5 = jnp.swapaxes(jnp.concatenate([state_s5_re[:, l].reshape(b, 2, S5_NSTATE),
                                                      state_s5_im[:, l].reshape(b, 2, S5_NSTATE)], axis=-1), 0, 1)

            h_lru, lru_fin = lru_scan(proj3, h0_lru, lru_conv_w[l], lru_conv_b[l].reshape(1, -1), wg, bg, sp, nb,
                                      st['s5_tt'])
            o_f, o_b, dn_fin = dn_scan(dn_prep(proj3, dn_conv_w[l], min(t, 512)), ba3, s0_dn, al, dtb,
                                       st['dn_nbat'], st['dn_tb'])
            y_s5, s5_fin = s5_scan(proj3, h0_s5, bb, cc, a5, nb, st['s5_tt'])
            lru_out, dn_out, s5_out = mix_post(
                proj, h_lru, o_f, o_b, y_s5, dn_norm_g[l].reshape(1, DN_DV), s5_d[l].reshape(1, MIXW),
                s5_glu_w[l].astype(BF16), s5_glu_b[l].reshape(1, MIXW))
            if is_ctx:
                lrus.append(jnp.swapaxes(lru_fin, 0, 1))
                dns.append(jnp.swapaxes(dn_fin, 0, 1))
                s5_fin = jnp.swapaxes(s5_fin, 0, 1)
                s5rs.append(s5_fin[..., :S5_NSTATE].reshape(b, 2, S5_GROUPS, S5_STATE))
                s5is.append(s5_fin[..., S5_NSTATE:].reshape(b, 2, S5_GROUPS, S5_STATE))

            x1 = out_proj(st['x'], [attn, lru_out, dn_out, s5_out], w_out_b, l, mod,
                          ln1_g[l].reshape(1, d), ln1_b[l].reshape(1, d), st['mod_row'])
            st['x'] = mlp(x1, w1_b, w2_b, l, mod, ln2_g[l].reshape(1, d), ln2_b[l].reshape(1, d), st['mod_row'])

    y_prompt = streams['ctx']['x'].reshape(bp, tp, d)
    y_sample = streams['lat']['x'].reshape(bs, ts, d)
    return (y_prompt, y_sample, jnp.stack(ks, axis=1), jnp.stack(vs, axis=1), jnp.stack(lrus, axis=1),
            jnp.stack(dns, axis=1), jnp.stack(s5rs, axis=1), jnp.stack(s5is, axis=1))
```
